```python
import math
import jax, jax.numpy as jnp
from jax import lax
import numpy as np

D_MODEL = 1024
BATCH = 16
SEQ = 256
DEPTH = 4
DEC_BATCH = 8
DEC_SEQ = 4096
PAST_LEN = 256

GRID_W = 64
EPS = 1e-6
N_DIR = 2
SSD_HEADS = 16
SSD_HEADDIM = 64
SSD_INNER = SSD_HEADS * SSD_HEADDIM
SSD_GROUPS = 2
SSD_STATE = 64
SSD_CONV = 5
SSD_CHUNK = 128
SSD_XBC = SSD_INNER + 2 * SSD_GROUPS * SSD_STATE
GLA_HEADS = 8
GLA_DK = 64
GLA_DV = 128
GLA_KDIM = GLA_HEADS * GLA_DK
GLA_VDIM = GLA_HEADS * GLA_DV
GLA_GATE_RANK = 16
GLA_GATE_TAU = 16.0
GLA_CHUNK = 64
FFN_HIDDEN = 2816
N_EXPERTS = 8
TOP_K = 2
EXPERT_HIDDEN = 2816
N_DENSE = (DEPTH + 1) // 2
N_MOE = DEPTH // 2
IN_SIZES = (SSD_INNER, SSD_XBC, N_DIR * SSD_HEADS, GLA_KDIM, GLA_KDIM, GLA_VDIM, GLA_VDIM, N_DIR * GLA_GATE_RANK, D_MODEL, D_MODEL)
D_IN = SSD_INNER + SSD_XBC + N_DIR * SSD_HEADS + 2 * GLA_KDIM + 2 * GLA_VDIM + N_DIR * GLA_GATE_RANK + 2 * D_MODEL

kernel_name = 'hybrid_ssd_gla_flow_trunk_step'

F32 = jnp.float32


def rmsnorm(x, g):
    xf = x.astype(F32)
    y = xf * lax.rsqrt(jnp.mean(xf * xf, axis=-1, keepdims=True) + EPS)
    return (y * g.astype(F32)).astype(x.dtype)


def flip(a):
    return jnp.flip(a, axis=1)


def grid_pos_embed(l, d):
    rows = l // GRID_W
    row = jnp.repeat(jnp.arange(rows, dtype=F32), GRID_W)
    col = jnp.tile(jnp.arange(GRID_W, dtype=F32), rows)
    quarter = d // 4
    omega = jnp.exp(-math.log(10000.0) * jnp.arange(quarter, dtype=F32) / quarter)
    er = row[:, None] * omega
    ec = col[:, None] * omega
    return jnp.concatenate([jnp.sin(er), jnp.cos(er), jnp.sin(ec), jnp.cos(ec)], axis=-1)


def dwconv_centred(x, w, b):
    l = x.shape[1]
    pad = SSD_CONV // 2
    xp = jnp.pad(x, ((0, 0), (pad, pad), (0, 0)))
    y = b
    for kk in range(SSD_CONV):
        y = y + xp[:, kk:kk + l] * w[kk]
    return y


def ssd_scan(x, dt, A, B, C, h0):
    b, l, h, p = x.shape
    g, n = B.shape[2], B.shape[3]
    q = SSD_CHUNK
    nc = l // q
    hg = h // g
    xc = x.reshape(b, nc, q, g, hg, p)
    dtc = dt.reshape(b, nc, q, g, hg)
    Bc = B.reshape(b, nc, q, g, n)
    Cc = C.reshape(b, nc, q, g, n)
    acs = jnp.cumsum(dtc * A.reshape(g, hg), axis=2)
    causal = jnp.tril(jnp.ones((q, q), bool))[None, None, :, :, None, None]
    seg = acs[:, :, :, None] - acs[:, :, None]
    decay = jnp.exp(jnp.where(causal, seg, -jnp.inf))
    cb = jnp.einsum('bcign,bcjgn->bcijg', Cc, Bc)
    w = cb[..., None] * decay * dtc[:, :, None]
    y_diag = jnp.einsum('bcijgh,bcjghp->bcighp', w, xc)
    decay_out = jnp.exp(acs[:, :, -1:] - acs)
    chunk_state = jnp.einsum('bcjgn,bcjgh,bcjghp->bcghpn', Bc, decay_out * dtc, xc)
    chunk_decay = jnp.exp(acs[:, :, -1])

    def step(s, inp):
        dec, cs = inp
        return dec[..., None, None] * s + cs, s

    s_final, s_enter = lax.scan(step, h0.reshape(b, g, hg, p, n),
                                (jnp.moveaxis(chunk_decay, 1, 0), jnp.moveaxis(chunk_state, 1, 0)))
    s_enter = jnp.moveaxis(s_enter, 0, 1)
    y_off = jnp.einsum('bcign,bcghpn,bcigh->bcighp', Cc, s_enter, jnp.exp(acs))
    return (y_diag + y_off).reshape(b, l, h, p), s_final.reshape(b, h, p, n)


def gla_scan(q, k, v, logg, s0):
    b, l, h, dk = q.shape
    dv = v.shape[-1]
    cq = GLA_CHUNK
    nc = l // cq
    qc = q.reshape(b, nc, cq, h, dk)
    kc = k.reshape(b, nc, cq, h, dk)
    vc = v.reshape(b, nc, cq, h, dv)
    gcs = jnp.cumsum(logg.reshape(b, nc, cq, h, dk), axis=2)
    qe = qc * jnp.exp(gcs)
    ke = kc * jnp.exp(-gcs)
    att = jnp.einsum('bcihd,bcjhd->bchij', qe, ke)
    att = jnp.where(jnp.tril(jnp.ones((cq, cq), bool)), att, 0.0)
    o_intra = jnp.einsum('bchij,bcjhe->bcihe', att, vc)
    k_out = kc * jnp.exp(gcs[:, :, -1:] - gcs)
    chunk_state = jnp.einsum('bcjhd,bcjhe->bchde', k_out, vc)
    chunk_decay = jnp.exp(gcs[:, :, -1])

    def step(s, inp):
        dec, cs = inp
        return dec[..., None] * s + cs, s

    s_final, s_enter = lax.scan(step, s0, (jnp.moveaxis(chunk_decay, 1, 0), jnp.moveaxis(chunk_state, 1, 0)))
    s_enter = jnp.moveaxis(s_enter, 0, 1)
    o_inter = jnp.einsum('bcihd,bchde->bcihe', qe, s_enter)
    return (o_intra + o_inter).reshape(b, l, h, dv), s_final


def ssd_branch(xbc_raw, z, dt_raw, conv_w, conv_b, dt_bias, a_log, d_skip, norm_g, h0):
    b, l, _ = z.shape
    xbc = jax.nn.silu(dwconv_centred(xbc_raw, conv_w, conv_b))
    xs, bs, cs = jnp.split(xbc, [SSD_INNER, SSD_INNER + SSD_GROUPS * SSD_STATE], axis=-1)
    xs = xs.reshape(b, l, SSD_HEADS, SSD_HEADDIM).astype(F32)
    bs = bs.reshape(b, l, SSD_GROUPS, SSD_STATE).astype(F32)
    cs = cs.reshape(b, l, SSD_GROUPS, SSD_STATE).astype(F32)
    dt = jax.nn.softplus(dt_raw.reshape(b, l, N_DIR, SSD_HEADS).astype(F32) + dt_bias.astype(F32))
    A = -jnp.exp(a_log.astype(F32))
    y_f, s_f = ssd_scan(xs, dt[:, :, 0], A[0], bs, cs, h0[:, 0])
    y_b, s_b = ssd_scan(flip(xs), flip(dt[:, :, 1]), A[1], flip(bs), flip(cs), h0[:, 1])
    y = y_f + flip(y_b) + d_skip.astype(F32)[:, None] * xs
    y = y.reshape(b, l, SSD_INNER).astype(z.dtype) * jax.nn.silu(z)
    return rmsnorm(y, norm_g), jnp.stack([s_f, s_b], axis=1)


def gla_branch(q, k, v, r, g_lr, gate_w, gate_b, norm_g, s0):
    b, l, _ = q.shape
    qh = q.reshape(b, l, GLA_HEADS, GLA_DK).astype(F32) * (GLA_DK ** -0.5)
    kh = k.reshape(b, l, GLA_HEADS, GLA_DK).astype(F32)
    vh = v.reshape(b, l, GLA_HEADS, GLA_DV).astype(F32)
    lr = g_lr.reshape(b, l, N_DIR, GLA_GATE_RANK).astype(F32)
    logg = jax.nn.log_sigmoid(jnp.einsum('bldr,drk->bldk', lr, gate_w.astype(F32)) + gate_b.astype(F32)) / GLA_GATE_TAU
    logg = logg.reshape(b, l, N_DIR, GLA_HEADS, GLA_DK)
    o_f, s_f = gla_scan(qh, kh, vh, logg[:, :, 0], s0[:, 0])
    o_b, s_b = gla_scan(flip(qh), flip(kh), flip(vh), flip(logg[:, :, 1]), s0[:, 1])
    o = o_f + flip(o_b)
    o = o * lax.rsqrt(jnp.mean(o * o, axis=-1, keepdims=True) + EPS)
    o = o.reshape(b, l, GLA_VDIM) * norm_g.astype(F32)
    return o.astype(r.dtype) * jax.nn.silu(r), jnp.stack([s_f, s_b], axis=1)


def mixer(u, w_in, conv_w, conv_b, dt_bias, a_log, d_skip, ssd_norm_g, gate_w, gate_b, gla_norm_g,
          w_bs, w_bg, w_o, init_ssd, init_gla):
    proj = u @ w_in
    offs = [int(o) for o in np.cumsum(IN_SIZES)[:-1]]
    z, xbc, dt_raw, q, k, v, r, g_lr, ga, gb = jnp.split(proj, offs, axis=-1)
    y_ssd, s_ssd = ssd_branch(xbc, z, dt_raw, conv_w, conv_b, dt_bias, a_log, d_skip, ssd_norm_g, init_ssd)
    y_gla, s_gla = gla_branch(q, k, v, r, g_lr, gate_w, gate_b, gla_norm_g, init_gla)
    merged = jax.nn.sigmoid(ga) * (y_ssd @ w_bs) + jax.nn.sigmoid(gb) * (y_gla @ w_bg)
    return merged @ w_o, s_ssd, s_gla


def swiglu(x, w1, w3, w2):
    return (jax.nn.silu(x @ w1) * (x @ w3)) @ w2


def moe_ffn(x, router_w, router_b, w1, w3, w2):
    logits = (x @ router_w + router_b).astype(F32)
    top_v, top_i = lax.top_k(logits, TOP_K)
    gates = jax.nn.softmax(top_v, axis=-1)
    dense_gate = jnp.sum(jax.nn.one_hot(top_i, N_EXPERTS, dtype=F32) * gates[..., None], axis=-2).astype(x.dtype)
    out = jnp.zeros_like(x)
    for e in range(N_EXPERTS):
        out = out + dense_gate[..., e:e + 1] * swiglu(x, w1[e], w3[e], w2[e])
    return out


def setup_inputs(seed: int = 0) -> dict:
    key = jax.random.key(seed)
    ks = jax.random.split(key, 40)
    nrm = lambda k, shape, s: jax.random.normal(k, shape, F32) * s
    sd = D_MODEL ** -0.5
    u = jax.random.uniform(ks[12], (DEPTH, N_DIR, SSD_HEADS), F32)
    dt0 = jnp.exp(u * (math.log(0.1) - math.log(0.001)) + math.log(0.001))
    return {
        'x_prompt': nrm(ks[0], (BATCH, SEQ, D_MODEL), 1.0),
        'x_sample': nrm(ks[1], (DEC_BATCH, DEC_SEQ, D_MODEL), 1.0),
        'state_ssd': nrm(ks[2], (DEC_BATCH, DEPTH, N_DIR, SSD_HEADS, SSD_HEADDIM, SSD_STATE), 0.1),
        'state_gla': nrm(ks[3], (DEC_BATCH, DEPTH, N_DIR, GLA_HEADS, GLA_DK, GLA_DV), 1.0),
        'c': nrm(ks[4], (DEC_BATCH, D_MODEL), 1.0),
        'c_ctx': nrm(ks[5], (D_MODEL,), 1.0),
        'ada_w': nrm(ks[6], (DEPTH, D_MODEL, 6 * D_MODEL), 0.5 * sd),
        'ada_b': nrm(ks[7], (DEPTH, 6 * D_MODEL), 0.02),
        'norm1_g': 1.0 + nrm(ks[8], (DEPTH, D_MODEL), 0.02),
        'norm2_g': 1.0 + nrm(ks[9], (DEPTH, D_MODEL), 0.02),
        'w_in': nrm(ks[10], (DEPTH, D_MODEL, D_IN), sd),
        'ssd_conv_w': nrm(ks[11], (DEPTH, SSD_CONV, SSD_XBC), SSD_CONV ** -0.5),
        'ssd_conv_b': nrm(ks[13], (DEPTH, SSD_XBC), 0.02),
        'ssd_dt_bias': dt0 + jnp.log(-jnp.expm1(-dt0)),
        'ssd_a_log': jnp.log(jax.random.uniform(ks[14], (DEPTH, N_DIR, SSD_HEADS), F32, 1.0, 16.0)),
        'ssd_d': 1.0 + nrm(ks[15], (DEPTH, SSD_HEADS), 0.02),
        'ssd_norm_g': 1.0 + nrm(ks[16], (DEPTH, SSD_INNER), 0.02),
        'gla_gate_w': nrm(ks[17], (DEPTH, N_DIR, GLA_GATE_RANK, GLA_KDIM), GLA_GATE_RANK ** -0.5),
        'gla_gate_b': nrm(ks[18], (DEPTH, N_DIR, GLA_KDIM), 0.02),
        'gla_norm_g': 1.0 + nrm(ks[19], (DEPTH, GLA_VDIM), 0.02),
        'w_branch_ssd': nrm(ks[20], (DEPTH, SSD_INNER, D_MODEL), SSD_INNER ** -0.5),
        'w_branch_gla': nrm(ks[21], (DEPTH, GLA_VDIM, D_MODEL), GLA_VDIM ** -0.5),
        'w_out': nrm(ks[22], (DEPTH, D_MODEL, D_MODEL), sd),
        'ffn_w1': nrm(ks[23], (N_DENSE, D_MODEL, FFN_HIDDEN), sd),
        'ffn_w3': nrm(ks[24], (N_DENSE, D_MODEL, FFN_HIDDEN), sd),
        'ffn_w2': nrm(ks[25], (N_DENSE, FFN_HIDDEN, D_MODEL), FFN_HIDDEN ** -0.5),
        'router_w': nrm(ks[26], (N_MOE, D_MODEL, N_EXPERTS), sd),
        'router_b': nrm(ks[27], (N_MOE, N_EXPERTS), 0.01),
        'moe_w1': nrm(ks[28], (N_MOE, N_EXPERTS, D_MODEL, EXPERT_HIDDEN), sd),
        'moe_w3': nrm(ks[29], (N_MOE, N_EXPERTS, D_MODEL, EXPERT_HIDDEN), sd),
        'moe_w2': nrm(ks[30], (N_MOE, N_EXPERTS, EXPERT_HIDDEN, D_MODEL), EXPERT_HIDDEN ** -0.5),
        'final_norm_g': 1.0 + nrm(ks[31], (D_MODEL,), 0.02),
    }


def reference(x_prompt, x_sample, state_ssd, state_gla, c, c_ctx, ada_w, ada_b, norm1_g, norm2_g, w_in,
              ssd_conv_w, ssd_conv_b, ssd_dt_bias, ssd_a_log, ssd_d, ssd_norm_g, gla_gate_w, gla_gate_b,
              gla_norm_g, w_branch_ssd, w_branch_gla, w_out, ffn_w1, ffn_w3, ffn_w2, router_w, router_b,
              moe_w1, moe_w3, moe_w2, final_norm_g):

    def layer(x, mod, i, init_ssd, init_gla):
        shift1, scale1, gate1, shift2, scale2, gate2 = jnp.split(mod, 6, axis=-1)
        u = rmsnorm(x, norm1_g[i]) * (1 + scale1) + shift1
        mix, s_ssd, s_gla = mixer(u, w_in[i], ssd_conv_w[i], ssd_conv_b[i], ssd_dt_bias[i], ssd_a_log[i], ssd_d[i],
                                  ssd_norm_g[i], gla_gate_w[i], gla_gate_b[i], gla_norm_g[i], w_branch_ssd[i],
                                  w_branch_gla[i], w_out[i], init_ssd, init_gla)
        x = x + gate1 * mix
        v = rmsnorm(x, norm2_g[i]) * (1 + scale2) + shift2
        j = i // 2
        if i % 2 == 0:
            f = swiglu(v, ffn_w1[j], ffn_w3[j], ffn_w2[j])
        else:
            f = moe_ffn(v, router_w[j], router_b[j], moe_w1[j], moe_w3[j], moe_w2[j])
        return x + gate2 * f, s_ssd, s_gla

    bp = x_prompt.shape[0]
    zeros_ssd = jnp.zeros((bp, N_DIR, SSD_HEADS, SSD_HEADDIM, SSD_STATE), F32)
    zeros_gla = jnp.zeros((bp, N_DIR, GLA_HEADS, GLA_DK, GLA_DV), F32)
    xp = x_prompt
    ssd_states = []
    gla_states = []
    for i in range(DEPTH):
        mod_ctx = (jax.nn.silu(c_ctx) @ ada_w[i] + ada_b[i])[None, None, :]
        xp, s1, s2 = layer(xp, mod_ctx, i, zeros_ssd, zeros_gla)
        ssd_states.append(s1)
        gla_states.append(s2)
    y_prompt = rmsnorm(xp, final_norm_g)
    new_state_ssd = jnp.stack(ssd_states, axis=1).astype(x_prompt.dtype)
    new_state_gla = jnp.stack(gla_states, axis=1).astype(x_prompt.dtype)

    xs = x_sample + grid_pos_embed(x_sample.shape[1], D_MODEL).astype(x_sample.dtype)
    for i in range(DEPTH):
        mod = (jax.nn.silu(c) @ ada_w[i] + ada_b[i])[:, None, :]
        xs, _, _ = layer(xs, mod, i, state_ssd[:, i].astype(F32), state_gla[:, i].astype(F32))
    y_sample = rmsnorm(xs, final_norm_g)
    return (y_prompt, y_sample, new_state_ssd, new_state_gla)
```

```python
import functools
import math

import numpy as np
import jax
import jax.numpy as jnp
from jax import lax
from jax.experimental import pallas as pl
from jax.experimental.pallas import tpu as pltpu

F32 = jnp.float32
BF16 = jnp.bfloat16

D_MODEL = 1024
DEPTH = 4
GRID_W = 64
EPS = 1e-6
SSD_HEADS = 16
SSD_HEADDIM = 64
SSD_INNER = SSD_HEADS * SSD_HEADDIM
SSD_GROUPS = 2
SSD_STATE = 64
SSD_CONV = 5
SSD_CHUNK = 128
GLA_HEADS = 8
GLA_DK = 64
GLA_DV = 128
GLA_KDIM = GLA_HEADS * GLA_DK
GLA_VDIM = GLA_HEADS * GLA_DV
GLA_GATE_RANK = 16
GLA_GATE_TAU = 16.0
GLA_CHUNK = 64
FFN_HIDDEN = 2816
N_EXPERTS = 8

LANES = 128
SUBLANES = 8
VMEM_LIMIT_BYTES = 52 * 1024 * 1024

COL_Z, COL_X, COL_V, COL_R, COL_GA, COL_GB = 0, 1024, 2048, 3072, 4096, 5120
COL_QK = 6144
COL_BC = 7168
COL_SMALL = 7424
D_PROJ = 7680
SM_DT = 0
SM_LR = 2 * SSD_HEADS

NEG_BIG = -1e30

TM_PROJ, TN_PROJ = 1024, 1536
TM_MERGE = 512
TM_FFN, TH_FFN = 512, 1408
TM_NORM = 1024
TN_MODS = 1536
MOD_ROWS = 16


def _params(*sem):
    return pltpu.CompilerParams(dimension_semantics=sem, vmem_limit_bytes=VMEM_LIMIT_BYTES)


def _silu(x):
    return x * jax.nn.sigmoid(x)


def _softplus(x):
    return jnp.maximum(x, 0.0) + jnp.log1p(jnp.exp(-jnp.abs(x)))


def _split3(a):
    a1 = a.astype(BF16)
    r1 = a - a1.astype(F32)
    a2 = r1.astype(BF16)
    a3 = (r1 - a2.astype(F32)).astype(BF16)
    return a1, a2, a3


def _dot(a, b):
    return jnp.dot(a, b, preferred_element_type=F32)


def _dot_nt(a, b):
    return lax.dot_general(a, b, (((1,), (1,)), ((), ())), preferred_element_type=F32)


def _dot_tn(a, b):
    return lax.dot_general(a, b, (((0,), (0,)), ((), ())), preferred_element_type=F32)


def _dot_exact_lhs(t01, a):
    a1, a2, a3 = _split3(a)
    return _dot(t01, a1) + _dot(t01, a2) + _dot(t01, a3)


def _dot_f32(a, b):
    a1, a2, a3 = _split3(a)
    b1, b2, b3 = _split3(b)
    return (_dot(a1, b1) + _dot(a1, b2) + _dot(a2, b1)
            + _dot(a1, b3) + _dot(a2, b2) + _dot(a3, b1))


def _tri_mask(n, upper):
    i = lax.broadcasted_iota(jnp.int32, (n, n), 0)
    j = lax.broadcasted_iota(jnp.int32, (n, n), 1)
    return (j >= i) if upper else (j <= i)


def _rms(x, g):
    return x * lax.rsqrt(jnp.mean(x * x, axis=-1, keepdims=True) + EPS) * g


def _chunk_pos(c, q, regions):
    (n0, l0), (n1, l1) = regions
    per0, per1 = l0 // q, l1 // q
    nc0 = n0 * per0
    c1 = jnp.maximum(c - nc0, 0)
    in0 = c < nc0
    seq = jnp.where(in0, c // per0, n0 + c1 // per1)
    pos = jnp.where(in0, c % per0, c1 % per1)
    last = jnp.where(in0, per0 - 1, per1 - 1)
    return seq, pos == 0, pos == last


def _mod_row(i, tm, regions):
    (n0, l0), (n1, l1) = regions
    t1 = jnp.maximum(i * tm - n0 * l0, 0)
    return jnp.where(i * tm < n0 * l0, 0, 1 + t1 // l1)


def _n_tokens(regions):
    return sum(n * l for n, l in regions)


def _assemble_kernel(xp_ref, xs_ref, pe_ref, o_ref, *, n_ctx_tiles):
    i = pl.program_id(0)

    @pl.when(i < n_ctx_tiles)
    def _():
        o_ref[...] = xp_ref[...]

    @pl.when(i >= n_ctx_tiles)
    def _():
        o_ref[...] = xs_ref[...] + pe_ref[...]


def _assemble_tokens(xp, xs, pe, regions):
    (n0, l0), (n1, l1) = regions
    tm = min(1024, l0 * n0, l1)
    t0, t1 = n0 * l0, n1 * l1
    n_ctx = t0 // tm
    pe_tiles = l1 // tm
    return pl.pallas_call(
        functools.partial(_assemble_kernel, n_ctx_tiles=n_ctx),
        out_shape=jax.ShapeDtypeStruct((t0 + t1, D_MODEL), F32),
        grid=((t0 + t1) // tm,),
        in_specs=[
            pl.BlockSpec((tm, D_MODEL), lambda i: (jnp.minimum(i, n_ctx - 1), 0)),
            pl.BlockSpec((tm, D_MODEL), lambda i: (jnp.maximum(i - n_ctx, 0), 0)),
            pl.BlockSpec((tm, D_MODEL), lambda i: (jnp.maximum(i - n_ctx, 0) % pe_tiles, 0)),
        ],
        out_specs=pl.BlockSpec((tm, D_MODEL), lambda i: (i, 0)),
        compiler_params=_params("arbitrary"),
        name="assemble_tokens",
    )(xp.reshape(t0, D_MODEL), xs.reshape(t1, D_MODEL), pe)


def _mods_kernel(cc_ref, w_ref, b_ref, o_ref):
    a = _silu(cc_ref[...])
    o_ref[...] = _dot_f32(a, w_ref[...]) + b_ref[...]


def _modulation_table(cc, ada_w, ada_b):
    depth = ada_w.shape[0]
    n = ada_w.shape[2]
    return pl.pallas_call(
        _mods_kernel,
        out_shape=jax.ShapeDtypeStruct((depth, MOD_ROWS, n), F32),
        grid=(depth, n // TN_MODS),
        in_specs=[
            pl.BlockSpec((MOD_ROWS, D_MODEL), lambda l, j: (0, 0)),
            pl.BlockSpec((None, D_MODEL, TN_MODS), lambda l, j: (l, 0, j)),
            pl.BlockSpec((None, 1, TN_MODS), lambda l, j: (l, 0, j)),
        ],
        out_specs=pl.BlockSpec((None, MOD_ROWS, TN_MODS), lambda l, j: (l, 0, j)),
        compiler_params=_params("arbitrary", "arbitrary"),
        name="modulation_table",
    )(cc, ada_w, ada_b.reshape(depth, 1, n))


def _inproj_kernel(x_ref, mod_ref, g_ref, w_ref, o_ref, u_ref):
    @pl.when(pl.program_id(1) == 0)
    def _():
        y = _rms(x_ref[...], g_ref[...])
        u = y * (1.0 + mod_ref[:, D_MODEL:2 * D_MODEL]) + mod_ref[:, 0:D_MODEL]
        u_ref[...] = u.astype(BF16)

    o_ref[...] = _dot(u_ref[...], w_ref[...])


def _in_projection(x, mods, g, w, regions):
    t = x.shape[0]
    tm = min(TM_PROJ, regions[0][0] * regions[0][1], regions[1][1])
    return pl.pallas_call(
        _inproj_kernel,
        out_shape=jax.ShapeDtypeStruct((t, D_PROJ), F32),
        grid=(t // tm, D_PROJ // TN_PROJ),
        in_specs=[
            pl.BlockSpec((tm, D_MODEL), lambda i, j: (i, 0)),
            pl.BlockSpec((None, 1, 6 * D_MODEL), lambda i, j: (_mod_row(i, tm, regions), 0, 0)),
            pl.BlockSpec((1, D_MODEL), lambda i, j: (0, 0)),
            pl.BlockSpec((D_MODEL, TN_PROJ), lambda i, j: (0, j)),
        ],
        out_specs=pl.BlockSpec((tm, TN_PROJ), lambda i, j: (i, j)),
        scratch_shapes=[pltpu.VMEM((tm, D_MODEL), BF16)],
        compiler_params=_params("parallel", "arbitrary"),
        name="norm_in_projection",
    )(x, mods, g, w)


HALO = SUBLANES
CONV_PAD = SSD_CONV // 2


def _conv_silu(cur_ref, prev_ref, next_ref, ext_ref, w_ref, b_ref, first, last, q):
    ext_ref[0:HALO, :] = jnp.where(first, 0.0, prev_ref[...])
    ext_ref[HALO:HALO + q, :] = cur_ref[...]
    ext_ref[HALO + q:2 * HALO + q, :] = jnp.where(last, 0.0, next_ref[...])
    acc = b_ref[...]
    for k in range(SSD_CONV):
        off = HALO - CONV_PAD + k
        acc = acc + w_ref[k:k + 1, :] * ext_ref[off:off + q, :]
    return _silu(acc)


def _ssd_decay_terms(sm_ref, dtb_ref, alog_ref, q):
    lane = lax.broadcasted_iota(jnp.int32, (1, LANES), 1)
    dtv = _softplus(sm_ref[...] + dtb_ref[...])
    a_neg = jnp.where(lane < 2 * SSD_HEADS, -jnp.exp(alog_ref[...]), 0.0)
    a = dtv * a_neg
    acs = _dot_exact_lhs(_tri_mask(q, False).astype(BF16), a)
    rev = _dot_exact_lhs(_tri_mask(q, True).astype(BF16), a)
    m = jnp.where(lane < SSD_HEADS, acs, rev)
    tot = acs[q - 1:q, :]
    return m, tot, dtv


def _col(x, idx, width=LANES):
    return jnp.broadcast_to(x[:, idx:idx + 1], (x.shape[0], width))


def _pair_cols(x, base, p, lo_half):
    return jnp.where(lo_half, _col(x, base + 2 * p), _col(x, base + 2 * p + 1))


def _expand_state(s):
    lane = lax.broadcasted_iota(jnp.int32, s.shape, 1)
    half = SSD_INNER // SSD_GROUPS
    return jnp.concatenate([jnp.where(lane < half, s, 0.0), jnp.where(lane >= half, s, 0.0)], axis=0)


def _compact_state(s2):
    return s2[0:SSD_STATE, :] + s2[SSD_STATE:2 * SSD_STATE, :]


def _state_update_mask():
    row = lax.broadcasted_iota(jnp.int32, (SSD_GROUPS * SSD_STATE, SSD_INNER), 0)
    lane = lax.broadcasted_iota(jnp.int32, (SSD_GROUPS * SSD_STATE, SSD_INNER), 1)
    return (row < SSD_STATE) == (lane < SSD_INNER // SSD_GROUPS)


def _ssd_bwd_kernel(x_ref, xp_ref, xn_ref, bc_ref, bcp_ref, bcn_ref, sm_ref,
                    cwx_ref, cwbc_ref, cbx_ref, cbbc_ref, dtb_ref, alog_ref, init_ref,
                    enter_ref, fin_ref, s_ref, xe_ref, bce_ref, *, regions, n_chunks):
    q = SSD_CHUNK
    c = n_chunks - 1 - pl.program_id(0)
    _, first, last = _chunk_pos(c, q, regions)

    @pl.when(last)
    def _():
        s_ref[...] = _expand_state(init_ref[...])

    xs = _conv_silu(x_ref, xp_ref, xn_ref, xe_ref, cwx_ref, cbx_ref, first, last, q)
    bcs = _conv_silu(bc_ref, bcp_ref, bcn_ref, bce_ref, cwbc_ref, cbbc_ref, first, last, q)
    bsb = bcs[:, 0:LANES].astype(BF16)
    m, tot, dtv = _ssd_decay_terms(sm_ref, dtb_ref, alog_ref, q)
    wgt = jnp.exp(tot - m) * dtv
    etot = jnp.exp(tot)
    lo_half = lax.broadcasted_iota(jnp.int32, (1, LANES), 1) < SSD_HEADDIM

    s_old = s_ref[...]
    enter_ref[...] = _compact_state(s_old)
    xw, dec = [], []
    for p in range(SSD_HEADS // 2):
        sl = slice(p * LANES, (p + 1) * LANES)
        xw.append((xs[:, sl] * _pair_cols(wgt, SSD_HEADS, p, lo_half)).astype(BF16))
        dec.append(_pair_cols(etot, SSD_HEADS, p, lo_half))
    upd = _dot_tn(bsb, jnp.concatenate(xw, axis=1))
    s_new = s_old * jnp.concatenate(dec, axis=1) + jnp.where(_state_update_mask(), upd, 0.0)
    s_ref[...] = s_new

    @pl.when(first)
    def _():
        fin_ref[...] = _compact_state(s_new)


def _ssd_out_kernel(z_ref, x_ref, xp_ref, xn_ref, bc_ref, bcp_ref, bcn_ref, sm_ref,
                    cwx_ref, cwbc_ref, cbx_ref, cbbc_ref, dtb_ref, alog_ref, dexp_ref, ng_ref,
                    init_ref, enter_ref, y_ref, fin_ref, s_ref, xe_ref, bce_ref, *, regions):
    q = SSD_CHUNK
    c = pl.program_id(0)
    _, first, last = _chunk_pos(c, q, regions)

    @pl.when(first)
    def _():
        s_ref[...] = _expand_state(init_ref[...])

    xs = _conv_silu(x_ref, xp_ref, xn_ref, xe_ref, cwx_ref, cbx_ref, first, last, q)
    bcs = _conv_silu(bc_ref, bcp_ref, bcn_ref, bce_ref, cwbc_ref, cbbc_ref, first, last, q)
    bsb = bcs[:, 0:LANES].astype(BF16)
    csb = bcs[:, LANES:2 * LANES].astype(BF16)
    m, tot, dtv = _ssd_decay_terms(sm_ref, dtb_ref, alog_ref, q)
    em = jnp.exp(m)
    wgt = jnp.exp(tot - m) * dtv
    etot = jnp.exp(tot)
    mt = m.T
    dtt = dtv.T

    lane = lax.broadcasted_iota(jnp.int32, (1, LANES), 1)
    lo_half = lane < SSD_HEADDIM
    tril = _tri_mask(q, False)
    triu = _tri_mask(q, True)
    zero_b = jnp.zeros_like(csb)
    cb = [_dot_nt(jnp.where(lo_half, csb, zero_b), bsb), _dot_nt(jnp.where(lo_half, zero_b, csb), bsb)]

    s_old = s_ref[...]
    cs_f = _dot(csb, s_old.astype(BF16))
    cs_b = _dot(csb, _expand_state(enter_ref[...]).astype(BF16))
    dexp = dexp_ref[...]

    ys, xw, dec = [], [], []
    for p in range(SSD_HEADS // 2):
        sl = slice(p * LANES, (p + 1) * LANES)
        g = (2 * p) // (SSD_HEADS // SSD_GROUPS)
        xs_p = xs[:, sl]
        xs_pb = xs_p.astype(BF16)
        yd = []
        for h in (2 * p, 2 * p + 1):
            hb = SSD_HEADS + h
            seg_f = _col(m, h, q) - mt[h:h + 1, :]
            seg_b = _col(m, hb, q) - mt[hb:hb + 1, :]
            dl = (jnp.exp(jnp.where(tril, seg_f, NEG_BIG)) * dtt[h:h + 1, :]
                  + jnp.exp(jnp.where(triu, seg_b, NEG_BIG)) * dtt[hb:hb + 1, :])
            yd.append(_dot((cb[g] * dl).astype(BF16), xs_pb))
        y_p = jnp.where(lo_half, yd[0], yd[1])
        y_p = y_p + _pair_cols(em, 0, p, lo_half) * cs_f[:, sl]
        y_p = y_p + _pair_cols(em, SSD_HEADS, p, lo_half) * cs_b[:, sl]
        y_p = y_p + dexp[:, sl] * xs_p
        ys.append(y_p)
        xw.append((xs_p * _pair_cols(wgt, 0, p, lo_half)).astype(BF16))
        dec.append(_pair_cols(etot, 0, p, lo_half))

    y = jnp.concatenate(ys, axis=1) * _silu(z_ref[...])
    y_ref[...] = _rms(y, ng_ref[...]).astype(y_ref.dtype)

    upd = _dot_tn(bsb, jnp.concatenate(xw, axis=1))
    s_new = s_old * jnp.concatenate(dec, axis=1) + jnp.where(_state_update_mask(), upd, 0.0)
    s_ref[...] = s_new

    @pl.when(last)
    def _():
        fin_ref[...] = _compact_state(s_new)


def _ssd_mixer(proj, init, lw, regions):
    q = SSD_CHUNK
    t = proj.shape[0]
    n_chunks = t // q
    n_seq = regions[0][0] + regions[1][0]
    rb = q // HALO
    n_rb = t // HALO
    xcol, bccol, smcol = COL_X // SSD_INNER, COL_BC // (2 * LANES), COL_SMALL // LANES

    def chunk_specs(cidx):
        prev = lambda s: jnp.maximum(cidx(s) * rb - 1, 0)
        nxt = lambda s: jnp.minimum((cidx(s) + 1) * rb, n_rb - 1)
        return [
            pl.BlockSpec((q, SSD_INNER), lambda s: (cidx(s), xcol)),
            pl.BlockSpec((HALO, SSD_INNER), lambda s: (prev(s), xcol)),
            pl.BlockSpec((HALO, SSD_INNER), lambda s: (nxt(s), xcol)),
            pl.BlockSpec((q, 2 * LANES), lambda s: (cidx(s), bccol)),
            pl.BlockSpec((HALO, 2 * LANES), lambda s: (prev(s), bccol)),
            pl.BlockSpec((HALO, 2 * LANES), lambda s: (nxt(s), bccol)),
            pl.BlockSpec((q, LANES), lambda s: (cidx(s), smcol)),
        ]

    def const_spec(a):
        return pl.BlockSpec(a.shape, lambda s: (0,) * a.ndim)

    consts = [lw["conv_w_x"], lw["conv_w_bc"], lw["conv_b_x"], lw["conv_b_bc"], lw["dt_bias"], lw["a_log"]]
    scratch = [pltpu.VMEM((SSD_GROUPS * SSD_STATE, SSD_INNER), F32),
               pltpu.VMEM((q + 2 * HALO, SSD_INNER), F32),
               pltpu.VMEM((q + 2 * HALO, 2 * LANES), F32)]
    state_block = (None, None, SSD_STATE, SSD_INNER)

    bidx = lambda s: n_chunks - 1 - s
    seq_b = lambda s: _chunk_pos(bidx(s), q, regions)[0]
    enter_b, fin_b = pl.pallas_call(
        functools.partial(_ssd_bwd_kernel, regions=regions, n_chunks=n_chunks),
        out_shape=(jax.ShapeDtypeStruct((n_chunks, SSD_STATE, SSD_INNER), F32),
                   jax.ShapeDtypeStruct((n_seq, SSD_STATE, SSD_INNER), F32)),
        grid=(n_chunks,),
        in_specs=chunk_specs(bidx) + [const_spec(a) for a in consts]
        + [pl.BlockSpec(state_block, lambda s: (seq_b(s), 1, 0, 0))],
        out_specs=(pl.BlockSpec((None, SSD_STATE, SSD_INNER), lambda s: (bidx(s), 0, 0)),
                   pl.BlockSpec((None, SSD_STATE, SSD_INNER), lambda s: (seq_b(s), 0, 0))),
        scratch_shapes=scratch,
        compiler_params=_params("arbitrary"),
        name="ssd_backward_states",
    )(*([proj] * 7), *consts, init)

    fidx = lambda s: s
    seq_f = lambda s: _chunk_pos(s, q, regions)[0]
    consts2 = consts + [lw["d_exp"], lw["ssd_norm_g"]]
    y, fin_f = pl.pallas_call(
        functools.partial(_ssd_out_kernel, regions=regions),
        out_shape=(jax.ShapeDtypeStruct((t, SSD_INNER), BF16),
                   jax.ShapeDtypeStruct((n_seq, SSD_STATE, SSD_INNER), F32)),
        grid=(n_chunks,),
        in_specs=[pl.BlockSpec((q, SSD_INNER), lambda s: (s, COL_Z // SSD_INNER))]
        + chunk_specs(fidx) + [const_spec(a) for a in consts2]
        + [pl.BlockSpec(state_block, lambda s: (seq_f(s), 0, 0, 0)),
           pl.BlockSpec((None, SSD_STATE, SSD_INNER), lambda s: (s, 0, 0))],
        out_specs=(pl.BlockSpec((q, SSD_INNER), lambda s: (s, 0)),
                   pl.BlockSpec((None, SSD_STATE, SSD_INNER), lambda s: (seq_f(s), 0, 0))),
        scratch_shapes=scratch,
        compiler_params=_params("arbitrary"),
        name="ssd_forward_outputs",
    )(*([proj] * 8), *consts2, init, enter_b)
    return y, fin_f, fin_b


def _gla_log_decay(sm_ref, gw_ref, gb_ref, d):
    logit = _dot_f32(sm_ref[...], gw_ref[d]) + gb_ref[d]
    return -_softplus(-logit) * (1.0 / GLA_GATE_TAU)


def _gla_bwd_kernel(qk_ref, v_ref, sm_ref, gw_ref, gb_ref, init_ref,
                    enter_ref, fin_ref, s_ref, *, regions, n_chunks):
    cq = GLA_CHUNK
    c = n_chunks - 1 - pl.program_id(0)
    _, first, last = _chunk_pos(c, cq, regions)

    @pl.when(last)
    def _():
        s_ref[...] = init_ref[...]

    lg = _gla_log_decay(sm_ref, gw_ref, gb_ref, 1)
    rev = _dot_exact_lhs(_tri_mask(cq, True).astype(BF16), lg)
    tot = rev[0:1, :]
    ko = (qk_ref[:, GLA_KDIM:2 * GLA_KDIM] * jnp.exp(tot - rev)).astype(BF16)
    etot = jnp.exp(tot)
    lo_half = lax.broadcasted_iota(jnp.int32, (1, LANES), 1) < GLA_DK

    s_old = s_ref[...]
    enter_ref[...] = s_old
    new = []
    for p in range(GLA_HEADS // 2):
        sl = slice(p * LANES, (p + 1) * LANES)
        u0 = _dot_tn(v_ref[:, (2 * p) * GLA_DV:(2 * p + 1) * GLA_DV].astype(BF16), ko[:, sl])
        u1 = _dot_tn(v_ref[:, (2 * p + 1) * GLA_DV:(2 * p + 2) * GLA_DV].astype(BF16), ko[:, sl])
        new.append(s_old[:, sl] * etot[:, sl] + jnp.where(lo_half, u0, u1))
    s_new = jnp.concatenate(new, axis=1)
    s_ref[...] = s_new

    @pl.when(first)
    def _():
        fin_ref[...] = s_new


def _gla_out_kernel(qk_ref, v_ref, r_ref, sm_ref, gw_ref, gb_ref, ng_ref, init_ref, enter_ref,
                    y_ref, fin_ref, s_ref, *, regions):
    cq = GLA_CHUNK
    c = pl.program_id(0)
    _, first, last = _chunk_pos(c, cq, regions)

    @pl.when(first)
    def _():
        s_ref[...] = init_ref[...]

    tril = _tri_mask(cq, False)
    triu = _tri_mask(cq, True)
    lg_f = _gla_log_decay(sm_ref, gw_ref, gb_ref, 0)
    lg_b = _gla_log_decay(sm_ref, gw_ref, gb_ref, 1)
    gcs = _dot_exact_lhs(tril.astype(BF16), lg_f)
    rev = _dot_exact_lhs(triu.astype(BF16), lg_b)
    tot_f = gcs[cq - 1:cq, :]
    qs = qk_ref[:, 0:GLA_KDIM] * (GLA_DK ** -0.5)
    ks = qk_ref[:, GLA_KDIM:2 * GLA_KDIM]
    qe_f = (qs * jnp.exp(gcs)).astype(BF16)
    ke_f = (ks * jnp.exp(-gcs)).astype(BF16)
    ko_f = (ks * jnp.exp(tot_f - gcs)).astype(BF16)
    qe_b = (qs * jnp.exp(rev)).astype(BF16)
    ke_b = (ks * jnp.exp(-rev)).astype(BF16)
    etot = jnp.exp(tot_f)
    lo_half = lax.broadcasted_iota(jnp.int32, (1, LANES), 1) < GLA_DK
    zero_b = jnp.zeros((cq, LANES), BF16)

    s_old = s_ref[...]
    s_oldb = s_old.astype(BF16)
    s_entb = enter_ref[...].astype(BF16)
    ng = ng_ref[...]
    new = []
    for p in range(GLA_HEADS // 2):
        sl = slice(p * LANES, (p + 1) * LANES)
        upd = []
        for hh in (0, 1):
            h = 2 * p + hh
            hs = slice(h * GLA_DV, (h + 1) * GLA_DV)
            qf = jnp.where(lo_half, qe_f[:, sl], zero_b) if hh == 0 else jnp.where(lo_half, zero_b, qe_f[:, sl])
            qb = jnp.where(lo_half, qe_b[:, sl], zero_b) if hh == 0 else jnp.where(lo_half, zero_b, qe_b[:, sl])
            att = (jnp.where(tril, _dot_nt(qf, ke_f[:, sl]), 0.0)
                   + jnp.where(triu, _dot_nt(qb, ke_b[:, sl]), 0.0))
            vh = v_ref[:, hs].astype(BF16)
            o = _dot(att.astype(BF16), vh) + _dot_nt(qf, s_oldb[:, sl]) + _dot_nt(qb, s_entb[:, sl])
            o = o * lax.rsqrt(jnp.mean(o * o, axis=-1, keepdims=True) + EPS) * ng[:, hs]
            y_ref[:, hs] = (o * _silu(r_ref[:, hs])).astype(y_ref.dtype)
            upd.append(_dot_tn(vh, ko_f[:, sl]))
        new.append(s_old[:, sl] * etot[:, sl] + jnp.where(lo_half, upd[0], upd[1]))
    s_new = jnp.concatenate(new, axis=1)
    s_ref[...] = s_new

    @pl.when(last)
    def _():
        fin_ref[...] = s_new


def _gla_mixer(proj, init, lw, regions):
    cq = GLA_CHUNK
    t = proj.shape[0]
    n_chunks = t // cq
    n_seq = regions[0][0] + regions[1][0]
    qkcol, vcol, rcol, smcol = COL_QK // 1024, COL_V // 1024, COL_R // 1024, COL_SMALL // LANES
    state_block = (None, None, GLA_DV, GLA_KDIM)
    gw, gb = lw["gate_w"], lw["gate_b"]

    def const_spec(a):
        return pl.BlockSpec(a.shape, lambda s: (0,) * a.ndim)

    bidx = lambda s: n_chunks - 1 - s
    seq_b = lambda s: _chunk_pos(bidx(s), cq, regions)[0]
    enter_b, fin_b = pl.pallas_call(
        functools.partial(_gla_bwd_kernel, regions=regions, n_chunks=n_chunks),
        out_shape=(jax.ShapeDtypeStruct((n_chunks, GLA_DV, GLA_KDIM), F32),
                   jax.ShapeDtypeStruct((n_seq, GLA_DV, GLA_KDIM), F32)),
        grid=(n_chunks,),
        in_specs=[pl.BlockSpec((cq, 2 * GLA_KDIM), lambda s: (bidx(s), qkcol)),
                  pl.BlockSpec((cq, GLA_VDIM), lambda s: (bidx(s), vcol)),
                  pl.BlockSpec((cq, LANES), lambda s: (bidx(s), smcol)),
                  const_spec(gw), const_spec(gb),
                  pl.BlockSpec(state_block, lambda s: (seq_b(s), 1, 0, 0))],
        out_specs=(pl.BlockSpec((None, GLA_DV, GLA_KDIM), lambda s: (bidx(s), 0, 0)),
                   pl.BlockSpec((None, GLA_DV, GLA_KDIM), lambda s: (seq_b(s), 0, 0))),
        scratch_shapes=[pltpu.VMEM((GLA_DV, GLA_KDIM), F32)],
        compiler_params=_params("arbitrary"),
        name="gla_backward_states",
    )(proj, proj, proj, gw, gb, init)

    seq_f = lambda s: _chunk_pos(s, cq, regions)[0]
    ng = lw["gla_norm_g"]
    y, fin_f = pl.pallas_call(
        functools.partial(_gla_out_kernel, regions=regions),
        out_shape=(jax.ShapeDtypeStruct((t, GLA_VDIM), BF16),
                   jax.ShapeDtypeStruct((n_seq, GLA_DV, GLA_KDIM), F32)),
        grid=(n_chunks,),
        in_specs=[pl.BlockSpec((cq, 2 * GLA_KDIM), lambda s: (s, qkcol)),
                  pl.BlockSpec((cq, GLA_VDIM), lambda s: (s, vcol)),
                  pl.BlockSpec((cq, GLA_VDIM), lambda s: (s, rcol)),
                  pl.BlockSpec((cq, LANES), lambda s: (s, smcol)),
                  const_spec(gw), const_spec(gb), const_spec(ng),
                  pl.BlockSpec(state_block, lambda s: (seq_f(s), 0, 0, 0)),
                  pl.BlockSpec((None, GLA_DV, GLA_KDIM), lambda s: (s, 0, 0))],
        out_specs=(pl.BlockSpec((cq, GLA_VDIM), lambda s: (s, 0)),
                   pl.BlockSpec((None, GLA_DV, GLA_KDIM), lambda s: (seq_f(s), 0, 0))),
        scratch_shapes=[pltpu.VMEM((GLA_DV, GLA_KDIM), F32)],
        compiler_params=_params("arbitrary"),
        name="gla_forward_outputs",
    )(proj, proj, proj, proj, gw, gb, ng, init, enter_b)
    return y, fin_f, fin_b


def _merge_kernel(x_ref, ys_ref, yg_ref, ga_ref, gb_ref, mod_ref, wbs_ref, wbg_ref, wo_ref, o_ref):
    merged = (jax.nn.sigmoid(ga_ref[...]) * _dot(ys_ref[...], wbs_ref[...])
              + jax.nn.sigmoid(gb_ref[...]) * _dot(yg_ref[...], wbg_ref[...]))
    mix = _dot(merged.astype(BF16), wo_ref[...])
    o_ref[...] = x_ref[...] + mod_ref[:, 2 * D_MODEL:3 * D_MODEL] * mix


def _merge(x, y_ssd, y_gla, proj, mods, lw, regions):
    t = x.shape[0]
    tm = min(TM_MERGE, regions[0][0] * regions[0][1], regions[1][1])
    tok = lambda col: pl.BlockSpec((tm, D_MODEL), lambda i: (i, col))
    wspec = pl.BlockSpec((D_MODEL, D_MODEL), lambda i: (0, 0))
    return pl.pallas_call(
        _merge_kernel,
        out_shape=jax.ShapeDtypeStruct((t, D_MODEL), F32),
        grid=(t // tm,),
        in_specs=[tok(0), tok(0), tok(0), tok(COL_GA // D_MODEL), tok(COL_GB // D_MODEL),
                  pl.BlockSpec((None, 1, 6 * D_MODEL), lambda i: (_mod_row(i, tm, regions), 0, 0)),
                  wspec, wspec, wspec],
        out_specs=tok(0),
        compiler_params=_params("parallel"),
        name="merge_out_projection",
    )(x, y_ssd, y_gla, proj, proj, mods, lw["w_bs"], lw["w_bg"], lw["w_o"])


def _ffn_prologue(x_ref, mod_ref, g_ref):
    y = _rms(x_ref[...], g_ref[...])
    return y * (1.0 + mod_ref[:, 4 * D_MODEL:5 * D_MODEL]) + mod_ref[:, 3 * D_MODEL:4 * D_MODEL]


def _swiglu_partial(v, w1_ref, w3_ref, w2_ref):
    h = _silu(_dot(v, w1_ref[...])) * _dot(v, w3_ref[...])
    return _dot(h.astype(BF16), w2_ref[...])


def _ffn_kernel(x_ref, mod_ref, g_ref, w1_ref, w3_ref, w2_ref, o_ref, v_ref, acc_ref):
    k = pl.program_id(1)

    @pl.when(k == 0)
    def _():
        v_ref[...] = _ffn_prologue(x_ref, mod_ref, g_ref).astype(BF16)
        acc_ref[...] = jnp.zeros_like(acc_ref)

    acc_ref[...] += _swiglu_partial(v_ref[...], w1_ref, w3_ref, w2_ref)

    @pl.when(k == pl.num_programs(1) - 1)
    def _():
        o_ref[...] = x_ref[...] + mod_ref[:, 5 * D_MODEL:6 * D_MODEL] * acc_ref[...]


def _dense_ffn(x, mods, g, w1, w3, w2, regions):
    t = x.shape[0]
    tm = min(TM_FFN, regions[0][0] * regions[0][1], regions[1][1])
    hid = w1.shape[1]
    return pl.pallas_call(
        _ffn_kernel,
        out_shape=jax.ShapeDtypeStruct((t, D_MODEL), F32),
        grid=(t // tm, hid // TH_FFN),
        in_specs=[pl.BlockSpec((tm, D_MODEL), lambda i, k: (i, 0)),
                  pl.BlockSpec((None, 1, 6 * D_MODEL), lambda i, k: (_mod_row(i, tm, regions), 0, 0)),
                  pl.BlockSpec((1, D_MODEL), lambda i, k: (0, 0)),
                  pl.BlockSpec((D_MODEL, TH_FFN), lambda i, k: (0, k)),
                  pl.BlockSpec((D_MODEL, TH_FFN), lambda i, k: (0, k)),
                  pl.BlockSpec((TH_FFN, D_MODEL), lambda i, k: (k, 0))],
        out_specs=pl.BlockSpec((tm, D_MODEL), lambda i, k: (i, 0)),
        scratch_shapes=[pltpu.VMEM((tm, D_MODEL), BF16), pltpu.VMEM((tm, D_MODEL), F32)],
        compiler_params=_params("parallel", "arbitrary"),
        name="dense_swiglu",
    )(x, mods, g, w1, w3, w2)


def _top2_gates(logits):
    lane = lax.broadcasted_iota(jnp.int32, logits.shape, 1)
    lg = jnp.where(lane < N_EXPERTS, logits, -jnp.inf)
    m1 = jnp.max(lg, axis=-1, keepdims=True)
    i1 = jnp.min(jnp.where(lg == m1, lane, LANES), axis=-1, keepdims=True)
    lg2 = jnp.where(lane == i1, -jnp.inf, lg)
    m2 = jnp.max(lg2, axis=-1, keepdims=True)
    i2 = jnp.min(jnp.where(lg2 == m2, lane, LANES), axis=-1, keepdims=True)
    e2 = jnp.exp(m2 - m1)
    den = 1.0 + e2
    return jnp.where(lane == i1, 1.0 / den, 0.0) + jnp.where(lane == i2, e2 / den, 0.0)


def _moe_kernel(x_ref, mod_ref, g_ref, rw_ref, rb_ref, w1_ref, w3_ref, w2_ref, o_ref,
                v_ref, acc_ref, gate_ref):
    e = pl.program_id(1)
    k = pl.program_id(2)

    @pl.when(jnp.logical_and(e == 0, k == 0))
    def _():
        v = _ffn_prologue(x_ref, mod_ref, g_ref)
        v_ref[...] = v.astype(BF16)
        acc_ref[...] = jnp.zeros_like(acc_ref)
        gate_ref[...] = _top2_gates(_dot_f32(v, rw_ref[...]) + rb_ref[...])

    lane = lax.broadcasted_iota(jnp.int32, gate_ref.shape, 1)
    gate = jnp.sum(jnp.where(lane == e, gate_ref[...], 0.0), axis=-1, keepdims=True)
    acc_ref[...] += gate * _swiglu_partial(v_ref[...], w1_ref, w3_ref, w2_ref)

    @pl.when(jnp.logical_and(e == pl.num_programs(1) - 1, k == pl.num_programs(2) - 1))
    def _():
        o_ref[...] = x_ref[...] + mod_ref[:, 5 * D_MODEL:6 * D_MODEL] * acc_ref[...]


def _moe_ffn(x, mods, g, rw, rb, w1, w3, w2, regions):
    t = x.shape[0]
    tm = min(TM_FFN, regions[0][0] * regions[0][1], regions[1][1])
    n_e, _, hid = w1.shape
    return pl.pallas_call(
        _moe_kernel,
        out_shape=jax.ShapeDtypeStruct((t, D_MODEL), F32),
        grid=(t // tm, n_e, hid // TH_FFN),
        in_specs=[pl.BlockSpec((tm, D_MODEL), lambda i, e, k: (i, 0)),
                  pl.BlockSpec((None, 1, 6 * D_MODEL), lambda i, e, k: (_mod_row(i, tm, regions), 0, 0)),
                  pl.BlockSpec((1, D_MODEL), lambda i, e, k: (0, 0)),
                  pl.BlockSpec((D_MODEL, LANES), lambda i, e, k: (0, 0)),
                  pl.BlockSpec((1, LANES), lambda i, e, k: (0, 0)),
                  pl.BlockSpec((None, D_MODEL, TH_FFN), lambda i, e, k: (e, 0, k)),
                  pl.BlockSpec((None, D_MODEL, TH_FFN), lambda i, e, k: (e, 0, k)),
                  pl.BlockSpec((None, TH_FFN, D_MODEL), lambda i, e, k: (e, k, 0))],
        out_specs=pl.BlockSpec((tm, D_MODEL), lambda i, e, k: (i, 0)),
        scratch_shapes=[pltpu.VMEM((tm, D_MODEL), BF16), pltpu.VMEM((tm, D_MODEL), F32),
                        pltpu.VMEM((tm, LANES), F32)],
        compiler_params=_params("parallel", "arbitrary", "arbitrary"),
        name="moe_swiglu",
    )(x, mods, g, rw, rb, w1, w3, w2)


def _final_norm_kernel(x_ref, g_ref, o_ref):
    o_ref[...] = _rms(x_ref[...], g_ref[...])


def _final_norm(x, g, tok_off, n_tok):
    tm = min(TM_NORM, n_tok)
    off = tok_off // tm
    return pl.pallas_call(
        _final_norm_kernel,
        out_shape=jax.ShapeDtypeStruct((n_tok, D_MODEL), F32),
        grid=(n_tok // tm,),
        in_specs=[pl.BlockSpec((tm, D_MODEL), lambda i: (i + off, 0)),
                  pl.BlockSpec((1, D_MODEL), lambda i: (0, 0))],
        out_specs=pl.BlockSpec((tm, D_MODEL), lambda i: (i, 0)),
        compiler_params=_params("parallel"),
        name="final_norm",
    )(x, g)


def _grid_pos_embed(l, d):
    rows = l // GRID_W
    row = jnp.repeat(jnp.arange(rows, dtype=F32), GRID_W)
    col = jnp.tile(jnp.arange(GRID_W, dtype=F32), rows)
    quarter = d // 4
    omega = jnp.exp(-math.log(10000.0) * jnp.arange(quarter, dtype=F32) / quarter)
    er = row[:, None] * omega
    ec = col[:, None] * omega
    return jnp.concatenate([jnp.sin(er), jnp.cos(er), jnp.sin(ec), jnp.cos(ec)], axis=-1)


def _pad_lanes(a, width):
    return jnp.pad(a, [(0, 0)] * (a.ndim - 1) + [(0, width - a.shape[-1])])


def _reorder_w_in(w):
    o = np.cumsum([0, SSD_INNER, SSD_INNER + 2 * SSD_GROUPS * SSD_STATE, 2 * SSD_HEADS, GLA_KDIM, GLA_KDIM,
                   GLA_VDIM, GLA_VDIM, 2 * GLA_GATE_RANK, D_MODEL, D_MODEL])
    z, xbc, dt, qq, kk, vv, rr, lr, ga, gb = [w[:, int(o[i]):int(o[i + 1])] for i in range(10)]
    x, bc = xbc[:, :SSD_INNER], xbc[:, SSD_INNER:]
    cols = jnp.concatenate([z, x, vv, rr, ga, gb, qq, kk, bc, dt, lr], axis=1)
    return _pad_lanes(cols, D_PROJ).astype(BF16)


def _layer_weights(i, p):
    conv_w = jnp.pad(p["ssd_conv_w"][i], ((0, SUBLANES - SSD_CONV), (0, 0)))
    conv_b = p["ssd_conv_b"][i][None, :]
    gate_w = jnp.zeros((2, LANES, GLA_KDIM), F32)
    gate_w = gate_w.at[0, SM_LR:SM_LR + GLA_GATE_RANK].set(p["gla_gate_w"][i, 0])
    gate_w = gate_w.at[1, SM_LR + GLA_GATE_RANK:SM_LR + 2 * GLA_GATE_RANK].set(p["gla_gate_w"][i, 1])
    return {
        "w_in": _reorder_w_in(p["w_in"][i]),
        "conv_w_x": conv_w[:, :SSD_INNER], "conv_w_bc": conv_w[:, SSD_INNER:],
        "conv_b_x": conv_b[:, :SSD_INNER], "conv_b_bc": conv_b[:, SSD_INNER:],
        "dt_bias": _pad_lanes(p["ssd_dt_bias"][i].reshape(1, -1), LANES),
        "a_log": _pad_lanes(p["ssd_a_log"][i].reshape(1, -1), LANES),
        "d_exp": jnp.repeat(p["ssd_d"][i], SSD_HEADDIM)[None, :],
        "ssd_norm_g": p["ssd_norm_g"][i][None, :],
        "gate_w": gate_w, "gate_b": p["gla_gate_b"][i][:, None, :],
        "gla_norm_g": p["gla_norm_g"][i][None, :],
        "w_bs": p["w_branch_ssd"][i].astype(BF16), "w_bg": p["w_branch_gla"][i].astype(BF16),
        "w_o": p["w_out"][i].astype(BF16),
    }


def _ssd_state_in(s):
    b = s.shape[0]
    return jnp.transpose(s, (0, 1, 4, 2, 3)).reshape(b, 2, SSD_STATE, SSD_INNER)


def _ssd_state_out(f, b):
    s = jnp.stack([f, b], axis=1).reshape(-1, 2, SSD_STATE, SSD_HEADS, SSD_HEADDIM)
    return jnp.transpose(s, (0, 1, 3, 4, 2))


def _gla_state_in(s):
    b = s.shape[0]
    return jnp.transpose(s, (0, 1, 4, 2, 3)).reshape(b, 2, GLA_DV, GLA_KDIM)


def _gla_state_out(f, b):
    s = jnp.stack([f, b], axis=1).reshape(-1, 2, GLA_DV, GLA_HEADS, GLA_DK)
    return jnp.transpose(s, (0, 1, 3, 4, 2))


def _trunk(x_prompt, x_sample, state_ssd, state_gla, c, c_ctx, p):
    n0, l0, _ = x_prompt.shape
    n1, l1, _ = x_sample.shape
    regions = ((n0, l0), (n1, l1))
    depth = p["w_in"].shape[0]

    cc = jnp.zeros((MOD_ROWS, D_MODEL), F32).at[0].set(c_ctx).at[1:1 + n1].set(c)
    mods = _modulation_table(cc, p["ada_w"], p["ada_b"])[:, :, None, :]
    x = _assemble_tokens(x_prompt, x_sample, _grid_pos_embed(l1, D_MODEL), regions)

    ssd_states, gla_states = [], []
    for i in range(depth):
        lw = _layer_weights(i, p)
        init_ssd = jnp.concatenate([jnp.zeros((n0, 2, SSD_STATE, SSD_INNER), F32),
                                    _ssd_state_in(state_ssd[:, i])], axis=0)
        init_gla = jnp.concatenate([jnp.zeros((n0, 2, GLA_DV, GLA_KDIM), F32),
                                    _gla_state_in(state_gla[:, i])], axis=0)
        proj = _in_projection(x, mods[i], p["norm1_g"][i][None, :], lw["w_in"], regions)
        y_ssd, sf, sb = _ssd_mixer(proj, init_ssd, lw, regions)
        y_gla, gf, gb = _gla_mixer(proj, init_gla, lw, regions)
        ssd_states.append(_ssd_state_out(sf[:n0], sb[:n0]))
        gla_states.append(_gla_state_out(gf[:n0], gb[:n0]))
        x = _merge(x, y_ssd, y_gla, proj, mods[i], lw, regions)
        j = i // 2
        g2 = p["norm2_g"][i][None, :]
        if i % 2 == 0:
            x = _dense_ffn(x, mods[i], g2, p["ffn_w1"][j].astype(BF16), p["ffn_w3"][j].astype(BF16),
                           p["ffn_w2"][j].astype(BF16), regions)
        else:
            x = _moe_ffn(x, mods[i], g2, _pad_lanes(p["router_w"][j], LANES),
                         _pad_lanes(p["router_b"][j][None, :], LANES),
                         p["moe_w1"][j].astype(BF16), p["moe_w3"][j].astype(BF16),
                         p["moe_w2"][j].astype(BF16), regions)

    gfin = p["final_norm_g"][None, :]
    y_prompt = _final_norm(x, gfin, 0, n0 * l0).reshape(n0, l0, D_MODEL)
    y_sample = _final_norm(x, gfin, n0 * l0, n1 * l1).reshape(n1, l1, D_MODEL)
    return y_prompt, y_sample, jnp.stack(ssd_states, axis=1), jnp.stack(gla_states, axis=1)


def kernel(x_prompt, x_sample, state_ssd, state_gla, c, c_ctx, ada_w, ada_b, norm1_g, norm2_g, w_in, ssd_conv_w, ssd_conv_b, ssd_dt_bias, ssd_a_log, ssd_d, ssd_norm_g, gla_gate_w, gla_gate_b, gla_norm_g, w_branch_ssd, w_branch_gla, w_out, ffn_w1, ffn_w3, ffn_w2, router_w, router_b, moe_w1, moe_w3, moe_w2, final_norm_g):
    p = dict(ada_w=ada_w, ada_b=ada_b, norm1_g=norm1_g, norm2_g=norm2_g, w_in=w_in, ssd_conv_w=ssd_conv_w,
             ssd_conv_b=ssd_conv_b, ssd_dt_bias=ssd_dt_bias, ssd_a_log=ssd_a_log, ssd_d=ssd_d,
             ssd_norm_g=ssd_norm_g, gla_gate_w=gla_gate_w, gla_gate_b=gla_gate_b, gla_norm_g=gla_norm_g,
             w_branch_ssd=w_branch_ssd, w_branch_gla=w_branch_gla, w_out=w_out, ffn_w1=ffn_w1, ffn_w3=ffn_w3,
             ffn_w2=ffn_w2, router_w=router_w, router_b=router_b, moe_w1=moe_w1, moe_w3=moe_w3, moe_w2=moe_w2,
             final_norm_g=final_norm_g)
    return _trunk(x_prompt, x_sample, state_ssd, state_gla, c, c_ctx, p)
```

```python
import functools
import math

import numpy as np
import jax
import jax.numpy as jnp
from jax import lax
from jax.experimental import pallas as pl
from jax.experimental.pallas import tpu as pltpu

F32 = jnp.float32
BF16 = jnp.bfloat16

D_MODEL = 1024
DEPTH = 4
GRID_W = 64
EPS = 1e-6
SSD_HEADS = 16
SSD_HEADDIM = 64
SSD_INNER = SSD_HEADS * SSD_HEADDIM
SSD_GROUPS = 2
SSD_STATE = 64
SSD_CONV = 5
SSD_CHUNK = 128
GLA_HEADS = 8
GLA_DK = 64
GLA_DV = 128
GLA_KDIM = GLA_HEADS * GLA_DK
GLA_VDIM = GLA_HEADS * GLA_DV
GLA_GATE_RANK = 16
GLA_GATE_TAU = 16.0
GLA_CHUNK = 64
FFN_HIDDEN = 2816
N_EXPERTS = 8

LANES = 128
SUBLANES = 8
VMEM_LIMIT_BYTES = 56 * 1024 * 1024

COL_Z, COL_X, COL_V, COL_R, COL_GA, COL_GB = 0, 1024, 2048, 3072, 4096, 5120
COL_QK = 6144
COL_BC = 7168
COL_SMALL = 7424
D_PROJ = 7680
SM_DT = 0
SM_LR = 2 * SSD_HEADS

NEG_BIG = -1e30

TM_PROJ, TN_PROJ = 1024, 1536
TM_MERGE = 512
TM_FFN, TH_FFN = 512, 1408
TM_MOE = 1024
MOE_ROW_BLOCK = 256
TM_NORM = 1024
TN_MODS = 1536
MOD_ROWS = 16


def _params(*sem):
    return pltpu.CompilerParams(dimension_semantics=sem, vmem_limit_bytes=VMEM_LIMIT_BYTES)


def _silu(x):
    return x * jax.nn.sigmoid(x)


def _softplus(x):
    return jnp.maximum(x, 0.0) + jnp.log1p(jnp.exp(-jnp.abs(x)))


def _split3(a):
    a1 = a.astype(BF16)
    r1 = a - a1.astype(F32)
    a2 = r1.astype(BF16)
    a3 = (r1 - a2.astype(F32)).astype(BF16)
    return a1, a2, a3


def _dot(a, b):
    return jnp.dot(a, b, preferred_element_type=F32)


def _dot_nt(a, b):
    return lax.dot_general(a, b, (((1,), (1,)), ((), ())), preferred_element_type=F32)


def _dot_tn(a, b):
    return lax.dot_general(a, b, (((0,), (0,)), ((), ())), preferred_element_type=F32)


def _dot_exact_lhs(t01, a):
    a1, a2, a3 = _split3(a)
    return _dot(t01, a1) + _dot(t01, a2) + _dot(t01, a3)


def _dot_f32(a, b):
    a1, a2, a3 = _split3(a)
    b1, b2, b3 = _split3(b)
    return (_dot(a1, b1) + _dot(a1, b2) + _dot(a2, b1)
            + _dot(a1, b3) + _dot(a2, b2) + _dot(a3, b1))


def _tri_mask(n, upper):
    i = lax.broadcasted_iota(jnp.int32, (n, n), 0)
    j = lax.broadcasted_iota(jnp.int32, (n, n), 1)
    return (j >= i) if upper else (j <= i)


def _rms(x, g):
    return x * lax.rsqrt(jnp.mean(x * x, axis=-1, keepdims=True) + EPS) * g


def _chunk_pos(c, q, regions):
    (n0, l0), (n1, l1) = regions
    per0, per1 = l0 // q, l1 // q
    nc0 = n0 * per0
    c1 = jnp.maximum(c - nc0, 0)
    in0 = c < nc0
    seq = jnp.where(in0, c // per0, n0 + c1 // per1)
    pos = jnp.where(in0, c % per0, c1 % per1)
    last = jnp.where(in0, per0 - 1, per1 - 1)
    return seq, pos == 0, pos == last


def _mod_row(i, tm, regions):
    (n0, l0), (n1, l1) = regions
    t1 = jnp.maximum(i * tm - n0 * l0, 0)
    return jnp.where(i * tm < n0 * l0, 0, 1 + t1 // l1)


def _n_tokens(regions):
    return sum(n * l for n, l in regions)


def _assemble_kernel(xp_ref, xs_ref, pe_ref, o_ref, *, n_ctx_tiles):
    i = pl.program_id(0)

    @pl.when(i < n_ctx_tiles)
    def _():
        o_ref[...] = xp_ref[...]

    @pl.when(i >= n_ctx_tiles)
    def _():
        o_ref[...] = xs_ref[...] + pe_ref[...]


def _assemble_tokens(xp, xs, pe, regions):
    (n0, l0), (n1, l1) = regions
    tm = min(1024, l0 * n0, l1)
    t0, t1 = n0 * l0, n1 * l1
    n_ctx = t0 // tm
    pe_tiles = l1 // tm
    return pl.pallas_call(
        functools.partial(_assemble_kernel, n_ctx_tiles=n_ctx),
        out_shape=jax.ShapeDtypeStruct((t0 + t1, D_MODEL), F32),
        grid=((t0 + t1) // tm,),
        in_specs=[
            pl.BlockSpec((tm, D_MODEL), lambda i: (jnp.minimum(i, n_ctx - 1), 0)),
            pl.BlockSpec((tm, D_MODEL), lambda i: (jnp.maximum(i - n_ctx, 0), 0)),
            pl.BlockSpec((tm, D_MODEL), lambda i: (jnp.maximum(i - n_ctx, 0) % pe_tiles, 0)),
        ],
        out_specs=pl.BlockSpec((tm, D_MODEL), lambda i: (i, 0)),
        compiler_params=_params("arbitrary"),
        name="assemble_tokens",
    )(xp.reshape(t0, D_MODEL), xs.reshape(t1, D_MODEL), pe)


def _mods_kernel(cc_ref, w_ref, b_ref, o_ref):
    a = _silu(cc_ref[...])
    o_ref[...] = _dot_f32(a, w_ref[...]) + b_ref[...]


def _modulation_table(cc, ada_w, ada_b):
    depth = ada_w.shape[0]
    n = ada_w.shape[2]
    return pl.pallas_call(
        _mods_kernel,
        out_shape=jax.ShapeDtypeStruct((depth, MOD_ROWS, n), F32),
        grid=(depth, n // TN_MODS),
        in_specs=[
            pl.BlockSpec((MOD_ROWS, D_MODEL), lambda l, j: (0, 0)),
            pl.BlockSpec((None, D_MODEL, TN_MODS), lambda l, j: (l, 0, j)),
            pl.BlockSpec((None, 1, TN_MODS), lambda l, j: (l, 0, j)),
        ],
        out_specs=pl.BlockSpec((None, MOD_ROWS, TN_MODS), lambda l, j: (l, 0, j)),
        compiler_params=_params("arbitrary", "arbitrary"),
        name="modulation_table",
    )(cc, ada_w, ada_b.reshape(depth, 1, n))


def _inproj_kernel(x_ref, mod_ref, g_ref, w_ref, o_ref, u_ref):
    @pl.when(pl.program_id(1) == 0)
    def _():
        y = _rms(x_ref[...], g_ref[...])
        u = y * (1.0 + mod_ref[:, D_MODEL:2 * D_MODEL]) + mod_ref[:, 0:D_MODEL]
        u_ref[...] = u.astype(BF16)

    o_ref[...] = _dot(u_ref[...], w_ref[...])


def _in_projection(x, mods, g, w, regions):
    t = x.shape[0]
    tm = min(TM_PROJ, regions[0][0] * regions[0][1], regions[1][1])
    return pl.pallas_call(
        _inproj_kernel,
        out_shape=jax.ShapeDtypeStruct((t, D_PROJ), F32),
        grid=(t // tm, D_PROJ // TN_PROJ),
        in_specs=[
            pl.BlockSpec((tm, D_MODEL), lambda i, j: (i, 0)),
            pl.BlockSpec((None, 1, 6 * D_MODEL), lambda i, j: (_mod_row(i, tm, regions), 0, 0)),
            pl.BlockSpec((1, D_MODEL), lambda i, j: (0, 0)),
            pl.BlockSpec((D_MODEL, TN_PROJ), lambda i, j: (0, j)),
        ],
        out_specs=pl.BlockSpec((tm, TN_PROJ), lambda i, j: (i, j)),
        scratch_shapes=[pltpu.VMEM((tm, D_MODEL), BF16)],
        compiler_params=_params("parallel", "arbitrary"),
        name="norm_in_projection",
    )(x, mods, g, w)


HALO = SUBLANES
CONV_PAD = SSD_CONV // 2


def _conv_silu(cur_ref, prev_ref, next_ref, ext_ref, w_ref, b_ref, first, last, q):
    ext_ref[0:HALO, :] = jnp.where(first, 0.0, prev_ref[...])
    ext_ref[HALO:HALO + q, :] = cur_ref[...]
    ext_ref[HALO + q:2 * HALO + q, :] = jnp.where(last, 0.0, next_ref[...])
    acc = b_ref[...]
    for k in range(SSD_CONV):
        off = HALO - CONV_PAD + k
        acc = acc + w_ref[k:k + 1, :] * ext_ref[off:off + q, :]
    return _silu(acc)


def _ssd_decay_terms(sm_ref, dtb_ref, alog_ref, q):
    lane = lax.broadcasted_iota(jnp.int32, (1, LANES), 1)
    dtv = _softplus(sm_ref[...] + dtb_ref[...])
    a_neg = jnp.where(lane < 2 * SSD_HEADS, -jnp.exp(alog_ref[...]), 0.0)
    a = dtv * a_neg
    acs = _dot_exact_lhs(_tri_mask(q, False).astype(BF16), a)
    rev = _dot_exact_lhs(_tri_mask(q, True).astype(BF16), a)
    m = jnp.where(lane < SSD_HEADS, acs, rev)
    tot = acs[q - 1:q, :]
    return m, tot, dtv


def _col(x, idx, width=LANES):
    return jnp.broadcast_to(x[:, idx:idx + 1], (x.shape[0], width))


def _pair_cols(x, base, p, lo_half):
    return jnp.where(lo_half, _col(x, base + 2 * p), _col(x, base + 2 * p + 1))


def _expand_state(s):
    lane = lax.broadcasted_iota(jnp.int32, s.shape, 1)
    half = SSD_INNER // SSD_GROUPS
    return jnp.concatenate([jnp.where(lane < half, s, 0.0), jnp.where(lane >= half, s, 0.0)], axis=0)


def _compact_state(s2):
    return s2[0:SSD_STATE, :] + s2[SSD_STATE:2 * SSD_STATE, :]


def _state_update_mask():
    row = lax.broadcasted_iota(jnp.int32, (SSD_GROUPS * SSD_STATE, SSD_INNER), 0)
    lane = lax.broadcasted_iota(jnp.int32, (SSD_GROUPS * SSD_STATE, SSD_INNER), 1)
    return (row < SSD_STATE) == (lane < SSD_INNER // SSD_GROUPS)


def _ssd_bwd_kernel(x_ref, xp_ref, xn_ref, bc_ref, bcp_ref, bcn_ref, sm_ref,
                    cwx_ref, cwbc_ref, cbx_ref, cbbc_ref, dtb_ref, alog_ref, init_ref,
                    enter_ref, fin_ref, s_ref, xe_ref, bce_ref, *, regions, n_chunks):
    q = SSD_CHUNK
    c = n_chunks - 1 - pl.program_id(0)
    _, first, last = _chunk_pos(c, q, regions)

    @pl.when(last)
    def _():
        s_ref[...] = _expand_state(init_ref[...])

    xs = _conv_silu(x_ref, xp_ref, xn_ref, xe_ref, cwx_ref, cbx_ref, first, last, q)
    bcs = _conv_silu(bc_ref, bcp_ref, bcn_ref, bce_ref, cwbc_ref, cbbc_ref, first, last, q)
    bsb = bcs[:, 0:LANES].astype(BF16)
    m, tot, dtv = _ssd_decay_terms(sm_ref, dtb_ref, alog_ref, q)
    wgt = jnp.exp(tot - m) * dtv
    etot = jnp.exp(tot)
    lo_half = lax.broadcasted_iota(jnp.int32, (1, LANES), 1) < SSD_HEADDIM

    s_old = s_ref[...]
    enter_ref[...] = _compact_state(s_old)
    xw, dec = [], []
    for p in range(SSD_HEADS // 2):
        sl = slice(p * LANES, (p + 1) * LANES)
        xw.append((xs[:, sl] * _pair_cols(wgt, SSD_HEADS, p, lo_half)).astype(BF16))
        dec.append(_pair_cols(etot, SSD_HEADS, p, lo_half))
    upd = _dot_tn(bsb, jnp.concatenate(xw, axis=1))
    s_new = s_old * jnp.concatenate(dec, axis=1) + jnp.where(_state_update_mask(), upd, 0.0)
    s_ref[...] = s_new

    @pl.when(first)
    def _():
        fin_ref[...] = _compact_state(s_new)


def _ssd_out_kernel(z_ref, x_ref, xp_ref, xn_ref, bc_ref, bcp_ref, bcn_ref, sm_ref,
                    cwx_ref, cwbc_ref, cbx_ref, cbbc_ref, dtb_ref, alog_ref, dexp_ref, ng_ref,
                    init_ref, enter_ref, y_ref, fin_ref, s_ref, xe_ref, bce_ref, *, regions):
    q = SSD_CHUNK
    c = pl.program_id(0)
    _, first, last = _chunk_pos(c, q, regions)

    @pl.when(first)
    def _():
        s_ref[...] = _expand_state(init_ref[...])

    xs = _conv_silu(x_ref, xp_ref, xn_ref, xe_ref, cwx_ref, cbx_ref, first, last, q)
    bcs = _conv_silu(bc_ref, bcp_ref, bcn_ref, bce_ref, cwbc_ref, cbbc_ref, first, last, q)
    bsb = bcs[:, 0:LANES].astype(BF16)
    csb = bcs[:, LANES:2 * LANES].astype(BF16)
    m, tot, dtv = _ssd_decay_terms(sm_ref, dtb_ref, alog_ref, q)
    em = jnp.exp(m)
    wgt = jnp.exp(tot - m) * dtv
    etot = jnp.exp(tot)
    mt = m.T
    dtt = dtv.T

    lane = lax.broadcasted_iota(jnp.int32, (1, LANES), 1)
    lo_half = lane < SSD_HEADDIM
    tril = _tri_mask(q, False)
    triu = _tri_mask(q, True)
    zero_b = jnp.zeros_like(csb)
    cb = [_dot_nt(jnp.where(lo_half, csb, zero_b), bsb), _dot_nt(jnp.where(lo_half, zero_b, csb), bsb)]

    s_old = s_ref[...]
    cs_f = _dot(csb, s_old.astype(BF16))
    cs_b = _dot(csb, _expand_state(enter_ref[...]).astype(BF16))
    dexp = dexp_ref[...]

    ys, xw, dec = [], [], []
    for p in range(SSD_HEADS // 2):
        sl = slice(p * LANES, (p + 1) * LANES)
        g = (2 * p) // (SSD_HEADS // SSD_GROUPS)
        xs_p = xs[:, sl]
        xs_pb = xs_p.astype(BF16)
        yd = []
        for h in (2 * p, 2 * p + 1):
            hb = SSD_HEADS + h
            seg_f = _col(m, h, q) - mt[h:h + 1, :]
            seg_b = _col(m, hb, q) - mt[hb:hb + 1, :]
            dl = (jnp.exp(jnp.where(tril, seg_f, NEG_BIG)) * dtt[h:h + 1, :]
                  + jnp.exp(jnp.where(triu, seg_b, NEG_BIG)) * dtt[hb:hb + 1, :])
            yd.append(_dot((cb[g] * dl).astype(BF16), xs_pb))
        y_p = jnp.where(lo_half, yd[0], yd[1])
        y_p = y_p + _pair_cols(em, 0, p, lo_half) * cs_f[:, sl]
        y_p = y_p + _pair_cols(em, SSD_HEADS, p, lo_half) * cs_b[:, sl]
        y_p = y_p + dexp[:, sl] * xs_p
        ys.append(y_p)
        xw.append((xs_p * _pair_cols(wgt, 0, p, lo_half)).astype(BF16))
        dec.append(_pair_cols(etot, 0, p, lo_half))

    y = jnp.concatenate(ys, axis=1) * _silu(z_ref[...])
    y_ref[...] = _rms(y, ng_ref[...]).astype(y_ref.dtype)

    upd = _dot_tn(bsb, jnp.concatenate(xw, axis=1))
    s_new = s_old * jnp.concatenate(dec, axis=1) + jnp.where(_state_update_mask(), upd, 0.0)
    s_ref[...] = s_new

    @pl.when(last)
    def _():
        fin_ref[...] = _compact_state(s_new)


def _ssd_mixer(proj, init, lw, regions):
    q = SSD_CHUNK
    t = proj.shape[0]
    n_chunks = t // q
    n_seq = regions[0][0] + regions[1][0]
    rb = q // HALO
    n_rb = t // HALO
    xcol, bccol, smcol = COL_X // SSD_INNER, COL_BC // (2 * LANES), COL_SMALL // LANES

    def chunk_specs(cidx):
        prev = lambda s: jnp.maximum(cidx(s) * rb - 1, 0)
        nxt = lambda s: jnp.minimum((cidx(s) + 1) * rb, n_rb - 1)
        return [
            pl.BlockSpec((q, SSD_INNER), lambda s: (cidx(s), xcol)),
            pl.BlockSpec((HALO, SSD_INNER), lambda s: (prev(s), xcol)),
            pl.BlockSpec((HALO, SSD_INNER), lambda s: (nxt(s), xcol)),
            pl.BlockSpec((q, 2 * LANES), lambda s: (cidx(s), bccol)),
            pl.BlockSpec((HALO, 2 * LANES), lambda s: (prev(s), bccol)),
            pl.BlockSpec((HALO, 2 * LANES), lambda s: (nxt(s), bccol)),
            pl.BlockSpec((q, LANES), lambda s: (cidx(s), smcol)),
        ]

    def const_spec(a):
        return pl.BlockSpec(a.shape, lambda s: (0,) * a.ndim)

    consts = [lw["conv_w_x"], lw["conv_w_bc"], lw["conv_b_x"], lw["conv_b_bc"], lw["dt_bias"], lw["a_log"]]
    scratch = [pltpu.VMEM((SSD_GROUPS * SSD_STATE, SSD_INNER), F32),
               pltpu.VMEM((q + 2 * HALO, SSD_INNER), F32),
               pltpu.VMEM((q + 2 * HALO, 2 * LANES), F32)]
    state_block = (None, None, SSD_STATE, SSD_INNER)

    bidx = lambda s: n_chunks - 1 - s
    seq_b = lambda s: _chunk_pos(bidx(s), q, regions)[0]
    enter_b, fin_b = pl.pallas_call(
        functools.partial(_ssd_bwd_kernel, regions=regions, n_chunks=n_chunks),
        out_shape=(jax.ShapeDtypeStruct((n_chunks, SSD_STATE, SSD_INNER), F32),
                   jax.ShapeDtypeStruct((n_seq, SSD_STATE, SSD_INNER), F32)),
        grid=(n_chunks,),
        in_specs=chunk_specs(bidx) + [const_spec(a) for a in consts]
        + [pl.BlockSpec(state_block, lambda s: (seq_b(s), 1, 0, 0))],
        out_specs=(pl.BlockSpec((None, SSD_STATE, SSD_INNER), lambda s: (bidx(s), 0, 0)),
                   pl.BlockSpec((None, SSD_STATE, SSD_INNER), lambda s: (seq_b(s), 0, 0))),
        scratch_shapes=scratch,
        compiler_params=_params("arbitrary"),
        name="ssd_backward_states",
    )(*([proj] * 7), *consts, init)

    fidx = lambda s: s
    seq_f = lambda s: _chunk_pos(s, q, regions)[0]
    consts2 = consts + [lw["d_exp"], lw["ssd_norm_g"]]
    y, fin_f = pl.pallas_call(
        functools.partial(_ssd_out_kernel, regions=regions),
        out_shape=(jax.ShapeDtypeStruct((t, SSD_INNER), BF16),
                   jax.ShapeDtypeStruct((n_seq, SSD_STATE, SSD_INNER), F32)),
        grid=(n_chunks,),
        in_specs=[pl.BlockSpec((q, SSD_INNER), lambda s: (s, COL_Z // SSD_INNER))]
        + chunk_specs(fidx) + [const_spec(a) for a in consts2]
        + [pl.BlockSpec(state_block, lambda s: (seq_f(s), 0, 0, 0)),
           pl.BlockSpec((None, SSD_STATE, SSD_INNER), lambda s: (s, 0, 0))],
        out_specs=(pl.BlockSpec((q, SSD_INNER), lambda s: (s, 0)),
                   pl.BlockSpec((None, SSD_STATE, SSD_INNER), lambda s: (seq_f(s), 0, 0))),
        scratch_shapes=scratch,
        compiler_params=_params("arbitrary"),
        name="ssd_forward_outputs",
    )(*([proj] * 8), *consts2, init, enter_b)
    return y, fin_f, fin_b


def _gla_log_decay(sm_ref, gw_ref, gb_ref, d):
    logit = _dot_f32(sm_ref[...], gw_ref[d]) + gb_ref[d]
    return -_softplus(-logit) * (1.0 / GLA_GATE_TAU)


def _gla_bwd_kernel(qk_ref, v_ref, sm_ref, gw_ref, gb_ref, init_ref,
                    enter_ref, fin_ref, s_ref, *, regions, n_chunks):
    cq = GLA_CHUNK
    c = n_chunks - 1 - pl.program_id(0)
    _, first, last = _chunk_pos(c, cq, regions)

    @pl.when(last)
    def _():
        s_ref[...] = init_ref[...]

    lg = _gla_log_decay(sm_ref, gw_ref, gb_ref, 1)
    rev = _dot_exact_lhs(_tri_mask(cq, True).astype(BF16), lg)
    tot = rev[0:1, :]
    ko = (qk_ref[:, GLA_KDIM:2 * GLA_KDIM] * jnp.exp(tot - rev)).astype(BF16)
    etot = jnp.exp(tot)
    lo_half = lax.broadcasted_iota(jnp.int32, (1, LANES), 1) < GLA_DK

    s_old = s_ref[...]
    enter_ref[...] = s_old
    new = []
    for p in range(GLA_HEADS // 2):
        sl = slice(p * LANES, (p + 1) * LANES)
        u0 = _dot_tn(v_ref[:, (2 * p) * GLA_DV:(2 * p + 1) * GLA_DV].astype(BF16), ko[:, sl])
        u1 = _dot_tn(v_ref[:, (2 * p + 1) * GLA_DV:(2 * p + 2) * GLA_DV].astype(BF16), ko[:, sl])
        new.append(s_old[:, sl] * etot[:, sl] + jnp.where(lo_half, u0, u1))
    s_new = jnp.concatenate(new, axis=1)
    s_ref[...] = s_new

    @pl.when(first)
    def _():
        fin_ref[...] = s_new


def _gla_out_kernel(qk_ref, v_ref, r_ref, sm_ref, gw_ref, gb_ref, ng_ref, init_ref, enter_ref,
                    y_ref, fin_ref, s_ref, *, regions):
    cq = GLA_CHUNK
    c = pl.program_id(0)
    _, first, last = _chunk_pos(c, cq, regions)

    @pl.when(first)
    def _():
        s_ref[...] = init_ref[...]

    tril = _tri_mask(cq, False)
    triu = _tri_mask(cq, True)
    lg_f = _gla_log_decay(sm_ref, gw_ref, gb_ref, 0)
    lg_b = _gla_log_decay(sm_ref, gw_ref, gb_ref, 1)
    gcs = _dot_exact_lhs(tril.astype(BF16), lg_f)
    rev = _dot_exact_lhs(triu.astype(BF16), lg_b)
    tot_f = gcs[cq - 1:cq, :]
    qs = qk_ref[:, 0:GLA_KDIM] * (GLA_DK ** -0.5)
    ks = qk_ref[:, GLA_KDIM:2 * GLA_KDIM]
    qe_f = (qs * jnp.exp(gcs)).astype(BF16)
    ke_f = (ks * jnp.exp(-gcs)).astype(BF16)
    ko_f = (ks * jnp.exp(tot_f - gcs)).astype(BF16)
    qe_b = (qs * jnp.exp(rev)).astype(BF16)
    ke_b = (ks * jnp.exp(-rev)).astype(BF16)
    etot = jnp.exp(tot_f)
    lo_half = lax.broadcasted_iota(jnp.int32, (1, LANES), 1) < GLA_DK
    zero_b = jnp.zeros((cq, LANES), BF16)

    s_old = s_ref[...]
    s_oldb = s_old.astype(BF16)
    s_entb = enter_ref[...].astype(BF16)
    ng = ng_ref[...]
    new = []
    for p in range(GLA_HEADS // 2):
        sl = slice(p * LANES, (p + 1) * LANES)
        upd = []
        for hh in (0, 1):
            h = 2 * p + hh
            hs = slice(h * GLA_DV, (h + 1) * GLA_DV)
            qf = jnp.where(lo_half, qe_f[:, sl], zero_b) if hh == 0 else jnp.where(lo_half, zero_b, qe_f[:, sl])
            qb = jnp.where(lo_half, qe_b[:, sl], zero_b) if hh == 0 else jnp.where(lo_half, zero_b, qe_b[:, sl])
            att = (jnp.where(tril, _dot_nt(qf, ke_f[:, sl]), 0.0)
                   + jnp.where(triu, _dot_nt(qb, ke_b[:, sl]), 0.0))
            vh = v_ref[:, hs].astype(BF16)
            o = _dot(att.astype(BF16), vh) + _dot_nt(qf, s_oldb[:, sl]) + _dot_nt(qb, s_entb[:, sl])
            o = o * lax.rsqrt(jnp.mean(o * o, axis=-1, keepdims=True) + EPS) * ng[:, hs]
            y_ref[:, hs] = (o * _silu(r_ref[:, hs])).astype(y_ref.dtype)
            upd.append(_dot_tn(vh, ko_f[:, sl]))
        new.append(s_old[:, sl] * etot[:, sl] + jnp.where(lo_half, upd[0], upd[1]))
    s_new = jnp.concatenate(new, axis=1)
    s_ref[...] = s_new

    @pl.when(last)
    def _():
        fin_ref[...] = s_new


def _gla_mixer(proj, init, lw, regions):
    cq = GLA_CHUNK
    t = proj.shape[0]
    n_chunks = t // cq
    n_seq = regions[0][0] + regions[1][0]
    qkcol, vcol, rcol, smcol = COL_QK // 1024, COL_V // 1024, COL_R // 1024, COL_SMALL // LANES
    state_block = (None, None, GLA_DV, GLA_KDIM)
    gw, gb = lw["gate_w"], lw["gate_b"]

    def const_spec(a):
        return pl.BlockSpec(a.shape, lambda s: (0,) * a.ndim)

    bidx = lambda s: n_chunks - 1 - s
    seq_b = lambda s: _chunk_pos(bidx(s), cq, regions)[0]
    enter_b, fin_b = pl.pallas_call(
        functools.partial(_gla_bwd_kernel, regions=regions, n_chunks=n_chunks),
        out_shape=(jax.ShapeDtypeStruct((n_chunks, GLA_DV, GLA_KDIM), F32),
                   jax.ShapeDtypeStruct((n_seq, GLA_DV, GLA_KDIM), F32)),
        grid=(n_chunks,),
        in_specs=[pl.BlockSpec((cq, 2 * GLA_KDIM), lambda s: (bidx(s), qkcol)),
                  pl.BlockSpec((cq, GLA_VDIM), lambda s: (bidx(s), vcol)),
                  pl.BlockSpec((cq, LANES), lambda s: (bidx(s), smcol)),
                  const_spec(gw), const_spec(gb),
                  pl.BlockSpec(state_block, lambda s: (seq_b(s), 1, 0, 0))],
        out_specs=(pl.BlockSpec((None, GLA_DV, GLA_KDIM), lambda s: (bidx(s), 0, 0)),
                   pl.BlockSpec((None, GLA_DV, GLA_KDIM), lambda s: (seq_b(s), 0, 0))),
        scratch_shapes=[pltpu.VMEM((GLA_DV, GLA_KDIM), F32)],
        compiler_params=_params("arbitrary"),
        name="gla_backward_states",
    )(proj, proj, proj, gw, gb, init)

    seq_f = lambda s: _chunk_pos(s, cq, regions)[0]
    ng = lw["gla_norm_g"]
    y, fin_f = pl.pallas_call(
        functools.partial(_gla_out_kernel, regions=regions),
        out_shape=(jax.ShapeDtypeStruct((t, GLA_VDIM), BF16),
                   jax.ShapeDtypeStruct((n_seq, GLA_DV, GLA_KDIM), F32)),
        grid=(n_chunks,),
        in_specs=[pl.BlockSpec((cq, 2 * GLA_KDIM), lambda s: (s, qkcol)),
                  pl.BlockSpec((cq, GLA_VDIM), lambda s: (s, vcol)),
                  pl.BlockSpec((cq, GLA_VDIM), lambda s: (s, rcol)),
                  pl.BlockSpec((cq, LANES), lambda s: (s, smcol)),
                  const_spec(gw), const_spec(gb), const_spec(ng),
                  pl.BlockSpec(state_block, lambda s: (seq_f(s), 0, 0, 0)),
                  pl.BlockSpec((None, GLA_DV, GLA_KDIM), lambda s: (s, 0, 0))],
        out_specs=(pl.BlockSpec((cq, GLA_VDIM), lambda s: (s, 0)),
                   pl.BlockSpec((None, GLA_DV, GLA_KDIM), lambda s: (seq_f(s), 0, 0))),
        scratch_shapes=[pltpu.VMEM((GLA_DV, GLA_KDIM), F32)],
        compiler_params=_params("arbitrary"),
        name="gla_forward_outputs",
    )(proj, proj, proj, proj, gw, gb, ng, init, enter_b)
    return y, fin_f, fin_b


def _merge_kernel(x_ref, ys_ref, yg_ref, ga_ref, gb_ref, mod_ref, wbs_ref, wbg_ref, wo_ref, o_ref):
    merged = (jax.nn.sigmoid(ga_ref[...]) * _dot(ys_ref[...], wbs_ref[...])
              + jax.nn.sigmoid(gb_ref[...]) * _dot(yg_ref[...], wbg_ref[...]))
    mix = _dot(merged.astype(BF16), wo_ref[...])
    o_ref[...] = x_ref[...] + mod_ref[:, 2 * D_MODEL:3 * D_MODEL] * mix


def _merge(x, y_ssd, y_gla, proj, mods, lw, regions):
    t = x.shape[0]
    tm = min(TM_MERGE, regions[0][0] * regions[0][1], regions[1][1])
    tok = lambda col: pl.BlockSpec((tm, D_MODEL), lambda i: (i, col))
    wspec = pl.BlockSpec((D_MODEL, D_MODEL), lambda i: (0, 0))
    return pl.pallas_call(
        _merge_kernel,
        out_shape=jax.ShapeDtypeStruct((t, D_MODEL), F32),
        grid=(t // tm,),
        in_specs=[tok(0), tok(0), tok(0), tok(COL_GA // D_MODEL), tok(COL_GB // D_MODEL),
                  pl.BlockSpec((None, 1, 6 * D_MODEL), lambda i: (_mod_row(i, tm, regions), 0, 0)),
                  wspec, wspec, wspec],
        out_specs=tok(0),
        compiler_params=_params("parallel"),
        name="merge_out_projection",
    )(x, y_ssd, y_gla, proj, proj, mods, lw["w_bs"], lw["w_bg"], lw["w_o"])


def _ffn_prologue(x_ref, mod_ref, g_ref):
    y = _rms(x_ref[...], g_ref[...])
    return y * (1.0 + mod_ref[:, 4 * D_MODEL:5 * D_MODEL]) + mod_ref[:, 3 * D_MODEL:4 * D_MODEL]


def _swiglu_partial(v, w1_ref, w3_ref, w2_ref):
    h = _silu(_dot(v, w1_ref[...])) * _dot(v, w3_ref[...])
    return _dot(h.astype(BF16), w2_ref[...])


def _ffn_kernel(x_ref, mod_ref, g_ref, w1_ref, w3_ref, w2_ref, o_ref, v_ref, acc_ref):
    k = pl.program_id(1)

    @pl.when(k == 0)
    def _():
        v_ref[...] = _ffn_prologue(x_ref, mod_ref, g_ref).astype(BF16)
        acc_ref[...] = jnp.zeros_like(acc_ref)

    acc_ref[...] += _swiglu_partial(v_ref[...], w1_ref, w3_ref, w2_ref)

    @pl.when(k == pl.num_programs(1) - 1)
    def _():
        o_ref[...] = x_ref[...] + mod_ref[:, 5 * D_MODEL:6 * D_MODEL] * acc_ref[...]


def _dense_ffn(x, mods, g, w1, w3, w2, regions):
    t = x.shape[0]
    tm = min(TM_FFN, regions[0][0] * regions[0][1], regions[1][1])
    hid = w1.shape[1]
    return pl.pallas_call(
        _ffn_kernel,
        out_shape=jax.ShapeDtypeStruct((t, D_MODEL), F32),
        grid=(t // tm, hid // TH_FFN),
        in_specs=[pl.BlockSpec((tm, D_MODEL), lambda i, k: (i, 0)),
                  pl.BlockSpec((None, 1, 6 * D_MODEL), lambda i, k: (_mod_row(i, tm, regions), 0, 0)),
                  pl.BlockSpec((1, D_MODEL), lambda i, k: (0, 0)),
                  pl.BlockSpec((D_MODEL, TH_FFN), lambda i, k: (0, k)),
                  pl.BlockSpec((D_MODEL, TH_FFN), lambda i, k: (0, k)),
                  pl.BlockSpec((TH_FFN, D_MODEL), lambda i, k: (k, 0))],
        out_specs=pl.BlockSpec((tm, D_MODEL), lambda i, k: (i, 0)),
        scratch_shapes=[pltpu.VMEM((tm, D_MODEL), BF16), pltpu.VMEM((tm, D_MODEL), F32)],
        compiler_params=_params("parallel", "arbitrary"),
        name="dense_swiglu",
    )(x, mods, g, w1, w3, w2)


def _top2_gates(logits):
    lane = lax.broadcasted_iota(jnp.int32, logits.shape, 1)
    lg = jnp.where(lane < N_EXPERTS, logits, -jnp.inf)
    m1 = jnp.max(lg, axis=-1, keepdims=True)
    i1 = jnp.min(jnp.where(lg == m1, lane, LANES), axis=-1, keepdims=True)
    lg2 = jnp.where(lane == i1, -jnp.inf, lg)
    m2 = jnp.max(lg2, axis=-1, keepdims=True)
    i2 = jnp.min(jnp.where(lg2 == m2, lane, LANES), axis=-1, keepdims=True)
    e2 = jnp.exp(m2 - m1)
    den = 1.0 + e2
    return jnp.where(lane == i1, 1.0 / den, 0.0) + jnp.where(lane == i2, e2 / den, 0.0)


def _expert_column(tab_ref, e):
    lane = lax.broadcasted_iota(jnp.int32, tab_ref.shape, 1)
    return jnp.sum(jnp.where(lane == e, tab_ref[...], 0.0), axis=-1, keepdims=True)


def _moe_kernel(x_ref, mod_ref, g_ref, rw_ref, rb_ref, w1_ref, w3_ref, w2_ref, o_ref,
                v_ref, gate_ref, rank_ref, gate_t_ref, rank_t_ref, xg_ref, y_ref):
    e = pl.program_id(1)
    k = pl.program_id(2)
    last_k = pl.num_programs(2) - 1
    tm = x_ref.shape[0]
    rb = MOE_ROW_BLOCK

    @pl.when(jnp.logical_and(e == 0, k == 0))
    def _():
        v = _ffn_prologue(x_ref, mod_ref, g_ref)
        v_ref[...] = v.astype(BF16)
        gate = _top2_gates(_dot_f32(v, rw_ref[...]) + rb_ref[...])
        sel = jnp.where(gate > 0.0, 1.0, 0.0).astype(BF16)
        gate_ref[...] = gate
        gate_t_ref[...] = gate.T
        for blk in range(tm // rb):
            i = lax.broadcasted_iota(jnp.int32, (rb, tm), 0) + blk * rb
            j = lax.broadcasted_iota(jnp.int32, (rb, tm), 1)
            rank = _dot(jnp.where(j < i, 1.0, 0.0).astype(BF16), sel)
            rank_ref[blk * rb:(blk + 1) * rb, :] = rank
            rank_t_ref[:, blk * rb:(blk + 1) * rb] = rank.T
        o_ref[...] = jnp.zeros_like(o_ref)

    g_row = gate_t_ref[pl.ds(e, 1), :]
    r_row = rank_t_ref[pl.ds(e, 1), :].astype(jnp.int32)
    n_sel = jnp.sum(jnp.where(g_row > 0.0, 1.0, 0.0)).astype(jnp.int32)
    n_blocks = (n_sel + rb - 1) // rb

    @pl.when(k == 0)
    def _():
        def gather(b, carry):
            row = lax.broadcasted_iota(jnp.int32, (rb, tm), 0) + b * rb
            onehot = jnp.where(jnp.logical_and(row == r_row, g_row > 0.0), 1.0, 0.0).astype(BF16)
            off = pl.multiple_of(b * rb, rb)
            xg_ref[pl.ds(off, rb), :] = _dot(onehot, v_ref[...]).astype(BF16)
            return carry
        lax.fori_loop(0, n_blocks, gather, 0)

    def expert(b, carry):
        off = pl.multiple_of(b * rb, rb)
        part = _swiglu_partial(xg_ref[pl.ds(off, rb), :], w1_ref, w3_ref, w2_ref)

        @pl.when(k == 0)
        def _():
            y_ref[pl.ds(off, rb), :] = part

        @pl.when(k > 0)
        def _():
            y_ref[pl.ds(off, rb), :] += part
        return carry
    lax.fori_loop(0, n_blocks, expert, 0)

    @pl.when(k == last_k)
    def _():
        g_col = _expert_column(gate_ref, e)
        r_col = _expert_column(rank_ref, e).astype(jnp.int32)

        def scatter(b, carry):
            col = lax.broadcasted_iota(jnp.int32, (tm, rb), 1) + b * rb
            onehot = jnp.where(jnp.logical_and(col == r_col, g_col > 0.0), 1.0, 0.0).astype(BF16)
            off = pl.multiple_of(b * rb, rb)
            o_ref[...] += g_col * _dot(onehot, y_ref[pl.ds(off, rb), :].astype(BF16))
            return carry
        lax.fori_loop(0, n_blocks, scatter, 0)

    @pl.when(jnp.logical_and(e == pl.num_programs(1) - 1, k == last_k))
    def _():
        o_ref[...] = x_ref[...] + mod_ref[:, 5 * D_MODEL:6 * D_MODEL] * o_ref[...]


def _moe_ffn(x, mods, g, rw, rb, w1, w3, w2, regions):
    t = x.shape[0]
    tm = min(TM_MOE, regions[0][0] * regions[0][1], regions[1][1])
    n_e, _, hid = w1.shape
    return pl.pallas_call(
        _moe_kernel,
        out_shape=jax.ShapeDtypeStruct((t, D_MODEL), F32),
        grid=(t // tm, n_e, hid // TH_FFN),
        in_specs=[pl.BlockSpec((tm, D_MODEL), lambda i, e, k: (i, 0)),
                  pl.BlockSpec((None, 1, 6 * D_MODEL), lambda i, e, k: (_mod_row(i, tm, regions), 0, 0)),
                  pl.BlockSpec((1, D_MODEL), lambda i, e, k: (0, 0)),
                  pl.BlockSpec((D_MODEL, LANES), lambda i, e, k: (0, 0)),
                  pl.BlockSpec((1, LANES), lambda i, e, k: (0, 0)),
                  pl.BlockSpec((None, D_MODEL, TH_FFN), lambda i, e, k: (e, 0, k)),
                  pl.BlockSpec((None, D_MODEL, TH_FFN), lambda i, e, k: (e, 0, k)),
                  pl.BlockSpec((None, TH_FFN, D_MODEL), lambda i, e, k: (e, k, 0))],
        out_specs=pl.BlockSpec((tm, D_MODEL), lambda i, e, k: (i, 0)),
        scratch_shapes=[pltpu.VMEM((tm, D_MODEL), BF16),
                        pltpu.VMEM((tm, LANES), F32), pltpu.VMEM((tm, LANES), F32),
                        pltpu.VMEM((LANES, tm), F32), pltpu.VMEM((LANES, tm), F32),
                        pltpu.VMEM((tm, D_MODEL), BF16), pltpu.VMEM((tm, D_MODEL), F32)],
        compiler_params=_params("parallel", "arbitrary", "arbitrary"),
        name="moe_swiglu",
    )(x, mods, g, rw, rb, w1, w3, w2)


def _final_norm_kernel(x_ref, g_ref, o_ref):
    o_ref[...] = _rms(x_ref[...], g_ref[...])


def _final_norm(x, g, tok_off, n_tok):
    tm = min(TM_NORM, n_tok)
    off = tok_off // tm
    return pl.pallas_call(
        _final_norm_kernel,
        out_shape=jax.ShapeDtypeStruct((n_tok, D_MODEL), F32),
        grid=(n_tok // tm,),
        in_specs=[pl.BlockSpec((tm, D_MODEL), lambda i: (i + off, 0)),
                  pl.BlockSpec((1, D_MODEL), lambda i: (0, 0))],
        out_specs=pl.BlockSpec((tm, D_MODEL), lambda i: (i, 0)),
        compiler_params=_params("parallel"),
        name="final_norm",
    )(x, g)


def _grid_pos_embed(l, d):
    rows = l // GRID_W
    row = jnp.repeat(jnp.arange(rows, dtype=F32), GRID_W)
    col = jnp.tile(jnp.arange(GRID_W, dtype=F32), rows)
    quarter = d // 4
    omega = jnp.exp(-math.log(10000.0) * jnp.arange(quarter, dtype=F32) / quarter)
    er = row[:, None] * omega
    ec = col[:, None] * omega
    return jnp.concatenate([jnp.sin(er), jnp.cos(er), jnp.sin(ec), jnp.cos(ec)], axis=-1)


def _pad_lanes(a, width):
    return jnp.pad(a, [(0, 0)] * (a.ndim - 1) + [(0, width - a.shape[-1])])


def _reorder_w_in(w):
    o = np.cumsum([0, SSD_INNER, SSD_INNER + 2 * SSD_GROUPS * SSD_STATE, 2 * SSD_HEADS, GLA_KDIM, GLA_KDIM,
                   GLA_VDIM, GLA_VDIM, 2 * GLA_GATE_RANK, D_MODEL, D_MODEL])
    z, xbc, dt, qq, kk, vv, rr, lr, ga, gb = [w[:, int(o[i]):int(o[i + 1])] for i in range(10)]
    x, bc = xbc[:, :SSD_INNER], xbc[:, SSD_INNER:]
    cols = jnp.concatenate([z, x, vv, rr, ga, gb, qq, kk, bc, dt, lr], axis=1)
    return _pad_lanes(cols, D_PROJ).astype(BF16)


def _layer_weights(i, p):
    conv_w = jnp.pad(p["ssd_conv_w"][i], ((0, SUBLANES - SSD_CONV), (0, 0)))
    conv_b = p["ssd_conv_b"][i][None, :]
    gate_w = jnp.zeros((2, LANES, GLA_KDIM), F32)
    gate_w = gate_w.at[0, SM_LR:SM_LR + GLA_GATE_RANK].set(p["gla_gate_w"][i, 0])
    gate_w = gate_w.at[1, SM_LR + GLA_GATE_RANK:SM_LR + 2 * GLA_GATE_RANK].set(p["gla_gate_w"][i, 1])
    return {
        "w_in": _reorder_w_in(p["w_in"][i]),
        "conv_w_x": conv_w[:, :SSD_INNER], "conv_w_bc": conv_w[:, SSD_INNER:],
        "conv_b_x": conv_b[:, :SSD_INNER], "conv_b_bc": conv_b[:, SSD_INNER:],
        "dt_bias": _pad_lanes(p["ssd_dt_bias"][i].reshape(1, -1), LANES),
        "a_log": _pad_lanes(p["ssd_a_log"][i].reshape(1, -1), LANES),
        "d_exp": jnp.repeat(p["ssd_d"][i], SSD_HEADDIM)[None, :],
        "ssd_norm_g": p["ssd_norm_g"][i][None, :],
        "gate_w": gate_w, "gate_b": p["gla_gate_b"][i][:, None, :],
        "gla_norm_g": p["gla_norm_g"][i][None, :],
        "w_bs": p["w_branch_ssd"][i].astype(BF16), "w_bg": p["w_branch_gla"][i].astype(BF16),
        "w_o": p["w_out"][i].astype(BF16),
    }


def _ssd_state_in(s):
    b = s.shape[0]
    return jnp.transpose(s, (0, 1, 4, 2, 3)).reshape(b, 2, SSD_STATE, SSD_INNER)


def _ssd_state_out(f, b):
    s = jnp.stack([f, b], axis=1).reshape(-1, 2, SSD_STATE, SSD_HEADS, SSD_HEADDIM)
    return jnp.transpose(s, (0, 1, 3, 4, 2))


def _gla_state_in(s):
    b = s.shape[0]
    return jnp.transpose(s, (0, 1, 4, 2, 3)).reshape(b, 2, GLA_DV, GLA_KDIM)


def _gla_state_out(f, b):
    s = jnp.stack([f, b], axis=1).reshape(-1, 2, GLA_DV, GLA_HEADS, GLA_DK)
    return jnp.transpose(s, (0, 1, 3, 4, 2))


def _trunk(x_prompt, x_sample, state_ssd, state_gla, c, c_ctx, p):
    n0, l0, _ = x_prompt.shape
    n1, l1, _ = x_sample.shape
    regions = ((n0, l0), (n1, l1))
    depth = p["w_in"].shape[0]

    cc = jnp.zeros((MOD_ROWS, D_MODEL), F32).at[0].set(c_ctx).at[1:1 + n1].set(c)
    mods = _modulation_table(cc, p["ada_w"], p["ada_b"])[:, :, None, :]
    x = _assemble_tokens(x_prompt, x_sample, _grid_pos_embed(l1, D_MODEL), regions)

    ssd_states, gla_states = [], []
    for i in range(depth):
        lw = _layer_weights(i, p)
        init_ssd = jnp.concatenate([jnp.zeros((n0, 2, SSD_STATE, SSD_INNER), F32),
                                    _ssd_state_in(state_ssd[:, i])], axis=0)
        init_gla = jnp.concatenate([jnp.zeros((n0, 2, GLA_DV, GLA_KDIM), F32),
                                    _gla_state_in(state_gla[:, i])], axis=0)
        proj = _in_projection(x, mods[i], p["norm1_g"][i][None, :], lw["w_in"], regions)
        y_ssd, sf, sb = _ssd_mixer(proj, init_ssd, lw, regions)
        y_gla, gf, gb = _gla_mixer(proj, init_gla, lw, regions)
        ssd_states.append(_ssd_state_out(sf[:n0], sb[:n0]))
        gla_states.append(_gla_state_out(gf[:n0], gb[:n0]))
        x = _merge(x, y_ssd, y_gla, proj, mods[i], lw, regions)
        j = i // 2
        g2 = p["norm2_g"][i][None, :]
        if i % 2 == 0:
            x = _dense_ffn(x, mods[i], g2, p["ffn_w1"][j].astype(BF16), p["ffn_w3"][j].astype(BF16),
                           p["ffn_w2"][j].astype(BF16), regions)
        else:
            x = _moe_ffn(x, mods[i], g2, _pad_lanes(p["router_w"][j], LANES),
                         _pad_lanes(p["router_b"][j][None, :], LANES),
                         p["moe_w1"][j].astype(BF16), p["moe_w3"][j].astype(BF16),
                         p["moe_w2"][j].astype(BF16), regions)

    gfin = p["final_norm_g"][None, :]
    y_prompt = _final_norm(x, gfin, 0, n0 * l0).reshape(n0, l0, D_MODEL)
    y_sample = _final_norm(x, gfin, n0 * l0, n1 * l1).reshape(n1, l1, D_MODEL)
    return y_prompt, y_sample, jnp.stack(ssd_states, axis=1), jnp.stack(gla_states, axis=1)


def kernel(x_prompt, x_sample, state_ssd, state_gla, c, c_ctx, ada_w, ada_b, norm1_g, norm2_g, w_in, ssd_conv_w, ssd_conv_b, ssd_dt_bias, ssd_a_log, ssd_d, ssd_norm_g, gla_gate_w, gla_gate_b, gla_norm_g, w_branch_ssd, w_branch_gla, w_out, ffn_w1, ffn_w3, ffn_w2, router_w, router_b, moe_w1, moe_w3, moe_w2, final_norm_g):
    p = dict(ada_w=ada_w, ada_b=ada_b, norm1_g=norm1_g, norm2_g=norm2_g, w_in=w_in, ssd_conv_w=ssd_conv_w,
             ssd_conv_b=ssd_conv_b, ssd_dt_bias=ssd_dt_bias, ssd_a_log=ssd_a_log, ssd_d=ssd_d,
             ssd_norm_g=ssd_norm_g, gla_gate_w=gla_gate_w, gla_gate_b=gla_gate_b, gla_norm_g=gla_norm_g,
             w_branch_ssd=w_branch_ssd, w_branch_gla=w_branch_gla, w_out=w_out, ffn_w1=ffn_w1, ffn_w3=ffn_w3,
             ffn_w2=ffn_w2, router_w=router_w, router_b=router_b, moe_w1=moe_w1, moe_w3=moe_w3, moe_w2=moe_w2,
             final_norm_g=final_norm_g)
    return _trunk(x_prompt, x_sample, state_ssd, state_gla, c, c_ctx, p)
```

```python
import functools
import math

import numpy as np
import jax
import jax.numpy as jnp
from jax import lax
from jax.experimental import pallas as pl
from jax.experimental.pallas import tpu as pltpu

F32 = jnp.float32
BF16 = jnp.bfloat16

D_MODEL = 1024
GRID_W = 64
EPS = 1e-6
SSD_HEADS = 16
SSD_HEADDIM = 64
SSD_INNER = SSD_HEADS * SSD_HEADDIM
SSD_GROUPS = 2
SSD_STATE = 64
SSD_CONV = 5
SSD_CHUNK = 128
GLA_HEADS = 8
GLA_DK = 64
GLA_DV = 128
GLA_KDIM = GLA_HEADS * GLA_DK
GLA_VDIM = GLA_HEADS * GLA_DV
GLA_GATE_RANK = 16
GLA_GATE_TAU = 16.0
GLA_CHUNK = 64
N_EXPERTS = 8

LANES = 128
SUBLANES = 8
BF16_SUBLANES = 16
VMEM_LIMIT_BYTES = 56 * 1024 * 1024

COL_Z, COL_X, COL_V, COL_R, COL_GA, COL_GB = 0, 1024, 2048, 3072, 4096, 5120
COL_QK = 6144
COL_BC = 7168
D_PROJ = 7424
SM_LR = 2 * SSD_HEADS

TM_PROJ, TN_PROJ = 1024, 3712
TM_MERGE = 512
TM_FFN, TH_FFN = 512, 1408
TM_MOE = 1024
MOE_ROW_BLOCK = 256
TM_NORM = 1024
TN_MODS = 1536
MOD_ROWS = 16
SCAN_STEP = 256


def _params(*sem):
    return pltpu.CompilerParams(dimension_semantics=sem, vmem_limit_bytes=VMEM_LIMIT_BYTES)


def _sigmoid(x):
    return 0.5 * jnp.tanh(0.5 * x) + 0.5


def _silu(x):
    return x * _sigmoid(x)


def _softplus(x):
    return jnp.maximum(x, 0.0) + jnp.log(1.0 + jnp.exp(-jnp.abs(x)))


def _split2(a):
    a1 = a.astype(BF16)
    a2 = (a - a1.astype(F32)).astype(BF16)
    return a1, a2


def _split3(a):
    a1 = a.astype(BF16)
    r1 = a - a1.astype(F32)
    a2 = r1.astype(BF16)
    a3 = (r1 - a2.astype(F32)).astype(BF16)
    return a1, a2, a3


def _dot(a, b):
    return jnp.dot(a, b, preferred_element_type=F32)


def _dot_nt(a, b):
    return lax.dot_general(a, b, (((1,), (1,)), ((), ())), preferred_element_type=F32)


def _dot_tn(a, b):
    return lax.dot_general(a, b, (((0,), (0,)), ((), ())), preferred_element_type=F32)


def _dot_exact_lhs(t01, a):
    a1, a2 = _split2(a)
    return _dot(t01, a1) + _dot(t01, a2)


def _dot_f32(a, b):
    a1, a2, a3 = _split3(a)
    b1, b2, b3 = _split3(b)
    return (_dot(a1, b1) + _dot(a1, b2) + _dot(a2, b1)
            + _dot(a1, b3) + _dot(a2, b2) + _dot(a3, b1))


def _block_tri(n, blk, upper):
    i = lax.broadcasted_iota(jnp.int32, (n, n), 0)
    j = lax.broadcasted_iota(jnp.int32, (n, n), 1)
    tri = (j >= i) if upper else (j <= i)
    if blk == n:
        return tri
    return jnp.logical_and(tri, (i // blk) == (j // blk))


def _rms(x, g):
    return x * lax.rsqrt(jnp.mean(x * x, axis=-1, keepdims=True) + EPS) * g


def _chunk_pos(c, q, regions):
    (n0, l0), (n1, l1) = regions
    per0, per1 = l0 // q, l1 // q
    nc0 = n0 * per0
    c1 = jnp.maximum(c - nc0, 0)
    in0 = c < nc0
    seq = jnp.where(in0, c // per0, n0 + c1 // per1)
    pos = jnp.where(in0, c % per0, c1 % per1)
    last = jnp.where(in0, per0 - 1, per1 - 1)
    return seq, pos == 0, pos == last


def _mod_row(i, tm, regions):
    (n0, l0), (n1, l1) = regions
    t1 = jnp.maximum(i * tm - n0 * l0, 0)
    return jnp.where(i * tm < n0 * l0, 0, 1 + t1 // l1)


def _assemble_kernel(xp_ref, xs_ref, pe_ref, o_ref, *, n_ctx_tiles):
    i = pl.program_id(0)

    @pl.when(i < n_ctx_tiles)
    def _():
        o_ref[...] = xp_ref[...]

    @pl.when(i >= n_ctx_tiles)
    def _():
        o_ref[...] = xs_ref[...] + pe_ref[...]


def _assemble_tokens(xp, xs, pe, regions):
    (n0, l0), (n1, l1) = regions
    tm = min(1024, l0 * n0, l1)
    t0, t1 = n0 * l0, n1 * l1
    n_ctx = t0 // tm
    pe_tiles = l1 // tm
    return pl.pallas_call(
        functools.partial(_assemble_kernel, n_ctx_tiles=n_ctx),
        out_shape=jax.ShapeDtypeStruct((t0 + t1, D_MODEL), F32),
        grid=((t0 + t1) // tm,),
        in_specs=[
            pl.BlockSpec((tm, D_MODEL), lambda i: (jnp.minimum(i, n_ctx - 1), 0)),
            pl.BlockSpec((tm, D_MODEL), lambda i: (jnp.maximum(i - n_ctx, 0), 0)),
            pl.BlockSpec((tm, D_MODEL), lambda i: (jnp.maximum(i - n_ctx, 0) % pe_tiles, 0)),
        ],
        out_specs=pl.BlockSpec((tm, D_MODEL), lambda i: (i, 0)),
        compiler_params=_params("arbitrary"),
        name="assemble_tokens",
    )(xp.reshape(t0, D_MODEL), xs.reshape(t1, D_MODEL), pe)


def _mods_kernel(cc_ref, w_ref, b_ref, o_ref):
    a = _silu(cc_ref[...])
    o_ref[...] = _dot_f32(a, w_ref[...]) + b_ref[...]


def _modulation_table(cc, ada_w, ada_b):
    depth = ada_w.shape[0]
    n = ada_w.shape[2]
    return pl.pallas_call(
        _mods_kernel,
        out_shape=jax.ShapeDtypeStruct((depth, MOD_ROWS, n), F32),
        grid=(depth, n // TN_MODS),
        in_specs=[
            pl.BlockSpec((MOD_ROWS, D_MODEL), lambda l, j: (0, 0)),
            pl.BlockSpec((None, D_MODEL, TN_MODS), lambda l, j: (l, 0, j)),
            pl.BlockSpec((None, 1, TN_MODS), lambda l, j: (l, 0, j)),
        ],
        out_specs=pl.BlockSpec((None, MOD_ROWS, TN_MODS), lambda l, j: (l, 0, j)),
        compiler_params=_params("arbitrary", "arbitrary"),
        name="modulation_table",
    )(cc, ada_w, ada_b.reshape(depth, 1, n))


def _inproj_kernel(x_ref, mod_ref, g_ref, w_ref, ws_ref, o_ref, os_ref, u_ref):
    @pl.when(pl.program_id(1) == 0)
    def _():
        y = _rms(x_ref[...], g_ref[...])
        u = y * (1.0 + mod_ref[:, D_MODEL:2 * D_MODEL]) + mod_ref[:, 0:D_MODEL]
        u_ref[...] = u.astype(BF16)
        os_ref[...] = _dot(u_ref[...], ws_ref[...])

    o_ref[...] = _dot(u_ref[...], w_ref[...]).astype(BF16)


def _in_projection(x, mods, g, w, w_small, regions):
    t = x.shape[0]
    tm = min(TM_PROJ, regions[0][0] * regions[0][1], regions[1][1])
    return pl.pallas_call(
        _inproj_kernel,
        out_shape=(jax.ShapeDtypeStruct((t, D_PROJ), BF16), jax.ShapeDtypeStruct((t, LANES), F32)),
        grid=(t // tm, D_PROJ // TN_PROJ),
        in_specs=[
            pl.BlockSpec((tm, D_MODEL), lambda i, j: (i, 0)),
            pl.BlockSpec((None, 1, 6 * D_MODEL), lambda i, j: (_mod_row(i, tm, regions), 0, 0)),
            pl.BlockSpec((1, D_MODEL), lambda i, j: (0, 0)),
            pl.BlockSpec((D_MODEL, TN_PROJ), lambda i, j: (0, j)),
            pl.BlockSpec((D_MODEL, LANES), lambda i, j: (0, 0)),
        ],
        out_specs=(pl.BlockSpec((tm, TN_PROJ), lambda i, j: (i, j)),
                   pl.BlockSpec((tm, LANES), lambda i, j: (i, 0))),
        scratch_shapes=[pltpu.VMEM((tm, D_MODEL), BF16)],
        compiler_params=_params("parallel", "arbitrary"),
        name="norm_in_projection",
    )(x, mods, g, w, w_small)


HALO = BF16_SUBLANES
CONV_PAD = SSD_CONV // 2


def _conv_silu(cur_ref, prev_ref, next_ref, ext_ref, w_ref, b_ref, first, last, n):
    ext_ref[0:HALO, :] = jnp.where(first, 0.0, prev_ref[...].astype(F32))
    ext_ref[HALO:HALO + n, :] = cur_ref[...].astype(F32)
    ext_ref[HALO + n:2 * HALO + n, :] = jnp.where(last, 0.0, next_ref[...].astype(F32))
    acc = b_ref[...]
    for k in range(SSD_CONV):
        off = HALO - CONV_PAD + k
        acc = acc + w_ref[k:k + 1, :] * ext_ref[off:off + n, :]
    return _silu(acc)


def _ssd_decay_terms(sm, dtb, alog, q):
    lane = lax.broadcasted_iota(jnp.int32, (1, LANES), 1)
    dtv = _softplus(sm + dtb)
    a_neg = jnp.where(lane < 2 * SSD_HEADS, -jnp.exp(alog), 0.0)
    a = dtv * a_neg
    acs = _dot_exact_lhs(_block_tri(q, q, False).astype(BF16), a)
    rev = _dot_exact_lhs(_block_tri(q, q, True).astype(BF16), a)
    m = jnp.where(lane < SSD_HEADS, acs, rev)
    tot = acs[q - 1:q, :]
    return m, tot, dtv


def _col(x, idx, width=LANES):
    return jnp.broadcast_to(x[:, idx:idx + 1], (x.shape[0], width))


def _pair_cols(x, base, p, lo_half):
    return jnp.where(lo_half, _col(x, base + 2 * p), _col(x, base + 2 * p + 1))


def _expand_state(s):
    lane = lax.broadcasted_iota(jnp.int32, s.shape, 1)
    half = SSD_INNER // SSD_GROUPS
    return jnp.concatenate([jnp.where(lane < half, s, 0.0), jnp.where(lane >= half, s, 0.0)], axis=0)


def _compact_state(s2):
    return s2[0:SSD_STATE, :] + s2[SSD_STATE:2 * SSD_STATE, :]


def _state_update_mask():
    row = lax.broadcasted_iota(jnp.int32, (SSD_GROUPS * SSD_STATE, SSD_INNER), 0)
    lane = lax.broadcasted_iota(jnp.int32, (SSD_GROUPS * SSD_STATE, SSD_INNER), 1)
    return (row < SSD_STATE) == (lane < SSD_INNER // SSD_GROUPS)


def _ssd_bwd_kernel(x_ref, xp_ref, xn_ref, bc_ref, bcp_ref, bcn_ref, sm_ref,
                    cwx_ref, cwbc_ref, cbx_ref, cbbc_ref, dtb_ref, alog_ref, init_ref,
                    enter_ref, fin_ref, s_ref, xe_ref, bce_ref, *, regions, n_steps):
    q = SSD_CHUNK
    c = n_steps - 1 - pl.program_id(0)
    _, first, last = _chunk_pos(c, SCAN_STEP, regions)

    @pl.when(last)
    def _():
        s_ref[...] = _expand_state(init_ref[...])

    xs_all = _conv_silu(x_ref, xp_ref, xn_ref, xe_ref, cwx_ref, cbx_ref, first, last, SCAN_STEP)
    bcs_all = _conv_silu(bc_ref, bcp_ref, bcn_ref, bce_ref, cwbc_ref, cbbc_ref, first, last, SCAN_STEP)
    lo_half = lax.broadcasted_iota(jnp.int32, (1, LANES), 1) < SSD_HEADDIM
    upd_mask = _state_update_mask()

    s_cur = s_ref[...]
    for ci in reversed(range(SCAN_STEP // q)):
        rows = slice(ci * q, (ci + 1) * q)
        xs = xs_all[rows]
        bsb = bcs_all[rows, 0:LANES].astype(BF16)
        m, tot, dtv = _ssd_decay_terms(sm_ref[rows, :], dtb_ref[...], alog_ref[...], q)
        wgt = jnp.exp(tot - m) * dtv
        etot = jnp.exp(tot)
        enter_ref[ci] = _compact_state(s_cur)
        xw, dec = [], []
        for p in range(SSD_HEADS // 2):
            sl = slice(p * LANES, (p + 1) * LANES)
            xw.append((xs[:, sl] * _pair_cols(wgt, SSD_HEADS, p, lo_half)).astype(BF16))
            dec.append(_pair_cols(etot, SSD_HEADS, p, lo_half))
        upd = _dot_tn(bsb, jnp.concatenate(xw, axis=1))
        s_cur = s_cur * jnp.concatenate(dec, axis=1) + jnp.where(upd_mask, upd, 0.0)
    s_ref[...] = s_cur

    @pl.when(first)
    def _():
        fin_ref[...] = _compact_state(s_cur)


def _ssd_out_kernel(z_ref, x_ref, xp_ref, xn_ref, bc_ref, bcp_ref, bcn_ref, sm_ref,
                    cwx_ref, cwbc_ref, cbx_ref, cbbc_ref, dtb_ref, alog_ref, dexp_ref, ng_ref,
                    init_ref, enter_ref, y_ref, fin_ref, s_ref, xe_ref, bce_ref, *, regions):
    q = SSD_CHUNK
    c = pl.program_id(0)
    _, first, last = _chunk_pos(c, SCAN_STEP, regions)

    @pl.when(first)
    def _():
        s_ref[...] = _expand_state(init_ref[...])

    xs_all = _conv_silu(x_ref, xp_ref, xn_ref, xe_ref, cwx_ref, cbx_ref, first, last, SCAN_STEP)
    bcs_all = _conv_silu(bc_ref, bcp_ref, bcn_ref, bce_ref, cwbc_ref, cbbc_ref, first, last, SCAN_STEP)

    lane = lax.broadcasted_iota(jnp.int32, (1, LANES), 1)
    lo_half = lane < SSD_HEADDIM
    tril = _block_tri(q, q, False)
    ii = lax.broadcasted_iota(jnp.int32, (q, q), 0)
    jj = lax.broadcasted_iota(jnp.int32, (q, q), 1)
    eye = ii == jj
    upd_mask = _state_update_mask()
    dexp = dexp_ref[...]
    ng = ng_ref[...]

    s_cur = s_ref[...]
    for ci in range(SCAN_STEP // q):
        rows = slice(ci * q, (ci + 1) * q)
        xs = xs_all[rows]
        bsb = bcs_all[rows, 0:LANES].astype(BF16)
        csb = bcs_all[rows, LANES:2 * LANES].astype(BF16)
        m, tot, dtv = _ssd_decay_terms(sm_ref[rows, :], dtb_ref[...], alog_ref[...], q)
        wgt = jnp.exp(tot - m) * dtv
        mt = m.T
        dtt = dtv.T
        zero_b = jnp.zeros_like(csb)
        cb = [_dot_nt(jnp.where(lo_half, csb, zero_b), bsb), _dot_nt(jnp.where(lo_half, zero_b, csb), bsb)]
        cs_f = _dot(csb, s_cur.astype(BF16))
        cs_b = _dot(csb, _expand_state(enter_ref[ci]).astype(BF16))

        ys, xw, dec = [], [], []
        for p in range(SSD_HEADS // 2):
            sl = slice(p * LANES, (p + 1) * LANES)
            g = (2 * p) // (SSD_HEADS // SSD_GROUPS)
            xs_p = xs[:, sl]
            xs_pb = xs_p.astype(BF16)
            yd, colf, colb = [], [], []
            for h in (2 * p, 2 * p + 1):
                hb = SSD_HEADS + h
                cf = _col(m, h, q)
                cbk = _col(m, hb, q)
                seg = jnp.where(tril, cf - mt[h:h + 1, :], cbk - mt[hb:hb + 1, :])
                dl = jnp.exp(seg) * jnp.where(tril, dtt[h:h + 1, :], dtt[hb:hb + 1, :])
                dl = jnp.where(eye, dl + dtt[hb:hb + 1, :], dl)
                yd.append(_dot((cb[g] * dl).astype(BF16), xs_pb))
                colf.append(cf)
                colb.append(cbk)
            ef = jnp.exp(jnp.where(lo_half, colf[0], colf[1]))
            eb = jnp.exp(jnp.where(lo_half, colb[0], colb[1]))
            y_p = jnp.where(lo_half, yd[0], yd[1]) + ef * cs_f[:, sl] + eb * cs_b[:, sl] + dexp[:, sl] * xs_p
            ys.append(y_p)
            xw.append((xs_p * _pair_cols(wgt, 0, p, lo_half)).astype(BF16))
            dec.append(ef[q - 1:q, :])

        y = jnp.concatenate(ys, axis=1) * _silu(z_ref[rows, :].astype(F32))
        y_ref[rows, :] = _rms(y, ng).astype(y_ref.dtype)

        upd = _dot_tn(bsb, jnp.concatenate(xw, axis=1))
        s_cur = s_cur * jnp.concatenate(dec, axis=1) + jnp.where(upd_mask, upd, 0.0)
    s_ref[...] = s_cur

    @pl.when(last)
    def _():
        fin_ref[...] = _compact_state(s_cur)


def _ssd_mixer(proj, small, init, lw, regions):
    st = SCAN_STEP
    cps = st // SSD_CHUNK
    t = proj.shape[0]
    n_steps = t // st
    n_seq = regions[0][0] + regions[1][0]
    rb = st // HALO
    n_rb = t // HALO
    xcol, bccol = COL_X // SSD_INNER, COL_BC // (2 * LANES)

    def chunk_specs(cidx):
        prev = lambda s: jnp.maximum(cidx(s) * rb - 1, 0)
        nxt = lambda s: jnp.minimum((cidx(s) + 1) * rb, n_rb - 1)
        return [
            pl.BlockSpec((st, SSD_INNER), lambda s: (cidx(s), xcol)),
            pl.BlockSpec((HALO, SSD_INNER), lambda s: (prev(s), xcol)),
            pl.BlockSpec((HALO, SSD_INNER), lambda s: (nxt(s), xcol)),
            pl.BlockSpec((st, 2 * LANES), lambda s: (cidx(s), bccol)),
            pl.BlockSpec((HALO, 2 * LANES), lambda s: (prev(s), bccol)),
            pl.BlockSpec((HALO, 2 * LANES), lambda s: (nxt(s), bccol)),
            pl.BlockSpec((st, LANES), lambda s: (cidx(s), 0)),
        ]

    def const_spec(a):
        return pl.BlockSpec(a.shape, lambda s: (0,) * a.ndim)

    consts = [lw["conv_w_x"], lw["conv_w_bc"], lw["conv_b_x"], lw["conv_b_bc"], lw["dt_bias"], lw["a_log"]]
    scratch = [pltpu.VMEM((SSD_GROUPS * SSD_STATE, SSD_INNER), F32),
               pltpu.VMEM((st + 2 * HALO, SSD_INNER), F32),
               pltpu.VMEM((st + 2 * HALO, 2 * LANES), F32)]
    state_block = (None, None, SSD_STATE, SSD_INNER)
    enter_block = (cps, SSD_STATE, SSD_INNER)

    bidx = lambda s: n_steps - 1 - s
    seq_b = lambda s: _chunk_pos(bidx(s), st, regions)[0]
    enter_b, fin_b = pl.pallas_call(
        functools.partial(_ssd_bwd_kernel, regions=regions, n_steps=n_steps),
        out_shape=(jax.ShapeDtypeStruct((n_steps * cps, SSD_STATE, SSD_INNER), F32),
                   jax.ShapeDtypeStruct((n_seq, SSD_STATE, SSD_INNER), F32)),
        grid=(n_steps,),
        in_specs=chunk_specs(bidx)[:-1] + [pl.BlockSpec((st, LANES), lambda s: (bidx(s), 0))]
        + [const_spec(a) for a in consts]
        + [pl.BlockSpec(state_block, lambda s: (seq_b(s), 1, 0, 0))],
        out_specs=(pl.BlockSpec(enter_block, lambda s: (bidx(s), 0, 0)),
                   pl.BlockSpec((None, SSD_STATE, SSD_INNER), lambda s: (seq_b(s), 0, 0))),
        scratch_shapes=scratch,
        compiler_params=_params("arbitrary"),
        name="ssd_backward_states",
    )(*([proj] * 6), small, *consts, init)

    fidx = lambda s: s
    seq_f = lambda s: _chunk_pos(s, st, regions)[0]
    consts2 = consts + [lw["d_exp"], lw["ssd_norm_g"]]
    y, fin_f = pl.pallas_call(
        functools.partial(_ssd_out_kernel, regions=regions),
        out_shape=(jax.ShapeDtypeStruct((t, SSD_INNER), BF16),
                   jax.ShapeDtypeStruct((n_seq, SSD_STATE, SSD_INNER), F32)),
        grid=(n_steps,),
        in_specs=[pl.BlockSpec((st, SSD_INNER), lambda s: (s, COL_Z // SSD_INNER))]
        + chunk_specs(fidx) + [const_spec(a) for a in consts2]
        + [pl.BlockSpec(state_block, lambda s: (seq_f(s), 0, 0, 0)),
           pl.BlockSpec(enter_block, lambda s: (s, 0, 0))],
        out_specs=(pl.BlockSpec((st, SSD_INNER), lambda s: (s, 0)),
                   pl.BlockSpec((None, SSD_STATE, SSD_INNER), lambda s: (seq_f(s), 0, 0))),
        scratch_shapes=scratch,
        compiler_params=_params("arbitrary"),
        name="ssd_forward_outputs",
    )(*([proj] * 7), small, *consts2, init, enter_b)
    return y, fin_f, fin_b


def _gla_log_decay(sm, gw_hi_ref, gw_lo_ref, gb_ref):
    s1, s2 = _split2(sm)
    logit = _dot(s1, gw_hi_ref[...]) + _dot(s2, gw_hi_ref[...]) + _dot(s1, gw_lo_ref[...]) + gb_ref[...]
    return -_softplus(-logit) * (1.0 / GLA_GATE_TAU)


def _chunk_rows(x, n_chunks, rows_per_chunk):
    return jnp.concatenate([jnp.broadcast_to(x[c:c + 1, :], (rows_per_chunk, x.shape[1]))
                            for c in range(n_chunks)], axis=0)


def _gla_bwd_kernel(qk_ref, v_ref, sm_ref, gwh_ref, gwl_ref, gb_ref, init_ref,
                    enter_ref, fin_ref, s_ref, *, regions, n_steps):
    cq = GLA_CHUNK
    st = SCAN_STEP
    ncs = st // cq
    c = n_steps - 1 - pl.program_id(0)
    _, first, last = _chunk_pos(c, st, regions)

    @pl.when(last)
    def _():
        s_ref[...] = init_ref[...]

    lg = _gla_log_decay(sm_ref[...], gwh_ref, gwl_ref, gb_ref)[:, GLA_KDIM:2 * GLA_KDIM]
    rev = _dot_exact_lhs(_block_tri(st, cq, True).astype(BF16), lg)
    tot = jnp.concatenate([rev[ci * cq:ci * cq + 1, :] for ci in range(ncs)], axis=0)
    ko = (qk_ref[:, GLA_KDIM:2 * GLA_KDIM].astype(F32) * jnp.exp(_chunk_rows(tot, ncs, cq) - rev)).astype(BF16)
    etot = jnp.exp(tot)
    lo_half = lax.broadcasted_iota(jnp.int32, (1, LANES), 1) < GLA_DK

    s_cur = s_ref[...]
    for ci in reversed(range(ncs)):
        rows = slice(ci * cq, (ci + 1) * cq)
        enter_ref[ci] = s_cur
        new = []
        for p in range(GLA_HEADS // 2):
            sl = slice(p * LANES, (p + 1) * LANES)
            u0 = _dot_tn(v_ref[rows, (2 * p) * GLA_DV:(2 * p + 1) * GLA_DV], ko[rows, sl])
            u1 = _dot_tn(v_ref[rows, (2 * p + 1) * GLA_DV:(2 * p + 2) * GLA_DV], ko[rows, sl])
            new.append(s_cur[:, sl] * etot[ci:ci + 1, sl] + jnp.where(lo_half, u0, u1))
        s_cur = jnp.concatenate(new, axis=1)
    s_ref[...] = s_cur

    @pl.when(first)
    def _():
        fin_ref[...] = s_cur


def _gla_out_kernel(qk_ref, v_ref, r_ref, sm_ref, gwh_ref, gwl_ref, gb_ref, ng_ref, init_ref, enter_ref,
                    y_ref, fin_ref, s_ref, *, regions):
    cq = GLA_CHUNK
    st = SCAN_STEP
    ncs = st // cq
    c = pl.program_id(0)
    _, first, last = _chunk_pos(c, st, regions)

    @pl.when(first)
    def _():
        s_ref[...] = init_ref[...]

    lg = _gla_log_decay(sm_ref[...], gwh_ref, gwl_ref, gb_ref)
    gcs = _dot_exact_lhs(_block_tri(st, cq, False).astype(BF16), lg[:, 0:GLA_KDIM])
    rev = _dot_exact_lhs(_block_tri(st, cq, True).astype(BF16), lg[:, GLA_KDIM:2 * GLA_KDIM])
    tot = jnp.concatenate([gcs[(ci + 1) * cq - 1:(ci + 1) * cq, :] for ci in range(ncs)], axis=0)
    qs = qk_ref[:, 0:GLA_KDIM].astype(F32) * (GLA_DK ** -0.5)
    ks = qk_ref[:, GLA_KDIM:2 * GLA_KDIM].astype(F32)
    qe_f = (qs * jnp.exp(gcs)).astype(BF16)
    ke_f = (ks * jnp.exp(-gcs)).astype(BF16)
    ko_f = (ks * jnp.exp(_chunk_rows(tot, ncs, cq) - gcs)).astype(BF16)
    qe_b = (qs * jnp.exp(rev)).astype(BF16)
    ke_b = (ks * jnp.exp(-rev)).astype(BF16)
    etot = jnp.exp(tot)

    lo_half = lax.broadcasted_iota(jnp.int32, (1, LANES), 1) < GLA_DK
    zero_b = jnp.zeros((cq, LANES), BF16)
    ii = lax.broadcasted_iota(jnp.int32, (2 * cq, cq), 0) % cq
    jj = lax.broadcasted_iota(jnp.int32, (2 * cq, cq), 1)
    tril2 = jj <= ii
    triu2 = jj >= ii
    ng = ng_ref[...]

    def stack_heads(x):
        return jnp.concatenate([jnp.where(lo_half, x, zero_b), jnp.where(lo_half, zero_b, x)], axis=0)

    s_cur = s_ref[...]
    for ci in range(ncs):
        rows = slice(ci * cq, (ci + 1) * cq)
        s_curb = s_cur.astype(BF16)
        s_entb = enter_ref[ci].astype(BF16)
        new = []
        for p in range(GLA_HEADS // 2):
            sl = slice(p * LANES, (p + 1) * LANES)
            qf2 = stack_heads(qe_f[rows, sl])
            qb2 = stack_heads(qe_b[rows, sl])
            att = (jnp.where(tril2, _dot_nt(qf2, ke_f[rows, sl]), 0.0)
                   + jnp.where(triu2, _dot_nt(qb2, ke_b[rows, sl]), 0.0)).astype(BF16)
            inter = _dot_nt(qf2, s_curb[:, sl]) + _dot_nt(qb2, s_entb[:, sl])
            upd = []
            for hh in (0, 1):
                h = 2 * p + hh
                hs = slice(h * GLA_DV, (h + 1) * GLA_DV)
                hr = slice(hh * cq, (hh + 1) * cq)
                vh = v_ref[rows, hs]
                o = _dot(att[hr], vh) + inter[hr]
                o = o * lax.rsqrt(jnp.mean(o * o, axis=-1, keepdims=True) + EPS) * ng[:, hs]
                y_ref[rows, hs] = (o * _silu(r_ref[rows, hs].astype(F32))).astype(y_ref.dtype)
                upd.append(_dot_tn(vh, ko_f[rows, sl]))
            new.append(s_cur[:, sl] * etot[ci:ci + 1, sl] + jnp.where(lo_half, upd[0], upd[1]))
        s_cur = jnp.concatenate(new, axis=1)
    s_ref[...] = s_cur

    @pl.when(last)
    def _():
        fin_ref[...] = s_cur


def _gla_mixer(proj, small, init, lw, regions):
    st = SCAN_STEP
    ncs = st // GLA_CHUNK
    t = proj.shape[0]
    n_steps = t // st
    n_seq = regions[0][0] + regions[1][0]
    qkcol, vcol, rcol = COL_QK // 1024, COL_V // 1024, COL_R // 1024
    state_block = (None, None, GLA_DV, GLA_KDIM)
    enter_block = (ncs, GLA_DV, GLA_KDIM)
    gwh, gwl, gb = lw["gate_w_hi"], lw["gate_w_lo"], lw["gate_b"]

    def const_spec(a):
        return pl.BlockSpec(a.shape, lambda s: (0,) * a.ndim)

    bidx = lambda s: n_steps - 1 - s
    seq_b = lambda s: _chunk_pos(bidx(s), st, regions)[0]
    enter_b, fin_b = pl.pallas_call(
        functools.partial(_gla_bwd_kernel, regions=regions, n_steps=n_steps),
        out_shape=(jax.ShapeDtypeStruct((n_steps * ncs, GLA_DV, GLA_KDIM), F32),
                   jax.ShapeDtypeStruct((n_seq, GLA_DV, GLA_KDIM), F32)),
        grid=(n_steps,),
        in_specs=[pl.BlockSpec((st, 2 * GLA_KDIM), lambda s: (bidx(s), qkcol)),
                  pl.BlockSpec((st, GLA_VDIM), lambda s: (bidx(s), vcol)),
                  pl.BlockSpec((st, LANES), lambda s: (bidx(s), 0)),
                  const_spec(gwh), const_spec(gwl), const_spec(gb),
                  pl.BlockSpec(state_block, lambda s: (seq_b(s), 1, 0, 0))],
        out_specs=(pl.BlockSpec(enter_block, lambda s: (bidx(s), 0, 0)),
                   pl.BlockSpec((None, GLA_DV, GLA_KDIM), lambda s: (seq_b(s), 0, 0))),
        scratch_shapes=[pltpu.VMEM((GLA_DV, GLA_KDIM), F32)],
        compiler_params=_params("arbitrary"),
        name="gla_backward_states",
    )(proj, proj, small, gwh, gwl, gb, init)

    seq_f = lambda s: _chunk_pos(s, st, regions)[0]
    ng = lw["gla_norm_g"]
    y, fin_f = pl.pallas_call(
        functools.partial(_gla_out_kernel, regions=regions),
        out_shape=(jax.ShapeDtypeStruct((t, GLA_VDIM), BF16),
                   jax.ShapeDtypeStruct((n_seq, GLA_DV, GLA_KDIM), F32)),
        grid=(n_steps,),
        in_specs=[pl.BlockSpec((st, 2 * GLA_KDIM), lambda s: (s, qkcol)),
                  pl.BlockSpec((st, GLA_VDIM), lambda s: (s, vcol)),
                  pl.BlockSpec((st, GLA_VDIM), lambda s: (s, rcol)),
                  pl.BlockSpec((st, LANES), lambda s: (s, 0)),
                  const_spec(gwh), const_spec(gwl), const_spec(gb), const_spec(ng),
                  pl.BlockSpec(state_block, lambda s: (seq_f(s), 0, 0, 0)),
                  pl.BlockSpec(enter_block, lambda s: (s, 0, 0))],
        out_specs=(pl.BlockSpec((st, GLA_VDIM), lambda s: (s, 0)),
                   pl.BlockSpec((None, GLA_DV, GLA_KDIM), lambda s: (seq_f(s), 0, 0))),
        scratch_shapes=[pltpu.VMEM((GLA_DV, GLA_KDIM), F32)],
        compiler_params=_params("arbitrary"),
        name="gla_forward_outputs",
    )(proj, proj, proj, small, gwh, gwl, gb, ng, init, enter_b)
    return y, fin_f, fin_b


def _merge_kernel(x_ref, ys_ref, yg_ref, ga_ref, gb_ref, mod_ref, wbs_ref, wbg_ref, wo_ref, o_ref):
    merged = (_sigmoid(ga_ref[...].astype(F32)) * _dot(ys_ref[...], wbs_ref[...])
              + _sigmoid(gb_ref[...].astype(F32)) * _dot(yg_ref[...], wbg_ref[...]))
    mix = _dot(merged.astype(BF16), wo_ref[...])
    o_ref[...] = x_ref[...] + mod_ref[:, 2 * D_MODEL:3 * D_MODEL] * mix


def _merge(x, y_ssd, y_gla, proj, mods, lw, regions):
    t = x.shape[0]
    tm = min(TM_MERGE, regions[0][0] * regions[0][1], regions[1][1])
    tok = lambda col: pl.BlockSpec((tm, D_MODEL), lambda i: (i, col))
    wspec = pl.BlockSpec((D_MODEL, D_MODEL), lambda i: (0, 0))
    return pl.pallas_call(
        _merge_kernel,
        out_shape=jax.ShapeDtypeStruct((t, D_MODEL), F32),
        grid=(t // tm,),
        in_specs=[tok(0), tok(0), tok(0), tok(COL_GA // D_MODEL), tok(COL_GB // D_MODEL),
                  pl.BlockSpec((None, 1, 6 * D_MODEL), lambda i: (_mod_row(i, tm, regions), 0, 0)),
                  wspec, wspec, wspec],
        out_specs=tok(0),
        compiler_params=_params("parallel"),
        name="merge_out_projection",
    )(x, y_ssd, y_gla, proj, proj, mods, lw["w_bs"], lw["w_bg"], lw["w_o"])


def _ffn_prologue(x_ref, mod_ref, g_ref):
    y = _rms(x_ref[...], g_ref[...])
    return y * (1.0 + mod_ref[:, 4 * D_MODEL:5 * D_MODEL]) + mod_ref[:, 3 * D_MODEL:4 * D_MODEL]


def _swiglu_partial(v, w1_ref, w3_ref, w2_ref):
    h = _silu(_dot(v, w1_ref[...])) * _dot(v, w3_ref[...])
    return _dot(h.astype(BF16), w2_ref[...])


def _ffn_kernel(x_ref, mod_ref, g_ref, w1_ref, w3_ref, w2_ref, o_ref, v_ref, acc_ref):
    k = pl.program_id(1)

    @pl.when(k == 0)
    def _():
        v_ref[...] = _ffn_prologue(x_ref, mod_ref, g_ref).astype(BF16)
        acc_ref[...] = jnp.zeros_like(acc_ref)

    acc_ref[...] += _swiglu_partial(v_ref[...], w1_ref, w3_ref, w2_ref)

    @pl.when(k == pl.num_programs(1) - 1)
    def _():
        o_ref[...] = x_ref[...] + mod_ref[:, 5 * D_MODEL:6 * D_MODEL] * acc_ref[...]


def _dense_ffn(x, mods, g, w1, w3, w2, regions):
    t = x.shape[0]
    tm = min(TM_FFN, regions[0][0] * regions[0][1], regions[1][1])
    hid = w1.shape[1]
    return pl.pallas_call(
        _ffn_kernel,
        out_shape=jax.ShapeDtypeStruct((t, D_MODEL), F32),
        grid=(t // tm, hid // TH_FFN),
        in_specs=[pl.BlockSpec((tm, D_MODEL), lambda i, k: (i, 0)),
                  pl.BlockSpec((None, 1, 6 * D_MODEL), lambda i, k: (_mod_row(i, tm, regions), 0, 0)),
                  pl.BlockSpec((1, D_MODEL), lambda i, k: (0, 0)),
                  pl.BlockSpec((D_MODEL, TH_FFN), lambda i, k: (0, k)),
                  pl.BlockSpec((D_MODEL, TH_FFN), lambda i, k: (0, k)),
                  pl.BlockSpec((TH_FFN, D_MODEL), lambda i, k: (k, 0))],
        out_specs=pl.BlockSpec((tm, D_MODEL), lambda i, k: (i, 0)),
        scratch_shapes=[pltpu.VMEM((tm, D_MODEL), BF16), pltpu.VMEM((tm, D_MODEL), F32)],
        compiler_params=_params("parallel", "arbitrary"),
        name="dense_swiglu",
    )(x, mods, g, w1, w3, w2)


def _top2_gates(logits):
    lane = lax.broadcasted_iota(jnp.int32, logits.shape, 1)
    lg = jnp.where(lane < N_EXPERTS, logits, -jnp.inf)
    m1 = jnp.max(lg, axis=-1, keepdims=True)
    i1 = jnp.min(jnp.where(lg == m1, lane, LANES), axis=-1, keepdims=True)
    lg2 = jnp.where(lane == i1, -jnp.inf, lg)
    m2 = jnp.max(lg2, axis=-1, keepdims=True)
    i2 = jnp.min(jnp.where(lg2 == m2, lane, LANES), axis=-1, keepdims=True)
    e2 = jnp.exp(m2 - m1)
    den = 1.0 + e2
    return jnp.where(lane == i1, 1.0 / den, 0.0) + jnp.where(lane == i2, e2 / den, 0.0)


def _expert_column(tab_ref, e):
    lane = lax.broadcasted_iota(jnp.int32, tab_ref.shape, 1)
    return jnp.sum(jnp.where(lane == e, tab_ref[...], 0.0), axis=-1, keepdims=True)


def _moe_kernel(x_ref, mod_ref, g_ref, rw_ref, rb_ref, w1_ref, w3_ref, w2_ref, o_ref,
                v_ref, gate_ref, rank_ref, gate_t_ref, rank_t_ref, xg_ref, y_ref):
    e = pl.program_id(1)
    k = pl.program_id(2)
    last_k = pl.num_programs(2) - 1
    tm = x_ref.shape[0]
    rb = MOE_ROW_BLOCK

    @pl.when(jnp.logical_and(e == 0, k == 0))
    def _():
        v = _ffn_prologue(x_ref, mod_ref, g_ref)
        v_ref[...] = v.astype(BF16)
        gate = _top2_gates(_dot_f32(v, rw_ref[...]) + rb_ref[...])
        sel = jnp.where(gate > 0.0, 1.0, 0.0).astype(BF16)
        gate_ref[...] = gate
        gate_t_ref[...] = gate.T
        for blk in range(tm // rb):
            i = lax.broadcasted_iota(jnp.int32, (rb, tm), 0) + blk * rb
            j = lax.broadcasted_iota(jnp.int32, (rb, tm), 1)
            rank = _dot(jnp.where(j < i, 1.0, 0.0).astype(BF16), sel)
            rank_ref[blk * rb:(blk + 1) * rb, :] = rank
            rank_t_ref[:, blk * rb:(blk + 1) * rb] = rank.T
        o_ref[...] = jnp.zeros_like(o_ref)

    g_row = gate_t_ref[pl.ds(e, 1), :]
    r_row = rank_t_ref[pl.ds(e, 1), :].astype(jnp.int32)
    n_sel = jnp.sum(jnp.where(g_row > 0.0, 1.0, 0.0)).astype(jnp.int32)
    n_blocks = (n_sel + rb - 1) // rb

    @pl.when(k == 0)
    def _():
        def gather(b, carry):
            row = lax.broadcasted_iota(jnp.int32, (rb, tm), 0) + b * rb
            onehot = jnp.where(jnp.logical_and(row == r_row, g_row > 0.0), 1.0, 0.0).astype(BF16)
            off = pl.multiple_of(b * rb, rb)
            xg_ref[pl.ds(off, rb), :] = _dot(onehot, v_ref[...]).astype(BF16)
            return carry
        lax.fori_loop(0, n_blocks, gather, 0)

    def expert(b, carry):
        off = pl.multiple_of(b * rb, rb)
        part = _swiglu_partial(xg_ref[pl.ds(off, rb), :], w1_ref, w3_ref, w2_ref)

        @pl.when(k == 0)
        def _():
            y_ref[pl.ds(off, rb), :] = part

        @pl.when(k > 0)
        def _():
            y_ref[pl.ds(off, rb), :] += part
        return carry
    lax.fori_loop(0, n_blocks, expert, 0)

    @pl.when(k == last_k)
    def _():
        g_col = _expert_column(gate_ref, e)
        r_col = _expert_column(rank_ref, e).astype(jnp.int32)

        def scatter(b, carry):
            col = lax.broadcasted_iota(jnp.int32, (tm, rb), 1) + b * rb
            onehot = jnp.where(jnp.logical_and(col == r_col, g_col > 0.0), 1.0, 0.0).astype(BF16)
            off = pl.multiple_of(b * rb, rb)
            o_ref[...] += g_col * _dot(onehot, y_ref[pl.ds(off, rb), :].astype(BF16))
            return carry
        lax.fori_loop(0, n_blocks, scatter, 0)

    @pl.when(jnp.logical_and(e == pl.num_programs(1) - 1, k == last_k))
    def _():
        o_ref[...] = x_ref[...] + mod_ref[:, 5 * D_MODEL:6 * D_MODEL] * o_ref[...]


def _moe_ffn(x, mods, g, rw, rb, w1, w3, w2, regions):
    t = x.shape[0]
    tm = min(TM_MOE, regions[0][0] * regions[0][1], regions[1][1])
    n_e, _, hid = w1.shape
    return pl.pallas_call(
        _moe_kernel,
        out_shape=jax.ShapeDtypeStruct((t, D_MODEL), F32),
        grid=(t // tm, n_e, hid // TH_FFN),
        in_specs=[pl.BlockSpec((tm, D_MODEL), lambda i, e, k: (i, 0)),
                  pl.BlockSpec((None, 1, 6 * D_MODEL), lambda i, e, k: (_mod_row(i, tm, regions), 0, 0)),
                  pl.BlockSpec((1, D_MODEL), lambda i, e, k: (0, 0)),
                  pl.BlockSpec((D_MODEL, LANES), lambda i, e, k: (0, 0)),
                  pl.BlockSpec((1, LANES), lambda i, e, k: (0, 0)),
                  pl.BlockSpec((None, D_MODEL, TH_FFN), lambda i, e, k: (e, 0, k)),
                  pl.BlockSpec((None, D_MODEL, TH_FFN), lambda i, e, k: (e, 0, k)),
                  pl.BlockSpec((None, TH_FFN, D_MODEL), lambda i, e, k: (e, k, 0))],
        out_specs=pl.BlockSpec((tm, D_MODEL), lambda i, e, k: (i, 0)),
        scratch_shapes=[pltpu.VMEM((tm, D_MODEL), BF16),
                        pltpu.VMEM((tm, LANES), F32), pltpu.VMEM((tm, LANES), F32),
                        pltpu.VMEM((LANES, tm), F32), pltpu.VMEM((LANES, tm), F32),
                        pltpu.VMEM((tm, D_MODEL), BF16), pltpu.VMEM((tm, D_MODEL), F32)],
        compiler_params=_params("parallel", "arbitrary", "arbitrary"),
        name="moe_swiglu",
    )(x, mods, g, rw, rb, w1, w3, w2)


def _final_norm_kernel(x_ref, g_ref, o_ref):
    o_ref[...] = _rms(x_ref[...], g_ref[...])


def _final_norm(x, g, tok_off, n_tok):
    tm = min(TM_NORM, n_tok)
    off = tok_off // tm
    return pl.pallas_call(
        _final_norm_kernel,
        out_shape=jax.ShapeDtypeStruct((n_tok, D_MODEL), F32),
        grid=(n_tok // tm,),
        in_specs=[pl.BlockSpec((tm, D_MODEL), lambda i: (i + off, 0)),
                  pl.BlockSpec((1, D_MODEL), lambda i: (0, 0))],
        out_specs=pl.BlockSpec((tm, D_MODEL), lambda i: (i, 0)),
        compiler_params=_params("parallel"),
        name="final_norm",
    )(x, g)


def _grid_pos_embed(l, d):
    rows = l // GRID_W
    row = jnp.repeat(jnp.arange(rows, dtype=F32), GRID_W)
    col = jnp.tile(jnp.arange(GRID_W, dtype=F32), rows)
    quarter = d // 4
    omega = jnp.exp(-math.log(10000.0) * jnp.arange(quarter, dtype=F32) / quarter)
    er = row[:, None] * omega
    ec = col[:, None] * omega
    return jnp.concatenate([jnp.sin(er), jnp.cos(er), jnp.sin(ec), jnp.cos(ec)], axis=-1)


def _pad_lanes(a, width):
    return jnp.pad(a, [(0, 0)] * (a.ndim - 1) + [(0, width - a.shape[-1])])


def _reorder_w_in(w):
    o = np.cumsum([0, SSD_INNER, SSD_INNER + 2 * SSD_GROUPS * SSD_STATE, 2 * SSD_HEADS, GLA_KDIM, GLA_KDIM,
                   GLA_VDIM, GLA_VDIM, 2 * GLA_GATE_RANK, D_MODEL, D_MODEL])
    z, xbc, dt, qq, kk, vv, rr, lr, ga, gb = [w[:, int(o[i]):int(o[i + 1])] for i in range(10)]
    x, bc = xbc[:, :SSD_INNER], xbc[:, SSD_INNER:]
    main = jnp.concatenate([z, x, vv, rr, ga, gb, qq, kk, bc], axis=1).astype(BF16)
    small = _pad_lanes(jnp.concatenate([dt, lr], axis=1), LANES).astype(BF16)
    return main, small


def _layer_weights(i, p):
    conv_w = jnp.pad(p["ssd_conv_w"][i], ((0, SUBLANES - SSD_CONV), (0, 0)))
    conv_b = p["ssd_conv_b"][i][None, :]
    gate_w = jnp.zeros((LANES, 2 * GLA_KDIM), F32)
    gate_w = gate_w.at[SM_LR:SM_LR + GLA_GATE_RANK, :GLA_KDIM].set(p["gla_gate_w"][i, 0])
    gate_w = gate_w.at[SM_LR + GLA_GATE_RANK:SM_LR + 2 * GLA_GATE_RANK, GLA_KDIM:].set(p["gla_gate_w"][i, 1])
    gate_w_hi = gate_w.astype(BF16)
    gate_w_lo = (gate_w - gate_w_hi.astype(F32)).astype(BF16)
    w_main, w_small = _reorder_w_in(p["w_in"][i])
    return {
        "w_in": w_main, "w_in_small": w_small,
        "conv_w_x": conv_w[:, :SSD_INNER], "conv_w_bc": conv_w[:, SSD_INNER:],
        "conv_b_x": conv_b[:, :SSD_INNER], "conv_b_bc": conv_b[:, SSD_INNER:],
        "dt_bias": _pad_lanes(p["ssd_dt_bias"][i].reshape(1, -1), LANES),
        "a_log": _pad_lanes(p["ssd_a_log"][i].reshape(1, -1), LANES),
        "d_exp": jnp.repeat(p["ssd_d"][i], SSD_HEADDIM)[None, :],
        "ssd_norm_g": p["ssd_norm_g"][i][None, :],
        "gate_w_hi": gate_w_hi, "gate_w_lo": gate_w_lo, "gate_b": p["gla_gate_b"][i].reshape(1, -1),
        "gla_norm_g": p["gla_norm_g"][i][None, :],
        "w_bs": p["w_branch_ssd"][i].astype(BF16), "w_bg": p["w_branch_gla"][i].astype(BF16),
        "w_o": p["w_out"][i].astype(BF16),
    }


def _ssd_state_in(s):
    b = s.shape[0]
    return jnp.transpose(s, (0, 1, 4, 2, 3)).reshape(b, 2, SSD_STATE, SSD_INNER)


def _ssd_state_out(f, b):
    s = jnp.stack([f, b], axis=1).reshape(-1, 2, SSD_STATE, SSD_HEADS, SSD_HEADDIM)
    return jnp.transpose(s, (0, 1, 3, 4, 2))


def _gla_state_in(s):
    b = s.shape[0]
    return jnp.transpose(s, (0, 1, 4, 2, 3)).reshape(b, 2, GLA_DV, GLA_KDIM)


def _gla_state_out(f, b):
    s = jnp.stack([f, b], axis=1).reshape(-1, 2, GLA_DV, GLA_HEADS, GLA_DK)
    return jnp.transpose(s, (0, 1, 3, 4, 2))


def _trunk(x_prompt, x_sample, state_ssd, state_gla, c, c_ctx, p):
    n0, l0, _ = x_prompt.shape
    n1, l1, _ = x_sample.shape
    regions = ((n0, l0), (n1, l1))
    depth = p["w_in"].shape[0]

    cc = jnp.zeros((MOD_ROWS, D_MODEL), F32).at[0].set(c_ctx).at[1:1 + n1].set(c)
    mods = _modulation_table(cc, p["ada_w"], p["ada_b"])[:, :, None, :]
    x = _assemble_tokens(x_prompt, x_sample, _grid_pos_embed(l1, D_MODEL), regions)

    ssd_states, gla_states = [], []
    for i in range(depth):
        lw = _layer_weights(i, p)
        init_ssd = jnp.concatenate([jnp.zeros((n0, 2, SSD_STATE, SSD_INNER), F32),
                                    _ssd_state_in(state_ssd[:, i])], axis=0)
        init_gla = jnp.concatenate([jnp.zeros((n0, 2, GLA_DV, GLA_KDIM), F32),
                                    _gla_state_in(state_gla[:, i])], axis=0)
        proj, small = _in_projection(x, mods[i], p["norm1_g"][i][None, :], lw["w_in"], lw["w_in_small"], regions)
        y_ssd, sf, sb = _ssd_mixer(proj, small, init_ssd, lw, regions)
        y_gla, gf, gb = _gla_mixer(proj, small, init_gla, lw, regions)
        ssd_states.append(_ssd_state_out(sf[:n0], sb[:n0]))
        gla_states.append(_gla_state_out(gf[:n0], gb[:n0]))
        x = _merge(x, y_ssd, y_gla, proj, mods[i], lw, regions)
        j = i // 2
        g2 = p["norm2_g"][i][None, :]
        if i % 2 == 0:
            x = _dense_ffn(x, mods[i], g2, p["ffn_w1"][j].astype(BF16), p["ffn_w3"][j].astype(BF16),
                           p["ffn_w2"][j].astype(BF16), regions)
        else:
            x = _moe_ffn(x, mods[i], g2, _pad_lanes(p["router_w"][j], LANES),
                         _pad_lanes(p["router_b"][j][None, :], LANES),
                         p["moe_w1"][j].astype(BF16), p["moe_w3"][j].astype(BF16),
                         p["moe_w2"][j].astype(BF16), regions)

    gfin = p["final_norm_g"][None, :]
    y_prompt = _final_norm(x, gfin, 0, n0 * l0).reshape(n0, l0, D_MODEL)
    y_sample = _final_norm(x, gfin, n0 * l0, n1 * l1).reshape(n1, l1, D_MODEL)
    return y_prompt, y_sample, jnp.stack(ssd_states, axis=1), jnp.stack(gla_states, axis=1)


def kernel(x_prompt, x_sample, state_ssd, state_gla, c, c_ctx, ada_w, ada_b, norm1_g, norm2_g, w_in, ssd_conv_w, ssd_conv_b, ssd_dt_bias, ssd_a_log, ssd_d, ssd_norm_g, gla_gate_w, gla_gate_b, gla_norm_g, w_branch_ssd, w_branch_gla, w_out, ffn_w1, ffn_w3, ffn_w2, router_w, router_b, moe_w1, moe_w3, moe_w2, final_norm_g):
    p = dict(ada_w=ada_w, ada_b=ada_b, norm1_g=norm1_g, norm2_g=norm2_g, w_in=w_in, ssd_conv_w=ssd_conv_w,
             ssd_conv_b=ssd_conv_b, ssd_dt_bias=ssd_dt_bias, ssd_a_log=ssd_a_log, ssd_d=ssd_d,
             ssd_norm_g=ssd_norm_g, gla_gate_w=gla_gate_w, gla_gate_b=gla_gate_b, gla_norm_g=gla_norm_g,
             w_branch_ssd=w_branch_ssd, w_branch_gla=w_branch_gla, w_out=w_out, ffn_w1=ffn_w1, ffn_w3=ffn_w3,
             ffn_w2=ffn_w2, router_w=router_w, router_b=router_b, moe_w1=moe_w1, moe_w3=moe_w3, moe_w2=moe_w2,
             final_norm_g=final_norm_g)
    return _trunk(x_prompt, x_sample, state_ssd, state_gla, c, c_ctx, p)
```

```python
import functools
import math

import numpy as np
import jax
import jax.numpy as jnp
from jax import lax
from jax.experimental import pallas as pl
from jax.experimental.pallas import tpu as pltpu

F32 = jnp.float32
BF16 = jnp.bfloat16

D_MODEL = 1024
GRID_W = 64
EPS = 1e-6
SSD_HEADS = 16
SSD_HEADDIM = 64
SSD_INNER = SSD_HEADS * SSD_HEADDIM
SSD_GROUPS = 2
SSD_STATE = 64
SSD_CONV = 5
SSD_CHUNK = 128
GLA_HEADS = 8
GLA_DK = 64
GLA_DV = 128
GLA_KDIM = GLA_HEADS * GLA_DK
GLA_VDIM = GLA_HEADS * GLA_DV
GLA_GATE_RANK = 16
GLA_GATE_TAU = 16.0
GLA_CHUNK = 64
N_EXPERTS = 8

LANES = 128
SUBLANES = 8
BF16_SUBLANES = 16
VMEM_LIMIT_BYTES = 56 * 1024 * 1024

COL_Z, COL_X, COL_V, COL_R, COL_GA, COL_GB = 0, 1024, 2048, 3072, 4096, 5120
COL_QK = 6144
COL_BC = 7168
D_PROJ = 7424
SM_LR = 2 * SSD_HEADS

TM_PROJ, TN_PROJ = 1024, 3712
TM_MERGE = 512
TM_FFN, TH_FFN = 512, 1408
TM_MOE = 1024
MOE_ROW_BLOCK = 256
MOE_ROW_ALIGN = 16
TR_EXPERT = 512
TM_NORM = 1024
TN_MODS = 1536
MOD_ROWS = 16
SCAN_STEP = 256


def _params(*sem):
    return pltpu.CompilerParams(dimension_semantics=sem, vmem_limit_bytes=VMEM_LIMIT_BYTES)


def _sigmoid(x):
    return 0.5 * jnp.tanh(0.5 * x) + 0.5


def _silu(x):
    return x * _sigmoid(x)


def _softplus(x):
    return jnp.maximum(x, 0.0) + jnp.log(1.0 + jnp.exp(-jnp.abs(x)))


def _split2(a):
    a1 = a.astype(BF16)
    a2 = (a - a1.astype(F32)).astype(BF16)
    return a1, a2


def _split3(a):
    a1 = a.astype(BF16)
    r1 = a - a1.astype(F32)
    a2 = r1.astype(BF16)
    a3 = (r1 - a2.astype(F32)).astype(BF16)
    return a1, a2, a3


def _dot(a, b):
    return jnp.dot(a, b, preferred_element_type=F32)


def _dot_nt(a, b):
    return lax.dot_general(a, b, (((1,), (1,)), ((), ())), preferred_element_type=F32)


def _dot_tn(a, b):
    return lax.dot_general(a, b, (((0,), (0,)), ((), ())), preferred_element_type=F32)


def _dot_exact_lhs(t01, a):
    a1, a2 = _split2(a)
    return _dot(t01, a1) + _dot(t01, a2)


def _dot_f32(a, b):
    a1, a2, a3 = _split3(a)
    b1, b2, b3 = _split3(b)
    return (_dot(a1, b1) + _dot(a1, b2) + _dot(a2, b1)
            + _dot(a1, b3) + _dot(a2, b2) + _dot(a3, b1))


def _block_tri(n, blk, upper):
    i = lax.broadcasted_iota(jnp.int32, (n, n), 0)
    j = lax.broadcasted_iota(jnp.int32, (n, n), 1)
    tri = (j >= i) if upper else (j <= i)
    if blk == n:
        return tri
    return jnp.logical_and(tri, (i // blk) == (j // blk))


def _rms(x, g):
    return x * lax.rsqrt(jnp.mean(x * x, axis=-1, keepdims=True) + EPS) * g


def _chunk_pos(c, q, regions):
    (n0, l0), (n1, l1) = regions
    per0, per1 = l0 // q, l1 // q
    nc0 = n0 * per0
    c1 = jnp.maximum(c - nc0, 0)
    in0 = c < nc0
    seq = jnp.where(in0, c // per0, n0 + c1 // per1)
    pos = jnp.where(in0, c % per0, c1 % per1)
    last = jnp.where(in0, per0 - 1, per1 - 1)
    return seq, pos == 0, pos == last


def _mod_row(i, tm, regions):
    (n0, l0), (n1, l1) = regions
    t1 = jnp.maximum(i * tm - n0 * l0, 0)
    return jnp.where(i * tm < n0 * l0, 0, 1 + t1 // l1)


def _assemble_kernel(xp_ref, xs_ref, pe_ref, o_ref, *, n_ctx_tiles):
    i = pl.program_id(0)

    @pl.when(i < n_ctx_tiles)
    def _():
        o_ref[...] = xp_ref[...]

    @pl.when(i >= n_ctx_tiles)
    def _():
        o_ref[...] = xs_ref[...] + pe_ref[...]


def _assemble_tokens(xp, xs, pe, regions):
    (n0, l0), (n1, l1) = regions
    tm = min(1024, l0 * n0, l1)
    t0, t1 = n0 * l0, n1 * l1
    n_ctx = t0 // tm
    pe_tiles = l1 // tm
    return pl.pallas_call(
        functools.partial(_assemble_kernel, n_ctx_tiles=n_ctx),
        out_shape=jax.ShapeDtypeStruct((t0 + t1, D_MODEL), F32),
        grid=((t0 + t1) // tm,),
        in_specs=[
            pl.BlockSpec((tm, D_MODEL), lambda i: (jnp.minimum(i, n_ctx - 1), 0)),
            pl.BlockSpec((tm, D_MODEL), lambda i: (jnp.maximum(i - n_ctx, 0), 0)),
            pl.BlockSpec((tm, D_MODEL), lambda i: (jnp.maximum(i - n_ctx, 0) % pe_tiles, 0)),
        ],
        out_specs=pl.BlockSpec((tm, D_MODEL), lambda i: (i, 0)),
        compiler_params=_params("arbitrary"),
        name="assemble_tokens",
    )(xp.reshape(t0, D_MODEL), xs.reshape(t1, D_MODEL), pe)


def _mods_kernel(cc_ref, w_ref, b_ref, o_ref):
    a = _silu(cc_ref[...])
    o_ref[...] = _dot_f32(a, w_ref[...]) + b_ref[...]


def _modulation_table(cc, ada_w, ada_b):
    depth = ada_w.shape[0]
    n = ada_w.shape[2]
    return pl.pallas_call(
        _mods_kernel,
        out_shape=jax.ShapeDtypeStruct((depth, MOD_ROWS, n), F32),
        grid=(depth, n // TN_MODS),
        in_specs=[
            pl.BlockSpec((MOD_ROWS, D_MODEL), lambda l, j: (0, 0)),
            pl.BlockSpec((None, D_MODEL, TN_MODS), lambda l, j: (l, 0, j)),
            pl.BlockSpec((None, 1, TN_MODS), lambda l, j: (l, 0, j)),
        ],
        out_specs=pl.BlockSpec((None, MOD_ROWS, TN_MODS), lambda l, j: (l, 0, j)),
        compiler_params=_params("arbitrary", "arbitrary"),
        name="modulation_table",
    )(cc, ada_w, ada_b.reshape(depth, 1, n))


def _inproj_kernel(x_ref, mod_ref, g_ref, w_ref, ws_ref, o_ref, os_ref, u_ref):
    @pl.when(pl.program_id(1) == 0)
    def _():
        y = _rms(x_ref[...], g_ref[...])
        u = y * (1.0 + mod_ref[:, D_MODEL:2 * D_MODEL]) + mod_ref[:, 0:D_MODEL]
        u_ref[...] = u.astype(BF16)
        os_ref[...] = _dot(u_ref[...], ws_ref[...])

    o_ref[...] = _dot(u_ref[...], w_ref[...]).astype(BF16)


def _in_projection(x, mods, g, w, w_small, regions):
    t = x.shape[0]
    tm = min(TM_PROJ, regions[0][0] * regions[0][1], regions[1][1])
    return pl.pallas_call(
        _inproj_kernel,
        out_shape=(jax.ShapeDtypeStruct((t, D_PROJ), BF16), jax.ShapeDtypeStruct((t, LANES), F32)),
        grid=(t // tm, D_PROJ // TN_PROJ),
        in_specs=[
            pl.BlockSpec((tm, D_MODEL), lambda i, j: (i, 0)),
            pl.BlockSpec((None, 1, 6 * D_MODEL), lambda i, j: (_mod_row(i, tm, regions), 0, 0)),
            pl.BlockSpec((1, D_MODEL), lambda i, j: (0, 0)),
            pl.BlockSpec((D_MODEL, TN_PROJ), lambda i, j: (0, j)),
            pl.BlockSpec((D_MODEL, LANES), lambda i, j: (0, 0)),
        ],
        out_specs=(pl.BlockSpec((tm, TN_PROJ), lambda i, j: (i, j)),
                   pl.BlockSpec((tm, LANES), lambda i, j: (i, 0))),
        scratch_shapes=[pltpu.VMEM((tm, D_MODEL), BF16)],
        compiler_params=_params("parallel", "arbitrary"),
        name="norm_in_projection",
    )(x, mods, g, w, w_small)


HALO = BF16_SUBLANES
CONV_PAD = SSD_CONV // 2


def _conv_silu(cur_ref, prev_ref, next_ref, ext_ref, w_ref, b_ref, first, last, n):
    ext_ref[0:HALO, :] = jnp.where(first, 0.0, prev_ref[...].astype(F32))
    ext_ref[HALO:HALO + n, :] = cur_ref[...].astype(F32)
    ext_ref[HALO + n:2 * HALO + n, :] = jnp.where(last, 0.0, next_ref[...].astype(F32))
    acc = b_ref[...]
    for k in range(SSD_CONV):
        off = HALO - CONV_PAD + k
        acc = acc + w_ref[k:k + 1, :] * ext_ref[off:off + n, :]
    return _silu(acc)


def _ssd_decay_terms(sm, dtb, alog, q):
    lane = lax.broadcasted_iota(jnp.int32, (1, LANES), 1)
    dtv = _softplus(sm + dtb)
    a_neg = jnp.where(lane < 2 * SSD_HEADS, -jnp.exp(alog), 0.0)
    a = dtv * a_neg
    acs = _dot_exact_lhs(_block_tri(q, q, False).astype(BF16), a)
    rev = _dot_exact_lhs(_block_tri(q, q, True).astype(BF16), a)
    m = jnp.where(lane < SSD_HEADS, acs, rev)
    tot = acs[q - 1:q, :]
    return m, tot, dtv


def _col(x, idx, width=LANES):
    return jnp.broadcast_to(x[:, idx:idx + 1], (x.shape[0], width))


def _pair_cols(x, base, p, lo_half):
    return jnp.where(lo_half, _col(x, base + 2 * p), _col(x, base + 2 * p + 1))


def _expand_state(s):
    lane = lax.broadcasted_iota(jnp.int32, s.shape, 1)
    half = SSD_INNER // SSD_GROUPS
    return jnp.concatenate([jnp.where(lane < half, s, 0.0), jnp.where(lane >= half, s, 0.0)], axis=0)


def _compact_state(s2):
    return s2[0:SSD_STATE, :] + s2[SSD_STATE:2 * SSD_STATE, :]


def _state_update_mask():
    row = lax.broadcasted_iota(jnp.int32, (SSD_GROUPS * SSD_STATE, SSD_INNER), 0)
    lane = lax.broadcasted_iota(jnp.int32, (SSD_GROUPS * SSD_STATE, SSD_INNER), 1)
    return (row < SSD_STATE) == (lane < SSD_INNER // SSD_GROUPS)


def _ssd_bwd_kernel(x_ref, xp_ref, xn_ref, bc_ref, bcp_ref, bcn_ref, sm_ref,
                    cwx_ref, cwbc_ref, cbx_ref, cbbc_ref, dtb_ref, alog_ref, init_ref,
                    enter_ref, fin_ref, s_ref, xe_ref, bce_ref, *, regions, n_steps):
    q = SSD_CHUNK
    c = n_steps - 1 - pl.program_id(0)
    _, first, last = _chunk_pos(c, SCAN_STEP, regions)

    @pl.when(last)
    def _():
        s_ref[...] = _expand_state(init_ref[...])

    xs_all = _conv_silu(x_ref, xp_ref, xn_ref, xe_ref, cwx_ref, cbx_ref, first, last, SCAN_STEP)
    bcs_all = _conv_silu(bc_ref, bcp_ref, bcn_ref, bce_ref, cwbc_ref, cbbc_ref, first, last, SCAN_STEP)
    lo_half = lax.broadcasted_iota(jnp.int32, (1, LANES), 1) < SSD_HEADDIM
    upd_mask = _state_update_mask()

    s_cur = s_ref[...]
    for ci in reversed(range(SCAN_STEP // q)):
        rows = slice(ci * q, (ci + 1) * q)
        xs = xs_all[rows]
        bsb = bcs_all[rows, 0:LANES].astype(BF16)
        m, tot, dtv = _ssd_decay_terms(sm_ref[rows, :], dtb_ref[...], alog_ref[...], q)
        wgt = jnp.exp(tot - m) * dtv
        etot = jnp.exp(tot)
        enter_ref[ci] = _compact_state(s_cur)
        xw, dec = [], []
        for p in range(SSD_HEADS // 2):
            sl = slice(p * LANES, (p + 1) * LANES)
            xw.append((xs[:, sl] * _pair_cols(wgt, SSD_HEADS, p, lo_half)).astype(BF16))
            dec.append(_pair_cols(etot, SSD_HEADS, p, lo_half))
        upd = _dot_tn(bsb, jnp.concatenate(xw, axis=1))
        s_cur = s_cur * jnp.concatenate(dec, axis=1) + jnp.where(upd_mask, upd, 0.0)
    s_ref[...] = s_cur

    @pl.when(first)
    def _():
        fin_ref[...] = _compact_state(s_cur)


def _ssd_out_kernel(z_ref, x_ref, xp_ref, xn_ref, bc_ref, bcp_ref, bcn_ref, sm_ref,
                    cwx_ref, cwbc_ref, cbx_ref, cbbc_ref, dtb_ref, alog_ref, dexp_ref, ng_ref,
                    init_ref, enter_ref, y_ref, fin_ref, s_ref, xe_ref, bce_ref, *, regions):
    q = SSD_CHUNK
    c = pl.program_id(0)
    _, first, last = _chunk_pos(c, SCAN_STEP, regions)

    @pl.when(first)
    def _():
        s_ref[...] = _expand_state(init_ref[...])

    xs_all = _conv_silu(x_ref, xp_ref, xn_ref, xe_ref, cwx_ref, cbx_ref, first, last, SCAN_STEP)
    bcs_all = _conv_silu(bc_ref, bcp_ref, bcn_ref, bce_ref, cwbc_ref, cbbc_ref, first, last, SCAN_STEP)

    lane = lax.broadcasted_iota(jnp.int32, (1, LANES), 1)
    lo_half = lane < SSD_HEADDIM
    tril = _block_tri(q, q, False)
    ii = lax.broadcasted_iota(jnp.int32, (q, q), 0)
    jj = lax.broadcasted_iota(jnp.int32, (q, q), 1)
    eye = ii == jj
    upd_mask = _state_update_mask()
    dexp = dexp_ref[...]
    ng = ng_ref[...]

    s_cur = s_ref[...]
    for ci in range(SCAN_STEP // q):
        rows = slice(ci * q, (ci + 1) * q)
        xs = xs_all[rows]
        bsb = bcs_all[rows, 0:LANES].astype(BF16)
        csb = bcs_all[rows, LANES:2 * LANES].astype(BF16)
        m, tot, dtv = _ssd_decay_terms(sm_ref[rows, :], dtb_ref[...], alog_ref[...], q)
        wgt = jnp.exp(tot - m) * dtv
        mt = m.T
        dtt = dtv.T
        zero_b = jnp.zeros_like(csb)
        cb = [_dot_nt(jnp.where(lo_half, csb, zero_b), bsb), _dot_nt(jnp.where(lo_half, zero_b, csb), bsb)]
        cs_f = _dot(csb, s_cur.astype(BF16))
        cs_b = _dot(csb, _expand_state(enter_ref[ci]).astype(BF16))

        ys, xw, dec = [], [], []
        for p in range(SSD_HEADS // 2):
            sl = slice(p * LANES, (p + 1) * LANES)
            g = (2 * p) // (SSD_HEADS // SSD_GROUPS)
            xs_p = xs[:, sl]
            xs_pb = xs_p.astype(BF16)
            yd, colf, colb = [], [], []
            for h in (2 * p, 2 * p + 1):
                hb = SSD_HEADS + h
                cf = _col(m, h, q)
                cbk = _col(m, hb, q)
                seg = jnp.where(tril, cf - mt[h:h + 1, :], cbk - mt[hb:hb + 1, :])
                dl = jnp.exp(seg) * jnp.where(tril, dtt[h:h + 1, :], dtt[hb:hb + 1, :])
                dl = jnp.where(eye, dl + dtt[hb:hb + 1, :], dl)
                yd.append(_dot((cb[g] * dl).astype(BF16), xs_pb))
                colf.append(cf)
                colb.append(cbk)
            ef = jnp.exp(jnp.where(lo_half, colf[0], colf[1]))
            eb = jnp.exp(jnp.where(lo_half, colb[0], colb[1]))
            y_p = jnp.where(lo_half, yd[0], yd[1]) + ef * cs_f[:, sl] + eb * cs_b[:, sl] + dexp[:, sl] * xs_p
            ys.append(y_p)
            xw.append((xs_p * _pair_cols(wgt, 0, p, lo_half)).astype(BF16))
            dec.append(ef[q - 1:q, :])

        y = jnp.concatenate(ys, axis=1) * _silu(z_ref[rows, :].astype(F32))
        y_ref[rows, :] = _rms(y, ng).astype(y_ref.dtype)

        upd = _dot_tn(bsb, jnp.concatenate(xw, axis=1))
        s_cur = s_cur * jnp.concatenate(dec, axis=1) + jnp.where(upd_mask, upd, 0.0)
    s_ref[...] = s_cur

    @pl.when(last)
    def _():
        fin_ref[...] = _compact_state(s_cur)


def _ssd_mixer(proj, small, init, lw, regions):
    st = SCAN_STEP
    cps = st // SSD_CHUNK
    t = proj.shape[0]
    n_steps = t // st
    n_seq = regions[0][0] + regions[1][0]
    rb = st // HALO
    n_rb = t // HALO
    xcol, bccol = COL_X // SSD_INNER, COL_BC // (2 * LANES)

    def chunk_specs(cidx):
        prev = lambda s: jnp.maximum(cidx(s) * rb - 1, 0)
        nxt = lambda s: jnp.minimum((cidx(s) + 1) * rb, n_rb - 1)
        return [
            pl.BlockSpec((st, SSD_INNER), lambda s: (cidx(s), xcol)),
            pl.BlockSpec((HALO, SSD_INNER), lambda s: (prev(s), xcol)),
            pl.BlockSpec((HALO, SSD_INNER), lambda s: (nxt(s), xcol)),
            pl.BlockSpec((st, 2 * LANES), lambda s: (cidx(s), bccol)),
            pl.BlockSpec((HALO, 2 * LANES), lambda s: (prev(s), bccol)),
            pl.BlockSpec((HALO, 2 * LANES), lambda s: (nxt(s), bccol)),
            pl.BlockSpec((st, LANES), lambda s: (cidx(s), 0)),
        ]

    def const_spec(a):
        return pl.BlockSpec(a.shape, lambda s: (0,) * a.ndim)

    consts = [lw["conv_w_x"], lw["conv_w_bc"], lw["conv_b_x"], lw["conv_b_bc"], lw["dt_bias"], lw["a_log"]]
    scratch = [pltpu.VMEM((SSD_GROUPS * SSD_STATE, SSD_INNER), F32),
               pltpu.VMEM((st + 2 * HALO, SSD_INNER), F32),
               pltpu.VMEM((st + 2 * HALO, 2 * LANES), F32)]
    state_block = (None, None, SSD_STATE, SSD_INNER)
    enter_block = (cps, SSD_STATE, SSD_INNER)

    bidx = lambda s: n_steps - 1 - s
    seq_b = lambda s: _chunk_pos(bidx(s), st, regions)[0]
    enter_b, fin_b = pl.pallas_call(
        functools.partial(_ssd_bwd_kernel, regions=regions, n_steps=n_steps),
        out_shape=(jax.ShapeDtypeStruct((n_steps * cps, SSD_STATE, SSD_INNER), F32),
                   jax.ShapeDtypeStruct((n_seq, SSD_STATE, SSD_INNER), F32)),
        grid=(n_steps,),
        in_specs=chunk_specs(bidx)[:-1] + [pl.BlockSpec((st, LANES), lambda s: (bidx(s), 0))]
        + [const_spec(a) for a in consts]
        + [pl.BlockSpec(state_block, lambda s: (seq_b(s), 1, 0, 0))],
        out_specs=(pl.BlockSpec(enter_block, lambda s: (bidx(s), 0, 0)),
                   pl.BlockSpec((None, SSD_STATE, SSD_INNER), lambda s: (seq_b(s), 0, 0))),
        scratch_shapes=scratch,
        compiler_params=_params("arbitrary"),
        name="ssd_backward_states",
    )(*([proj] * 6), small, *consts, init)

    fidx = lambda s: s
    seq_f = lambda s: _chunk_pos(s, st, regions)[0]
    consts2 = consts + [lw["d_exp"], lw["ssd_norm_g"]]
    y, fin_f = pl.pallas_call(
        functools.partial(_ssd_out_kernel, regions=regions),
        out_shape=(jax.ShapeDtypeStruct((t, SSD_INNER), BF16),
                   jax.ShapeDtypeStruct((n_seq, SSD_STATE, SSD_INNER), F32)),
        grid=(n_steps,),
        in_specs=[pl.BlockSpec((st, SSD_INNER), lambda s: (s, COL_Z // SSD_INNER))]
        + chunk_specs(fidx) + [const_spec(a) for a in consts2]
        + [pl.BlockSpec(state_block, lambda s: (seq_f(s), 0, 0, 0)),
           pl.BlockSpec(enter_block, lambda s: (s, 0, 0))],
        out_specs=(pl.BlockSpec((st, SSD_INNER), lambda s: (s, 0)),
                   pl.BlockSpec((None, SSD_STATE, SSD_INNER), lambda s: (seq_f(s), 0, 0))),
        scratch_shapes=scratch,
        compiler_params=_params("arbitrary"),
        name="ssd_forward_outputs",
    )(*([proj] * 7), small, *consts2, init, enter_b)
    return y, fin_f, fin_b


def _gla_log_decay(sm, gw_hi_ref, gw_lo_ref, gb_ref):
    s1, s2 = _split2(sm)
    logit = _dot(s1, gw_hi_ref[...]) + _dot(s2, gw_hi_ref[...]) + _dot(s1, gw_lo_ref[...]) + gb_ref[...]
    return -_softplus(-logit) * (1.0 / GLA_GATE_TAU)


def _chunk_rows(x, n_chunks, rows_per_chunk):
    return jnp.concatenate([jnp.broadcast_to(x[c:c + 1, :], (rows_per_chunk, x.shape[1]))
                            for c in range(n_chunks)], axis=0)


def _gla_bwd_kernel(qk_ref, v_ref, sm_ref, gwh_ref, gwl_ref, gb_ref, init_ref,
                    enter_ref, fin_ref, s_ref, *, regions, n_steps):
    cq = GLA_CHUNK
    st = SCAN_STEP
    ncs = st // cq
    c = n_steps - 1 - pl.program_id(0)
    _, first, last = _chunk_pos(c, st, regions)

    @pl.when(last)
    def _():
        s_ref[...] = init_ref[...]

    lg = _gla_log_decay(sm_ref[...], gwh_ref, gwl_ref, gb_ref)[:, GLA_KDIM:2 * GLA_KDIM]
    rev = _dot_exact_lhs(_block_tri(st, cq, True).astype(BF16), lg)
    tot = jnp.concatenate([rev[ci * cq:ci * cq + 1, :] for ci in range(ncs)], axis=0)
    ko = (qk_ref[:, GLA_KDIM:2 * GLA_KDIM].astype(F32) * jnp.exp(_chunk_rows(tot, ncs, cq) - rev)).astype(BF16)
    etot = jnp.exp(tot)
    lo_half = lax.broadcasted_iota(jnp.int32, (1, LANES), 1) < GLA_DK

    s_cur = s_ref[...]
    for ci in reversed(range(ncs)):
        rows = slice(ci * cq, (ci + 1) * cq)
        enter_ref[ci] = s_cur
        new = []
        for p in range(GLA_HEADS // 2):
            sl = slice(p * LANES, (p + 1) * LANES)
            u0 = _dot_tn(v_ref[rows, (2 * p) * GLA_DV:(2 * p + 1) * GLA_DV], ko[rows, sl])
            u1 = _dot_tn(v_ref[rows, (2 * p + 1) * GLA_DV:(2 * p + 2) * GLA_DV], ko[rows, sl])
            new.append(s_cur[:, sl] * etot[ci:ci + 1, sl] + jnp.where(lo_half, u0, u1))
        s_cur = jnp.concatenate(new, axis=1)
    s_ref[...] = s_cur

    @pl.when(first)
    def _():
        fin_ref[...] = s_cur


def _gla_out_kernel(qk_ref, v_ref, r_ref, sm_ref, gwh_ref, gwl_ref, gb_ref, ng_ref, init_ref, enter_ref,
                    y_ref, fin_ref, s_ref, *, regions):
    cq = GLA_CHUNK
    st = SCAN_STEP
    ncs = st // cq
    c = pl.program_id(0)
    _, first, last = _chunk_pos(c, st, regions)

    @pl.when(first)
    def _():
        s_ref[...] = init_ref[...]

    lg = _gla_log_decay(sm_ref[...], gwh_ref, gwl_ref, gb_ref)
    gcs = _dot_exact_lhs(_block_tri(st, cq, False).astype(BF16), lg[:, 0:GLA_KDIM])
    rev = _dot_exact_lhs(_block_tri(st, cq, True).astype(BF16), lg[:, GLA_KDIM:2 * GLA_KDIM])
    tot = jnp.concatenate([gcs[(ci + 1) * cq - 1:(ci + 1) * cq, :] for ci in range(ncs)], axis=0)
    qs = qk_ref[:, 0:GLA_KDIM].astype(F32) * (GLA_DK ** -0.5)
    ks = qk_ref[:, GLA_KDIM:2 * GLA_KDIM].astype(F32)
    qe_f = (qs * jnp.exp(gcs)).astype(BF16)
    ke_f = (ks * jnp.exp(-gcs)).astype(BF16)
    ko_f = (ks * jnp.exp(_chunk_rows(tot, ncs, cq) - gcs)).astype(BF16)
    qe_b = (qs * jnp.exp(rev)).astype(BF16)
    ke_b = (ks * jnp.exp(-rev)).astype(BF16)
    etot = jnp.exp(tot)

    lo_half = lax.broadcasted_iota(jnp.int32, (1, LANES), 1) < GLA_DK
    zero_b = jnp.zeros((cq, LANES), BF16)
    ii = lax.broadcasted_iota(jnp.int32, (2 * cq, cq), 0) % cq
    jj = lax.broadcasted_iota(jnp.int32, (2 * cq, cq), 1)
    tril2 = jj <= ii
    triu2 = jj >= ii
    ng = ng_ref[...]

    def stack_heads(x):
        return jnp.concatenate([jnp.where(lo_half, x, zero_b), jnp.where(lo_half, zero_b, x)], axis=0)

    s_cur = s_ref[...]
    for ci in range(ncs):
        rows = slice(ci * cq, (ci + 1) * cq)
        s_curb = s_cur.astype(BF16)
        s_entb = enter_ref[ci].astype(BF16)
        new = []
        for p in range(GLA_HEADS // 2):
            sl = slice(p * LANES, (p + 1) * LANES)
            qf2 = stack_heads(qe_f[rows, sl])
            qb2 = stack_heads(qe_b[rows, sl])
            att = (jnp.where(tril2, _dot_nt(qf2, ke_f[rows, sl]), 0.0)
                   + jnp.where(triu2, _dot_nt(qb2, ke_b[rows, sl]), 0.0)).astype(BF16)
            inter = _dot_nt(qf2, s_curb[:, sl]) + _dot_nt(qb2, s_entb[:, sl])
            upd = []
            for hh in (0, 1):
                h = 2 * p + hh
                hs = slice(h * GLA_DV, (h + 1) * GLA_DV)
                hr = slice(hh * cq, (hh + 1) * cq)
                vh = v_ref[rows, hs]
                o = _dot(att[hr], vh) + inter[hr]
                o = o * lax.rsqrt(jnp.mean(o * o, axis=-1, keepdims=True) + EPS) * ng[:, hs]
                y_ref[rows, hs] = (o * _silu(r_ref[rows, hs].astype(F32))).astype(y_ref.dtype)
                upd.append(_dot_tn(vh, ko_f[rows, sl]))
            new.append(s_cur[:, sl] * etot[ci:ci + 1, sl] + jnp.where(lo_half, upd[0], upd[1]))
        s_cur = jnp.concatenate(new, axis=1)
    s_ref[...] = s_cur

    @pl.when(last)
    def _():
        fin_ref[...] = s_cur


def _gla_mixer(proj, small, init, lw, regions):
    st = SCAN_STEP
    ncs = st // GLA_CHUNK
    t = proj.shape[0]
    n_steps = t // st
    n_seq = regions[0][0] + regions[1][0]
    qkcol, vcol, rcol = COL_QK // 1024, COL_V // 1024, COL_R // 1024
    state_block = (None, None, GLA_DV, GLA_KDIM)
    enter_block = (ncs, GLA_DV, GLA_KDIM)
    gwh, gwl, gb = lw["gate_w_hi"], lw["gate_w_lo"], lw["gate_b"]

    def const_spec(a):
        return pl.BlockSpec(a.shape, lambda s: (0,) * a.ndim)

    bidx = lambda s: n_steps - 1 - s
    seq_b = lambda s: _chunk_pos(bidx(s), st, regions)[0]
    enter_b, fin_b = pl.pallas_call(
        functools.partial(_gla_bwd_kernel, regions=regions, n_steps=n_steps),
        out_shape=(jax.ShapeDtypeStruct((n_steps * ncs, GLA_DV, GLA_KDIM), F32),
                   jax.ShapeDtypeStruct((n_seq, GLA_DV, GLA_KDIM), F32)),
        grid=(n_steps,),
        in_specs=[pl.BlockSpec((st, 2 * GLA_KDIM), lambda s: (bidx(s), qkcol)),
                  pl.BlockSpec((st, GLA_VDIM), lambda s: (bidx(s), vcol)),
                  pl.BlockSpec((st, LANES), lambda s: (bidx(s), 0)),
                  const_spec(gwh), const_spec(gwl), const_spec(gb),
                  pl.BlockSpec(state_block, lambda s: (seq_b(s), 1, 0, 0))],
        out_specs=(pl.BlockSpec(enter_block, lambda s: (bidx(s), 0, 0)),
                   pl.BlockSpec((None, GLA_DV, GLA_KDIM), lambda s: (seq_b(s), 0, 0))),
        scratch_shapes=[pltpu.VMEM((GLA_DV, GLA_KDIM), F32)],
        compiler_params=_params("arbitrary"),
        name="gla_backward_states",
    )(proj, proj, small, gwh, gwl, gb, init)

    seq_f = lambda s: _chunk_pos(s, st, regions)[0]
    ng = lw["gla_norm_g"]
    y, fin_f = pl.pallas_call(
        functools.partial(_gla_out_kernel, regions=regions),
        out_shape=(jax.ShapeDtypeStruct((t, GLA_VDIM), BF16),
                   jax.ShapeDtypeStruct((n_seq, GLA_DV, GLA_KDIM), F32)),
        grid=(n_steps,),
        in_specs=[pl.BlockSpec((st, 2 * GLA_KDIM), lambda s: (s, qkcol)),
                  pl.BlockSpec((st, GLA_VDIM), lambda s: (s, vcol)),
                  pl.BlockSpec((st, GLA_VDIM), lambda s: (s, rcol)),
                  pl.BlockSpec((st, LANES), lambda s: (s, 0)),
                  const_spec(gwh), const_spec(gwl), const_spec(gb), const_spec(ng),
                  pl.BlockSpec(state_block, lambda s: (seq_f(s), 0, 0, 0)),
                  pl.BlockSpec(enter_block, lambda s: (s, 0, 0))],
        out_specs=(pl.BlockSpec((st, GLA_VDIM), lambda s: (s, 0)),
                   pl.BlockSpec((None, GLA_DV, GLA_KDIM), lambda s: (seq_f(s), 0, 0))),
        scratch_shapes=[pltpu.VMEM((GLA_DV, GLA_KDIM), F32)],
        compiler_params=_params("arbitrary"),
        name="gla_forward_outputs",
    )(proj, proj, proj, small, gwh, gwl, gb, ng, init, enter_b)
    return y, fin_f, fin_b


def _merge_kernel(x_ref, ys_ref, yg_ref, ga_ref, gb_ref, mod_ref, wbs_ref, wbg_ref, wo_ref, o_ref):
    merged = (_sigmoid(ga_ref[...].astype(F32)) * _dot(ys_ref[...], wbs_ref[...])
              + _sigmoid(gb_ref[...].astype(F32)) * _dot(yg_ref[...], wbg_ref[...]))
    mix = _dot(merged.astype(BF16), wo_ref[...])
    o_ref[...] = x_ref[...] + mod_ref[:, 2 * D_MODEL:3 * D_MODEL] * mix


def _merge(x, y_ssd, y_gla, proj, mods, lw, regions):
    t = x.shape[0]
    tm = min(TM_MERGE, regions[0][0] * regions[0][1], regions[1][1])
    tok = lambda col: pl.BlockSpec((tm, D_MODEL), lambda i: (i, col))
    wspec = pl.BlockSpec((D_MODEL, D_MODEL), lambda i: (0, 0))
    return pl.pallas_call(
        _merge_kernel,
        out_shape=jax.ShapeDtypeStruct((t, D_MODEL), F32),
        grid=(t // tm,),
        in_specs=[tok(0), tok(0), tok(0), tok(COL_GA // D_MODEL), tok(COL_GB // D_MODEL),
                  pl.BlockSpec((None, 1, 6 * D_MODEL), lambda i: (_mod_row(i, tm, regions), 0, 0)),
                  wspec, wspec, wspec],
        out_specs=tok(0),
        compiler_params=_params("parallel"),
        name="merge_out_projection",
    )(x, y_ssd, y_gla, proj, proj, mods, lw["w_bs"], lw["w_bg"], lw["w_o"])


def _ffn_prologue(x_ref, mod_ref, g_ref):
    y = _rms(x_ref[...], g_ref[...])
    return y * (1.0 + mod_ref[:, 4 * D_MODEL:5 * D_MODEL]) + mod_ref[:, 3 * D_MODEL:4 * D_MODEL]


def _swiglu_partial(v, w1_ref, w3_ref, w2_ref):
    h = _silu(_dot(v, w1_ref[...])) * _dot(v, w3_ref[...])
    return _dot(h.astype(BF16), w2_ref[...])


def _ffn_kernel(x_ref, mod_ref, g_ref, w1_ref, w3_ref, w2_ref, o_ref, v_ref, acc_ref):
    k = pl.program_id(1)

    @pl.when(k == 0)
    def _():
        v_ref[...] = _ffn_prologue(x_ref, mod_ref, g_ref).astype(BF16)
        acc_ref[...] = jnp.zeros_like(acc_ref)

    acc_ref[...] += _swiglu_partial(v_ref[...], w1_ref, w3_ref, w2_ref)

    @pl.when(k == pl.num_programs(1) - 1)
    def _():
        o_ref[...] = x_ref[...] + mod_ref[:, 5 * D_MODEL:6 * D_MODEL] * acc_ref[...]


def _dense_ffn(x, mods, g, w1, w3, w2, regions):
    t = x.shape[0]
    tm = min(TM_FFN, regions[0][0] * regions[0][1], regions[1][1])
    hid = w1.shape[1]
    return pl.pallas_call(
        _ffn_kernel,
        out_shape=jax.ShapeDtypeStruct((t, D_MODEL), F32),
        grid=(t // tm, hid // TH_FFN),
        in_specs=[pl.BlockSpec((tm, D_MODEL), lambda i, k: (i, 0)),
                  pl.BlockSpec((None, 1, 6 * D_MODEL), lambda i, k: (_mod_row(i, tm, regions), 0, 0)),
                  pl.BlockSpec((1, D_MODEL), lambda i, k: (0, 0)),
                  pl.BlockSpec((D_MODEL, TH_FFN), lambda i, k: (0, k)),
                  pl.BlockSpec((D_MODEL, TH_FFN), lambda i, k: (0, k)),
                  pl.BlockSpec((TH_FFN, D_MODEL), lambda i, k: (k, 0))],
        out_specs=pl.BlockSpec((tm, D_MODEL), lambda i, k: (i, 0)),
        scratch_shapes=[pltpu.VMEM((tm, D_MODEL), BF16), pltpu.VMEM((tm, D_MODEL), F32)],
        compiler_params=_params("parallel", "arbitrary"),
        name="dense_swiglu",
    )(x, mods, g, w1, w3, w2)


def _top2_gates(logits):
    lane = lax.broadcasted_iota(jnp.int32, logits.shape, 1)
    lg = jnp.where(lane < N_EXPERTS, logits, -jnp.inf)
    m1 = jnp.max(lg, axis=-1, keepdims=True)
    i1 = jnp.min(jnp.where(lg == m1, lane, LANES), axis=-1, keepdims=True)
    lg2 = jnp.where(lane == i1, -jnp.inf, lg)
    m2 = jnp.max(lg2, axis=-1, keepdims=True)
    i2 = jnp.min(jnp.where(lg2 == m2, lane, LANES), axis=-1, keepdims=True)
    e2 = jnp.exp(m2 - m1)
    den = 1.0 + e2
    return jnp.where(lane == i1, 1.0 / den, 0.0) + jnp.where(lane == i2, e2 / den, 0.0)


def _lane_scalar(vec, e):
    lane = lax.broadcasted_iota(jnp.int32, vec.shape, 1)
    return jnp.sum(jnp.where(lane == e, vec, 0.0)).astype(jnp.int32)


def _n_row_blocks(n):
    return (n + MOE_ROW_BLOCK - 1) // MOE_ROW_BLOCK


def _moe_block_copy(stage_ref, sem, slot, hbm_ref, row, to_hbm):
    if not isinstance(row, int):
        row = pl.multiple_of(row, MOE_ROW_ALIGN)
    rows = hbm_ref.at[pl.ds(row, MOE_ROW_BLOCK), :]
    if to_hbm:
        return pltpu.make_async_copy(stage_ref.at[slot], rows, sem.at[slot])
    return pltpu.make_async_copy(rows, stage_ref.at[slot], sem.at[slot])


def _moe_dispatch_kernel(x_ref, mod_ref, g_ref, rw_ref, rb_ref, donor_hbm,
                         xs_hbm, gate_ref, rank_ref, off_ref, tot_ref,
                         v_ref, gate_t_ref, rank_t_ref, stage_ref, sem, run_ref, pend_ref, *, cap):
    del donor_hbm
    i = pl.program_id(0)
    tm = x_ref.shape[0]
    rb = MOE_ROW_BLOCK

    @pl.when(i == 0)
    def _():
        run_ref[...] = jnp.zeros_like(run_ref)
        pend_ref[0] = 0

    v = _ffn_prologue(x_ref, mod_ref, g_ref)
    v_ref[...] = v.astype(BF16)
    gate = _top2_gates(_dot_f32(v, rw_ref[...]) + rb_ref[...])
    sel_f = jnp.where(gate > 0.0, 1.0, 0.0)
    sel = sel_f.astype(BF16)
    gate_ref[...] = gate
    gate_t_ref[...] = gate.T
    for blk in range(tm // rb):
        ii = lax.broadcasted_iota(jnp.int32, (rb, tm), 0) + blk * rb
        jj = lax.broadcasted_iota(jnp.int32, (rb, tm), 1)
        rank = _dot(jnp.where(jj < ii, 1.0, 0.0).astype(BF16), sel)
        rank_ref[blk * rb:(blk + 1) * rb, :] = rank
        rank_t_ref[:, blk * rb:(blk + 1) * rb] = rank.T

    cnt = jnp.sum(sel_f, axis=0, keepdims=True)
    cnt_pad = jnp.floor((cnt + (MOE_ROW_ALIGN - 1)) * (1.0 / MOE_ROW_ALIGN)) * MOE_ROW_ALIGN
    run = run_ref[...]
    off_ref[...] = run

    def wait_prev(s, carry):
        _moe_block_copy(stage_ref, sem, s, xs_hbm, 0, True).wait()
        return carry
    lax.fori_loop(0, pend_ref[0], wait_prev, 0)

    slot = jnp.int32(0)
    for e in range(N_EXPERTS):
        n_e = _lane_scalar(cnt, e)
        base = e * cap + _lane_scalar(run, e)
        g_row = gate_t_ref[e:e + 1, :]
        r_row = rank_t_ref[e:e + 1, :].astype(jnp.int32)

        def pack(b, s, base=base, g_row=g_row, r_row=r_row):
            row = lax.broadcasted_iota(jnp.int32, (rb, tm), 0) + b * rb
            onehot = jnp.where(jnp.logical_and(row == r_row, g_row > 0.0), 1.0, 0.0).astype(BF16)
            stage_ref[s] = _dot(onehot, v_ref[...]).astype(BF16)
            _moe_block_copy(stage_ref, sem, s, xs_hbm, base + b * rb, True).start()
            return s + 1
        slot = lax.fori_loop(0, _n_row_blocks(n_e), pack, slot)

    pend_ref[0] = slot
    run_ref[...] = run + cnt_pad
    tot_ref[...] = run + cnt_pad

    @pl.when(i == pl.num_programs(0) - 1)
    def _():
        def wait_last(s, carry):
            _moe_block_copy(stage_ref, sem, s, xs_hbm, 0, True).wait()
            return carry
        lax.fori_loop(0, slot, wait_last, 0)


def _moe_expert_kernel(te_ref, tr_ref, na_ref, xs_ref, w1_ref, w3_ref, w2_ref, ys_ref):
    del te_ref, tr_ref

    @pl.when(pl.program_id(0) < na_ref[0])
    def _():
        x = xs_ref[...]
        hid = w1_ref.shape[1]
        acc = None
        for hh in range(hid // TH_FFN):
            cols = slice(hh * TH_FFN, (hh + 1) * TH_FFN)
            h = _silu(_dot(x, w1_ref[:, cols])) * _dot(x, w3_ref[:, cols])
            part = _dot(h.astype(BF16), w2_ref[cols, :])
            acc = part if acc is None else acc + part
        ys_ref[...] = acc.astype(ys_ref.dtype)


def _moe_combine_kernel(off_ref, x_ref, mod_ref, gate_ref, rank_ref, ys_hbm, o_ref, stage_ref, sem, *, cap):
    i = pl.program_id(0)
    tm = x_ref.shape[0]
    rb = MOE_ROW_BLOCK
    gate = gate_ref[...]
    rank = rank_ref[...]
    cnt = jnp.sum(jnp.where(gate > 0.0, 1.0, 0.0), axis=0, keepdims=True)
    n_blocks = [_n_row_blocks(_lane_scalar(cnt, e)) for e in range(N_EXPERTS)]
    bases = [e * cap + off_ref[i * LANES + e] for e in range(N_EXPERTS)]

    slot = jnp.int32(0)
    for e in range(N_EXPERTS):
        def fetch(b, s, base=bases[e]):
            _moe_block_copy(stage_ref, sem, s, ys_hbm, base + b * rb, False).start()
            return s + 1
        slot = lax.fori_loop(0, n_blocks[e], fetch, slot)

    o_ref[...] = jnp.zeros_like(o_ref)
    slot = jnp.int32(0)
    for e in range(N_EXPERTS):
        g_col = gate[:, e:e + 1]
        r_col = rank[:, e:e + 1].astype(jnp.int32)

        def combine(b, s, g_col=g_col, r_col=r_col):
            _moe_block_copy(stage_ref, sem, s, ys_hbm, 0, False).wait()
            col = lax.broadcasted_iota(jnp.int32, (tm, rb), 1) + b * rb
            onehot = jnp.where(jnp.logical_and(col == r_col, g_col > 0.0), 1.0, 0.0).astype(BF16)
            o_ref[...] += g_col * _dot(onehot, stage_ref[s])
            return s + 1
        slot = lax.fori_loop(0, n_blocks[e], combine, slot)

    o_ref[...] = x_ref[...] + mod_ref[:, 5 * D_MODEL:6 * D_MODEL] * o_ref[...]


def _moe_tile_table(totals, cap, n_steps):
    extent = totals + MOE_ROW_BLOCK
    n_tiles = (extent + TR_EXPERT - 1) // TR_EXPERT
    ends = jnp.cumsum(n_tiles)
    starts = ends - n_tiles
    n_active = ends[-1]
    t = jnp.minimum(jnp.arange(n_steps, dtype=jnp.int32), n_active - 1)
    e = jnp.sum((t[:, None] >= ends[None, :]).astype(jnp.int32), axis=1)
    row_block = e * (cap // TR_EXPERT) + (t - starts[e])
    return e.astype(jnp.int32), row_block.astype(jnp.int32), n_active.reshape(1).astype(jnp.int32)


def _moe_geometry(t, regions, n_e):
    tm = min(TM_MOE, regions[0][0] * regions[0][1], regions[1][1])
    n_tiles = t // tm
    cap_rows = t + n_tiles * (MOE_ROW_ALIGN - 1) + MOE_ROW_BLOCK
    cap = -(-cap_rows // TR_EXPERT) * TR_EXPERT
    return tm, n_tiles, cap


def _moe_ffn(x, mods, g, rw, rb, w1, w3, w2, regions, donor):
    t = x.shape[0]
    n_e, _, hid = w1.shape
    tm, n_tiles, cap = _moe_geometry(t, regions, n_e)
    max_slots = 2 * tm // MOE_ROW_BLOCK + n_e
    n_steps = (2 * t + n_tiles * n_e * (MOE_ROW_ALIGN - 1) + n_e * MOE_ROW_BLOCK) // TR_EXPERT + n_e
    mod_spec = pl.BlockSpec((None, 1, 6 * D_MODEL), lambda i, *_: (_mod_row(i, tm, regions), 0, 0))
    stage = [pltpu.VMEM((max_slots, MOE_ROW_BLOCK, D_MODEL), BF16), pltpu.SemaphoreType.DMA((max_slots,))]

    xs, gate, rank, offs, totals = pl.pallas_call(
        functools.partial(_moe_dispatch_kernel, cap=cap),
        out_shape=(jax.ShapeDtypeStruct((n_e * cap, D_MODEL), BF16),
                   jax.ShapeDtypeStruct((t, LANES), F32), jax.ShapeDtypeStruct((t, LANES), F32),
                   jax.ShapeDtypeStruct((n_tiles, 1, LANES), F32), jax.ShapeDtypeStruct((1, LANES), F32)),
        grid=(n_tiles,),
        in_specs=[pl.BlockSpec((tm, D_MODEL), lambda i: (i, 0)), mod_spec,
                  pl.BlockSpec((1, D_MODEL), lambda i: (0, 0)),
                  pl.BlockSpec((D_MODEL, LANES), lambda i: (0, 0)),
                  pl.BlockSpec((1, LANES), lambda i: (0, 0)),
                  pl.BlockSpec(memory_space=pl.ANY)],
        input_output_aliases={5: 0},
        out_specs=(pl.BlockSpec(memory_space=pl.ANY),
                   pl.BlockSpec((tm, LANES), lambda i: (i, 0)), pl.BlockSpec((tm, LANES), lambda i: (i, 0)),
                   pl.BlockSpec((None, 1, LANES), lambda i: (i, 0, 0)),
                   pl.BlockSpec((1, LANES), lambda i: (0, 0))),
        scratch_shapes=[pltpu.VMEM((tm, D_MODEL), BF16),
                        pltpu.VMEM((LANES, tm), F32), pltpu.VMEM((LANES, tm), F32)] + stage
        + [pltpu.VMEM((1, LANES), F32), pltpu.SMEM((1,), jnp.int32)],
        compiler_params=_params("arbitrary"),
        name="moe_dispatch",
    )(x, mods, g, rw, rb, donor)

    tile_e, tile_rb, n_active = _moe_tile_table(totals[0, :n_e].astype(jnp.int32), cap, n_steps)
    ys = pl.pallas_call(
        _moe_expert_kernel,
        out_shape=jax.ShapeDtypeStruct((n_e * cap, D_MODEL), BF16),
        grid_spec=pltpu.PrefetchScalarGridSpec(
            num_scalar_prefetch=3,
            grid=(n_steps,),
            in_specs=[pl.BlockSpec((TR_EXPERT, D_MODEL), lambda s, te, tr, na: (tr[s], 0)),
                      pl.BlockSpec((None, D_MODEL, hid), lambda s, te, tr, na: (te[s], 0, 0)),
                      pl.BlockSpec((None, D_MODEL, hid), lambda s, te, tr, na: (te[s], 0, 0)),
                      pl.BlockSpec((None, hid, D_MODEL), lambda s, te, tr, na: (te[s], 0, 0))],
            out_specs=pl.BlockSpec((TR_EXPERT, D_MODEL), lambda s, te, tr, na: (tr[s], 0))),
        input_output_aliases={3: 0},
        compiler_params=_params("arbitrary"),
        name="moe_experts",
    )(tile_e, tile_rb, n_active, xs, w1, w3, w2)

    out = pl.pallas_call(
        functools.partial(_moe_combine_kernel, cap=cap),
        out_shape=jax.ShapeDtypeStruct((t, D_MODEL), F32),
        grid_spec=pltpu.PrefetchScalarGridSpec(
            num_scalar_prefetch=1,
            grid=(n_tiles,),
            in_specs=[pl.BlockSpec((tm, D_MODEL), lambda i, off: (i, 0)), mod_spec,
                      pl.BlockSpec((tm, LANES), lambda i, off: (i, 0)),
                      pl.BlockSpec((tm, LANES), lambda i, off: (i, 0)),
                      pl.BlockSpec(memory_space=pl.ANY)],
            out_specs=pl.BlockSpec((tm, D_MODEL), lambda i, off: (i, 0)),
            scratch_shapes=stage),
        compiler_params=_params("arbitrary"),
        name="moe_combine",
    )(offs.reshape(-1).astype(jnp.int32), x, mods, gate, rank, ys)
    return out, ys


def _final_norm_kernel(x_ref, g_ref, o_ref):
    o_ref[...] = _rms(x_ref[...], g_ref[...])


def _final_norm(x, g, tok_off, n_tok):
    tm = min(TM_NORM, n_tok)
    off = tok_off // tm
    return pl.pallas_call(
        _final_norm_kernel,
        out_shape=jax.ShapeDtypeStruct((n_tok, D_MODEL), F32),
        grid=(n_tok // tm,),
        in_specs=[pl.BlockSpec((tm, D_MODEL), lambda i: (i + off, 0)),
                  pl.BlockSpec((1, D_MODEL), lambda i: (0, 0))],
        out_specs=pl.BlockSpec((tm, D_MODEL), lambda i: (i, 0)),
        compiler_params=_params("parallel"),
        name="final_norm",
    )(x, g)


def _grid_pos_embed(l, d):
    rows = l // GRID_W
    row = jnp.repeat(jnp.arange(rows, dtype=F32), GRID_W)
    col = jnp.tile(jnp.arange(GRID_W, dtype=F32), rows)
    quarter = d // 4
    omega = jnp.exp(-math.log(10000.0) * jnp.arange(quarter, dtype=F32) / quarter)
    er = row[:, None] * omega
    ec = col[:, None] * omega
    return jnp.concatenate([jnp.sin(er), jnp.cos(er), jnp.sin(ec), jnp.cos(ec)], axis=-1)


def _pad_lanes(a, width):
    return jnp.pad(a, [(0, 0)] * (a.ndim - 1) + [(0, width - a.shape[-1])])


def _reorder_w_in(w):
    o = np.cumsum([0, SSD_INNER, SSD_INNER + 2 * SSD_GROUPS * SSD_STATE, 2 * SSD_HEADS, GLA_KDIM, GLA_KDIM,
                   GLA_VDIM, GLA_VDIM, 2 * GLA_GATE_RANK, D_MODEL, D_MODEL])
    z, xbc, dt, qq, kk, vv, rr, lr, ga, gb = [w[:, int(o[i]):int(o[i + 1])] for i in range(10)]
    x, bc = xbc[:, :SSD_INNER], xbc[:, SSD_INNER:]
    main = jnp.concatenate([z, x, vv, rr, ga, gb, qq, kk, bc], axis=1).astype(BF16)
    small = _pad_lanes(jnp.concatenate([dt, lr], axis=1), LANES).astype(BF16)
    return main, small


def _layer_weights(i, p):
    conv_w = jnp.pad(p["ssd_conv_w"][i], ((0, SUBLANES - SSD_CONV), (0, 0)))
    conv_b = p["ssd_conv_b"][i][None, :]
    gate_w = jnp.zeros((LANES, 2 * GLA_KDIM), F32)
    gate_w = gate_w.at[SM_LR:SM_LR + GLA_GATE_RANK, :GLA_KDIM].set(p["gla_gate_w"][i, 0])
    gate_w = gate_w.at[SM_LR + GLA_GATE_RANK:SM_LR + 2 * GLA_GATE_RANK, GLA_KDIM:].set(p["gla_gate_w"][i, 1])
    gate_w_hi = gate_w.astype(BF16)
    gate_w_lo = (gate_w - gate_w_hi.astype(F32)).astype(BF16)
    w_main, w_small = _reorder_w_in(p["w_in"][i])
    return {
        "w_in": w_main, "w_in_small": w_small,
        "conv_w_x": conv_w[:, :SSD_INNER], "conv_w_bc": conv_w[:, SSD_INNER:],
        "conv_b_x": conv_b[:, :SSD_INNER], "conv_b_bc": conv_b[:, SSD_INNER:],
        "dt_bias": _pad_lanes(p["ssd_dt_bias"][i].reshape(1, -1), LANES),
        "a_log": _pad_lanes(p["ssd_a_log"][i].reshape(1, -1), LANES),
        "d_exp": jnp.repeat(p["ssd_d"][i], SSD_HEADDIM)[None, :],
        "ssd_norm_g": p["ssd_norm_g"][i][None, :],
        "gate_w_hi": gate_w_hi, "gate_w_lo": gate_w_lo, "gate_b": p["gla_gate_b"][i].reshape(1, -1),
        "gla_norm_g": p["gla_norm_g"][i][None, :],
        "w_bs": p["w_branch_ssd"][i].astype(BF16), "w_bg": p["w_branch_gla"][i].astype(BF16),
        "w_o": p["w_out"][i].astype(BF16),
    }


def _ssd_state_in(s):
    b = s.shape[0]
    return jnp.transpose(s, (0, 1, 4, 2, 3)).reshape(b, 2, SSD_STATE, SSD_INNER)


def _ssd_state_out(f, b):
    s = jnp.stack([f, b], axis=1).reshape(-1, 2, SSD_STATE, SSD_HEADS, SSD_HEADDIM)
    return jnp.transpose(s, (0, 1, 3, 4, 2))


def _gla_state_in(s):
    b = s.shape[0]
    return jnp.transpose(s, (0, 1, 4, 2, 3)).reshape(b, 2, GLA_DV, GLA_KDIM)


def _gla_state_out(f, b):
    s = jnp.stack([f, b], axis=1).reshape(-1, 2, GLA_DV, GLA_HEADS, GLA_DK)
    return jnp.transpose(s, (0, 1, 3, 4, 2))


def _trunk(x_prompt, x_sample, state_ssd, state_gla, c, c_ctx, p):
    n0, l0, _ = x_prompt.shape
    n1, l1, _ = x_sample.shape
    regions = ((n0, l0), (n1, l1))
    depth = p["w_in"].shape[0]

    cc = jnp.zeros((MOD_ROWS, D_MODEL), F32).at[0].set(c_ctx).at[1:1 + n1].set(c)
    mods = _modulation_table(cc, p["ada_w"], p["ada_b"])[:, :, None, :]
    x = _assemble_tokens(x_prompt, x_sample, _grid_pos_embed(l1, D_MODEL), regions)

    ssd_states, gla_states = [], []
    moe_rows = None
    for i in range(depth):
        lw = _layer_weights(i, p)
        init_ssd = jnp.concatenate([jnp.zeros((n0, 2, SSD_STATE, SSD_INNER), F32),
                                    _ssd_state_in(state_ssd[:, i])], axis=0)
        init_gla = jnp.concatenate([jnp.zeros((n0, 2, GLA_DV, GLA_KDIM), F32),
                                    _gla_state_in(state_gla[:, i])], axis=0)
        proj, small = _in_projection(x, mods[i], p["norm1_g"][i][None, :], lw["w_in"], lw["w_in_small"], regions)
        y_ssd, sf, sb = _ssd_mixer(proj, small, init_ssd, lw, regions)
        y_gla, gf, gb = _gla_mixer(proj, small, init_gla, lw, regions)
        ssd_states.append(_ssd_state_out(sf[:n0], sb[:n0]))
        gla_states.append(_gla_state_out(gf[:n0], gb[:n0]))
        x = _merge(x, y_ssd, y_gla, proj, mods[i], lw, regions)
        j = i // 2
        g2 = p["norm2_g"][i][None, :]
        if i % 2 == 0:
            x = _dense_ffn(x, mods[i], g2, p["ffn_w1"][j].astype(BF16), p["ffn_w3"][j].astype(BF16),
                           p["ffn_w2"][j].astype(BF16), regions)
        else:
            if moe_rows is None:
                n_e = p["moe_w1"].shape[1]
                moe_rows = jnp.zeros((n_e * _moe_geometry(x.shape[0], regions, n_e)[2], D_MODEL), BF16)
            x, moe_rows = _moe_ffn(x, mods[i], g2, _pad_lanes(p["router_w"][j], LANES),
                                   _pad_lanes(p["router_b"][j][None, :], LANES),
                                   p["moe_w1"][j].astype(BF16), p["moe_w3"][j].astype(BF16),
                                   p["moe_w2"][j].astype(BF16), regions, moe_rows)

    gfin = p["final_norm_g"][None, :]
    y_prompt = _final_norm(x, gfin, 0, n0 * l0).reshape(n0, l0, D_MODEL)
    y_sample = _final_norm(x, gfin, n0 * l0, n1 * l1).reshape(n1, l1, D_MODEL)
    return y_prompt, y_sample, jnp.stack(ssd_states, axis=1), jnp.stack(gla_states, axis=1)


def kernel(x_prompt, x_sample, state_ssd, state_gla, c, c_ctx, ada_w, ada_b, norm1_g, norm2_g, w_in, ssd_conv_w, ssd_conv_b, ssd_dt_bias, ssd_a_log, ssd_d, ssd_norm_g, gla_gate_w, gla_gate_b, gla_norm_g, w_branch_ssd, w_branch_gla, w_out, ffn_w1, ffn_w3, ffn_w2, router_w, router_b, moe_w1, moe_w3, moe_w2, final_norm_g):
    p = dict(ada_w=ada_w, ada_b=ada_b, norm1_g=norm1_g, norm2_g=norm2_g, w_in=w_in, ssd_conv_w=ssd_conv_w,
             ssd_conv_b=ssd_conv_b, ssd_dt_bias=ssd_dt_bias, ssd_a_log=ssd_a_log, ssd_d=ssd_d,
             ssd_norm_g=ssd_norm_g, gla_gate_w=gla_gate_w, gla_gate_b=gla_gate_b, gla_norm_g=gla_norm_g,
             w_branch_ssd=w_branch_ssd, w_branch_gla=w_branch_gla, w_out=w_out, ffn_w1=ffn_w1, ffn_w3=ffn_w3,
             ffn_w2=ffn_w2, router_w=router_w, router_b=router_b, moe_w1=moe_w1, moe_w3=moe_w3, moe_w2=moe_w2,
             final_norm_g=final_norm_g)
    return _trunk(x_prompt, x_sample, state_ssd, state_gla, c, c_ctx, p)
```

```python
import functools
import math

import numpy as np
import jax
import jax.numpy as jnp
from jax import lax
from jax.experimental import pallas as pl
from jax.experimental.pallas import tpu as pltpu

F32 = jnp.float32
BF16 = jnp.bfloat16

D_MODEL = 1024
GRID_W = 64
EPS = 1e-6
SSD_HEADS = 16
SSD_HEADDIM = 64
SSD_INNER = SSD_HEADS * SSD_HEADDIM
SSD_GROUPS = 2
SSD_STATE = 64
SSD_CONV = 5
SSD_CHUNK = 128
GLA_HEADS = 8
GLA_DK = 64
GLA_DV = 128
GLA_KDIM = GLA_HEADS * GLA_DK
GLA_VDIM = GLA_HEADS * GLA_DV
GLA_GATE_RANK = 16
GLA_GATE_TAU = 16.0
GLA_CHUNK = 64
N_EXPERTS = 8

LANES = 128
SUBLANES = 8
BF16_SUBLANES = 16
VMEM_LIMIT_BYTES = 56 * 1024 * 1024

COL_Z, COL_V, COL_R, COL_GA, COL_GB, COL_QK = 0, 1024, 2048, 3072, 4096, 5120
D_PROJ = 6144
CONV_X, CONV_BC = 0, 1024
D_CONV = 1280
SM_LR = 2 * SSD_HEADS

TM_PROJ, TN_PROJ = 1024, 3072
TM_MERGE = 512
TM_FFN, TH_FFN = 512, 1408
TM_MOE = 512
MOE_ROW_BLOCK = 256
MOE_ROW_ALIGN = 16
TR_EXPERT = 512
TM_NORM = 1024
TN_MODS = 1536
MOD_ROWS = 16
SCAN_STEP = 256


def _params(*sem):
    return pltpu.CompilerParams(dimension_semantics=sem, vmem_limit_bytes=VMEM_LIMIT_BYTES)


def _sigmoid(x):
    return 0.5 * jnp.tanh(0.5 * x) + 0.5


def _silu(x):
    return x * _sigmoid(x)


def _softplus(x):
    return jnp.maximum(x, 0.0) + jnp.log(1.0 + jnp.exp(-jnp.abs(x)))


def _split2(a):
    a1 = a.astype(BF16)
    a2 = (a - a1.astype(F32)).astype(BF16)
    return a1, a2


def _split3(a):
    a1 = a.astype(BF16)
    r1 = a - a1.astype(F32)
    a2 = r1.astype(BF16)
    a3 = (r1 - a2.astype(F32)).astype(BF16)
    return a1, a2, a3


def _dot(a, b):
    return jnp.dot(a, b, preferred_element_type=F32)


def _dot_nt(a, b):
    return lax.dot_general(a, b, (((1,), (1,)), ((), ())), preferred_element_type=F32)


def _dot_tn(a, b):
    return lax.dot_general(a, b, (((0,), (0,)), ((), ())), preferred_element_type=F32)


def _dot_exact_lhs(t01, a):
    a1, a2 = _split2(a)
    return _dot(t01, a1) + _dot(t01, a2)


def _dot_f32(a, b):
    a1, a2, a3 = _split3(a)
    b1, b2, b3 = _split3(b)
    return (_dot(a1, b1) + _dot(a1, b2) + _dot(a2, b1)
            + _dot(a1, b3) + _dot(a2, b2) + _dot(a3, b1))


def _dot_hilo(a, b):
    a1, a2 = _split2(a)
    b1, b2 = _split2(b)
    return _dot(a1, b1) + _dot(a2, b1) + _dot(a1, b2)


def _block_tri(n, blk, upper):
    i = lax.broadcasted_iota(jnp.int32, (n, n), 0)
    j = lax.broadcasted_iota(jnp.int32, (n, n), 1)
    tri = (j >= i) if upper else (j <= i)
    if blk == n:
        return tri
    return jnp.logical_and(tri, (i // blk) == (j // blk))


def _rms(x, g):
    return x * lax.rsqrt(jnp.mean(x * x, axis=-1, keepdims=True) + EPS) * g


def _chunk_pos(c, q, regions):
    (n0, l0), (n1, l1) = regions
    per0, per1 = l0 // q, l1 // q
    nc0 = n0 * per0
    c1 = jnp.maximum(c - nc0, 0)
    in0 = c < nc0
    seq = jnp.where(in0, c // per0, n0 + c1 // per1)
    pos = jnp.where(in0, c % per0, c1 % per1)
    last = jnp.where(in0, per0 - 1, per1 - 1)
    return seq, pos == 0, pos == last


def _mod_row(i, tm, regions):
    (n0, l0), (n1, l1) = regions
    t1 = jnp.maximum(i * tm - n0 * l0, 0)
    return jnp.where(i * tm < n0 * l0, 0, 1 + t1 // l1)


def _assemble_kernel(xp_ref, xs_ref, pe_ref, o_ref, *, n_ctx_tiles):
    i = pl.program_id(0)

    @pl.when(i < n_ctx_tiles)
    def _():
        o_ref[...] = xp_ref[...]

    @pl.when(i >= n_ctx_tiles)
    def _():
        o_ref[...] = xs_ref[...] + pe_ref[...]


def _assemble_tokens(xp, xs, pe, regions):
    (n0, l0), (n1, l1) = regions
    tm = min(1024, l0 * n0, l1)
    t0, t1 = n0 * l0, n1 * l1
    n_ctx = t0 // tm
    pe_tiles = l1 // tm
    return pl.pallas_call(
        functools.partial(_assemble_kernel, n_ctx_tiles=n_ctx),
        out_shape=jax.ShapeDtypeStruct((t0 + t1, D_MODEL), F32),
        grid=((t0 + t1) // tm,),
        in_specs=[
            pl.BlockSpec((tm, D_MODEL), lambda i: (jnp.minimum(i, n_ctx - 1), 0)),
            pl.BlockSpec((tm, D_MODEL), lambda i: (jnp.maximum(i - n_ctx, 0), 0)),
            pl.BlockSpec((tm, D_MODEL), lambda i: (jnp.maximum(i - n_ctx, 0) % pe_tiles, 0)),
        ],
        out_specs=pl.BlockSpec((tm, D_MODEL), lambda i: (i, 0)),
        compiler_params=_params("arbitrary"),
        name="assemble_tokens",
    )(xp.reshape(t0, D_MODEL), xs.reshape(t1, D_MODEL), pe)


def _mods_kernel(cc_ref, w_ref, b_ref, o_ref):
    a = _silu(cc_ref[...])
    o_ref[...] = _dot_f32(a, w_ref[...]) + b_ref[...]


def _modulation_table(cc, ada_w, ada_b):
    depth = ada_w.shape[0]
    n = ada_w.shape[2]
    return pl.pallas_call(
        _mods_kernel,
        out_shape=jax.ShapeDtypeStruct((depth, MOD_ROWS, n), F32),
        grid=(depth, n // TN_MODS),
        in_specs=[
            pl.BlockSpec((MOD_ROWS, D_MODEL), lambda l, j: (0, 0)),
            pl.BlockSpec((None, D_MODEL, TN_MODS), lambda l, j: (l, 0, j)),
            pl.BlockSpec((None, 1, TN_MODS), lambda l, j: (l, 0, j)),
        ],
        out_specs=pl.BlockSpec((None, MOD_ROWS, TN_MODS), lambda l, j: (l, 0, j)),
        compiler_params=_params("arbitrary", "arbitrary"),
        name="modulation_table",
    )(cc, ada_w, ada_b.reshape(depth, 1, n))


HALO = SUBLANES
CONV_PAD = SSD_CONV // 2


def _inproj_kernel(x_ref, xp_ref, xn_ref, mod_ref, g_ref, w_ref, wc_ref, ws_ref, cw_ref, cb_ref,
                   o_ref, oc_ref, os_ref, u_ref, uh_ref, ext_ref, *, regions):
    i = pl.program_id(0)
    j = pl.program_id(1)
    tm = x_ref.shape[0]

    def modnorm(x):
        y = _rms(x, g_ref[...])
        return y * (1.0 + mod_ref[:, D_MODEL:2 * D_MODEL]) + mod_ref[:, 0:D_MODEL]

    @pl.when(j == 0)
    def _():
        u_ref[...] = modnorm(x_ref[...]).astype(BF16)
        uh_ref[...] = jnp.concatenate([modnorm(xp_ref[...]), modnorm(xn_ref[...])], axis=0).astype(BF16)
        os_ref[...] = _dot(u_ref[...], ws_ref[...])

    u = u_ref[...]
    tc = wc_ref.shape[1]
    halo = _dot(uh_ref[...], wc_ref[...])
    ext_ref[0:HALO, :] = halo[0:HALO]
    ext_ref[HALO:HALO + tm, :] = _dot(u, wc_ref[...])
    ext_ref[HALO + tm:2 * HALO + tm, :] = halo[HALO:2 * HALO]

    def conv(rows, cols, valid=None):
        acc = cb_ref[:, cols]
        for k in range(SSD_CONV):
            d = k - CONV_PAD
            tap = ext_ref[HALO + rows.start + d:HALO + rows.stop + d, cols]
            if valid is not None and d != 0:
                tap = jnp.where(valid(d), tap, 0.0)
            acc = acc + cw_ref[k:k + 1, cols] * tap
        return _silu(acc).astype(BF16)

    n_chunks = tc // LANES
    n_lane_tiles = w_ref.shape[1] // LANES
    bounds = [LANES * (n_lane_tiles * c // n_chunks) for c in range(n_chunks + 1)]
    for c in range(n_chunks):
        cols = slice(c * LANES, (c + 1) * LANES)
        oc_ref[:, cols] = conv(slice(0, tm), cols)
        ms = slice(bounds[c], bounds[c + 1])
        o_ref[:, ms] = _dot(u, w_ref[:, ms]).astype(BF16)

    (n0, l0), (n1, l1) = regions
    seq_len = jnp.where(i * tm < n0 * l0, l0, l1)
    win = BF16_SUBLANES
    for b in range(0, tm + 1, math.gcd(l0, l1, tm)):
        @pl.when(lax.rem(i * tm + b, seq_len) == 0)
        def _(b=b):
            for r0 in (b - win, b):
                if 0 <= r0 and r0 + win <= tm:
                    rr = r0 + lax.broadcasted_iota(jnp.int32, (win, 1), 0)
                    valid = (lambda d, rr=rr: rr < b - d) if r0 < b else (lambda d, rr=rr: rr >= b - d)
                    oc_ref[r0:r0 + win, :] = conv(slice(r0, r0 + win), slice(0, tc), valid)


def _in_projection(x, mods, g, lw, regions):
    t = x.shape[0]
    tm = min(TM_PROJ, regions[0][0] * regions[0][1], regions[1][1])
    hb = tm // HALO
    n_hb = t // HALO
    n_col = D_PROJ // TN_PROJ
    tc = D_CONV // n_col
    return pl.pallas_call(
        functools.partial(_inproj_kernel, regions=regions),
        out_shape=(jax.ShapeDtypeStruct((t, D_PROJ), BF16), jax.ShapeDtypeStruct((t, D_CONV), BF16),
                   jax.ShapeDtypeStruct((t, LANES), F32)),
        grid=(t // tm, n_col),
        in_specs=[
            pl.BlockSpec((tm, D_MODEL), lambda i, j: (i, 0)),
            pl.BlockSpec((HALO, D_MODEL), lambda i, j: (jnp.maximum(i * hb - 1, 0), 0)),
            pl.BlockSpec((HALO, D_MODEL), lambda i, j: (jnp.minimum((i + 1) * hb, n_hb - 1), 0)),
            pl.BlockSpec((None, 1, 6 * D_MODEL), lambda i, j: (_mod_row(i, tm, regions), 0, 0)),
            pl.BlockSpec((1, D_MODEL), lambda i, j: (0, 0)),
            pl.BlockSpec((D_MODEL, TN_PROJ), lambda i, j: (0, j)),
            pl.BlockSpec((D_MODEL, tc), lambda i, j: (0, j)),
            pl.BlockSpec((D_MODEL, LANES), lambda i, j: (0, 0)),
            pl.BlockSpec((SUBLANES, tc), lambda i, j: (0, j)),
            pl.BlockSpec((1, tc), lambda i, j: (0, j)),
        ],
        out_specs=(pl.BlockSpec((tm, TN_PROJ), lambda i, j: (i, j)),
                   pl.BlockSpec((tm, tc), lambda i, j: (i, j)),
                   pl.BlockSpec((tm, LANES), lambda i, j: (i, 0))),
        scratch_shapes=[pltpu.VMEM((tm, D_MODEL), BF16), pltpu.VMEM((2 * HALO, D_MODEL), BF16),
                        pltpu.VMEM((tm + 2 * HALO, tc), F32)],
        compiler_params=_params("parallel", "arbitrary"),
        name="norm_in_projection",
    )(x, x, x, mods, g, lw["w_in"], lw["w_in_conv"], lw["w_in_small"], lw["conv_w"], lw["conv_b"])


def _ssd_decay_terms(sm, dtb, alog, q):
    lane = lax.broadcasted_iota(jnp.int32, (1, LANES), 1)
    dtv = _softplus(sm + dtb)
    a_neg = jnp.where(lane < 2 * SSD_HEADS, -jnp.exp(alog), 0.0)
    a = dtv * a_neg
    acs = _dot_exact_lhs(_block_tri(q, q, False).astype(BF16), a)
    rev = _dot_exact_lhs(_block_tri(q, q, True).astype(BF16), a)
    m = jnp.where(lane < SSD_HEADS, acs, rev)
    tot = acs[q - 1:q, :]
    return m, tot, dtv


def _col(x, idx, width=LANES):
    return jnp.broadcast_to(x[:, idx:idx + 1], (x.shape[0], width))


def _pair_cols(x, base, p, lo_half):
    return jnp.where(lo_half, _col(x, base + 2 * p), _col(x, base + 2 * p + 1))


def _expand_state(s):
    lane = lax.broadcasted_iota(jnp.int32, s.shape, 1)
    half = SSD_INNER // SSD_GROUPS
    return jnp.concatenate([jnp.where(lane < half, s, 0.0), jnp.where(lane >= half, s, 0.0)], axis=0)


def _compact_state(s2):
    return s2[0:SSD_STATE, :] + s2[SSD_STATE:2 * SSD_STATE, :]


def _state_update_mask():
    row = lax.broadcasted_iota(jnp.int32, (SSD_GROUPS * SSD_STATE, SSD_INNER), 0)
    lane = lax.broadcasted_iota(jnp.int32, (SSD_GROUPS * SSD_STATE, SSD_INNER), 1)
    return (row < SSD_STATE) == (lane < SSD_INNER // SSD_GROUPS)


def _ssd_bwd_kernel(x_ref, bc_ref, sm_ref, dtb_ref, alog_ref, init_ref,
                    enter_ref, fin_ref, s_ref, *, regions, n_steps):
    q = SSD_CHUNK
    c = n_steps - 1 - pl.program_id(0)
    _, first, last = _chunk_pos(c, SCAN_STEP, regions)

    @pl.when(last)
    def _():
        s_ref[...] = _expand_state(init_ref[...])

    lo_half = lax.broadcasted_iota(jnp.int32, (1, LANES), 1) < SSD_HEADDIM
    upd_mask = _state_update_mask()

    s_cur = s_ref[...]
    for ci in reversed(range(SCAN_STEP // q)):
        rows = slice(ci * q, (ci + 1) * q)
        xs = x_ref[rows, :].astype(F32)
        bsb = bc_ref[rows, 0:LANES]
        m, tot, dtv = _ssd_decay_terms(sm_ref[rows, :], dtb_ref[...], alog_ref[...], q)
        wgt = jnp.exp(tot - m) * dtv
        etot = jnp.exp(tot)
        enter_ref[ci] = _compact_state(s_cur)
        xw, dec = [], []
        for p in range(SSD_HEADS // 2):
            sl = slice(p * LANES, (p + 1) * LANES)
            xw.append((xs[:, sl] * _pair_cols(wgt, SSD_HEADS, p, lo_half)).astype(BF16))
            dec.append(_pair_cols(etot, SSD_HEADS, p, lo_half))
        upd = _dot_tn(bsb, jnp.concatenate(xw, axis=1))
        s_cur = s_cur * jnp.concatenate(dec, axis=1) + jnp.where(upd_mask, upd, 0.0)
    s_ref[...] = s_cur

    @pl.when(first)
    def _():
        fin_ref[...] = _compact_state(s_cur)


def _ssd_out_kernel(z_ref, x_ref, bc_ref, sm_ref, dtb_ref, alog_ref, dexp_ref, ng_ref,
                    init_ref, enter_ref, y_ref, fin_ref, s_ref, *, regions):
    q = SSD_CHUNK
    c = pl.program_id(0)
    _, first, last = _chunk_pos(c, SCAN_STEP, regions)

    @pl.when(first)
    def _():
        s_ref[...] = _expand_state(init_ref[...])

    lane = lax.broadcasted_iota(jnp.int32, (1, LANES), 1)
    lo_half = lane < SSD_HEADDIM
    tril = _block_tri(q, q, False)
    ii = lax.broadcasted_iota(jnp.int32, (q, q), 0)
    jj = lax.broadcasted_iota(jnp.int32, (q, q), 1)
    eye = ii == jj
    upd_mask = _state_update_mask()
    dexp = dexp_ref[...]
    ng = ng_ref[...]

    s_cur = s_ref[...]
    for ci in range(SCAN_STEP // q):
        rows = slice(ci * q, (ci + 1) * q)
        xs = x_ref[rows, :].astype(F32)
        bsb = bc_ref[rows, 0:LANES]
        csb = bc_ref[rows, LANES:2 * LANES]
        m, tot, dtv = _ssd_decay_terms(sm_ref[rows, :], dtb_ref[...], alog_ref[...], q)
        wgt = jnp.exp(tot - m) * dtv
        mt = (m - jnp.log(dtv)).T
        ldt = jnp.log(dtv + pltpu.roll(dtv, LANES - SSD_HEADS, 1)).T
        zero_b = jnp.zeros_like(csb)
        cb = [_dot_nt(jnp.where(lo_half, csb, zero_b), bsb).astype(BF16),
              _dot_nt(jnp.where(lo_half, zero_b, csb), bsb).astype(BF16)]
        cs_f = _dot(csb, s_cur.astype(BF16))
        cs_b = _dot(csb, _expand_state(enter_ref[ci]).astype(BF16))

        ys, xw, dec = [], [], []
        for p in range(SSD_HEADS // 2):
            sl = slice(p * LANES, (p + 1) * LANES)
            g = (2 * p) // (SSD_HEADS // SSD_GROUPS)
            xs_p = xs[:, sl]
            xs_pb = xs_p.astype(BF16)
            yd, colf, colb = [], [], []
            for h in (2 * p, 2 * p + 1):
                hb = SSD_HEADS + h
                cf = _col(m, h, q)
                cbk = _col(m, hb, q)
                seg = jnp.where(tril, cf - mt[h:h + 1, :], cbk - mt[hb:hb + 1, :])
                seg = jnp.where(eye, ldt[h:h + 1, :], seg)
                yd.append(_dot(cb[g] * jnp.exp(seg).astype(BF16), xs_pb))
                colf.append(cf)
                colb.append(cbk)
            ef = jnp.exp(jnp.where(lo_half, colf[0], colf[1]))
            eb = jnp.exp(jnp.where(lo_half, colb[0], colb[1]))
            y_p = jnp.where(lo_half, yd[0], yd[1]) + ef * cs_f[:, sl] + eb * cs_b[:, sl] + dexp[:, sl] * xs_p
            ys.append(y_p)
            xw.append((xs_p * _pair_cols(wgt, 0, p, lo_half)).astype(BF16))
            dec.append(ef[q - 1:q, :])

        y = jnp.concatenate(ys, axis=1) * _silu(z_ref[rows, :].astype(F32))
        y_ref[rows, :] = _rms(y, ng).astype(y_ref.dtype)

        upd = _dot_tn(bsb, jnp.concatenate(xw, axis=1))
        s_cur = s_cur * jnp.concatenate(dec, axis=1) + jnp.where(upd_mask, upd, 0.0)
    s_ref[...] = s_cur

    @pl.when(last)
    def _():
        fin_ref[...] = _compact_state(s_cur)


def _ssd_mixer(proj, xbc, small, init, lw, regions):
    st = SCAN_STEP
    cps = st // SSD_CHUNK
    t = proj.shape[0]
    n_steps = t // st
    n_seq = regions[0][0] + regions[1][0]
    xcol, bccol = CONV_X // SSD_INNER, CONV_BC // (2 * LANES)

    def chunk_specs(cidx):
        return [
            pl.BlockSpec((st, SSD_INNER), lambda s: (cidx(s), xcol)),
            pl.BlockSpec((st, 2 * LANES), lambda s: (cidx(s), bccol)),
            pl.BlockSpec((st, LANES), lambda s: (cidx(s), 0)),
        ]

    def const_spec(a):
        return pl.BlockSpec(a.shape, lambda s: (0,) * a.ndim)

    consts = [lw["dt_bias"], lw["a_log"]]
    scratch = [pltpu.VMEM((SSD_GROUPS * SSD_STATE, SSD_INNER), F32)]
    state_block = (None, None, SSD_STATE, SSD_INNER)
    enter_block = (cps, SSD_STATE, SSD_INNER)

    bidx = lambda s: n_steps - 1 - s
    seq_b = lambda s: _chunk_pos(bidx(s), st, regions)[0]
    enter_b, fin_b = pl.pallas_call(
        functools.partial(_ssd_bwd_kernel, regions=regions, n_steps=n_steps),
        out_shape=(jax.ShapeDtypeStruct((n_steps * cps, SSD_STATE, SSD_INNER), F32),
                   jax.ShapeDtypeStruct((n_seq, SSD_STATE, SSD_INNER), F32)),
        grid=(n_steps,),
        in_specs=chunk_specs(bidx) + [const_spec(a) for a in consts]
        + [pl.BlockSpec(state_block, lambda s: (seq_b(s), 1, 0, 0))],
        out_specs=(pl.BlockSpec(enter_block, lambda s: (bidx(s), 0, 0)),
                   pl.BlockSpec((None, SSD_STATE, SSD_INNER), lambda s: (seq_b(s), 0, 0))),
        scratch_shapes=scratch,
        compiler_params=_params("arbitrary"),
        name="ssd_backward_states",
    )(xbc, xbc, small, *consts, init)

    fidx = lambda s: s
    seq_f = lambda s: _chunk_pos(s, st, regions)[0]
    consts2 = consts + [lw["d_exp"], lw["ssd_norm_g"]]
    y, fin_f = pl.pallas_call(
        functools.partial(_ssd_out_kernel, regions=regions),
        out_shape=(jax.ShapeDtypeStruct((t, SSD_INNER), BF16),
                   jax.ShapeDtypeStruct((n_seq, SSD_STATE, SSD_INNER), F32)),
        grid=(n_steps,),
        in_specs=[pl.BlockSpec((st, SSD_INNER), lambda s: (s, COL_Z // SSD_INNER))]
        + chunk_specs(fidx) + [const_spec(a) for a in consts2]
        + [pl.BlockSpec(state_block, lambda s: (seq_f(s), 0, 0, 0)),
           pl.BlockSpec(enter_block, lambda s: (s, 0, 0))],
        out_specs=(pl.BlockSpec((st, SSD_INNER), lambda s: (s, 0)),
                   pl.BlockSpec((None, SSD_STATE, SSD_INNER), lambda s: (seq_f(s), 0, 0))),
        scratch_shapes=scratch,
        compiler_params=_params("arbitrary"),
        name="ssd_forward_outputs",
    )(proj, xbc, xbc, small, *consts2, init, enter_b)
    return y, fin_f, fin_b


def _gla_log_decay(sm, gw_hi_ref, gw_lo_ref, gb_ref):
    s1, s2 = _split2(sm)
    logit = _dot(s1, gw_hi_ref[...]) + _dot(s2, gw_hi_ref[...]) + _dot(s1, gw_lo_ref[...]) + gb_ref[...]
    return -_softplus(-logit) * (1.0 / GLA_GATE_TAU)


def _chunk_rows(x, n_chunks, rows_per_chunk):
    return jnp.concatenate([jnp.broadcast_to(x[c:c + 1, :], (rows_per_chunk, x.shape[1]))
                            for c in range(n_chunks)], axis=0)


def _gla_bwd_kernel(qk_ref, v_ref, sm_ref, gwh_ref, gwl_ref, gb_ref, init_ref,
                    enter_ref, fin_ref, s_ref, *, regions, n_steps):
    cq = GLA_CHUNK
    st = SCAN_STEP
    ncs = st // cq
    c = n_steps - 1 - pl.program_id(0)
    _, first, last = _chunk_pos(c, st, regions)

    @pl.when(last)
    def _():
        s_ref[...] = init_ref[...]

    bwd = slice(GLA_KDIM, 2 * GLA_KDIM)
    lg = _gla_log_decay(sm_ref[...], gwh_ref.at[:, bwd], gwl_ref.at[:, bwd], gb_ref.at[:, bwd])
    rev = _dot_exact_lhs(_block_tri(st, cq, True).astype(BF16), lg)
    tot = jnp.concatenate([rev[ci * cq:ci * cq + 1, :] for ci in range(ncs)], axis=0)
    ko = (qk_ref[:, GLA_KDIM:2 * GLA_KDIM].astype(F32) * jnp.exp(_chunk_rows(tot, ncs, cq) - rev)).astype(BF16)
    etot = jnp.exp(tot)
    lo_half = lax.broadcasted_iota(jnp.int32, (1, LANES), 1) < GLA_DK

    s_cur = s_ref[...]
    for ci in reversed(range(ncs)):
        rows = slice(ci * cq, (ci + 1) * cq)
        enter_ref[ci] = s_cur
        new = []
        for p in range(GLA_HEADS // 2):
            sl = slice(p * LANES, (p + 1) * LANES)
            u0 = _dot_tn(v_ref[rows, (2 * p) * GLA_DV:(2 * p + 1) * GLA_DV], ko[rows, sl])
            u1 = _dot_tn(v_ref[rows, (2 * p + 1) * GLA_DV:(2 * p + 2) * GLA_DV], ko[rows, sl])
            new.append(s_cur[:, sl] * etot[ci:ci + 1, sl] + jnp.where(lo_half, u0, u1))
        s_cur = jnp.concatenate(new, axis=1)
    s_ref[...] = s_cur

    @pl.when(first)
    def _():
        fin_ref[...] = s_cur


def _gla_out_kernel(qk_ref, v_ref, r_ref, sm_ref, gwh_ref, gwl_ref, gb_ref, ng_ref, init_ref, enter_ref,
                    y_ref, fin_ref, s_ref, *, regions):
    cq = GLA_CHUNK
    st = SCAN_STEP
    ncs = st // cq
    c = pl.program_id(0)
    _, first, last = _chunk_pos(c, st, regions)

    @pl.when(first)
    def _():
        s_ref[...] = init_ref[...]

    lg = _gla_log_decay(sm_ref[...], gwh_ref, gwl_ref, gb_ref)
    gcs = _dot_exact_lhs(_block_tri(st, cq, False).astype(BF16), lg[:, 0:GLA_KDIM])
    rev = _dot_exact_lhs(_block_tri(st, cq, True).astype(BF16), lg[:, GLA_KDIM:2 * GLA_KDIM])
    lo_half = lax.broadcasted_iota(jnp.int32, (1, LANES), 1) < GLA_DK
    zero_b = jnp.zeros((cq, LANES), BF16)
    ii = lax.broadcasted_iota(jnp.int32, (2 * cq, cq), 0) % cq
    jj = lax.broadcasted_iota(jnp.int32, (2 * cq, cq), 1)
    tril2 = jj <= ii
    triu2 = jj >= ii
    ng = ng_ref[...]

    def stack_heads(x):
        return jnp.concatenate([jnp.where(lo_half, x, zero_b), jnp.where(lo_half, zero_b, x)], axis=0)

    s_cur = s_ref[...]
    for ci in range(ncs):
        rows = slice(ci * cq, (ci + 1) * cq)
        g_f = gcs[rows]
        g_b = rev[rows]
        etot = jnp.exp(g_f[cq - 1:cq, :])
        qs = qk_ref[rows, 0:GLA_KDIM].astype(F32) * (GLA_DK ** -0.5)
        ks = qk_ref[rows, GLA_KDIM:2 * GLA_KDIM].astype(F32)
        qe_f = (qs * jnp.exp(g_f)).astype(BF16)
        ke = ks * jnp.exp(-g_f)
        ke_f = ke.astype(BF16)
        ko_f = (ke * etot).astype(BF16)
        qe_b = (qs * jnp.exp(g_b)).astype(BF16)
        ke_b = (ks * jnp.exp(-g_b)).astype(BF16)
        s_curb = s_cur.astype(BF16)
        s_entb = enter_ref[ci].astype(BF16)
        new = []
        for p in range(GLA_HEADS // 2):
            sl = slice(p * LANES, (p + 1) * LANES)
            qf2 = stack_heads(qe_f[:, sl])
            qb2 = stack_heads(qe_b[:, sl])
            att = (jnp.where(tril2, _dot_nt(qf2, ke_f[:, sl]), 0.0)
                   + jnp.where(triu2, _dot_nt(qb2, ke_b[:, sl]), 0.0)).astype(BF16)
            inter = _dot_nt(qf2, s_curb[:, sl]) + _dot_nt(qb2, s_entb[:, sl])
            upd = []
            for hh in (0, 1):
                h = 2 * p + hh
                hs = slice(h * GLA_DV, (h + 1) * GLA_DV)
                hr = slice(hh * cq, (hh + 1) * cq)
                vh = v_ref[rows, hs]
                o = _dot(att[hr], vh) + inter[hr]
                o = o * lax.rsqrt(jnp.mean(o * o, axis=-1, keepdims=True) + EPS) * ng[:, hs]
                y_ref[rows, hs] = (o * _silu(r_ref[rows, hs].astype(F32))).astype(y_ref.dtype)
                upd.append(_dot_tn(vh, ko_f[:, sl]))
            new.append(s_cur[:, sl] * etot[:, sl] + jnp.where(lo_half, upd[0], upd[1]))
        s_cur = jnp.concatenate(new, axis=1)
    s_ref[...] = s_cur

    @pl.when(last)
    def _():
        fin_ref[...] = s_cur


def _gla_mixer(proj, small, init, lw, regions):
    st = SCAN_STEP
    ncs = st // GLA_CHUNK
    t = proj.shape[0]
    n_steps = t // st
    n_seq = regions[0][0] + regions[1][0]
    qkcol, vcol, rcol = COL_QK // 1024, COL_V // 1024, COL_R // 1024
    state_block = (None, None, GLA_DV, GLA_KDIM)
    enter_block = (ncs, GLA_DV, GLA_KDIM)
    gwh, gwl, gb = lw["gate_w_hi"], lw["gate_w_lo"], lw["gate_b"]

    def const_spec(a):
        return pl.BlockSpec(a.shape, lambda s: (0,) * a.ndim)

    bidx = lambda s: n_steps - 1 - s
    seq_b = lambda s: _chunk_pos(bidx(s), st, regions)[0]
    enter_b, fin_b = pl.pallas_call(
        functools.partial(_gla_bwd_kernel, regions=regions, n_steps=n_steps),
        out_shape=(jax.ShapeDtypeStruct((n_steps * ncs, GLA_DV, GLA_KDIM), F32),
                   jax.ShapeDtypeStruct((n_seq, GLA_DV, GLA_KDIM), F32)),
        grid=(n_steps,),
        in_specs=[pl.BlockSpec((st, 2 * GLA_KDIM), lambda s: (bidx(s), qkcol)),
                  pl.BlockSpec((st, GLA_VDIM), lambda s: (bidx(s), vcol)),
                  pl.BlockSpec((st, LANES), lambda s: (bidx(s), 0)),
                  const_spec(gwh), const_spec(gwl), const_spec(gb),
                  pl.BlockSpec(state_block, lambda s: (seq_b(s), 1, 0, 0))],
        out_specs=(pl.BlockSpec(enter_block, lambda s: (bidx(s), 0, 0)),
                   pl.BlockSpec((None, GLA_DV, GLA_KDIM), lambda s: (seq_b(s), 0, 0))),
        scratch_shapes=[pltpu.VMEM((GLA_DV, GLA_KDIM), F32)],
        compiler_params=_params("arbitrary"),
        name="gla_backward_states",
    )(proj, proj, small, gwh, gwl, gb, init)

    seq_f = lambda s: _chunk_pos(s, st, regions)[0]
    ng = lw["gla_norm_g"]
    y, fin_f = pl.pallas_call(
        functools.partial(_gla_out_kernel, regions=regions),
        out_shape=(jax.ShapeDtypeStruct((t, GLA_VDIM), BF16),
                   jax.ShapeDtypeStruct((n_seq, GLA_DV, GLA_KDIM), F32)),
        grid=(n_steps,),
        in_specs=[pl.BlockSpec((st, 2 * GLA_KDIM), lambda s: (s, qkcol)),
                  pl.BlockSpec((st, GLA_VDIM), lambda s: (s, vcol)),
                  pl.BlockSpec((st, GLA_VDIM), lambda s: (s, rcol)),
                  pl.BlockSpec((st, LANES), lambda s: (s, 0)),
                  const_spec(gwh), const_spec(gwl), const_spec(gb), const_spec(ng),
                  pl.BlockSpec(state_block, lambda s: (seq_f(s), 0, 0, 0)),
                  pl.BlockSpec(enter_block, lambda s: (s, 0, 0))],
        out_specs=(pl.BlockSpec((st, GLA_VDIM), lambda s: (s, 0)),
                   pl.BlockSpec((None, GLA_DV, GLA_KDIM), lambda s: (seq_f(s), 0, 0))),
        scratch_shapes=[pltpu.VMEM((GLA_DV, GLA_KDIM), F32)],
        compiler_params=_params("arbitrary"),
        name="gla_forward_outputs",
    )(proj, proj, proj, small, gwh, gwl, gb, ng, init, enter_b)
    return y, fin_f, fin_b


def _merge_kernel(x_ref, ys_ref, yg_ref, ga_ref, gb_ref, mod_ref, wbs_ref, wbg_ref, wo_ref, o_ref):
    merged = (_sigmoid(ga_ref[...].astype(F32)) * _dot(ys_ref[...], wbs_ref[...])
              + _sigmoid(gb_ref[...].astype(F32)) * _dot(yg_ref[...], wbg_ref[...]))
    mix = _dot(merged.astype(BF16), wo_ref[...])
    o_ref[...] = x_ref[...] + mod_ref[:, 2 * D_MODEL:3 * D_MODEL] * mix


def _merge(x, y_ssd, y_gla, proj, mods, lw, regions):
    t = x.shape[0]
    tm = min(TM_MERGE, regions[0][0] * regions[0][1], regions[1][1])
    tok = lambda col: pl.BlockSpec((tm, D_MODEL), lambda i: (i, col))
    wspec = pl.BlockSpec((D_MODEL, D_MODEL), lambda i: (0, 0))
    return pl.pallas_call(
        _merge_kernel,
        out_shape=jax.ShapeDtypeStruct((t, D_MODEL), F32),
        grid=(t // tm,),
        in_specs=[tok(0), tok(0), tok(0), tok(COL_GA // D_MODEL), tok(COL_GB // D_MODEL),
                  pl.BlockSpec((None, 1, 6 * D_MODEL), lambda i: (_mod_row(i, tm, regions), 0, 0)),
                  wspec, wspec, wspec],
        out_specs=tok(0),
        compiler_params=_params("parallel"),
        name="merge_out_projection",
    )(x, y_ssd, y_gla, proj, proj, mods, lw["w_bs"], lw["w_bg"], lw["w_o"])


def _ffn_prologue(x_ref, mod_ref, g_ref):
    y = _rms(x_ref[...], g_ref[...])
    return y * (1.0 + mod_ref[:, 4 * D_MODEL:5 * D_MODEL]) + mod_ref[:, 3 * D_MODEL:4 * D_MODEL]


def _swiglu_partial(v, w1_ref, w3_ref, w2_ref):
    h = _silu(_dot(v, w1_ref[...])) * _dot(v, w3_ref[...])
    return _dot(h.astype(BF16), w2_ref[...])


def _ffn_kernel(x_ref, mod_ref, g_ref, w1_ref, w3_ref, w2_ref, o_ref, v_ref, acc_ref):
    k = pl.program_id(1)

    @pl.when(k == 0)
    def _():
        v_ref[...] = _ffn_prologue(x_ref, mod_ref, g_ref).astype(BF16)
        acc_ref[...] = jnp.zeros_like(acc_ref)

    acc_ref[...] += _swiglu_partial(v_ref[...], w1_ref, w3_ref, w2_ref)

    @pl.when(k == pl.num_programs(1) - 1)
    def _():
        o_ref[...] = x_ref[...] + mod_ref[:, 5 * D_MODEL:6 * D_MODEL] * acc_ref[...]


def _dense_ffn(x, mods, g, w1, w3, w2, regions):
    t = x.shape[0]
    tm = min(TM_FFN, regions[0][0] * regions[0][1], regions[1][1])
    hid = w1.shape[1]
    return pl.pallas_call(
        _ffn_kernel,
        out_shape=jax.ShapeDtypeStruct((t, D_MODEL), F32),
        grid=(t // tm, hid // TH_FFN),
        in_specs=[pl.BlockSpec((tm, D_MODEL), lambda i, k: (i, 0)),
                  pl.BlockSpec((None, 1, 6 * D_MODEL), lambda i, k: (_mod_row(i, tm, regions), 0, 0)),
                  pl.BlockSpec((1, D_MODEL), lambda i, k: (0, 0)),
                  pl.BlockSpec((D_MODEL, TH_FFN), lambda i, k: (0, k)),
                  pl.BlockSpec((D_MODEL, TH_FFN), lambda i, k: (0, k)),
                  pl.BlockSpec((TH_FFN, D_MODEL), lambda i, k: (k, 0))],
        out_specs=pl.BlockSpec((tm, D_MODEL), lambda i, k: (i, 0)),
        scratch_shapes=[pltpu.VMEM((tm, D_MODEL), BF16), pltpu.VMEM((tm, D_MODEL), F32)],
        compiler_params=_params("parallel", "arbitrary"),
        name="dense_swiglu",
    )(x, mods, g, w1, w3, w2)


def _top2_gates(logits):
    lane = lax.broadcasted_iota(jnp.int32, logits.shape, 1)
    lg = jnp.where(lane < N_EXPERTS, logits, -jnp.inf)
    m1 = jnp.max(lg, axis=-1, keepdims=True)
    i1 = jnp.min(jnp.where(lg == m1, lane, LANES), axis=-1, keepdims=True)
    lg2 = jnp.where(lane == i1, -jnp.inf, lg)
    m2 = jnp.max(lg2, axis=-1, keepdims=True)
    i2 = jnp.min(jnp.where(lg2 == m2, lane, LANES), axis=-1, keepdims=True)
    e2 = jnp.exp(m2 - m1)
    den = 1.0 + e2
    return jnp.where(lane == i1, 1.0 / den, 0.0) + jnp.where(lane == i2, e2 / den, 0.0)


def _lane_scalar(vec, e):
    lane = lax.broadcasted_iota(jnp.int32, vec.shape, 1)
    return jnp.sum(jnp.where(lane == e, vec, 0.0)).astype(jnp.int32)


def _n_row_blocks(n):
    return (n + MOE_ROW_BLOCK - 1) // MOE_ROW_BLOCK


def _moe_block_copy(stage_ref, sem, slot, hbm_ref, row, to_hbm):
    if not isinstance(row, int):
        row = pl.multiple_of(row, MOE_ROW_ALIGN)
    rows = hbm_ref.at[pl.ds(row, MOE_ROW_BLOCK), :]
    if to_hbm:
        return pltpu.make_async_copy(stage_ref.at[slot], rows, sem.at[slot])
    return pltpu.make_async_copy(rows, stage_ref.at[slot], sem.at[slot])


def _moe_dispatch_kernel(x_ref, mod_ref, g_ref, rw_ref, rb_ref, donor_hbm,
                         xs_hbm, gate_ref, rank_ref, off_ref, tot_ref,
                         v_ref, gate_t_ref, rank_t_ref, stage_ref, sem, run_ref, pend_ref, *, cap):
    del donor_hbm
    i = pl.program_id(0)
    tm = x_ref.shape[0]
    rb = MOE_ROW_BLOCK

    @pl.when(i == 0)
    def _():
        run_ref[...] = jnp.zeros_like(run_ref)
        pend_ref[0] = 0

    v = _ffn_prologue(x_ref, mod_ref, g_ref)
    v_ref[...] = v.astype(BF16)
    gate = _top2_gates(_dot_hilo(v, rw_ref[...]) + rb_ref[...])
    sel_f = jnp.where(gate > 0.0, 1.0, 0.0)
    sel = sel_f.astype(BF16)
    gate_ref[...] = gate
    gate_t_ref[...] = gate.T
    for blk in range(tm // rb):
        ii = lax.broadcasted_iota(jnp.int32, (rb, tm), 0) + blk * rb
        jj = lax.broadcasted_iota(jnp.int32, (rb, tm), 1)
        rank = _dot(jnp.where(jj < ii, 1.0, 0.0).astype(BF16), sel)
        rank_ref[blk * rb:(blk + 1) * rb, :] = rank
        rank_t_ref[:, blk * rb:(blk + 1) * rb] = rank.T

    cnt = jnp.sum(sel_f, axis=0, keepdims=True)
    cnt_pad = jnp.floor((cnt + (MOE_ROW_ALIGN - 1)) * (1.0 / MOE_ROW_ALIGN)) * MOE_ROW_ALIGN
    run = run_ref[...]
    off_ref[...] = run

    def wait_prev(s, carry):
        _moe_block_copy(stage_ref, sem, s, xs_hbm, 0, True).wait()
        return carry
    lax.fori_loop(0, pend_ref[0], wait_prev, 0)

    slot = jnp.int32(0)
    for e in range(N_EXPERTS):
        n_e = _lane_scalar(cnt, e)
        base = e * cap + _lane_scalar(run, e)
        g_row = gate_t_ref[e:e + 1, :]
        r_row = rank_t_ref[e:e + 1, :].astype(jnp.int32)

        def pack(b, s, base=base, g_row=g_row, r_row=r_row):
            row = lax.broadcasted_iota(jnp.int32, (rb, tm), 0) + b * rb
            onehot = jnp.where(jnp.logical_and(row == r_row, g_row > 0.0), 1.0, 0.0).astype(BF16)
            stage_ref[s] = _dot(onehot, v_ref[...]).astype(BF16)
            _moe_block_copy(stage_ref, sem, s, xs_hbm, base + b * rb, True).start()
            return s + 1
        slot = lax.fori_loop(0, _n_row_blocks(n_e), pack, slot)

    pend_ref[0] = slot
    run_ref[...] = run + cnt_pad
    tot_ref[...] = run + cnt_pad

    @pl.when(i == pl.num_programs(0) - 1)
    def _():
        def wait_last(s, carry):
            _moe_block_copy(stage_ref, sem, s, xs_hbm, 0, True).wait()
            return carry
        lax.fori_loop(0, slot, wait_last, 0)


def _moe_expert_kernel(te_ref, tr_ref, na_ref, xs_ref, w1_ref, w3_ref, w2_ref, ys_ref):
    del te_ref, tr_ref

    @pl.when(pl.program_id(0) < na_ref[0])
    def _():
        x = xs_ref[...]
        hid = w1_ref.shape[1]
        acc = None
        for hh in range(hid // TH_FFN):
            cols = slice(hh * TH_FFN, (hh + 1) * TH_FFN)
            h = _silu(_dot(x, w1_ref[:, cols])) * _dot(x, w3_ref[:, cols])
            part = _dot(h.astype(BF16), w2_ref[cols, :])
            acc = part if acc is None else acc + part
        ys_ref[...] = acc.astype(ys_ref.dtype)


def _moe_combine_kernel(off_ref, x_ref, mod_ref, gate_ref, rank_ref, ys_hbm, o_ref, stage_ref, sem, *, cap):
    i = pl.program_id(0)
    tm = x_ref.shape[0]
    rb = MOE_ROW_BLOCK
    gate = gate_ref[...]
    rank = rank_ref[...]
    cnt = jnp.sum(jnp.where(gate > 0.0, 1.0, 0.0), axis=0, keepdims=True)
    n_blocks = [_n_row_blocks(_lane_scalar(cnt, e)) for e in range(N_EXPERTS)]
    bases = [e * cap + off_ref[i * LANES + e] for e in range(N_EXPERTS)]

    slot = jnp.int32(0)
    for e in range(N_EXPERTS):
        def fetch(b, s, base=bases[e]):
            _moe_block_copy(stage_ref, sem, s, ys_hbm, base + b * rb, False).start()
            return s + 1
        slot = lax.fori_loop(0, n_blocks[e], fetch, slot)

    o_ref[...] = jnp.zeros_like(o_ref)
    slot = jnp.int32(0)
    for e in range(N_EXPERTS):
        g_col = gate[:, e:e + 1]
        r_col = rank[:, e:e + 1].astype(jnp.int32)

        def combine(b, s, g_col=g_col, r_col=r_col):
            _moe_block_copy(stage_ref, sem, s, ys_hbm, 0, False).wait()
            col = lax.broadcasted_iota(jnp.int32, (tm, rb), 1) + b * rb
            onehot = jnp.where(jnp.logical_and(col == r_col, g_col > 0.0), 1.0, 0.0).astype(BF16)
            o_ref[...] += g_col * _dot(onehot, stage_ref[s])
            return s + 1
        slot = lax.fori_loop(0, n_blocks[e], combine, slot)

    o_ref[...] = x_ref[...] + mod_ref[:, 5 * D_MODEL:6 * D_MODEL] * o_ref[...]


def _moe_tile_table(totals, cap, n_steps):
    extent = totals + MOE_ROW_BLOCK
    n_tiles = (extent + TR_EXPERT - 1) // TR_EXPERT
    ends = jnp.cumsum(n_tiles)
    starts = ends - n_tiles
    n_active = ends[-1]
    t = jnp.minimum(jnp.arange(n_steps, dtype=jnp.int32), n_active - 1)
    e = jnp.sum((t[:, None] >= ends[None, :]).astype(jnp.int32), axis=1)
    row_block = e * (cap // TR_EXPERT) + (t - starts[e])
    return e.astype(jnp.int32), row_block.astype(jnp.int32), n_active.reshape(1).astype(jnp.int32)


def _moe_geometry(t, regions, n_e):
    tm = min(TM_MOE, regions[0][0] * regions[0][1], regions[1][1])
    n_tiles = t // tm
    cap_rows = t + n_tiles * (MOE_ROW_ALIGN - 1) + MOE_ROW_BLOCK
    cap = -(-cap_rows // TR_EXPERT) * TR_EXPERT
    return tm, n_tiles, cap


def _moe_ffn(x, mods, g, rw, rb, w1, w3, w2, regions, donor):
    t = x.shape[0]
    n_e, _, hid = w1.shape
    tm, n_tiles, cap = _moe_geometry(t, regions, n_e)
    max_slots = 2 * tm // MOE_ROW_BLOCK + n_e
    n_steps = (2 * t + n_tiles * n_e * (MOE_ROW_ALIGN - 1) + n_e * MOE_ROW_BLOCK) // TR_EXPERT + n_e
    mod_spec = pl.BlockSpec((None, 1, 6 * D_MODEL), lambda i, *_: (_mod_row(i, tm, regions), 0, 0))
    stage = [pltpu.VMEM((max_slots, MOE_ROW_BLOCK, D_MODEL), BF16), pltpu.SemaphoreType.DMA((max_slots,))]

    xs, gate, rank, offs, totals = pl.pallas_call(
        functools.partial(_moe_dispatch_kernel, cap=cap),
        out_shape=(jax.ShapeDtypeStruct((n_e * cap, D_MODEL), BF16),
                   jax.ShapeDtypeStruct((t, LANES), F32), jax.ShapeDtypeStruct((t, LANES), F32),
                   jax.ShapeDtypeStruct((n_tiles, 1, LANES), F32), jax.ShapeDtypeStruct((1, LANES), F32)),
        grid=(n_tiles,),
        in_specs=[pl.BlockSpec((tm, D_MODEL), lambda i: (i, 0)), mod_spec,
                  pl.BlockSpec((1, D_MODEL), lambda i: (0, 0)),
                  pl.BlockSpec((D_MODEL, LANES), lambda i: (0, 0)),
                  pl.BlockSpec((1, LANES), lambda i: (0, 0)),
                  pl.BlockSpec(memory_space=pl.ANY)],
        input_output_aliases={5: 0},
        out_specs=(pl.BlockSpec(memory_space=pl.ANY),
                   pl.BlockSpec((tm, LANES), lambda i: (i, 0)), pl.BlockSpec((tm, LANES), lambda i: (i, 0)),
                   pl.BlockSpec((None, 1, LANES), lambda i: (i, 0, 0)),
                   pl.BlockSpec((1, LANES), lambda i: (0, 0))),
        scratch_shapes=[pltpu.VMEM((tm, D_MODEL), BF16),
                        pltpu.VMEM((LANES, tm), F32), pltpu.VMEM((LANES, tm), F32)] + stage
        + [pltpu.VMEM((1, LANES), F32), pltpu.SMEM((1,), jnp.int32)],
        compiler_params=_params("arbitrary"),
        name="moe_dispatch",
    )(x, mods, g, rw, rb, donor)

    tile_e, tile_rb, n_active = _moe_tile_table(totals[0, :n_e].astype(jnp.int32), cap, n_steps)
    ys = pl.pallas_call(
        _moe_expert_kernel,
        out_shape=jax.ShapeDtypeStruct((n_e * cap, D_MODEL), BF16),
        grid_spec=pltpu.PrefetchScalarGridSpec(
            num_scalar_prefetch=3,
            grid=(n_steps,),
            in_specs=[pl.BlockSpec((TR_EXPERT, D_MODEL), lambda s, te, tr, na: (tr[s], 0)),
                      pl.BlockSpec((None, D_MODEL, hid), lambda s, te, tr, na: (te[s], 0, 0)),
                      pl.BlockSpec((None, D_MODEL, hid), lambda s, te, tr, na: (te[s], 0, 0)),
                      pl.BlockSpec((None, hid, D_MODEL), lambda s, te, tr, na: (te[s], 0, 0))],
            out_specs=pl.BlockSpec((TR_EXPERT, D_MODEL), lambda s, te, tr, na: (tr[s], 0))),
        input_output_aliases={3: 0},
        compiler_params=_params("arbitrary"),
        name="moe_experts",
    )(tile_e, tile_rb, n_active, xs, w1, w3, w2)

    out = pl.pallas_call(
        functools.partial(_moe_combine_kernel, cap=cap),
        out_shape=jax.ShapeDtypeStruct((t, D_MODEL), F32),
        grid_spec=pltpu.PrefetchScalarGridSpec(
            num_scalar_prefetch=1,
            grid=(n_tiles,),
            in_specs=[pl.BlockSpec((tm, D_MODEL), lambda i, off: (i, 0)), mod_spec,
                      pl.BlockSpec((tm, LANES), lambda i, off: (i, 0)),
                      pl.BlockSpec((tm, LANES), lambda i, off: (i, 0)),
                      pl.BlockSpec(memory_space=pl.ANY)],
            out_specs=pl.BlockSpec((tm, D_MODEL), lambda i, off: (i, 0)),
            scratch_shapes=stage),
        compiler_params=_params("arbitrary"),
        name="moe_combine",
    )(offs.reshape(-1).astype(jnp.int32), x, mods, gate, rank, ys)
    return out, ys


def _final_norm_kernel(x_ref, g_ref, o_ref):
    o_ref[...] = _rms(x_ref[...], g_ref[...])


def _final_norm(x, g, tok_off, n_tok):
    tm = min(TM_NORM, n_tok)
    off = tok_off // tm
    return pl.pallas_call(
        _final_norm_kernel,
        out_shape=jax.ShapeDtypeStruct((n_tok, D_MODEL), F32),
        grid=(n_tok // tm,),
        in_specs=[pl.BlockSpec((tm, D_MODEL), lambda i: (i + off, 0)),
                  pl.BlockSpec((1, D_MODEL), lambda i: (0, 0))],
        out_specs=pl.BlockSpec((tm, D_MODEL), lambda i: (i, 0)),
        compiler_params=_params("parallel"),
        name="final_norm",
    )(x, g)


def _grid_pos_embed(l, d):
    rows = l // GRID_W
    row = jnp.repeat(jnp.arange(rows, dtype=F32), GRID_W)
    col = jnp.tile(jnp.arange(GRID_W, dtype=F32), rows)
    quarter = d // 4
    omega = jnp.exp(-math.log(10000.0) * jnp.arange(quarter, dtype=F32) / quarter)
    er = row[:, None] * omega
    ec = col[:, None] * omega
    return jnp.concatenate([jnp.sin(er), jnp.cos(er), jnp.sin(ec), jnp.cos(ec)], axis=-1)


def _pad_lanes(a, width):
    return jnp.pad(a, [(0, 0)] * (a.ndim - 1) + [(0, width - a.shape[-1])])


def _reorder_w_in(w):
    o = np.cumsum([0, SSD_INNER, SSD_INNER + 2 * SSD_GROUPS * SSD_STATE, 2 * SSD_HEADS, GLA_KDIM, GLA_KDIM,
                   GLA_VDIM, GLA_VDIM, 2 * GLA_GATE_RANK, D_MODEL, D_MODEL])
    z, xbc, dt, qq, kk, vv, rr, lr, ga, gb = [w[:, int(o[i]):int(o[i + 1])] for i in range(10)]
    main = jnp.concatenate([z, vv, rr, ga, gb, qq, kk], axis=1).astype(BF16)
    small = _pad_lanes(jnp.concatenate([dt, lr], axis=1), LANES).astype(BF16)
    return main, xbc.astype(BF16), small


def _layer_weights(i, p):
    conv_w = jnp.pad(p["ssd_conv_w"][i], ((0, SUBLANES - SSD_CONV), (0, 0)))
    conv_b = p["ssd_conv_b"][i][None, :]
    gate_w = jnp.zeros((LANES, 2 * GLA_KDIM), F32)
    gate_w = gate_w.at[SM_LR:SM_LR + GLA_GATE_RANK, :GLA_KDIM].set(p["gla_gate_w"][i, 0])
    gate_w = gate_w.at[SM_LR + GLA_GATE_RANK:SM_LR + 2 * GLA_GATE_RANK, GLA_KDIM:].set(p["gla_gate_w"][i, 1])
    gate_w_hi = gate_w.astype(BF16)
    gate_w_lo = (gate_w - gate_w_hi.astype(F32)).astype(BF16)
    w_main, w_conv, w_small = _reorder_w_in(p["w_in"][i])
    return {
        "w_in": w_main, "w_in_conv": w_conv, "w_in_small": w_small,
        "conv_w": conv_w, "conv_b": conv_b,
        "dt_bias": _pad_lanes(p["ssd_dt_bias"][i].reshape(1, -1), LANES),
        "a_log": _pad_lanes(p["ssd_a_log"][i].reshape(1, -1), LANES),
        "d_exp": jnp.repeat(p["ssd_d"][i], SSD_HEADDIM)[None, :],
        "ssd_norm_g": p["ssd_norm_g"][i][None, :],
        "gate_w_hi": gate_w_hi, "gate_w_lo": gate_w_lo, "gate_b": p["gla_gate_b"][i].reshape(1, -1),
        "gla_norm_g": p["gla_norm_g"][i][None, :],
        "w_bs": p["w_branch_ssd"][i].astype(BF16), "w_bg": p["w_branch_gla"][i].astype(BF16),
        "w_o": p["w_out"][i].astype(BF16),
    }


def _ssd_state_in(s):
    b = s.shape[0]
    return jnp.transpose(s, (0, 1, 4, 2, 3)).reshape(b, 2, SSD_STATE, SSD_INNER)


def _ssd_state_out(f, b):
    s = jnp.stack([f, b], axis=1).reshape(-1, 2, SSD_STATE, SSD_HEADS, SSD_HEADDIM)
    return jnp.transpose(s, (0, 1, 3, 4, 2))


def _gla_state_in(s):
    b = s.shape[0]
    return jnp.transpose(s, (0, 1, 4, 2, 3)).reshape(b, 2, GLA_DV, GLA_KDIM)


def _gla_state_out(f, b):
    s = jnp.stack([f, b], axis=1).reshape(-1, 2, GLA_DV, GLA_HEADS, GLA_DK)
    return jnp.transpose(s, (0, 1, 3, 4, 2))


def _trunk(x_prompt, x_sample, state_ssd, state_gla, c, c_ctx, p):
    n0, l0, _ = x_prompt.shape
    n1, l1, _ = x_sample.shape
    regions = ((n0, l0), (n1, l1))
    depth = p["w_in"].shape[0]

    cc = jnp.zeros((MOD_ROWS, D_MODEL), F32).at[0].set(c_ctx).at[1:1 + n1].set(c)
    mods = _modulation_table(cc, p["ada_w"], p["ada_b"])[:, :, None, :]
    x = _assemble_tokens(x_prompt, x_sample, _grid_pos_embed(l1, D_MODEL), regions)

    ssd_states, gla_states = [], []
    moe_rows = None
    for i in range(depth):
        lw = _layer_weights(i, p)
        init_ssd = jnp.concatenate([jnp.zeros((n0, 2, SSD_STATE, SSD_INNER), F32),
                                    _ssd_state_in(state_ssd[:, i])], axis=0)
        init_gla = jnp.concatenate([jnp.zeros((n0, 2, GLA_DV, GLA_KDIM), F32),
                                    _gla_state_in(state_gla[:, i])], axis=0)
        proj, xbc, small = _in_projection(x, mods[i], p["norm1_g"][i][None, :], lw, regions)
        y_ssd, sf, sb = _ssd_mixer(proj, xbc, small, init_ssd, lw, regions)
        y_gla, gf, gb = _gla_mixer(proj, small, init_gla, lw, regions)
        ssd_states.append(_ssd_state_out(sf[:n0], sb[:n0]))
        gla_states.append(_gla_state_out(gf[:n0], gb[:n0]))
        x = _merge(x, y_ssd, y_gla, proj, mods[i], lw, regions)
        j = i // 2
        g2 = p["norm2_g"][i][None, :]
        if i % 2 == 0:
            x = _dense_ffn(x, mods[i], g2, p["ffn_w1"][j].astype(BF16), p["ffn_w3"][j].astype(BF16),
                           p["ffn_w2"][j].astype(BF16), regions)
        else:
            if moe_rows is None:
                n_e = p["moe_w1"].shape[1]
                moe_rows = jnp.zeros((n_e * _moe_geometry(x.shape[0], regions, n_e)[2], D_MODEL), BF16)
            x, moe_rows = _moe_ffn(x, mods[i], g2, _pad_lanes(p["router_w"][j], LANES),
                                   _pad_lanes(p["router_b"][j][None, :], LANES),
                                   p["moe_w1"][j].astype(BF16), p["moe_w3"][j].astype(BF16),
                                   p["moe_w2"][j].astype(BF16), regions, moe_rows)

    gfin = p["final_norm_g"][None, :]
    y_prompt = _final_norm(x, gfin, 0, n0 * l0).reshape(n0, l0, D_MODEL)
    y_sample = _final_norm(x, gfin, n0 * l0, n1 * l1).reshape(n1, l1, D_MODEL)
    return y_prompt, y_sample, jnp.stack(ssd_states, axis=1), jnp.stack(gla_states, axis=1)


def kernel(x_prompt, x_sample, state_ssd, state_gla, c, c_ctx, ada_w, ada_b, norm1_g, norm2_g, w_in, ssd_conv_w, ssd_conv_b, ssd_dt_bias, ssd_a_log, ssd_d, ssd_norm_g, gla_gate_w, gla_gate_b, gla_norm_g, w_branch_ssd, w_branch_gla, w_out, ffn_w1, ffn_w3, ffn_w2, router_w, router_b, moe_w1, moe_w3, moe_w2, final_norm_g):
    p = dict(ada_w=ada_w, ada_b=ada_b, norm1_g=norm1_g, norm2_g=norm2_g, w_in=w_in, ssd_conv_w=ssd_conv_w,
             ssd_conv_b=ssd_conv_b, ssd_dt_bias=ssd_dt_bias, ssd_a_log=ssd_a_log, ssd_d=ssd_d,
             ssd_norm_g=ssd_norm_g, gla_gate_w=gla_gate_w, gla_gate_b=gla_gate_b, gla_norm_g=gla_norm_g,
             w_branch_ssd=w_branch_ssd, w_branch_gla=w_branch_gla, w_out=w_out, ffn_w1=ffn_w1, ffn_w3=ffn_w3,
             ffn_w2=ffn_w2, router_w=router_w, router_b=router_b, moe_w1=moe_w1, moe_w3=moe_w3, moe_w2=moe_w2,
             final_norm_g=final_norm_g)
    return _trunk(x_prompt, x_sample, state_ssd, state_gla, c, c_ctx, p)
```

```python
import functools
import math

import numpy as np
import jax
import jax.numpy as jnp
from jax import lax
from jax.experimental import pallas as pl
from jax.experimental.pallas import tpu as pltpu

F32 = jnp.float32
BF16 = jnp.bfloat16

D_MODEL = 1024
GRID_W = 64
EPS = 1e-6
SSD_HEADS = 16
SSD_HEADDIM = 64
SSD_INNER = SSD_HEADS * SSD_HEADDIM
SSD_GROUPS = 2
SSD_STATE = 64
SSD_CONV = 5
SSD_CHUNK = 128
GLA_HEADS = 8
GLA_DK = 64
GLA_DV = 128
GLA_KDIM = GLA_HEADS * GLA_DK
GLA_VDIM = GLA_HEADS * GLA_DV
GLA_GATE_RANK = 16
GLA_GATE_TAU = 16.0
GLA_CHUNK = 64
N_EXPERTS = 8

LANES = 128
SUBLANES = 8
BF16_SUBLANES = 16
VMEM_LIMIT_BYTES = 56 * 1024 * 1024

COL_Z, COL_V, COL_R, COL_GA, COL_GB, COL_QK = 0, 1024, 2048, 3072, 4096, 5120
D_PROJ = 6144
CONV_X, CONV_BC = 0, 1024
D_CONV = 1280
SM_LR = 2 * SSD_HEADS

TM_PROJ, TN_PROJ = 1024, 3072
TM_MERGE = 512
TM_FFN, TH_FFN = 512, 1408
TM_MOE = 512
MOE_ROW_BLOCK = 256
MOE_ROW_ALIGN = 16
TR_EXPERT = 512
TM_NORM = 1024
TN_MODS = 1536
MOD_ROWS = 16
SCAN_STEP = 256


def _params(*sem):
    return pltpu.CompilerParams(dimension_semantics=sem, vmem_limit_bytes=VMEM_LIMIT_BYTES)


def _sigmoid(x):
    return 0.5 * jnp.tanh(0.5 * x) + 0.5


def _silu(x):
    return x * _sigmoid(x)


def _softplus(x):
    return jnp.maximum(x, 0.0) + jnp.log(1.0 + jnp.exp(-jnp.abs(x)))


def _split2(a):
    a1 = a.astype(BF16)
    a2 = (a - a1.astype(F32)).astype(BF16)
    return a1, a2


def _split3(a):
    a1 = a.astype(BF16)
    r1 = a - a1.astype(F32)
    a2 = r1.astype(BF16)
    a3 = (r1 - a2.astype(F32)).astype(BF16)
    return a1, a2, a3


def _dot(a, b):
    return jnp.dot(a, b, preferred_element_type=F32)


def _dot_nt(a, b):
    return lax.dot_general(a, b, (((1,), (1,)), ((), ())), preferred_element_type=F32)


def _dot_tn(a, b):
    return lax.dot_general(a, b, (((0,), (0,)), ((), ())), preferred_element_type=F32)


def _dot_exact_lhs(t01, a):
    a1, a2 = _split2(a)
    return _dot(t01, a1) + _dot(t01, a2)


def _dot_f32(a, b):
    a1, a2, a3 = _split3(a)
    b1, b2, b3 = _split3(b)
    return (_dot(a1, b1) + _dot(a1, b2) + _dot(a2, b1)
            + _dot(a1, b3) + _dot(a2, b2) + _dot(a3, b1))


def _dot_hilo(a, b):
    a1, a2 = _split2(a)
    b1, b2 = _split2(b)
    return _dot(a1, b1) + _dot(a2, b1) + _dot(a1, b2)


def _block_tri(n, blk, upper):
    i = lax.broadcasted_iota(jnp.int32, (n, n), 0)
    j = lax.broadcasted_iota(jnp.int32, (n, n), 1)
    tri = (j >= i) if upper else (j <= i)
    if blk == n:
        return tri
    return jnp.logical_and(tri, (i // blk) == (j // blk))


def _rms(x, g):
    return x * lax.rsqrt(jnp.mean(x * x, axis=-1, keepdims=True) + EPS) * g


def _chunk_pos(c, q, regions):
    (n0, l0), (n1, l1) = regions
    per0, per1 = l0 // q, l1 // q
    nc0 = n0 * per0
    c1 = jnp.maximum(c - nc0, 0)
    in0 = c < nc0
    seq = jnp.where(in0, c // per0, n0 + c1 // per1)
    pos = jnp.where(in0, c % per0, c1 % per1)
    last = jnp.where(in0, per0 - 1, per1 - 1)
    return seq, pos == 0, pos == last


def _mod_row(i, tm, regions):
    (n0, l0), (n1, l1) = regions
    t1 = jnp.maximum(i * tm - n0 * l0, 0)
    return jnp.where(i * tm < n0 * l0, 0, 1 + t1 // l1)


def _assemble_kernel(xp_ref, xs_ref, pe_ref, o_ref, *, n_ctx_tiles):
    i = pl.program_id(0)

    @pl.when(i < n_ctx_tiles)
    def _():
        o_ref[...] = xp_ref[...]

    @pl.when(i >= n_ctx_tiles)
    def _():
        o_ref[...] = xs_ref[...] + pe_ref[...]


def _assemble_tokens(xp, xs, pe, regions):
    (n0, l0), (n1, l1) = regions
    tm = min(1024, l0 * n0, l1)
    t0, t1 = n0 * l0, n1 * l1
    n_ctx = t0 // tm
    pe_tiles = l1 // tm
    return pl.pallas_call(
        functools.partial(_assemble_kernel, n_ctx_tiles=n_ctx),
        out_shape=jax.ShapeDtypeStruct((t0 + t1, D_MODEL), F32),
        grid=((t0 + t1) // tm,),
        in_specs=[
            pl.BlockSpec((tm, D_MODEL), lambda i: (jnp.minimum(i, n_ctx - 1), 0)),
            pl.BlockSpec((tm, D_MODEL), lambda i: (jnp.maximum(i - n_ctx, 0), 0)),
            pl.BlockSpec((tm, D_MODEL), lambda i: (jnp.maximum(i - n_ctx, 0) % pe_tiles, 0)),
        ],
        out_specs=pl.BlockSpec((tm, D_MODEL), lambda i: (i, 0)),
        compiler_params=_params("arbitrary"),
        name="assemble_tokens",
    )(xp.reshape(t0, D_MODEL), xs.reshape(t1, D_MODEL), pe)


def _mods_kernel(cc_ref, w_ref, b_ref, o_ref):
    a = _silu(cc_ref[...])
    o_ref[...] = _dot_f32(a, w_ref[...]) + b_ref[...]


def _modulation_table(cc, ada_w, ada_b):
    depth = ada_w.shape[0]
    n = ada_w.shape[2]
    return pl.pallas_call(
        _mods_kernel,
        out_shape=jax.ShapeDtypeStruct((depth, MOD_ROWS, n), F32),
        grid=(depth, n // TN_MODS),
        in_specs=[
            pl.BlockSpec((MOD_ROWS, D_MODEL), lambda l, j: (0, 0)),
            pl.BlockSpec((None, D_MODEL, TN_MODS), lambda l, j: (l, 0, j)),
            pl.BlockSpec((None, 1, TN_MODS), lambda l, j: (l, 0, j)),
        ],
        out_specs=pl.BlockSpec((None, MOD_ROWS, TN_MODS), lambda l, j: (l, 0, j)),
        compiler_params=_params("arbitrary", "arbitrary"),
        name="modulation_table",
    )(cc, ada_w, ada_b.reshape(depth, 1, n))


def _inproj_kernel(x_ref, mod_ref, g_ref, w_ref, wc_ref, ws_ref, o_ref, oc_ref, os_ref, u_ref):
    def modnorm(x):
        y = _rms(x, g_ref[...])
        return y * (1.0 + mod_ref[:, D_MODEL:2 * D_MODEL]) + mod_ref[:, 0:D_MODEL]

    @pl.when(pl.program_id(1) == 0)
    def _():
        u_ref[...] = modnorm(x_ref[...]).astype(BF16)
        os_ref[...] = _dot(u_ref[...], ws_ref[...])

    u = u_ref[...]
    oc_ref[...] = _dot(u, wc_ref[...]).astype(BF16)
    o_ref[...] = _dot(u, w_ref[...]).astype(BF16)


def _in_projection(x, mods, g, lw, regions):
    t = x.shape[0]
    tm = min(TM_PROJ, regions[0][0] * regions[0][1], regions[1][1])
    n_col = D_PROJ // TN_PROJ
    tc = D_CONV // n_col
    return pl.pallas_call(
        _inproj_kernel,
        out_shape=(jax.ShapeDtypeStruct((t, D_PROJ), BF16), jax.ShapeDtypeStruct((t, D_CONV), BF16),
                   jax.ShapeDtypeStruct((t, LANES), F32)),
        grid=(t // tm, n_col),
        in_specs=[
            pl.BlockSpec((tm, D_MODEL), lambda i, j: (i, 0)),
            pl.BlockSpec((None, 1, 6 * D_MODEL), lambda i, j: (_mod_row(i, tm, regions), 0, 0)),
            pl.BlockSpec((1, D_MODEL), lambda i, j: (0, 0)),
            pl.BlockSpec((D_MODEL, TN_PROJ), lambda i, j: (0, j)),
            pl.BlockSpec((D_MODEL, tc), lambda i, j: (0, j)),
            pl.BlockSpec((D_MODEL, LANES), lambda i, j: (0, 0)),
        ],
        out_specs=(pl.BlockSpec((tm, TN_PROJ), lambda i, j: (i, j)),
                   pl.BlockSpec((tm, tc), lambda i, j: (i, j)),
                   pl.BlockSpec((tm, LANES), lambda i, j: (i, 0))),
        scratch_shapes=[pltpu.VMEM((tm, D_MODEL), BF16)],
        compiler_params=_params("parallel", "arbitrary"),
        name="norm_in_projection",
    )(x, mods, g, lw["w_in"], lw["w_in_conv"], lw["w_in_small"])


def _ssd_decay_terms(sm, dtb, alog, q):
    lane = lax.broadcasted_iota(jnp.int32, (1, LANES), 1)
    dtv = _softplus(sm + dtb)
    a_neg = jnp.where(lane < 2 * SSD_HEADS, -jnp.exp(alog), 0.0)
    a = dtv * a_neg
    acs = _dot_exact_lhs(_block_tri(q, q, False).astype(BF16), a)
    rev = _dot_exact_lhs(_block_tri(q, q, True).astype(BF16), a)
    m = jnp.where(lane < SSD_HEADS, acs, rev)
    tot = acs[q - 1:q, :]
    return m, tot, dtv


def _col(x, idx, width=LANES):
    return jnp.broadcast_to(x[:, idx:idx + 1], (x.shape[0], width))


def _pair_cols(x, base, p, lo_half):
    return jnp.where(lo_half, _col(x, base + 2 * p), _col(x, base + 2 * p + 1))


def _expand_state(s):
    lane = lax.broadcasted_iota(jnp.int32, s.shape, 1)
    half = SSD_INNER // SSD_GROUPS
    return jnp.concatenate([jnp.where(lane < half, s, 0.0), jnp.where(lane >= half, s, 0.0)], axis=0)


def _compact_state(s2):
    return s2[0:SSD_STATE, :] + s2[SSD_STATE:2 * SSD_STATE, :]


def _state_update_mask():
    row = lax.broadcasted_iota(jnp.int32, (SSD_GROUPS * SSD_STATE, SSD_INNER), 0)
    lane = lax.broadcasted_iota(jnp.int32, (SSD_GROUPS * SSD_STATE, SSD_INNER), 1)
    return (row < SSD_STATE) == (lane < SSD_INNER // SSD_GROUPS)


HALO = BF16_SUBLANES
CONV_PAD = SSD_CONV // 2


def _ssd_bwd_kernel(xbc_ref, prev_ref, next_ref, sm_ref, cw_ref, cb_ref, dtb_ref, alog_ref, init_ref,
                    conv_ref, enter_ref, fin_ref, s_ref, ext_ref, *, regions, n_steps):
    q = SSD_CHUNK
    st = SCAN_STEP
    c = n_steps - 1 - pl.program_id(0)
    _, first, last = _chunk_pos(c, st, regions)

    @pl.when(last)
    def _():
        s_ref[...] = _expand_state(init_ref[...])

    ext_ref[0:HALO, :] = jnp.where(first, 0.0, prev_ref[...].astype(F32))
    ext_ref[HALO:HALO + st, :] = xbc_ref[...].astype(F32)
    ext_ref[HALO + st:2 * HALO + st, :] = jnp.where(last, 0.0, next_ref[...].astype(F32))
    acc = cb_ref[...]
    for k in range(SSD_CONV):
        off = HALO - CONV_PAD + k
        acc = acc + cw_ref[k:k + 1, :] * ext_ref[off:off + st, :]
    conv_ref[...] = _silu(acc).astype(BF16)

    lo_half = lax.broadcasted_iota(jnp.int32, (1, LANES), 1) < SSD_HEADDIM
    upd_mask = _state_update_mask()

    s_cur = s_ref[...]
    for ci in reversed(range(st // q)):
        rows = slice(ci * q, (ci + 1) * q)
        xs = conv_ref[rows, 0:SSD_INNER].astype(F32)
        bsb = conv_ref[rows, SSD_INNER:SSD_INNER + LANES]
        m, tot, dtv = _ssd_decay_terms(sm_ref[rows, :], dtb_ref[...], alog_ref[...], q)
        wgt = jnp.exp(tot - m) * dtv
        etot = jnp.exp(tot)
        enter_ref[ci] = _compact_state(s_cur)
        xw, dec = [], []
        for p in range(SSD_HEADS // 2):
            sl = slice(p * LANES, (p + 1) * LANES)
            xw.append((xs[:, sl] * _pair_cols(wgt, SSD_HEADS, p, lo_half)).astype(BF16))
            dec.append(_pair_cols(etot, SSD_HEADS, p, lo_half))
        upd = _dot_tn(bsb, jnp.concatenate(xw, axis=1))
        s_cur = s_cur * jnp.concatenate(dec, axis=1) + jnp.where(upd_mask, upd, 0.0)
    s_ref[...] = s_cur

    @pl.when(first)
    def _():
        fin_ref[...] = _compact_state(s_cur)


def _ssd_out_kernel(z_ref, x_ref, bc_ref, sm_ref, dtb_ref, alog_ref, dexp_ref, ng_ref,
                    init_ref, enter_ref, y_ref, fin_ref, s_ref, *, regions):
    q = SSD_CHUNK
    c = pl.program_id(0)
    _, first, last = _chunk_pos(c, SCAN_STEP, regions)

    @pl.when(first)
    def _():
        s_ref[...] = _expand_state(init_ref[...])

    lane = lax.broadcasted_iota(jnp.int32, (1, LANES), 1)
    lo_half = lane < SSD_HEADDIM
    tril = _block_tri(q, q, False)
    ii = lax.broadcasted_iota(jnp.int32, (q, q), 0)
    jj = lax.broadcasted_iota(jnp.int32, (q, q), 1)
    eye = ii == jj
    upd_mask = _state_update_mask()
    dexp = dexp_ref[...]
    ng = ng_ref[...]

    s_cur = s_ref[...]
    for ci in range(SCAN_STEP // q):
        rows = slice(ci * q, (ci + 1) * q)
        xs = x_ref[rows, :].astype(F32)
        bsb = bc_ref[rows, 0:LANES]
        csb = bc_ref[rows, LANES:2 * LANES]
        m, tot, dtv = _ssd_decay_terms(sm_ref[rows, :], dtb_ref[...], alog_ref[...], q)
        wgt = jnp.exp(tot - m) * dtv
        mt = (m - jnp.log(dtv)).T
        ldt = jnp.log(dtv + pltpu.roll(dtv, LANES - SSD_HEADS, 1)).T
        zero_b = jnp.zeros_like(csb)
        cb = [_dot_nt(jnp.where(lo_half, csb, zero_b), bsb).astype(BF16),
              _dot_nt(jnp.where(lo_half, zero_b, csb), bsb).astype(BF16)]
        cs_f = _dot(csb, s_cur.astype(BF16))
        cs_b = _dot(csb, _expand_state(enter_ref[ci]).astype(BF16))

        ys, xw, dec = [], [], []
        for p in range(SSD_HEADS // 2):
            sl = slice(p * LANES, (p + 1) * LANES)
            g = (2 * p) // (SSD_HEADS // SSD_GROUPS)
            xs_p = xs[:, sl]
            xs_pb = xs_p.astype(BF16)
            yd, colf, colb = [], [], []
            for h in (2 * p, 2 * p + 1):
                hb = SSD_HEADS + h
                cf = _col(m, h, q)
                cbk = _col(m, hb, q)
                seg = jnp.where(tril, cf - mt[h:h + 1, :], cbk - mt[hb:hb + 1, :])
                seg = jnp.where(eye, ldt[h:h + 1, :], seg)
                yd.append(_dot(cb[g] * jnp.exp(seg).astype(BF16), xs_pb))
                colf.append(cf)
                colb.append(cbk)
            ef = jnp.exp(jnp.where(lo_half, colf[0], colf[1]))
            eb = jnp.exp(jnp.where(lo_half, colb[0], colb[1]))
            y_p = jnp.where(lo_half, yd[0], yd[1]) + ef * cs_f[:, sl] + eb * cs_b[:, sl] + dexp[:, sl] * xs_p
            ys.append(y_p)
            xw.append((xs_p * _pair_cols(wgt, 0, p, lo_half)).astype(BF16))
            dec.append(ef[q - 1:q, :])

        y = jnp.concatenate(ys, axis=1) * _silu(z_ref[rows, :].astype(F32))
        y_ref[rows, :] = _rms(y, ng).astype(y_ref.dtype)

        upd = _dot_tn(bsb, jnp.concatenate(xw, axis=1))
        s_cur = s_cur * jnp.concatenate(dec, axis=1) + jnp.where(upd_mask, upd, 0.0)
    s_ref[...] = s_cur

    @pl.when(last)
    def _():
        fin_ref[...] = _compact_state(s_cur)


def _ssd_mixer(proj, xbc, small, init, lw, regions):
    st = SCAN_STEP
    cps = st // SSD_CHUNK
    t = proj.shape[0]
    n_steps = t // st
    n_seq = regions[0][0] + regions[1][0]
    xcol, bccol = CONV_X // SSD_INNER, CONV_BC // (2 * LANES)

    def chunk_specs(cidx):
        return [
            pl.BlockSpec((st, SSD_INNER), lambda s: (cidx(s), xcol)),
            pl.BlockSpec((st, 2 * LANES), lambda s: (cidx(s), bccol)),
            pl.BlockSpec((st, LANES), lambda s: (cidx(s), 0)),
        ]

    def const_spec(a):
        return pl.BlockSpec(a.shape, lambda s: (0,) * a.ndim)

    consts = [lw["dt_bias"], lw["a_log"]]
    scratch = [pltpu.VMEM((SSD_GROUPS * SSD_STATE, SSD_INNER), F32)]
    state_block = (None, None, SSD_STATE, SSD_INNER)
    enter_block = (cps, SSD_STATE, SSD_INNER)

    bidx = lambda s: n_steps - 1 - s
    seq_b = lambda s: _chunk_pos(bidx(s), st, regions)[0]
    hb = st // HALO
    n_hb = t // HALO
    consts_b = [lw["conv_w"], lw["conv_b"]] + consts
    xbc, enter_b, fin_b = pl.pallas_call(
        functools.partial(_ssd_bwd_kernel, regions=regions, n_steps=n_steps),
        out_shape=(jax.ShapeDtypeStruct((t, D_CONV), BF16),
                   jax.ShapeDtypeStruct((n_steps * cps, SSD_STATE, SSD_INNER), F32),
                   jax.ShapeDtypeStruct((n_seq, SSD_STATE, SSD_INNER), F32)),
        grid=(n_steps,),
        in_specs=[pl.BlockSpec((st, D_CONV), lambda s: (bidx(s), 0)),
                  pl.BlockSpec((HALO, D_CONV), lambda s: (jnp.maximum(bidx(s) * hb - 1, 0), 0)),
                  pl.BlockSpec((HALO, D_CONV), lambda s: (jnp.minimum((bidx(s) + 1) * hb, n_hb - 1), 0)),
                  pl.BlockSpec((st, LANES), lambda s: (bidx(s), 0))]
        + [const_spec(a) for a in consts_b]
        + [pl.BlockSpec(state_block, lambda s: (seq_b(s), 1, 0, 0))],
        out_specs=(pl.BlockSpec((st, D_CONV), lambda s: (bidx(s), 0)),
                   pl.BlockSpec(enter_block, lambda s: (bidx(s), 0, 0)),
                   pl.BlockSpec((None, SSD_STATE, SSD_INNER), lambda s: (seq_b(s), 0, 0))),
        scratch_shapes=scratch + [pltpu.VMEM((st + 2 * HALO, D_CONV), F32)],
        compiler_params=_params("arbitrary"),
        name="ssd_backward_states",
    )(xbc, xbc, xbc, small, *consts_b, init)

    fidx = lambda s: s
    seq_f = lambda s: _chunk_pos(s, st, regions)[0]
    consts2 = consts + [lw["d_exp"], lw["ssd_norm_g"]]
    y, fin_f = pl.pallas_call(
        functools.partial(_ssd_out_kernel, regions=regions),
        out_shape=(jax.ShapeDtypeStruct((t, SSD_INNER), BF16),
                   jax.ShapeDtypeStruct((n_seq, SSD_STATE, SSD_INNER), F32)),
        grid=(n_steps,),
        in_specs=[pl.BlockSpec((st, SSD_INNER), lambda s: (s, COL_Z // SSD_INNER))]
        + chunk_specs(fidx) + [const_spec(a) for a in consts2]
        + [pl.BlockSpec(state_block, lambda s: (seq_f(s), 0, 0, 0)),
           pl.BlockSpec(enter_block, lambda s: (s, 0, 0))],
        out_specs=(pl.BlockSpec((st, SSD_INNER), lambda s: (s, 0)),
                   pl.BlockSpec((None, SSD_STATE, SSD_INNER), lambda s: (seq_f(s), 0, 0))),
        scratch_shapes=scratch,
        compiler_params=_params("arbitrary"),
        name="ssd_forward_outputs",
    )(proj, xbc, xbc, small, *consts2, init, enter_b)
    return y, fin_f, fin_b


def _gla_log_decay(sm, gw_hi_ref, gw_lo_ref, gb_ref):
    s1, s2 = _split2(sm)
    logit = _dot(s1, gw_hi_ref[...]) + _dot(s2, gw_hi_ref[...]) + _dot(s1, gw_lo_ref[...]) + gb_ref[...]
    return -_softplus(-logit) * (1.0 / GLA_GATE_TAU)


def _chunk_rows(x, n_chunks, rows_per_chunk):
    return jnp.concatenate([jnp.broadcast_to(x[c:c + 1, :], (rows_per_chunk, x.shape[1]))
                            for c in range(n_chunks)], axis=0)


def _gla_bwd_kernel(qk_ref, v_ref, sm_ref, gwh_ref, gwl_ref, gb_ref, init_ref,
                    enter_ref, fin_ref, s_ref, *, regions, n_steps):
    cq = GLA_CHUNK
    st = SCAN_STEP
    ncs = st // cq
    c = n_steps - 1 - pl.program_id(0)
    _, first, last = _chunk_pos(c, st, regions)

    @pl.when(last)
    def _():
        s_ref[...] = init_ref[...]

    bwd = slice(GLA_KDIM, 2 * GLA_KDIM)
    lg = _gla_log_decay(sm_ref[...], gwh_ref.at[:, bwd], gwl_ref.at[:, bwd], gb_ref.at[:, bwd])
    rev = _dot_exact_lhs(_block_tri(st, cq, True).astype(BF16), lg)
    tot = jnp.concatenate([rev[ci * cq:ci * cq + 1, :] for ci in range(ncs)], axis=0)
    ko = (qk_ref[:, GLA_KDIM:2 * GLA_KDIM].astype(F32) * jnp.exp(_chunk_rows(tot, ncs, cq) - rev)).astype(BF16)
    etot = jnp.exp(tot)
    lo_half = lax.broadcasted_iota(jnp.int32, (1, LANES), 1) < GLA_DK

    s_cur = s_ref[...]
    for ci in reversed(range(ncs)):
        rows = slice(ci * cq, (ci + 1) * cq)
        enter_ref[ci] = s_cur
        new = []
        for p in range(GLA_HEADS // 2):
            sl = slice(p * LANES, (p + 1) * LANES)
            u0 = _dot_tn(v_ref[rows, (2 * p) * GLA_DV:(2 * p + 1) * GLA_DV], ko[rows, sl])
            u1 = _dot_tn(v_ref[rows, (2 * p + 1) * GLA_DV:(2 * p + 2) * GLA_DV], ko[rows, sl])
            new.append(s_cur[:, sl] * etot[ci:ci + 1, sl] + jnp.where(lo_half, u0, u1))
        s_cur = jnp.concatenate(new, axis=1)
    s_ref[...] = s_cur

    @pl.when(first)
    def _():
        fin_ref[...] = s_cur


def _gla_out_kernel(qk_ref, v_ref, r_ref, sm_ref, gwh_ref, gwl_ref, gb_ref, ng_ref, init_ref, enter_ref,
                    y_ref, fin_ref, s_ref, *, regions):
    cq = GLA_CHUNK
    st = SCAN_STEP
    ncs = st // cq
    c = pl.program_id(0)
    _, first, last = _chunk_pos(c, st, regions)

    @pl.when(first)
    def _():
        s_ref[...] = init_ref[...]

    lg = _gla_log_decay(sm_ref[...], gwh_ref, gwl_ref, gb_ref)
    gcs = _dot_exact_lhs(_block_tri(st, cq, False).astype(BF16), lg[:, 0:GLA_KDIM])
    rev = _dot_exact_lhs(_block_tri(st, cq, True).astype(BF16), lg[:, GLA_KDIM:2 * GLA_KDIM])
    lo_half = lax.broadcasted_iota(jnp.int32, (1, LANES), 1) < GLA_DK
    zero_b = jnp.zeros((cq, LANES), BF16)
    ii = lax.broadcasted_iota(jnp.int32, (2 * cq, cq), 0) % cq
    jj = lax.broadcasted_iota(jnp.int32, (2 * cq, cq), 1)
    tril2 = jj <= ii
    triu2 = jj >= ii
    ng = ng_ref[...]

    def stack_heads(x):
        return jnp.concatenate([jnp.where(lo_half, x, zero_b), jnp.where(lo_half, zero_b, x)], axis=0)

    s_cur = s_ref[...]
    for ci in range(ncs):
        rows = slice(ci * cq, (ci + 1) * cq)
        g_f = gcs[rows]
        g_b = rev[rows]
        etot = jnp.exp(g_f[cq - 1:cq, :])
        qs = qk_ref[rows, 0:GLA_KDIM].astype(F32) * (GLA_DK ** -0.5)
        ks = qk_ref[rows, GLA_KDIM:2 * GLA_KDIM].astype(F32)
        qe_f = (qs * jnp.exp(g_f)).astype(BF16)
        ke = ks * jnp.exp(-g_f)
        ke_f = ke.astype(BF16)
        ko_f = (ke * etot).astype(BF16)
        qe_b = (qs * jnp.exp(g_b)).astype(BF16)
        ke_b = (ks * jnp.exp(-g_b)).astype(BF16)
        s_curb = s_cur.astype(BF16)
        s_entb = enter_ref[ci].astype(BF16)
        new = []
        for p in range(GLA_HEADS // 2):
            sl = slice(p * LANES, (p + 1) * LANES)
            qf2 = stack_heads(qe_f[:, sl])
            qb2 = stack_heads(qe_b[:, sl])
            att = (jnp.where(tril2, _dot_nt(qf2, ke_f[:, sl]), 0.0)
                   + jnp.where(triu2, _dot_nt(qb2, ke_b[:, sl]), 0.0)).astype(BF16)
            inter = _dot_nt(qf2, s_curb[:, sl]) + _dot_nt(qb2, s_entb[:, sl])
            upd = []
            for hh in (0, 1):
                h = 2 * p + hh
                hs = slice(h * GLA_DV, (h + 1) * GLA_DV)
                hr = slice(hh * cq, (hh + 1) * cq)
                vh = v_ref[rows, hs]
                o = _dot(att[hr], vh) + inter[hr]
                o = o * lax.rsqrt(jnp.mean(o * o, axis=-1, keepdims=True) + EPS) * ng[:, hs]
                y_ref[rows, hs] = (o * _silu(r_ref[rows, hs].astype(F32))).astype(y_ref.dtype)
                upd.append(_dot_tn(vh, ko_f[:, sl]))
            new.append(s_cur[:, sl] * etot[:, sl] + jnp.where(lo_half, upd[0], upd[1]))
        s_cur = jnp.concatenate(new, axis=1)
    s_ref[...] = s_cur

    @pl.when(last)
    def _():
        fin_ref[...] = s_cur


def _gla_mixer(proj, small, init, lw, regions):
    st = SCAN_STEP
    ncs = st // GLA_CHUNK
    t = proj.shape[0]
    n_steps = t // st
    n_seq = regions[0][0] + regions[1][0]
    qkcol, vcol, rcol = COL_QK // 1024, COL_V // 1024, COL_R // 1024
    state_block = (None, None, GLA_DV, GLA_KDIM)
    enter_block = (ncs, GLA_DV, GLA_KDIM)
    gwh, gwl, gb = lw["gate_w_hi"], lw["gate_w_lo"], lw["gate_b"]

    def const_spec(a):
        return pl.BlockSpec(a.shape, lambda s: (0,) * a.ndim)

    bidx = lambda s: n_steps - 1 - s
    seq_b = lambda s: _chunk_pos(bidx(s), st, regions)[0]
    enter_b, fin_b = pl.pallas_call(
        functools.partial(_gla_bwd_kernel, regions=regions, n_steps=n_steps),
        out_shape=(jax.ShapeDtypeStruct((n_steps * ncs, GLA_DV, GLA_KDIM), F32),
                   jax.ShapeDtypeStruct((n_seq, GLA_DV, GLA_KDIM), F32)),
        grid=(n_steps,),
        in_specs=[pl.BlockSpec((st, 2 * GLA_KDIM), lambda s: (bidx(s), qkcol)),
                  pl.BlockSpec((st, GLA_VDIM), lambda s: (bidx(s), vcol)),
                  pl.BlockSpec((st, LANES), lambda s: (bidx(s), 0)),
                  const_spec(gwh), const_spec(gwl), const_spec(gb),
                  pl.BlockSpec(state_block, lambda s: (seq_b(s), 1, 0, 0))],
        out_specs=(pl.BlockSpec(enter_block, lambda s: (bidx(s), 0, 0)),
                   pl.BlockSpec((None, GLA_DV, GLA_KDIM), lambda s: (seq_b(s), 0, 0))),
        scratch_shapes=[pltpu.VMEM((GLA_DV, GLA_KDIM), F32)],
        compiler_params=_params("arbitrary"),
        name="gla_backward_states",
    )(proj, proj, small, gwh, gwl, gb, init)

    seq_f = lambda s: _chunk_pos(s, st, regions)[0]
    ng = lw["gla_norm_g"]
    y, fin_f = pl.pallas_call(
        functools.partial(_gla_out_kernel, regions=regions),
        out_shape=(jax.ShapeDtypeStruct((t, GLA_VDIM), BF16),
                   jax.ShapeDtypeStruct((n_seq, GLA_DV, GLA_KDIM), F32)),
        grid=(n_steps,),
        in_specs=[pl.BlockSpec((st, 2 * GLA_KDIM), lambda s: (s, qkcol)),
                  pl.BlockSpec((st, GLA_VDIM), lambda s: (s, vcol)),
                  pl.BlockSpec((st, GLA_VDIM), lambda s: (s, rcol)),
                  pl.BlockSpec((st, LANES), lambda s: (s, 0)),
                  const_spec(gwh), const_spec(gwl), const_spec(gb), const_spec(ng),
                  pl.BlockSpec(state_block, lambda s: (seq_f(s), 0, 0, 0)),
                  pl.BlockSpec(enter_block, lambda s: (s, 0, 0))],
        out_specs=(pl.BlockSpec((st, GLA_VDIM), lambda s: (s, 0)),
                   pl.BlockSpec((None, GLA_DV, GLA_KDIM), lambda s: (seq_f(s), 0, 0))),
        scratch_shapes=[pltpu.VMEM((GLA_DV, GLA_KDIM), F32)],
        compiler_params=_params("arbitrary"),
        name="gla_forward_outputs",
    )(proj, proj, proj, small, gwh, gwl, gb, ng, init, enter_b)
    return y, fin_f, fin_b


def _merge_kernel(x_ref, ys_ref, yg_ref, ga_ref, gb_ref, mod_ref, wbs_ref, wbg_ref, wo_ref, o_ref):
    merged = (_sigmoid(ga_ref[...].astype(F32)) * _dot(ys_ref[...], wbs_ref[...])
              + _sigmoid(gb_ref[...].astype(F32)) * _dot(yg_ref[...], wbg_ref[...]))
    mix = _dot(merged.astype(BF16), wo_ref[...])
    o_ref[...] = x_ref[...] + mod_ref[:, 2 * D_MODEL:3 * D_MODEL] * mix


def _merge(x, y_ssd, y_gla, proj, mods, lw, regions):
    t = x.shape[0]
    tm = min(TM_MERGE, regions[0][0] * regions[0][1], regions[1][1])
    tok = lambda col: pl.BlockSpec((tm, D_MODEL), lambda i: (i, col))
    wspec = pl.BlockSpec((D_MODEL, D_MODEL), lambda i: (0, 0))
    return pl.pallas_call(
        _merge_kernel,
        out_shape=jax.ShapeDtypeStruct((t, D_MODEL), F32),
        grid=(t // tm,),
        in_specs=[tok(0), tok(0), tok(0), tok(COL_GA // D_MODEL), tok(COL_GB // D_MODEL),
                  pl.BlockSpec((None, 1, 6 * D_MODEL), lambda i: (_mod_row(i, tm, regions), 0, 0)),
                  wspec, wspec, wspec],
        out_specs=tok(0),
        compiler_params=_params("parallel"),
        name="merge_out_projection",
    )(x, y_ssd, y_gla, proj, proj, mods, lw["w_bs"], lw["w_bg"], lw["w_o"])


def _ffn_prologue(x_ref, mod_ref, g_ref):
    y = _rms(x_ref[...], g_ref[...])
    return y * (1.0 + mod_ref[:, 4 * D_MODEL:5 * D_MODEL]) + mod_ref[:, 3 * D_MODEL:4 * D_MODEL]


def _swiglu_resident(v, w1_ref, w3_ref, w2_ref):
    acc = None
    for hh in range(w1_ref.shape[1] // TH_FFN):
        cols = slice(hh * TH_FFN, (hh + 1) * TH_FFN)
        h = _silu(_dot(v, w1_ref[:, cols])) * _dot(v, w3_ref[:, cols])
        part = _dot(h.astype(BF16), w2_ref[cols, :])
        acc = part if acc is None else acc + part
    return acc


def _ffn_kernel(x_ref, mod_ref, g_ref, w1_ref, w3_ref, w2_ref, o_ref):
    v = _ffn_prologue(x_ref, mod_ref, g_ref).astype(BF16)
    o_ref[...] = x_ref[...] + mod_ref[:, 5 * D_MODEL:6 * D_MODEL] * _swiglu_resident(v, w1_ref, w3_ref, w2_ref)


def _dense_ffn(x, mods, g, w1, w3, w2, regions):
    t = x.shape[0]
    tm = min(TM_FFN, regions[0][0] * regions[0][1], regions[1][1])
    hid = w1.shape[1]
    once = pl.Buffered(1)
    return pl.pallas_call(
        _ffn_kernel,
        out_shape=jax.ShapeDtypeStruct((t, D_MODEL), F32),
        grid=(t // tm,),
        in_specs=[pl.BlockSpec((tm, D_MODEL), lambda i: (i, 0)),
                  pl.BlockSpec((None, 1, 6 * D_MODEL), lambda i: (_mod_row(i, tm, regions), 0, 0)),
                  pl.BlockSpec((1, D_MODEL), lambda i: (0, 0)),
                  pl.BlockSpec((D_MODEL, hid), lambda i: (0, 0), pipeline_mode=once),
                  pl.BlockSpec((D_MODEL, hid), lambda i: (0, 0), pipeline_mode=once),
                  pl.BlockSpec((hid, D_MODEL), lambda i: (0, 0), pipeline_mode=once)],
        out_specs=pl.BlockSpec((tm, D_MODEL), lambda i: (i, 0)),
        compiler_params=_params("parallel"),
        name="dense_swiglu",
    )(x, mods, g, w1, w3, w2)


def _top2_gates(logits):
    lane = lax.broadcasted_iota(jnp.int32, logits.shape, 1)
    lg = jnp.where(lane < N_EXPERTS, logits, -jnp.inf)
    m1 = jnp.max(lg, axis=-1, keepdims=True)
    i1 = jnp.min(jnp.where(lg == m1, lane, LANES), axis=-1, keepdims=True)
    lg2 = jnp.where(lane == i1, -jnp.inf, lg)
    m2 = jnp.max(lg2, axis=-1, keepdims=True)
    i2 = jnp.min(jnp.where(lg2 == m2, lane, LANES), axis=-1, keepdims=True)
    e2 = jnp.exp(m2 - m1)
    den = 1.0 + e2
    return jnp.where(lane == i1, 1.0 / den, 0.0) + jnp.where(lane == i2, e2 / den, 0.0)


def _lane_scalar(vec, e):
    lane = lax.broadcasted_iota(jnp.int32, vec.shape, 1)
    return jnp.sum(jnp.where(lane == e, vec, 0.0)).astype(jnp.int32)


def _n_row_blocks(n):
    return (n + MOE_ROW_BLOCK - 1) // MOE_ROW_BLOCK


def _moe_block_copy(stage_ref, sem, slot, hbm_ref, row, to_hbm):
    if not isinstance(row, int):
        row = pl.multiple_of(row, MOE_ROW_ALIGN)
    rows = hbm_ref.at[pl.ds(row, MOE_ROW_BLOCK), :]
    if to_hbm:
        return pltpu.make_async_copy(stage_ref.at[slot], rows, sem.at[slot])
    return pltpu.make_async_copy(rows, stage_ref.at[slot], sem.at[slot])


def _moe_dispatch_kernel(x_ref, mod_ref, g_ref, rw_ref, rb_ref, donor_hbm,
                         xs_hbm, gate_ref, rank_ref, off_ref, tot_ref,
                         v_ref, gate_t_ref, rank_t_ref, stage_ref, sem, run_ref, pend_ref, *, cap):
    del donor_hbm
    i = pl.program_id(0)
    tm = x_ref.shape[0]
    rb = MOE_ROW_BLOCK

    @pl.when(i == 0)
    def _():
        run_ref[...] = jnp.zeros_like(run_ref)
        pend_ref[0] = 0

    v = _ffn_prologue(x_ref, mod_ref, g_ref)
    v_ref[...] = v.astype(BF16)
    gate = _top2_gates(_dot_hilo(v, rw_ref[...]) + rb_ref[...])
    sel_f = jnp.where(gate > 0.0, 1.0, 0.0)
    sel = sel_f.astype(BF16)
    gate_ref[...] = gate
    gate_t_ref[...] = gate.T
    for blk in range(tm // rb):
        ii = lax.broadcasted_iota(jnp.int32, (rb, tm), 0) + blk * rb
        jj = lax.broadcasted_iota(jnp.int32, (rb, tm), 1)
        rank = _dot(jnp.where(jj < ii, 1.0, 0.0).astype(BF16), sel)
        rank_ref[blk * rb:(blk + 1) * rb, :] = rank
        rank_t_ref[:, blk * rb:(blk + 1) * rb] = rank.T

    cnt = jnp.sum(sel_f, axis=0, keepdims=True)
    cnt_pad = jnp.floor((cnt + (MOE_ROW_ALIGN - 1)) * (1.0 / MOE_ROW_ALIGN)) * MOE_ROW_ALIGN
    run = run_ref[...]
    off_ref[...] = run

    def wait_prev(s, carry):
        _moe_block_copy(stage_ref, sem, s, xs_hbm, 0, True).wait()
        return carry
    lax.fori_loop(0, pend_ref[0], wait_prev, 0)

    slot = jnp.int32(0)
    for e in range(N_EXPERTS):
        n_e = _lane_scalar(cnt, e)
        base = e * cap + _lane_scalar(run, e)
        g_row = gate_t_ref[e:e + 1, :]
        r_row = rank_t_ref[e:e + 1, :].astype(jnp.int32)

        def pack(b, s, base=base, g_row=g_row, r_row=r_row):
            row = lax.broadcasted_iota(jnp.int32, (rb, tm), 0) + b * rb
            onehot = jnp.where(jnp.logical_and(row == r_row, g_row > 0.0), 1.0, 0.0).astype(BF16)
            stage_ref[s] = _dot(onehot, v_ref[...]).astype(BF16)
            _moe_block_copy(stage_ref, sem, s, xs_hbm, base + b * rb, True).start()
            return s + 1
        slot = lax.fori_loop(0, _n_row_blocks(n_e), pack, slot)

    pend_ref[0] = slot
    run_ref[...] = run + cnt_pad
    tot_ref[...] = run + cnt_pad

    @pl.when(i == pl.num_programs(0) - 1)
    def _():
        def wait_last(s, carry):
            _moe_block_copy(stage_ref, sem, s, xs_hbm, 0, True).wait()
            return carry
        lax.fori_loop(0, slot, wait_last, 0)


def _moe_expert_kernel(te_ref, tr_ref, na_ref, xs_ref, w1_ref, w3_ref, w2_ref, ys_ref):
    del te_ref, tr_ref

    @pl.when(pl.program_id(0) < na_ref[0])
    def _():
        ys_ref[...] = _swiglu_resident(xs_ref[...], w1_ref, w3_ref, w2_ref).astype(ys_ref.dtype)


def _moe_combine_kernel(off_ref, x_ref, mod_ref, gate_ref, rank_ref, ys_hbm, o_ref, stage_ref, sem, *, cap):
    i = pl.program_id(0)
    tm = x_ref.shape[0]
    rb = MOE_ROW_BLOCK
    gate = gate_ref[...]
    rank = rank_ref[...]
    cnt = jnp.sum(jnp.where(gate > 0.0, 1.0, 0.0), axis=0, keepdims=True)
    n_blocks = [_n_row_blocks(_lane_scalar(cnt, e)) for e in range(N_EXPERTS)]
    bases = [e * cap + off_ref[i * LANES + e] for e in range(N_EXPERTS)]

    slot = jnp.int32(0)
    for e in range(N_EXPERTS):
        def fetch(b, s, base=bases[e]):
            _moe_block_copy(stage_ref, sem, s, ys_hbm, base + b * rb, False).start()
            return s + 1
        slot = lax.fori_loop(0, n_blocks[e], fetch, slot)

    o_ref[...] = jnp.zeros_like(o_ref)
    slot = jnp.int32(0)
    for e in range(N_EXPERTS):
        g_col = gate[:, e:e + 1]
        r_col = rank[:, e:e + 1].astype(jnp.int32)

        def combine(b, s, g_col=g_col, r_col=r_col):
            _moe_block_copy(stage_ref, sem, s, ys_hbm, 0, False).wait()
            col = lax.broadcasted_iota(jnp.int32, (tm, rb), 1) + b * rb
            onehot = jnp.where(jnp.logical_and(col == r_col, g_col > 0.0), 1.0, 0.0).astype(BF16)
            o_ref[...] += g_col * _dot(onehot, stage_ref[s])
            return s + 1
        slot = lax.fori_loop(0, n_blocks[e], combine, slot)

    o_ref[...] = x_ref[...] + mod_ref[:, 5 * D_MODEL:6 * D_MODEL] * o_ref[...]


def _moe_tile_table(totals, cap, n_steps):
    extent = totals + MOE_ROW_BLOCK
    n_tiles = (extent + TR_EXPERT - 1) // TR_EXPERT
    ends = jnp.cumsum(n_tiles)
    starts = ends - n_tiles
    n_active = ends[-1]
    t = jnp.minimum(jnp.arange(n_steps, dtype=jnp.int32), n_active - 1)
    e = jnp.sum((t[:, None] >= ends[None, :]).astype(jnp.int32), axis=1)
    row_block = e * (cap // TR_EXPERT) + (t - starts[e])
    return e.astype(jnp.int32), row_block.astype(jnp.int32), n_active.reshape(1).astype(jnp.int32)


def _moe_geometry(t, regions, n_e):
    tm = min(TM_MOE, regions[0][0] * regions[0][1], regions[1][1])
    n_tiles = t // tm
    cap_rows = t + n_tiles * (MOE_ROW_ALIGN - 1) + MOE_ROW_BLOCK
    cap = -(-cap_rows // TR_EXPERT) * TR_EXPERT
    return tm, n_tiles, cap


def _moe_ffn(x, mods, g, rw, rb, w1, w3, w2, regions, donor):
    t = x.shape[0]
    n_e, _, hid = w1.shape
    tm, n_tiles, cap = _moe_geometry(t, regions, n_e)
    max_slots = 2 * tm // MOE_ROW_BLOCK + n_e
    n_steps = (2 * t + n_tiles * n_e * (MOE_ROW_ALIGN - 1) + n_e * MOE_ROW_BLOCK) // TR_EXPERT + n_e
    mod_spec = pl.BlockSpec((None, 1, 6 * D_MODEL), lambda i, *_: (_mod_row(i, tm, regions), 0, 0))
    stage = [pltpu.VMEM((max_slots, MOE_ROW_BLOCK, D_MODEL), BF16), pltpu.SemaphoreType.DMA((max_slots,))]

    xs, gate, rank, offs, totals = pl.pallas_call(
        functools.partial(_moe_dispatch_kernel, cap=cap),
        out_shape=(jax.ShapeDtypeStruct((n_e * cap, D_MODEL), BF16),
                   jax.ShapeDtypeStruct((t, LANES), F32), jax.ShapeDtypeStruct((t, LANES), F32),
                   jax.ShapeDtypeStruct((n_tiles, 1, LANES), F32), jax.ShapeDtypeStruct((1, LANES), F32)),
        grid=(n_tiles,),
        in_specs=[pl.BlockSpec((tm, D_MODEL), lambda i: (i, 0)), mod_spec,
                  pl.BlockSpec((1, D_MODEL), lambda i: (0, 0)),
                  pl.BlockSpec((D_MODEL, LANES), lambda i: (0, 0)),
                  pl.BlockSpec((1, LANES), lambda i: (0, 0)),
                  pl.BlockSpec(memory_space=pl.ANY)],
        input_output_aliases={5: 0},
        out_specs=(pl.BlockSpec(memory_space=pl.ANY),
                   pl.BlockSpec((tm, LANES), lambda i: (i, 0)), pl.BlockSpec((tm, LANES), lambda i: (i, 0)),
                   pl.BlockSpec((None, 1, LANES), lambda i: (i, 0, 0)),
                   pl.BlockSpec((1, LANES), lambda i: (0, 0))),
        scratch_shapes=[pltpu.VMEM((tm, D_MODEL), BF16),
                        pltpu.VMEM((LANES, tm), F32), pltpu.VMEM((LANES, tm), F32)] + stage
        + [pltpu.VMEM((1, LANES), F32), pltpu.SMEM((1,), jnp.int32)],
        compiler_params=_params("arbitrary"),
        name="moe_dispatch",
    )(x, mods, g, rw, rb, donor)

    tile_e, tile_rb, n_active = _moe_tile_table(totals[0, :n_e].astype(jnp.int32), cap, n_steps)
    ys = pl.pallas_call(
        _moe_expert_kernel,
        out_shape=jax.ShapeDtypeStruct((n_e * cap, D_MODEL), BF16),
        grid_spec=pltpu.PrefetchScalarGridSpec(
            num_scalar_prefetch=3,
            grid=(n_steps,),
            in_specs=[pl.BlockSpec((TR_EXPERT, D_MODEL), lambda s, te, tr, na: (tr[s], 0)),
                      pl.BlockSpec((None, D_MODEL, hid), lambda s, te, tr, na: (te[s], 0, 0)),
                      pl.BlockSpec((None, D_MODEL, hid), lambda s, te, tr, na: (te[s], 0, 0)),
                      pl.BlockSpec((None, hid, D_MODEL), lambda s, te, tr, na: (te[s], 0, 0))],
            out_specs=pl.BlockSpec((TR_EXPERT, D_MODEL), lambda s, te, tr, na: (tr[s], 0))),
        input_output_aliases={3: 0},
        compiler_params=_params("arbitrary"),
        name="moe_experts",
    )(tile_e, tile_rb, n_active, xs, w1, w3, w2)

    out = pl.pallas_call(
        functools.partial(_moe_combine_kernel, cap=cap),
        out_shape=jax.ShapeDtypeStruct((t, D_MODEL), F32),
        grid_spec=pltpu.PrefetchScalarGridSpec(
            num_scalar_prefetch=1,
            grid=(n_tiles,),
            in_specs=[pl.BlockSpec((tm, D_MODEL), lambda i, off: (i, 0)), mod_spec,
                      pl.BlockSpec((tm, LANES), lambda i, off: (i, 0)),
                      pl.BlockSpec((tm, LANES), lambda i, off: (i, 0)),
                      pl.BlockSpec(memory_space=pl.ANY)],
            out_specs=pl.BlockSpec((tm, D_MODEL), lambda i, off: (i, 0)),
            scratch_shapes=stage),
        compiler_params=_params("arbitrary"),
        name="moe_combine",
    )(offs.reshape(-1).astype(jnp.int32), x, mods, gate, rank, ys)
    return out, ys


def _final_norm_kernel(x_ref, g_ref, o_ref):
    o_ref[...] = _rms(x_ref[...], g_ref[...])


def _final_norm(x, g, tok_off, n_tok):
    tm = min(TM_NORM, n_tok)
    off = tok_off // tm
    return pl.pallas_call(
        _final_norm_kernel,
        out_shape=jax.ShapeDtypeStruct((n_tok, D_MODEL), F32),
        grid=(n_tok // tm,),
        in_specs=[pl.BlockSpec((tm, D_MODEL), lambda i: (i + off, 0)),
                  pl.BlockSpec((1, D_MODEL), lambda i: (0, 0))],
        out_specs=pl.BlockSpec((tm, D_MODEL), lambda i: (i, 0)),
        compiler_params=_params("parallel"),
        name="final_norm",
    )(x, g)


def _grid_pos_embed(l, d):
    rows = l // GRID_W
    row = jnp.repeat(jnp.arange(rows, dtype=F32), GRID_W)
    col = jnp.tile(jnp.arange(GRID_W, dtype=F32), rows)
    quarter = d // 4
    omega = jnp.exp(-math.log(10000.0) * jnp.arange(quarter, dtype=F32) / quarter)
    er = row[:, None] * omega
    ec = col[:, None] * omega
    return jnp.concatenate([jnp.sin(er), jnp.cos(er), jnp.sin(ec), jnp.cos(ec)], axis=-1)


def _pad_lanes(a, width):
    return jnp.pad(a, [(0, 0)] * (a.ndim - 1) + [(0, width - a.shape[-1])])


def _reorder_w_in(w):
    o = np.cumsum([0, SSD_INNER, SSD_INNER + 2 * SSD_GROUPS * SSD_STATE, 2 * SSD_HEADS, GLA_KDIM, GLA_KDIM,
                   GLA_VDIM, GLA_VDIM, 2 * GLA_GATE_RANK, D_MODEL, D_MODEL])
    z, xbc, dt, qq, kk, vv, rr, lr, ga, gb = [w[:, int(o[i]):int(o[i + 1])] for i in range(10)]
    main = jnp.concatenate([z, vv, rr, ga, gb, qq, kk], axis=1).astype(BF16)
    small = _pad_lanes(jnp.concatenate([dt, lr], axis=1), LANES).astype(BF16)
    return main, xbc.astype(BF16), small


def _layer_weights(i, p):
    conv_w = jnp.pad(p["ssd_conv_w"][i], ((0, SUBLANES - SSD_CONV), (0, 0)))
    conv_b = p["ssd_conv_b"][i][None, :]
    gate_w = jnp.zeros((LANES, 2 * GLA_KDIM), F32)
    gate_w = gate_w.at[SM_LR:SM_LR + GLA_GATE_RANK, :GLA_KDIM].set(p["gla_gate_w"][i, 0])
    gate_w = gate_w.at[SM_LR + GLA_GATE_RANK:SM_LR + 2 * GLA_GATE_RANK, GLA_KDIM:].set(p["gla_gate_w"][i, 1])
    gate_w_hi = gate_w.astype(BF16)
    gate_w_lo = (gate_w - gate_w_hi.astype(F32)).astype(BF16)
    w_main, w_conv, w_small = _reorder_w_in(p["w_in"][i])
    return {
        "w_in": w_main, "w_in_conv": w_conv, "w_in_small": w_small,
        "conv_w": conv_w, "conv_b": conv_b,
        "dt_bias": _pad_lanes(p["ssd_dt_bias"][i].reshape(1, -1), LANES),
        "a_log": _pad_lanes(p["ssd_a_log"][i].reshape(1, -1), LANES),
        "d_exp": jnp.repeat(p["ssd_d"][i], SSD_HEADDIM)[None, :],
        "ssd_norm_g": p["ssd_norm_g"][i][None, :],
        "gate_w_hi": gate_w_hi, "gate_w_lo": gate_w_lo, "gate_b": p["gla_gate_b"][i].reshape(1, -1),
        "gla_norm_g": p["gla_norm_g"][i][None, :],
        "w_bs": p["w_branch_ssd"][i].astype(BF16), "w_bg": p["w_branch_gla"][i].astype(BF16),
        "w_o": p["w_out"][i].astype(BF16),
    }


def _ssd_state_in(s):
    b = s.shape[0]
    return jnp.transpose(s, (0, 1, 4, 2, 3)).reshape(b, 2, SSD_STATE, SSD_INNER)


def _ssd_state_out(f, b):
    s = jnp.stack([f, b], axis=1).reshape(-1, 2, SSD_STATE, SSD_HEADS, SSD_HEADDIM)
    return jnp.transpose(s, (0, 1, 3, 4, 2))


def _gla_state_in(s):
    b = s.shape[0]
    return jnp.transpose(s, (0, 1, 4, 2, 3)).reshape(b, 2, GLA_DV, GLA_KDIM)


def _gla_state_out(f, b):
    s = jnp.stack([f, b], axis=1).reshape(-1, 2, GLA_DV, GLA_HEADS, GLA_DK)
    return jnp.transpose(s, (0, 1, 3, 4, 2))


def _trunk(x_prompt, x_sample, state_ssd, state_gla, c, c_ctx, p):
    n0, l0, _ = x_prompt.shape
    n1, l1, _ = x_sample.shape
    regions = ((n0, l0), (n1, l1))
    depth = p["w_in"].shape[0]

    cc = jnp.zeros((MOD_ROWS, D_MODEL), F32).at[0].set(c_ctx).at[1:1 + n1].set(c)
    mods = _modulation_table(cc, p["ada_w"], p["ada_b"])[:, :, None, :]
    x = _assemble_tokens(x_prompt, x_sample, _grid_pos_embed(l1, D_MODEL), regions)

    ssd_states, gla_states = [], []
    moe_rows = None
    for i in range(depth):
        lw = _layer_weights(i, p)
        init_ssd = jnp.concatenate([jnp.zeros((n0, 2, SSD_STATE, SSD_INNER), F32),
                                    _ssd_state_in(state_ssd[:, i])], axis=0)
        init_gla = jnp.concatenate([jnp.zeros((n0, 2, GLA_DV, GLA_KDIM), F32),
                                    _gla_state_in(state_gla[:, i])], axis=0)
        proj, xbc, small = _in_projection(x, mods[i], p["norm1_g"][i][None, :], lw, regions)
        y_ssd, sf, sb = _ssd_mixer(proj, xbc, small, init_ssd, lw, regions)
        y_gla, gf, gb = _gla_mixer(proj, small, init_gla, lw, regions)
        ssd_states.append(_ssd_state_out(sf[:n0], sb[:n0]))
        gla_states.append(_gla_state_out(gf[:n0], gb[:n0]))
        x = _merge(x, y_ssd, y_gla, proj, mods[i], lw, regions)
        j = i // 2
        g2 = p["norm2_g"][i][None, :]
        if i % 2 == 0:
            x = _dense_ffn(x, mods[i], g2, p["ffn_w1"][j].astype(BF16), p["ffn_w3"][j].astype(BF16),
                           p["ffn_w2"][j].astype(BF16), regions)
        else:
            if moe_rows is None:
                n_e = p["moe_w1"].shape[1]
                moe_rows = jnp.zeros((n_e * _moe_geometry(x.shape[0], regions, n_e)[2], D_MODEL), BF16)
            x, moe_rows = _moe_ffn(x, mods[i], g2, _pad_lanes(p["router_w"][j], LANES),
                                   _pad_lanes(p["router_b"][j][None, :], LANES),
                                   p["moe_w1"][j].astype(BF16), p["moe_w3"][j].astype(BF16),
                                   p["moe_w2"][j].astype(BF16), regions, moe_rows)

    gfin = p["final_norm_g"][None, :]
    y_prompt = _final_norm(x, gfin, 0, n0 * l0).reshape(n0, l0, D_MODEL)
    y_sample = _final_norm(x, gfin, n0 * l0, n1 * l1).reshape(n1, l1, D_MODEL)
    return y_prompt, y_sample, jnp.stack(ssd_states, axis=1), jnp.stack(gla_states, axis=1)


def kernel(x_prompt, x_sample, state_ssd, state_gla, c, c_ctx, ada_w, ada_b, norm1_g, norm2_g, w_in, ssd_conv_w, ssd_conv_b, ssd_dt_bias, ssd_a_log, ssd_d, ssd_norm_g, gla_gate_w, gla_gate_b, gla_norm_g, w_branch_ssd, w_branch_gla, w_out, ffn_w1, ffn_w3, ffn_w2, router_w, router_b, moe_w1, moe_w3, moe_w2, final_norm_g):
    p = dict(ada_w=ada_w, ada_b=ada_b, norm1_g=norm1_g, norm2_g=norm2_g, w_in=w_in, ssd_conv_w=ssd_conv_w,
             ssd_conv_b=ssd_conv_b, ssd_dt_bias=ssd_dt_bias, ssd_a_log=ssd_a_log, ssd_d=ssd_d,
             ssd_norm_g=ssd_norm_g, gla_gate_w=gla_gate_w, gla_gate_b=gla_gate_b, gla_norm_g=gla_norm_g,
             w_branch_ssd=w_branch_ssd, w_branch_gla=w_branch_gla, w_out=w_out, ffn_w1=ffn_w1, ffn_w3=ffn_w3,
             ffn_w2=ffn_w2, router_w=router_w, router_b=router_b, moe_w1=moe_w1, moe_w3=moe_w3, moe_w2=moe_w2,
             final_norm_g=final_norm_g)
    return _trunk(x_prompt, x_sample, state_ssd, state_gla, c, c_ctx, p)
```

```python
import functools
import math

import numpy as np
import jax
import jax.numpy as jnp
from jax import lax
from jax.experimental import pallas as pl
from jax.experimental.pallas import tpu as pltpu

F32 = jnp.float32
BF16 = jnp.bfloat16

D_MODEL = 1024
GRID_W = 64
EPS = 1e-6
SSD_HEADS = 16
SSD_HEADDIM = 64
SSD_INNER = SSD_HEADS * SSD_HEADDIM
SSD_GROUPS = 2
SSD_STATE = 64
SSD_CONV = 5
SSD_CHUNK = 128
GLA_HEADS = 8
GLA_DK = 64
GLA_DV = 128
GLA_KDIM = GLA_HEADS * GLA_DK
GLA_VDIM = GLA_HEADS * GLA_DV
GLA_GATE_RANK = 16
GLA_GATE_TAU = 16.0
GLA_CHUNK = 64
N_EXPERTS = 8

LANES = 128
SUBLANES = 8
BF16_SUBLANES = 16
VMEM_LIMIT_BYTES = 56 * 1024 * 1024

COL_Z, COL_V, COL_R, COL_GA, COL_GB, COL_QK = 0, 1024, 2048, 3072, 4096, 5120
D_PROJ = 6144
CONV_X, CONV_BC = 0, 1024
D_CONV = 1280
SM_LR = 2 * SSD_HEADS

TM_PROJ, TN_PROJ = 1024, 3072
TM_MERGE = 512
TM_FFN, TH_FFN = 512, 1408
TM_MOE = 512
MOE_ROW_BLOCK = 256
MOE_ROW_ALIGN = 16
TR_EXPERT = 512
TM_NORM = 1024
TN_MODS = 1536
MOD_ROWS = 16
SCAN_STEP = 256


def _params(*sem):
    return pltpu.CompilerParams(dimension_semantics=sem, vmem_limit_bytes=VMEM_LIMIT_BYTES)


def _sigmoid(x):
    return 0.5 * jnp.tanh(0.5 * x) + 0.5


def _silu(x):
    return x * _sigmoid(x)


def _softplus(x):
    return jnp.maximum(x, 0.0) + jnp.log(1.0 + jnp.exp(-jnp.abs(x)))


def _split2(a):
    a1 = a.astype(BF16)
    a2 = (a - a1.astype(F32)).astype(BF16)
    return a1, a2


def _split3(a):
    a1 = a.astype(BF16)
    r1 = a - a1.astype(F32)
    a2 = r1.astype(BF16)
    a3 = (r1 - a2.astype(F32)).astype(BF16)
    return a1, a2, a3


def _dot(a, b):
    return jnp.dot(a, b, preferred_element_type=F32)


def _dot_nt(a, b):
    return lax.dot_general(a, b, (((1,), (1,)), ((), ())), preferred_element_type=F32)


def _dot_tn(a, b):
    return lax.dot_general(a, b, (((0,), (0,)), ((), ())), preferred_element_type=F32)


def _dot_exact_lhs(t01, a):
    a1, a2 = _split2(a)
    return _dot(t01, a1) + _dot(t01, a2)


def _dot_f32(a, b):
    a1, a2, a3 = _split3(a)
    b1, b2, b3 = _split3(b)
    return (_dot(a1, b1) + _dot(a1, b2) + _dot(a2, b1)
            + _dot(a1, b3) + _dot(a2, b2) + _dot(a3, b1))


def _dot_hilo(a, b):
    a1, a2 = _split2(a)
    b1, b2 = _split2(b)
    return _dot(a1, b1) + _dot(a2, b1) + _dot(a1, b2)


def _block_tri(n, blk, upper):
    i = lax.broadcasted_iota(jnp.int32, (n, n), 0)
    j = lax.broadcasted_iota(jnp.int32, (n, n), 1)
    tri = (j >= i) if upper else (j <= i)
    if blk == n:
        return tri
    return jnp.logical_and(tri, (i // blk) == (j // blk))


def _rms(x, g):
    return x * lax.rsqrt(jnp.mean(x * x, axis=-1, keepdims=True) + EPS) * g


def _chunk_pos(c, q, regions):
    (n0, l0), (n1, l1) = regions
    per0, per1 = l0 // q, l1 // q
    nc0 = n0 * per0
    c1 = jnp.maximum(c - nc0, 0)
    in0 = c < nc0
    seq = jnp.where(in0, c // per0, n0 + c1 // per1)
    pos = jnp.where(in0, c % per0, c1 % per1)
    last = jnp.where(in0, per0 - 1, per1 - 1)
    return seq, pos == 0, pos == last


def _mod_row(i, tm, regions):
    (n0, l0), (n1, l1) = regions
    t1 = jnp.maximum(i * tm - n0 * l0, 0)
    return jnp.where(i * tm < n0 * l0, 0, 1 + t1 // l1)


def _assemble_kernel(xp_ref, xs_ref, pe_ref, o_ref, *, n_ctx_tiles):
    i = pl.program_id(0)

    @pl.when(i < n_ctx_tiles)
    def _():
        o_ref[...] = xp_ref[...]

    @pl.when(i >= n_ctx_tiles)
    def _():
        o_ref[...] = xs_ref[...] + pe_ref[...]


def _assemble_tokens(xp, xs, pe, regions):
    (n0, l0), (n1, l1) = regions
    tm = min(1024, l0 * n0, l1)
    t0, t1 = n0 * l0, n1 * l1
    n_ctx = t0 // tm
    pe_tiles = l1 // tm
    return pl.pallas_call(
        functools.partial(_assemble_kernel, n_ctx_tiles=n_ctx),
        out_shape=jax.ShapeDtypeStruct((t0 + t1, D_MODEL), F32),
        grid=((t0 + t1) // tm,),
        in_specs=[
            pl.BlockSpec((tm, D_MODEL), lambda i: (jnp.minimum(i, n_ctx - 1), 0)),
            pl.BlockSpec((tm, D_MODEL), lambda i: (jnp.maximum(i - n_ctx, 0), 0)),
            pl.BlockSpec((tm, D_MODEL), lambda i: (jnp.maximum(i - n_ctx, 0) % pe_tiles, 0)),
        ],
        out_specs=pl.BlockSpec((tm, D_MODEL), lambda i: (i, 0)),
        compiler_params=_params("arbitrary"),
        name="assemble_tokens",
    )(xp.reshape(t0, D_MODEL), xs.reshape(t1, D_MODEL), pe)


def _mods_kernel(cc_ref, w_ref, b_ref, o_ref):
    a = _silu(cc_ref[...])
    o_ref[...] = _dot_f32(a, w_ref[...]) + b_ref[...]


def _modulation_table(cc, ada_w, ada_b):
    depth = ada_w.shape[0]
    n = ada_w.shape[2]
    return pl.pallas_call(
        _mods_kernel,
        out_shape=jax.ShapeDtypeStruct((depth, MOD_ROWS, n), F32),
        grid=(depth, n // TN_MODS),
        in_specs=[
            pl.BlockSpec((MOD_ROWS, D_MODEL), lambda l, j: (0, 0)),
            pl.BlockSpec((None, D_MODEL, TN_MODS), lambda l, j: (l, 0, j)),
            pl.BlockSpec((None, 1, TN_MODS), lambda l, j: (l, 0, j)),
        ],
        out_specs=pl.BlockSpec((None, MOD_ROWS, TN_MODS), lambda l, j: (l, 0, j)),
        compiler_params=_params("arbitrary", "arbitrary"),
        name="modulation_table",
    )(cc, ada_w, ada_b.reshape(depth, 1, n))


def _inproj_kernel(x_ref, mod_ref, g_ref, w_ref, wc_ref, ws_ref, o_ref, oc_ref, os_ref, u_ref):
    def modnorm(x):
        y = _rms(x, g_ref[...])
        return y * (1.0 + mod_ref[:, D_MODEL:2 * D_MODEL]) + mod_ref[:, 0:D_MODEL]

    @pl.when(pl.program_id(1) == 0)
    def _():
        u_ref[...] = modnorm(x_ref[...]).astype(BF16)
        os_ref[...] = _dot(u_ref[...], ws_ref[...])

    u = u_ref[...]
    oc_ref[...] = _dot(u, wc_ref[...]).astype(BF16)
    o_ref[...] = _dot(u, w_ref[...]).astype(BF16)


def _in_projection(x, mods, g, lw, regions):
    t = x.shape[0]
    tm = min(TM_PROJ, regions[0][0] * regions[0][1], regions[1][1])
    n_col = D_PROJ // TN_PROJ
    tc = D_CONV // n_col
    return pl.pallas_call(
        _inproj_kernel,
        out_shape=(jax.ShapeDtypeStruct((t, D_PROJ), BF16), jax.ShapeDtypeStruct((t, D_CONV), BF16),
                   jax.ShapeDtypeStruct((t, LANES), F32)),
        grid=(t // tm, n_col),
        in_specs=[
            pl.BlockSpec((tm, D_MODEL), lambda i, j: (i, 0)),
            pl.BlockSpec((None, 1, 6 * D_MODEL), lambda i, j: (_mod_row(i, tm, regions), 0, 0)),
            pl.BlockSpec((1, D_MODEL), lambda i, j: (0, 0)),
            pl.BlockSpec((D_MODEL, TN_PROJ), lambda i, j: (0, j)),
            pl.BlockSpec((D_MODEL, tc), lambda i, j: (0, j)),
            pl.BlockSpec((D_MODEL, LANES), lambda i, j: (0, 0)),
        ],
        out_specs=(pl.BlockSpec((tm, TN_PROJ), lambda i, j: (i, j)),
                   pl.BlockSpec((tm, tc), lambda i, j: (i, j)),
                   pl.BlockSpec((tm, LANES), lambda i, j: (i, 0))),
        scratch_shapes=[pltpu.VMEM((tm, D_MODEL), BF16)],
        compiler_params=_params("parallel", "arbitrary"),
        name="norm_in_projection",
    )(x, mods, g, lw["w_in"], lw["w_in_conv"], lw["w_in_small"])


def _ssd_decay_terms(sm, dtb, alog, q):
    lane = lax.broadcasted_iota(jnp.int32, (1, LANES), 1)
    dtv = _softplus(sm + dtb)
    a_neg = jnp.where(lane < 2 * SSD_HEADS, -jnp.exp(alog), 0.0)
    a = dtv * a_neg
    acs = _dot_exact_lhs(_block_tri(q, q, False).astype(BF16), a)
    rev = _dot_exact_lhs(_block_tri(q, q, True).astype(BF16), a)
    m = jnp.where(lane < SSD_HEADS, acs, rev)
    tot = acs[q - 1:q, :]
    return m, tot, dtv


def _col(x, idx, width=LANES):
    return jnp.broadcast_to(x[:, idx:idx + 1], (x.shape[0], width))


def _head_expand_matrix(base):
    k = lax.broadcasted_iota(jnp.int32, (LANES, SSD_INNER), 0)
    c = lax.broadcasted_iota(jnp.int32, (LANES, SSD_INNER), 1)
    return jnp.where(k == base + c // SSD_HEADDIM, 1.0, 0.0).astype(BF16)


def _pair_cols(x, base, p, lo_half):
    return jnp.where(lo_half, _col(x, base + 2 * p), _col(x, base + 2 * p + 1))


def _expand_state(s):
    lane = lax.broadcasted_iota(jnp.int32, s.shape, 1)
    half = SSD_INNER // SSD_GROUPS
    return jnp.concatenate([jnp.where(lane < half, s, 0.0), jnp.where(lane >= half, s, 0.0)], axis=0)


def _compact_state(s2):
    return s2[0:SSD_STATE, :] + s2[SSD_STATE:2 * SSD_STATE, :]


def _state_update_mask():
    row = lax.broadcasted_iota(jnp.int32, (SSD_GROUPS * SSD_STATE, SSD_INNER), 0)
    lane = lax.broadcasted_iota(jnp.int32, (SSD_GROUPS * SSD_STATE, SSD_INNER), 1)
    return (row < SSD_STATE) == (lane < SSD_INNER // SSD_GROUPS)


HALO = BF16_SUBLANES
CONV_PAD = SSD_CONV // 2


def _ssd_bwd_kernel(xbc_ref, prev_ref, next_ref, sm_ref, cw_ref, cb_ref, dtb_ref, alog_ref, init_ref,
                    conv_ref, tab_ref, enter_ref, fin_ref, s_ref, ext_ref, *, regions, n_steps):
    q = SSD_CHUNK
    st = SCAN_STEP
    c = n_steps - 1 - pl.program_id(0)
    _, first, last = _chunk_pos(c, st, regions)

    @pl.when(last)
    def _():
        s_ref[...] = _expand_state(init_ref[...])

    ext_ref[0:HALO, :] = jnp.where(first, 0.0, prev_ref[...].astype(F32))
    ext_ref[HALO:HALO + st, :] = xbc_ref[...].astype(F32)
    ext_ref[HALO + st:2 * HALO + st, :] = jnp.where(last, 0.0, next_ref[...].astype(F32))
    acc = cb_ref[...]
    for k in range(SSD_CONV):
        off = HALO - CONV_PAD + k
        acc = acc + cw_ref[k:k + 1, :] * ext_ref[off:off + st, :]
    conv_ref[...] = _silu(acc).astype(BF16)

    lo_half = lax.broadcasted_iota(jnp.int32, (1, LANES), 1) < SSD_HEADDIM
    upd_mask = _state_update_mask()

    s_cur = s_ref[...]
    for ci in reversed(range(st // q)):
        rows = slice(ci * q, (ci + 1) * q)
        xs = conv_ref[rows, 0:SSD_INNER].astype(F32)
        bsb = conv_ref[rows, SSD_INNER:SSD_INNER + LANES]
        m, tot, dtv = _ssd_decay_terms(sm_ref[rows, :], dtb_ref[...], alog_ref[...], q)
        tab_ref[rows, 0:LANES] = m
        tab_ref[rows, LANES:2 * LANES] = dtv
        wgt = jnp.exp(tot - m) * dtv
        etot = jnp.exp(tot)
        enter_ref[ci] = _compact_state(s_cur)
        xw, dec = [], []
        for p in range(SSD_HEADS // 2):
            sl = slice(p * LANES, (p + 1) * LANES)
            xw.append((xs[:, sl] * _pair_cols(wgt, SSD_HEADS, p, lo_half)).astype(BF16))
            dec.append(_pair_cols(etot, SSD_HEADS, p, lo_half))
        upd = _dot_tn(bsb, jnp.concatenate(xw, axis=1))
        s_cur = s_cur * jnp.concatenate(dec, axis=1) + jnp.where(upd_mask, upd, 0.0)
    s_ref[...] = s_cur

    @pl.when(first)
    def _():
        fin_ref[...] = _compact_state(s_cur)


def _ssd_out_kernel(z_ref, x_ref, bc_ref, tab_ref, dexp_ref, ng_ref,
                    init_ref, enter_ref, y_ref, fin_ref, s_ref, *, regions):
    q = SSD_CHUNK
    c = pl.program_id(0)
    _, first, last = _chunk_pos(c, SCAN_STEP, regions)

    @pl.when(first)
    def _():
        s_ref[...] = _expand_state(init_ref[...])

    lane = lax.broadcasted_iota(jnp.int32, (1, LANES), 1)
    lo_half = lane < SSD_HEADDIM
    tril = _block_tri(q, q, False)
    ii = lax.broadcasted_iota(jnp.int32, (q, q), 0)
    jj = lax.broadcasted_iota(jnp.int32, (q, q), 1)
    eye = ii == jj
    upd_mask = _state_update_mask()
    expand = _head_expand_matrix(0)
    dexp = dexp_ref[...]
    ng = ng_ref[...]

    s_cur = s_ref[...]
    for ci in range(SCAN_STEP // q):
        rows = slice(ci * q, (ci + 1) * q)
        xs = x_ref[rows, :].astype(F32)
        bsb = bc_ref[rows, 0:LANES]
        csb = bc_ref[rows, LANES:2 * LANES]
        m = tab_ref[rows, 0:LANES]
        dtv = tab_ref[rows, LANES:2 * LANES]
        wgt = jnp.exp(jnp.where(lane < SSD_HEADS, m[q - 1:q, :] - m, 0.0)) * dtv
        mt = (m - jnp.log(dtv)).T
        ldt = jnp.log(dtv + pltpu.roll(dtv, LANES - SSD_HEADS, 1)).T
        zero_b = jnp.zeros_like(csb)
        cb = [_dot_nt(jnp.where(lo_half, csb, zero_b), bsb).astype(BF16),
              _dot_nt(jnp.where(lo_half, zero_b, csb), bsb).astype(BF16)]
        cs_f = _dot(csb, s_cur.astype(BF16))
        cs_b = _dot(csb, _expand_state(enter_ref[ci]).astype(BF16))

        ys, dec = [], []
        for p in range(SSD_HEADS // 2):
            sl = slice(p * LANES, (p + 1) * LANES)
            g = (2 * p) // (SSD_HEADS // SSD_GROUPS)
            xs_p = xs[:, sl]
            xs_pb = xs_p.astype(BF16)
            yd, colf, colb = [], [], []
            for h in (2 * p, 2 * p + 1):
                hb = SSD_HEADS + h
                cf = _col(m, h, q)
                cbk = _col(m, hb, q)
                seg = jnp.where(tril, cf - mt[h:h + 1, :], cbk - mt[hb:hb + 1, :])
                seg = jnp.where(eye, ldt[h:h + 1, :], seg)
                yd.append(_dot(cb[g] * jnp.exp(seg).astype(BF16), xs_pb))
                colf.append(cf)
                colb.append(cbk)
            ef = jnp.exp(jnp.where(lo_half, colf[0], colf[1]))
            eb = jnp.exp(jnp.where(lo_half, colb[0], colb[1]))
            y_p = jnp.where(lo_half, yd[0], yd[1]) + ef * cs_f[:, sl] + eb * cs_b[:, sl] + dexp[:, sl] * xs_p
            ys.append(y_p)
            dec.append(ef[q - 1:q, :])

        y = jnp.concatenate(ys, axis=1) * _silu(z_ref[rows, :].astype(F32))
        y_ref[rows, :] = _rms(y, ng).astype(y_ref.dtype)

        xw = (xs * _dot(wgt.astype(BF16), expand)).astype(BF16)
        upd = _dot_tn(bsb, xw)
        s_cur = s_cur * jnp.concatenate(dec, axis=1) + jnp.where(upd_mask, upd, 0.0)
    s_ref[...] = s_cur

    @pl.when(last)
    def _():
        fin_ref[...] = _compact_state(s_cur)


def _const_spec(a):
    return pl.BlockSpec(a.shape, lambda s: (0,) * a.ndim)


def _sweep_call(parts, n_steps, name):
    n_in = [len(p["inputs"]) for p in parts]
    n_out = [len(p["out_shape"]) for p in parts]
    n_scr = [len(p["scratch"]) for p in parts]

    def kernel(*refs):
        ins, outs, scr = refs[:sum(n_in)], refs[sum(n_in):sum(n_in) + sum(n_out)], refs[sum(n_in) + sum(n_out):]
        for k, p in enumerate(parts):
            i0, o0, s0 = sum(n_in[:k]), sum(n_out[:k]), sum(n_scr[:k])
            p["kernel"](*ins[i0:i0 + n_in[k]], *outs[o0:o0 + n_out[k]], *scr[s0:s0 + n_scr[k]])

    res = pl.pallas_call(
        kernel,
        out_shape=tuple(o for p in parts for o in p["out_shape"]),
        grid=(n_steps,),
        in_specs=[s for p in parts for s in p["in_specs"]],
        out_specs=tuple(s for p in parts for s in p["out_specs"]),
        scratch_shapes=[s for p in parts for s in p["scratch"]],
        compiler_params=_params("arbitrary"),
        name=name,
    )(*[a for p in parts for a in p["inputs"]])
    return [res[sum(n_out[:k]):sum(n_out[:k]) + n_out[k]] for k in range(len(parts))]


def _ssd_bwd_part(xbc, small, init, lw, regions):
    st = SCAN_STEP
    cps = st // SSD_CHUNK
    t = xbc.shape[0]
    n_steps = t // st
    n_seq = regions[0][0] + regions[1][0]
    bidx = lambda s: n_steps - 1 - s
    seq_b = lambda s: _chunk_pos(bidx(s), st, regions)[0]
    hb = st // HALO
    n_hb = t // HALO
    consts = [lw["conv_w"], lw["conv_b"], lw["dt_bias"], lw["a_log"]]
    return dict(
        kernel=functools.partial(_ssd_bwd_kernel, regions=regions, n_steps=n_steps),
        inputs=[xbc, xbc, xbc, small, *consts, init],
        in_specs=[pl.BlockSpec((st, D_CONV), lambda s: (bidx(s), 0)),
                  pl.BlockSpec((HALO, D_CONV), lambda s: (jnp.maximum(bidx(s) * hb - 1, 0), 0)),
                  pl.BlockSpec((HALO, D_CONV), lambda s: (jnp.minimum((bidx(s) + 1) * hb, n_hb - 1), 0)),
                  pl.BlockSpec((st, LANES), lambda s: (bidx(s), 0))]
        + [_const_spec(a) for a in consts]
        + [pl.BlockSpec((None, None, SSD_STATE, SSD_INNER), lambda s: (seq_b(s), 1, 0, 0))],
        out_shape=[jax.ShapeDtypeStruct((t, D_CONV), BF16),
                   jax.ShapeDtypeStruct((t, 2 * LANES), F32),
                   jax.ShapeDtypeStruct((n_steps * cps, SSD_STATE, SSD_INNER), F32),
                   jax.ShapeDtypeStruct((n_seq, SSD_STATE, SSD_INNER), F32)],
        out_specs=[pl.BlockSpec((st, D_CONV), lambda s: (bidx(s), 0)),
                   pl.BlockSpec((st, 2 * LANES), lambda s: (bidx(s), 0)),
                   pl.BlockSpec((cps, SSD_STATE, SSD_INNER), lambda s: (bidx(s), 0, 0)),
                   pl.BlockSpec((None, SSD_STATE, SSD_INNER), lambda s: (seq_b(s), 0, 0))],
        scratch=[pltpu.VMEM((SSD_GROUPS * SSD_STATE, SSD_INNER), F32),
                 pltpu.VMEM((st + 2 * HALO, D_CONV), F32)])


def _ssd_fwd_part(proj, xbc, tables, init, enter_b, lw, regions):
    st = SCAN_STEP
    cps = st // SSD_CHUNK
    t = proj.shape[0]
    n_seq = regions[0][0] + regions[1][0]
    seq_f = lambda s: _chunk_pos(s, st, regions)[0]
    consts = [lw["d_exp"], lw["ssd_norm_g"]]
    return dict(
        kernel=functools.partial(_ssd_out_kernel, regions=regions),
        inputs=[proj, xbc, xbc, tables, *consts, init, enter_b],
        in_specs=[pl.BlockSpec((st, SSD_INNER), lambda s: (s, COL_Z // SSD_INNER)),
                  pl.BlockSpec((st, SSD_INNER), lambda s: (s, CONV_X // SSD_INNER)),
                  pl.BlockSpec((st, 2 * LANES), lambda s: (s, CONV_BC // (2 * LANES))),
                  pl.BlockSpec((st, 2 * LANES), lambda s: (s, 0))]
        + [_const_spec(a) for a in consts]
        + [pl.BlockSpec((None, None, SSD_STATE, SSD_INNER), lambda s: (seq_f(s), 0, 0, 0)),
           pl.BlockSpec((cps, SSD_STATE, SSD_INNER), lambda s: (s, 0, 0))],
        out_shape=[jax.ShapeDtypeStruct((t, SSD_INNER), BF16),
                   jax.ShapeDtypeStruct((n_seq, SSD_STATE, SSD_INNER), F32)],
        out_specs=[pl.BlockSpec((st, SSD_INNER), lambda s: (s, 0)),
                   pl.BlockSpec((None, SSD_STATE, SSD_INNER), lambda s: (seq_f(s), 0, 0))],
        scratch=[pltpu.VMEM((SSD_GROUPS * SSD_STATE, SSD_INNER), F32)])


def _gla_log_decay(sm, gw_hi_ref, gw_lo_ref, gb_ref):
    s1, s2 = _split2(sm)
    logit = _dot(s1, gw_hi_ref[...]) + _dot(s2, gw_hi_ref[...]) + _dot(s1, gw_lo_ref[...]) + gb_ref[...]
    return -_softplus(-logit) * (1.0 / GLA_GATE_TAU)


def _chunk_rows(x, n_chunks, rows_per_chunk):
    return jnp.concatenate([jnp.broadcast_to(x[c:c + 1, :], (rows_per_chunk, x.shape[1]))
                            for c in range(n_chunks)], axis=0)


def _gla_bwd_kernel(qk_ref, v_ref, sm_ref, gwh_ref, gwl_ref, gb_ref, init_ref,
                    rev_ref, enter_ref, fin_ref, s_ref, *, regions, n_steps):
    cq = GLA_CHUNK
    st = SCAN_STEP
    ncs = st // cq
    c = n_steps - 1 - pl.program_id(0)
    _, first, last = _chunk_pos(c, st, regions)

    @pl.when(last)
    def _():
        s_ref[...] = init_ref[...]

    bwd = slice(GLA_KDIM, 2 * GLA_KDIM)
    lg = _gla_log_decay(sm_ref[...], gwh_ref.at[:, bwd], gwl_ref.at[:, bwd], gb_ref.at[:, bwd])
    rev = _dot_exact_lhs(_block_tri(st, cq, True).astype(BF16), lg)
    rev_ref[...] = rev
    tot = jnp.concatenate([rev[ci * cq:ci * cq + 1, :] for ci in range(ncs)], axis=0)
    ko = (qk_ref[:, GLA_KDIM:2 * GLA_KDIM].astype(F32) * jnp.exp(_chunk_rows(tot, ncs, cq) - rev)).astype(BF16)
    etot = jnp.exp(tot)
    lo_half = lax.broadcasted_iota(jnp.int32, (1, LANES), 1) < GLA_DK

    s_cur = s_ref[...]
    for ci in reversed(range(ncs)):
        rows = slice(ci * cq, (ci + 1) * cq)
        enter_ref[ci] = s_cur
        new = []
        for p in range(GLA_HEADS // 2):
            sl = slice(p * LANES, (p + 1) * LANES)
            u0 = _dot_tn(v_ref[rows, (2 * p) * GLA_DV:(2 * p + 1) * GLA_DV], ko[rows, sl])
            u1 = _dot_tn(v_ref[rows, (2 * p + 1) * GLA_DV:(2 * p + 2) * GLA_DV], ko[rows, sl])
            new.append(s_cur[:, sl] * etot[ci:ci + 1, sl] + jnp.where(lo_half, u0, u1))
        s_cur = jnp.concatenate(new, axis=1)
    s_ref[...] = s_cur

    @pl.when(first)
    def _():
        fin_ref[...] = s_cur


def _gla_out_kernel(qk_ref, v_ref, r_ref, sm_ref, gwh_ref, gwl_ref, gb_ref, ng_ref, init_ref, enter_ref, rev_ref,
                    y_ref, fin_ref, s_ref, *, regions):
    cq = GLA_CHUNK
    st = SCAN_STEP
    ncs = st // cq
    c = pl.program_id(0)
    _, first, last = _chunk_pos(c, st, regions)

    @pl.when(first)
    def _():
        s_ref[...] = init_ref[...]

    fwd = slice(0, GLA_KDIM)
    lg = _gla_log_decay(sm_ref[...], gwh_ref.at[:, fwd], gwl_ref.at[:, fwd], gb_ref.at[:, fwd])
    gcs = _dot_exact_lhs(_block_tri(st, cq, False).astype(BF16), lg)
    rev = rev_ref[...]
    lo_half = lax.broadcasted_iota(jnp.int32, (1, LANES), 1) < GLA_DK
    zero_b = jnp.zeros((cq, LANES), BF16)
    ii = lax.broadcasted_iota(jnp.int32, (2 * cq, cq), 0) % cq
    jj = lax.broadcasted_iota(jnp.int32, (2 * cq, cq), 1)
    tril2 = jj <= ii
    triu2 = jj >= ii
    ng = ng_ref[...]

    def stack_heads(x):
        return jnp.concatenate([jnp.where(lo_half, x, zero_b), jnp.where(lo_half, zero_b, x)], axis=0)

    s_cur = s_ref[...]
    for ci in range(ncs):
        rows = slice(ci * cq, (ci + 1) * cq)
        g_f = gcs[rows]
        g_b = rev[rows]
        etot = jnp.exp(g_f[cq - 1:cq, :])
        qs = qk_ref[rows, 0:GLA_KDIM].astype(F32) * (GLA_DK ** -0.5)
        ks = qk_ref[rows, GLA_KDIM:2 * GLA_KDIM].astype(F32)
        qe_f = (qs * jnp.exp(g_f)).astype(BF16)
        ke = ks * jnp.exp(-g_f)
        ke_f = ke.astype(BF16)
        ko_f = (ke * etot).astype(BF16)
        qe_b = (qs * jnp.exp(g_b)).astype(BF16)
        ke_b = (ks * jnp.exp(-g_b)).astype(BF16)
        s_curb = s_cur.astype(BF16)
        s_entb = enter_ref[ci].astype(BF16)
        new = []
        for p in range(GLA_HEADS // 2):
            sl = slice(p * LANES, (p + 1) * LANES)
            qf2 = stack_heads(qe_f[:, sl])
            qb2 = stack_heads(qe_b[:, sl])
            att = (jnp.where(tril2, _dot_nt(qf2, ke_f[:, sl]), 0.0)
                   + jnp.where(triu2, _dot_nt(qb2, ke_b[:, sl]), 0.0)).astype(BF16)
            inter = _dot_nt(qf2, s_curb[:, sl]) + _dot_nt(qb2, s_entb[:, sl])
            upd = []
            for hh in (0, 1):
                h = 2 * p + hh
                hs = slice(h * GLA_DV, (h + 1) * GLA_DV)
                hr = slice(hh * cq, (hh + 1) * cq)
                vh = v_ref[rows, hs]
                o = _dot(att[hr], vh) + inter[hr]
                o = o * lax.rsqrt(jnp.mean(o * o, axis=-1, keepdims=True) + EPS) * ng[:, hs]
                y_ref[rows, hs] = (o * _silu(r_ref[rows, hs].astype(F32))).astype(y_ref.dtype)
                upd.append(_dot_tn(vh, ko_f[:, sl]))
            new.append(s_cur[:, sl] * etot[:, sl] + jnp.where(lo_half, upd[0], upd[1]))
        s_cur = jnp.concatenate(new, axis=1)
    s_ref[...] = s_cur

    @pl.when(last)
    def _():
        fin_ref[...] = s_cur


def _gla_bwd_part(proj, small, init, lw, regions):
    st = SCAN_STEP
    ncs = st // GLA_CHUNK
    t = proj.shape[0]
    n_steps = t // st
    n_seq = regions[0][0] + regions[1][0]
    consts = [lw["gate_w_hi"], lw["gate_w_lo"], lw["gate_b"]]
    bidx = lambda s: n_steps - 1 - s
    seq_b = lambda s: _chunk_pos(bidx(s), st, regions)[0]
    return dict(
        kernel=functools.partial(_gla_bwd_kernel, regions=regions, n_steps=n_steps),
        inputs=[proj, proj, small, *consts, init],
        in_specs=[pl.BlockSpec((st, 2 * GLA_KDIM), lambda s: (bidx(s), COL_QK // (2 * GLA_KDIM))),
                  pl.BlockSpec((st, GLA_VDIM), lambda s: (bidx(s), COL_V // GLA_VDIM)),
                  pl.BlockSpec((st, LANES), lambda s: (bidx(s), 0))]
        + [_const_spec(a) for a in consts]
        + [pl.BlockSpec((None, None, GLA_DV, GLA_KDIM), lambda s: (seq_b(s), 1, 0, 0))],
        out_shape=[jax.ShapeDtypeStruct((t, GLA_KDIM), F32),
                   jax.ShapeDtypeStruct((n_steps * ncs, GLA_DV, GLA_KDIM), F32),
                   jax.ShapeDtypeStruct((n_seq, GLA_DV, GLA_KDIM), F32)],
        out_specs=[pl.BlockSpec((st, GLA_KDIM), lambda s: (bidx(s), 0)),
                   pl.BlockSpec((ncs, GLA_DV, GLA_KDIM), lambda s: (bidx(s), 0, 0)),
                   pl.BlockSpec((None, GLA_DV, GLA_KDIM), lambda s: (seq_b(s), 0, 0))],
        scratch=[pltpu.VMEM((GLA_DV, GLA_KDIM), F32)])


def _gla_fwd_part(proj, small, init, enter_b, rev_b, lw, regions):
    st = SCAN_STEP
    ncs = st // GLA_CHUNK
    t = proj.shape[0]
    n_seq = regions[0][0] + regions[1][0]
    consts = [lw["gate_w_hi"], lw["gate_w_lo"], lw["gate_b"], lw["gla_norm_g"]]
    seq_f = lambda s: _chunk_pos(s, st, regions)[0]
    return dict(
        kernel=functools.partial(_gla_out_kernel, regions=regions),
        inputs=[proj, proj, proj, small, *consts, init, enter_b, rev_b],
        in_specs=[pl.BlockSpec((st, 2 * GLA_KDIM), lambda s: (s, COL_QK // (2 * GLA_KDIM))),
                  pl.BlockSpec((st, GLA_VDIM), lambda s: (s, COL_V // GLA_VDIM)),
                  pl.BlockSpec((st, GLA_VDIM), lambda s: (s, COL_R // GLA_VDIM)),
                  pl.BlockSpec((st, LANES), lambda s: (s, 0))]
        + [_const_spec(a) for a in consts]
        + [pl.BlockSpec((None, None, GLA_DV, GLA_KDIM), lambda s: (seq_f(s), 0, 0, 0)),
           pl.BlockSpec((ncs, GLA_DV, GLA_KDIM), lambda s: (s, 0, 0)),
           pl.BlockSpec((st, GLA_KDIM), lambda s: (s, 0))],
        out_shape=[jax.ShapeDtypeStruct((t, GLA_VDIM), BF16),
                   jax.ShapeDtypeStruct((n_seq, GLA_DV, GLA_KDIM), F32)],
        out_specs=[pl.BlockSpec((st, GLA_VDIM), lambda s: (s, 0)),
                   pl.BlockSpec((None, GLA_DV, GLA_KDIM), lambda s: (seq_f(s), 0, 0))],
        scratch=[pltpu.VMEM((GLA_DV, GLA_KDIM), F32)])


def _mixers(proj, xbc, small, init_ssd, init_gla, lw, regions):
    n_steps = proj.shape[0] // SCAN_STEP
    (xbc_c, tab_s, enter_s, fin_sb), (rev_g, enter_g, fin_gb) = _sweep_call(
        [_ssd_bwd_part(xbc, small, init_ssd, lw, regions), _gla_bwd_part(proj, small, init_gla, lw, regions)],
        n_steps, "scan_backward_states")
    (y_ssd, fin_sf), (y_gla, fin_gf) = _sweep_call(
        [_ssd_fwd_part(proj, xbc_c, tab_s, init_ssd, enter_s, lw, regions),
         _gla_fwd_part(proj, small, init_gla, enter_g, rev_g, lw, regions)],
        n_steps, "scan_forward_outputs")
    return y_ssd, fin_sf, fin_sb, y_gla, fin_gf, fin_gb


def _merge_kernel(x_ref, ys_ref, yg_ref, ga_ref, gb_ref, mod_ref, wbs_ref, wbg_ref, wo_ref, o_ref):
    merged = (_sigmoid(ga_ref[...].astype(F32)) * _dot(ys_ref[...], wbs_ref[...])
              + _sigmoid(gb_ref[...].astype(F32)) * _dot(yg_ref[...], wbg_ref[...]))
    mix = _dot(merged.astype(BF16), wo_ref[...])
    o_ref[...] = x_ref[...] + mod_ref[:, 2 * D_MODEL:3 * D_MODEL] * mix


def _merge(x, y_ssd, y_gla, proj, mods, lw, regions):
    t = x.shape[0]
    tm = min(TM_MERGE, regions[0][0] * regions[0][1], regions[1][1])
    tok = lambda col: pl.BlockSpec((tm, D_MODEL), lambda i: (i, col))
    wspec = pl.BlockSpec((D_MODEL, D_MODEL), lambda i: (0, 0))
    return pl.pallas_call(
        _merge_kernel,
        out_shape=jax.ShapeDtypeStruct((t, D_MODEL), F32),
        grid=(t // tm,),
        in_specs=[tok(0), tok(0), tok(0), tok(COL_GA // D_MODEL), tok(COL_GB // D_MODEL),
                  pl.BlockSpec((None, 1, 6 * D_MODEL), lambda i: (_mod_row(i, tm, regions), 0, 0)),
                  wspec, wspec, wspec],
        out_specs=tok(0),
        compiler_params=_params("parallel"),
        name="merge_out_projection",
    )(x, y_ssd, y_gla, proj, proj, mods, lw["w_bs"], lw["w_bg"], lw["w_o"])


def _ffn_prologue(x_ref, mod_ref, g_ref):
    y = _rms(x_ref[...], g_ref[...])
    return y * (1.0 + mod_ref[:, 4 * D_MODEL:5 * D_MODEL]) + mod_ref[:, 3 * D_MODEL:4 * D_MODEL]


def _swiglu_resident(v, w1_ref, w3_ref, w2_ref):
    acc = None
    for hh in range(w1_ref.shape[1] // TH_FFN):
        cols = slice(hh * TH_FFN, (hh + 1) * TH_FFN)
        h = _silu(_dot(v, w1_ref[:, cols])) * _dot(v, w3_ref[:, cols])
        part = _dot(h.astype(BF16), w2_ref[cols, :])
        acc = part if acc is None else acc + part
    return acc


def _ffn_kernel(x_ref, mod_ref, g_ref, w1_ref, w3_ref, w2_ref, o_ref):
    v = _ffn_prologue(x_ref, mod_ref, g_ref).astype(BF16)
    o_ref[...] = x_ref[...] + mod_ref[:, 5 * D_MODEL:6 * D_MODEL] * _swiglu_resident(v, w1_ref, w3_ref, w2_ref)


def _dense_ffn(x, mods, g, w1, w3, w2, regions):
    t = x.shape[0]
    tm = min(TM_FFN, regions[0][0] * regions[0][1], regions[1][1])
    hid = w1.shape[1]
    once = pl.Buffered(1)
    return pl.pallas_call(
        _ffn_kernel,
        out_shape=jax.ShapeDtypeStruct((t, D_MODEL), F32),
        grid=(t // tm,),
        in_specs=[pl.BlockSpec((tm, D_MODEL), lambda i: (i, 0)),
                  pl.BlockSpec((None, 1, 6 * D_MODEL), lambda i: (_mod_row(i, tm, regions), 0, 0)),
                  pl.BlockSpec((1, D_MODEL), lambda i: (0, 0)),
                  pl.BlockSpec((D_MODEL, hid), lambda i: (0, 0), pipeline_mode=once),
                  pl.BlockSpec((D_MODEL, hid), lambda i: (0, 0), pipeline_mode=once),
                  pl.BlockSpec((hid, D_MODEL), lambda i: (0, 0), pipeline_mode=once)],
        out_specs=pl.BlockSpec((tm, D_MODEL), lambda i: (i, 0)),
        compiler_params=_params("parallel"),
        name="dense_swiglu",
    )(x, mods, g, w1, w3, w2)


def _top2_gates(logits):
    lane = lax.broadcasted_iota(jnp.int32, logits.shape, 1)
    lg = jnp.where(lane < N_EXPERTS, logits, -jnp.inf)
    m1 = jnp.max(lg, axis=-1, keepdims=True)
    i1 = jnp.min(jnp.where(lg == m1, lane, LANES), axis=-1, keepdims=True)
    lg2 = jnp.where(lane == i1, -jnp.inf, lg)
    m2 = jnp.max(lg2, axis=-1, keepdims=True)
    i2 = jnp.min(jnp.where(lg2 == m2, lane, LANES), axis=-1, keepdims=True)
    e2 = jnp.exp(m2 - m1)
    den = 1.0 + e2
    return jnp.where(lane == i1, 1.0 / den, 0.0) + jnp.where(lane == i2, e2 / den, 0.0)


def _lane_scalar(vec, e):
    lane = lax.broadcasted_iota(jnp.int32, vec.shape, 1)
    return jnp.sum(jnp.where(lane == e, vec, 0.0)).astype(jnp.int32)


def _n_row_blocks(n):
    return (n + MOE_ROW_BLOCK - 1) // MOE_ROW_BLOCK


def _moe_block_copy(stage_ref, sem, slot, hbm_ref, row, to_hbm):
    if not isinstance(row, int):
        row = pl.multiple_of(row, MOE_ROW_ALIGN)
    rows = hbm_ref.at[pl.ds(row, MOE_ROW_BLOCK), :]
    if to_hbm:
        return pltpu.make_async_copy(stage_ref.at[slot], rows, sem.at[slot])
    return pltpu.make_async_copy(rows, stage_ref.at[slot], sem.at[slot])


def _moe_dispatch_kernel(x_ref, mod_ref, g_ref, rw_ref, rb_ref, donor_hbm,
                         xs_hbm, gate_ref, rank_ref, off_ref, tot_ref,
                         v_ref, gate_t_ref, rank_t_ref, stage_ref, sem, run_ref, pend_ref, *, cap):
    del donor_hbm
    i = pl.program_id(0)
    tm = x_ref.shape[0]
    rb = MOE_ROW_BLOCK

    @pl.when(i == 0)
    def _():
        run_ref[...] = jnp.zeros_like(run_ref)
        pend_ref[0] = 0

    v = _ffn_prologue(x_ref, mod_ref, g_ref)
    v_ref[...] = v.astype(BF16)
    gate = _top2_gates(_dot_hilo(v, rw_ref[...]) + rb_ref[...])
    sel_f = jnp.where(gate > 0.0, 1.0, 0.0)
    sel = sel_f.astype(BF16)
    gate_ref[...] = gate
    gate_t_ref[...] = gate.T
    for blk in range(tm // rb):
        ii = lax.broadcasted_iota(jnp.int32, (rb, tm), 0) + blk * rb
        jj = lax.broadcasted_iota(jnp.int32, (rb, tm), 1)
        rank = _dot(jnp.where(jj < ii, 1.0, 0.0).astype(BF16), sel)
        rank_ref[blk * rb:(blk + 1) * rb, :] = rank
        rank_t_ref[:, blk * rb:(blk + 1) * rb] = rank.T

    cnt = jnp.sum(sel_f, axis=0, keepdims=True)
    cnt_pad = jnp.floor((cnt + (MOE_ROW_ALIGN - 1)) * (1.0 / MOE_ROW_ALIGN)) * MOE_ROW_ALIGN
    run = run_ref[...]
    off_ref[...] = run

    def wait_prev(s, carry):
        _moe_block_copy(stage_ref, sem, s, xs_hbm, 0, True).wait()
        return carry
    lax.fori_loop(0, pend_ref[0], wait_prev, 0)

    slot = jnp.int32(0)
    for e in range(N_EXPERTS):
        n_e = _lane_scalar(cnt, e)
        base = e * cap + _lane_scalar(run, e)
        g_row = gate_t_ref[e:e + 1, :]
        r_row = rank_t_ref[e:e + 1, :].astype(jnp.int32)

        def pack(b, s, base=base, g_row=g_row, r_row=r_row):
            row = lax.broadcasted_iota(jnp.int32, (rb, tm), 0) + b * rb
            onehot = jnp.where(jnp.logical_and(row == r_row, g_row > 0.0), 1.0, 0.0).astype(BF16)
            stage_ref[s] = _dot(onehot, v_ref[...]).astype(BF16)
            _moe_block_copy(stage_ref, sem, s, xs_hbm, base + b * rb, True).start()
            return s + 1
        slot = lax.fori_loop(0, _n_row_blocks(n_e), pack, slot)

    pend_ref[0] = slot
    run_ref[...] = run + cnt_pad
    tot_ref[...] = run + cnt_pad

    @pl.when(i == pl.num_programs(0) - 1)
    def _():
        def wait_last(s, carry):
            _moe_block_copy(stage_ref, sem, s, xs_hbm, 0, True).wait()
            return carry
        lax.fori_loop(0, slot, wait_last, 0)


def _moe_expert_kernel(te_ref, tr_ref, na_ref, xs_ref, w1_ref, w3_ref, w2_ref, ys_ref):
    del te_ref, tr_ref

    @pl.when(pl.program_id(0) < na_ref[0])
    def _():
        ys_ref[...] = _swiglu_resident(xs_ref[...], w1_ref, w3_ref, w2_ref).astype(ys_ref.dtype)


def _moe_combine_kernel(off_ref, x_ref, mod_ref, gate_ref, rank_ref, ys_hbm, o_ref, stage_ref, sem, *, cap):
    i = pl.program_id(0)
    tm = x_ref.shape[0]
    rb = MOE_ROW_BLOCK
    gate = gate_ref[...]
    rank = rank_ref[...]
    cnt = jnp.sum(jnp.where(gate > 0.0, 1.0, 0.0), axis=0, keepdims=True)
    n_blocks = [_n_row_blocks(_lane_scalar(cnt, e)) for e in range(N_EXPERTS)]
    bases = [e * cap + off_ref[i * LANES + e] for e in range(N_EXPERTS)]

    slot = jnp.int32(0)
    for e in range(N_EXPERTS):
        def fetch(b, s, base=bases[e]):
            _moe_block_copy(stage_ref, sem, s, ys_hbm, base + b * rb, False).start()
            return s + 1
        slot = lax.fori_loop(0, n_blocks[e], fetch, slot)

    o_ref[...] = jnp.zeros_like(o_ref)
    slot = jnp.int32(0)
    for e in range(N_EXPERTS):
        g_col = gate[:, e:e + 1]
        r_col = rank[:, e:e + 1].astype(jnp.int32)

        def combine(b, s, g_col=g_col, r_col=r_col):
            _moe_block_copy(stage_ref, sem, s, ys_hbm, 0, False).wait()
            col = lax.broadcasted_iota(jnp.int32, (tm, rb), 1) + b * rb
            onehot = jnp.where(jnp.logical_and(col == r_col, g_col > 0.0), 1.0, 0.0).astype(BF16)
            o_ref[...] += g_col * _dot(onehot, stage_ref[s])
            return s + 1
        slot = lax.fori_loop(0, n_blocks[e], combine, slot)

    o_ref[...] = x_ref[...] + mod_ref[:, 5 * D_MODEL:6 * D_MODEL] * o_ref[...]


def _moe_tile_table(totals, cap, n_steps):
    extent = totals + MOE_ROW_BLOCK
    n_tiles = (extent + TR_EXPERT - 1) // TR_EXPERT
    ends = jnp.cumsum(n_tiles)
    starts = ends - n_tiles
    n_active = ends[-1]
    t = jnp.minimum(jnp.arange(n_steps, dtype=jnp.int32), n_active - 1)
    e = jnp.sum((t[:, None] >= ends[None, :]).astype(jnp.int32), axis=1)
    row_block = e * (cap // TR_EXPERT) + (t - starts[e])
    return e.astype(jnp.int32), row_block.astype(jnp.int32), n_active.reshape(1).astype(jnp.int32)


def _moe_geometry(t, regions, n_e):
    tm = min(TM_MOE, regions[0][0] * regions[0][1], regions[1][1])
    n_tiles = t // tm
    cap_rows = t + n_tiles * (MOE_ROW_ALIGN - 1) + MOE_ROW_BLOCK
    cap = -(-cap_rows // TR_EXPERT) * TR_EXPERT
    return tm, n_tiles, cap


def _moe_ffn(x, mods, g, rw, rb, w1, w3, w2, regions, donor):
    t = x.shape[0]
    n_e, _, hid = w1.shape
    tm, n_tiles, cap = _moe_geometry(t, regions, n_e)
    max_slots = 2 * tm // MOE_ROW_BLOCK + n_e
    n_steps = (2 * t + n_tiles * n_e * (MOE_ROW_ALIGN - 1) + n_e * MOE_ROW_BLOCK) // TR_EXPERT + n_e
    mod_spec = pl.BlockSpec((None, 1, 6 * D_MODEL), lambda i, *_: (_mod_row(i, tm, regions), 0, 0))
    stage = [pltpu.VMEM((max_slots, MOE_ROW_BLOCK, D_MODEL), BF16), pltpu.SemaphoreType.DMA((max_slots,))]

    xs, gate, rank, offs, totals = pl.pallas_call(
        functools.partial(_moe_dispatch_kernel, cap=cap),
        out_shape=(jax.ShapeDtypeStruct((n_e * cap, D_MODEL), BF16),
                   jax.ShapeDtypeStruct((t, LANES), F32), jax.ShapeDtypeStruct((t, LANES), F32),
                   jax.ShapeDtypeStruct((n_tiles, 1, LANES), F32), jax.ShapeDtypeStruct((1, LANES), F32)),
        grid=(n_tiles,),
        in_specs=[pl.BlockSpec((tm, D_MODEL), lambda i: (i, 0)), mod_spec,
                  pl.BlockSpec((1, D_MODEL), lambda i: (0, 0)),
                  pl.BlockSpec((D_MODEL, LANES), lambda i: (0, 0)),
                  pl.BlockSpec((1, LANES), lambda i: (0, 0)),
                  pl.BlockSpec(memory_space=pl.ANY)],
        input_output_aliases={5: 0},
        out_specs=(pl.BlockSpec(memory_space=pl.ANY),
                   pl.BlockSpec((tm, LANES), lambda i: (i, 0)), pl.BlockSpec((tm, LANES), lambda i: (i, 0)),
                   pl.BlockSpec((None, 1, LANES), lambda i: (i, 0, 0)),
                   pl.BlockSpec((1, LANES), lambda i: (0, 0))),
        scratch_shapes=[pltpu.VMEM((tm, D_MODEL), BF16),
                        pltpu.VMEM((LANES, tm), F32), pltpu.VMEM((LANES, tm), F32)] + stage
        + [pltpu.VMEM((1, LANES), F32), pltpu.SMEM((1,), jnp.int32)],
        compiler_params=_params("arbitrary"),
        name="moe_dispatch",
    )(x, mods, g, rw, rb, donor)

    tile_e, tile_rb, n_active = _moe_tile_table(totals[0, :n_e].astype(jnp.int32), cap, n_steps)
    ys = pl.pallas_call(
        _moe_expert_kernel,
        out_shape=jax.ShapeDtypeStruct((n_e * cap, D_MODEL), BF16),
        grid_spec=pltpu.PrefetchScalarGridSpec(
            num_scalar_prefetch=3,
            grid=(n_steps,),
            in_specs=[pl.BlockSpec((TR_EXPERT, D_MODEL), lambda s, te, tr, na: (tr[s], 0)),
                      pl.BlockSpec((None, D_MODEL, hid), lambda s, te, tr, na: (te[s], 0, 0)),
                      pl.BlockSpec((None, D_MODEL, hid), lambda s, te, tr, na: (te[s], 0, 0)),
                      pl.BlockSpec((None, hid, D_MODEL), lambda s, te, tr, na: (te[s], 0, 0))],
            out_specs=pl.BlockSpec((TR_EXPERT, D_MODEL), lambda s, te, tr, na: (tr[s], 0))),
        input_output_aliases={3: 0},
        compiler_params=_params("arbitrary"),
        name="moe_experts",
    )(tile_e, tile_rb, n_active, xs, w1, w3, w2)

    out = pl.pallas_call(
        functools.partial(_moe_combine_kernel, cap=cap),
        out_shape=jax.ShapeDtypeStruct((t, D_MODEL), F32),
        grid_spec=pltpu.PrefetchScalarGridSpec(
            num_scalar_prefetch=1,
            grid=(n_tiles,),
            in_specs=[pl.BlockSpec((tm, D_MODEL), lambda i, off: (i, 0)), mod_spec,
                      pl.BlockSpec((tm, LANES), lambda i, off: (i, 0)),
                      pl.BlockSpec((tm, LANES), lambda i, off: (i, 0)),
                      pl.BlockSpec(memory_space=pl.ANY)],
            out_specs=pl.BlockSpec((tm, D_MODEL), lambda i, off: (i, 0)),
            scratch_shapes=stage),
        compiler_params=_params("arbitrary"),
        name="moe_combine",
    )(offs.reshape(-1).astype(jnp.int32), x, mods, gate, rank, ys)
    return out, ys


def _final_norm_kernel(x_ref, g_ref, o_ref):
    o_ref[...] = _rms(x_ref[...], g_ref[...])


def _final_norm(x, g, tok_off, n_tok):
    tm = min(TM_NORM, n_tok)
    off = tok_off // tm
    return pl.pallas_call(
        _final_norm_kernel,
        out_shape=jax.ShapeDtypeStruct((n_tok, D_MODEL), F32),
        grid=(n_tok // tm,),
        in_specs=[pl.BlockSpec((tm, D_MODEL), lambda i: (i + off, 0)),
                  pl.BlockSpec((1, D_MODEL), lambda i: (0, 0))],
        out_specs=pl.BlockSpec((tm, D_MODEL), lambda i: (i, 0)),
        compiler_params=_params("parallel"),
        name="final_norm",
    )(x, g)


def _grid_pos_embed(l, d):
    rows = l // GRID_W
    row = jnp.repeat(jnp.arange(rows, dtype=F32), GRID_W)
    col = jnp.tile(jnp.arange(GRID_W, dtype=F32), rows)
    quarter = d // 4
    omega = jnp.exp(-math.log(10000.0) * jnp.arange(quarter, dtype=F32) / quarter)
    er = row[:, None] * omega
    ec = col[:, None] * omega
    return jnp.concatenate([jnp.sin(er), jnp.cos(er), jnp.sin(ec), jnp.cos(ec)], axis=-1)


def _pad_lanes(a, width):
    return jnp.pad(a, [(0, 0)] * (a.ndim - 1) + [(0, width - a.shape[-1])])


def _reorder_w_in(w):
    o = np.cumsum([0, SSD_INNER, SSD_INNER + 2 * SSD_GROUPS * SSD_STATE, 2 * SSD_HEADS, GLA_KDIM, GLA_KDIM,
                   GLA_VDIM, GLA_VDIM, 2 * GLA_GATE_RANK, D_MODEL, D_MODEL])
    z, xbc, dt, qq, kk, vv, rr, lr, ga, gb = [w[:, int(o[i]):int(o[i + 1])] for i in range(10)]
    main = jnp.concatenate([z, vv, rr, ga, gb, qq, kk], axis=1).astype(BF16)
    small = _pad_lanes(jnp.concatenate([dt, lr], axis=1), LANES).astype(BF16)
    return main, xbc.astype(BF16), small


def _layer_weights(i, p):
    conv_w = jnp.pad(p["ssd_conv_w"][i], ((0, SUBLANES - SSD_CONV), (0, 0)))
    conv_b = p["ssd_conv_b"][i][None, :]
    gate_w = jnp.zeros((LANES, 2 * GLA_KDIM), F32)
    gate_w = gate_w.at[SM_LR:SM_LR + GLA_GATE_RANK, :GLA_KDIM].set(p["gla_gate_w"][i, 0])
    gate_w = gate_w.at[SM_LR + GLA_GATE_RANK:SM_LR + 2 * GLA_GATE_RANK, GLA_KDIM:].set(p["gla_gate_w"][i, 1])
    gate_w_hi = gate_w.astype(BF16)
    gate_w_lo = (gate_w - gate_w_hi.astype(F32)).astype(BF16)
    w_main, w_conv, w_small = _reorder_w_in(p["w_in"][i])
    return {
        "w_in": w_main, "w_in_conv": w_conv, "w_in_small": w_small,
        "conv_w": conv_w, "conv_b": conv_b,
        "dt_bias": _pad_lanes(p["ssd_dt_bias"][i].reshape(1, -1), LANES),
        "a_log": _pad_lanes(p["ssd_a_log"][i].reshape(1, -1), LANES),
        "d_exp": jnp.repeat(p["ssd_d"][i], SSD_HEADDIM)[None, :],
        "ssd_norm_g": p["ssd_norm_g"][i][None, :],
        "gate_w_hi": gate_w_hi, "gate_w_lo": gate_w_lo, "gate_b": p["gla_gate_b"][i].reshape(1, -1),
        "gla_norm_g": p["gla_norm_g"][i][None, :],
        "w_bs": p["w_branch_ssd"][i].astype(BF16), "w_bg": p["w_branch_gla"][i].astype(BF16),
        "w_o": p["w_out"][i].astype(BF16),
    }


def _ssd_state_in(s):
    b = s.shape[0]
    return jnp.transpose(s, (0, 1, 4, 2, 3)).reshape(b, 2, SSD_STATE, SSD_INNER)


def _ssd_state_out(f, b):
    s = jnp.stack([f, b], axis=1).reshape(-1, 2, SSD_STATE, SSD_HEADS, SSD_HEADDIM)
    return jnp.transpose(s, (0, 1, 3, 4, 2))


def _gla_state_in(s):
    b = s.shape[0]
    return jnp.transpose(s, (0, 1, 4, 2, 3)).reshape(b, 2, GLA_DV, GLA_KDIM)


def _gla_state_out(f, b):
    s = jnp.stack([f, b], axis=1).reshape(-1, 2, GLA_DV, GLA_HEADS, GLA_DK)
    return jnp.transpose(s, (0, 1, 3, 4, 2))


def _trunk(x_prompt, x_sample, state_ssd, state_gla, c, c_ctx, p):
    n0, l0, _ = x_prompt.shape
    n1, l1, _ = x_sample.shape
    regions = ((n0, l0), (n1, l1))
    depth = p["w_in"].shape[0]

    cc = jnp.zeros((MOD_ROWS, D_MODEL), F32).at[0].set(c_ctx).at[1:1 + n1].set(c)
    mods = _modulation_table(cc, p["ada_w"], p["ada_b"])[:, :, None, :]
    x = _assemble_tokens(x_prompt, x_sample, _grid_pos_embed(l1, D_MODEL), regions)

    ssd_states, gla_states = [], []
    moe_rows = None
    for i in range(depth):
        lw = _layer_weights(i, p)
        init_ssd = jnp.concatenate([jnp.zeros((n0, 2, SSD_STATE, SSD_INNER), F32),
                                    _ssd_state_in(state_ssd[:, i])], axis=0)
        init_gla = jnp.concatenate([jnp.zeros((n0, 2, GLA_DV, GLA_KDIM), F32),
                                    _gla_state_in(state_gla[:, i])], axis=0)
        proj, xbc, small = _in_projection(x, mods[i], p["norm1_g"][i][None, :], lw, regions)
        y_ssd, sf, sb, y_gla, gf, gb = _mixers(proj, xbc, small, init_ssd, init_gla, lw, regions)
        ssd_states.append(_ssd_state_out(sf[:n0], sb[:n0]))
        gla_states.append(_gla_state_out(gf[:n0], gb[:n0]))
        x = _merge(x, y_ssd, y_gla, proj, mods[i], lw, regions)
        j = i // 2
        g2 = p["norm2_g"][i][None, :]
        if i % 2 == 0:
            x = _dense_ffn(x, mods[i], g2, p["ffn_w1"][j].astype(BF16), p["ffn_w3"][j].astype(BF16),
                           p["ffn_w2"][j].astype(BF16), regions)
        else:
            if moe_rows is None:
                n_e = p["moe_w1"].shape[1]
                moe_rows = jnp.zeros((n_e * _moe_geometry(x.shape[0], regions, n_e)[2], D_MODEL), BF16)
            x, moe_rows = _moe_ffn(x, mods[i], g2, _pad_lanes(p["router_w"][j], LANES),
                                   _pad_lanes(p["router_b"][j][None, :], LANES),
                                   p["moe_w1"][j].astype(BF16), p["moe_w3"][j].astype(BF16),
                                   p["moe_w2"][j].astype(BF16), regions, moe_rows)

    gfin = p["final_norm_g"][None, :]
    y_prompt = _final_norm(x, gfin, 0, n0 * l0).reshape(n0, l0, D_MODEL)
    y_sample = _final_norm(x, gfin, n0 * l0, n1 * l1).reshape(n1, l1, D_MODEL)
    return y_prompt, y_sample, jnp.stack(ssd_states, axis=1), jnp.stack(gla_states, axis=1)


def kernel(x_prompt, x_sample, state_ssd, state_gla, c, c_ctx, ada_w, ada_b, norm1_g, norm2_g, w_in, ssd_conv_w, ssd_conv_b, ssd_dt_bias, ssd_a_log, ssd_d, ssd_norm_g, gla_gate_w, gla_gate_b, gla_norm_g, w_branch_ssd, w_branch_gla, w_out, ffn_w1, ffn_w3, ffn_w2, router_w, router_b, moe_w1, moe_w3, moe_w2, final_norm_g):
    p = dict(ada_w=ada_w, ada_b=ada_b, norm1_g=norm1_g, norm2_g=norm2_g, w_in=w_in, ssd_conv_w=ssd_conv_w,
             ssd_conv_b=ssd_conv_b, ssd_dt_bias=ssd_dt_bias, ssd_a_log=ssd_a_log, ssd_d=ssd_d,
             ssd_norm_g=ssd_norm_g, gla_gate_w=gla_gate_w, gla_gate_b=gla_gate_b, gla_norm_g=gla_norm_g,
             w_branch_ssd=w_branch_ssd, w_branch_gla=w_branch_gla, w_out=w_out, ffn_w1=ffn_w1, ffn_w3=ffn_w3,
             ffn_w2=ffn_w2, router_w=router_w, router_b=router_b, moe_w1=moe_w1, moe_w3=moe_w3, moe_w2=moe_w2,
             final_norm_g=final_norm_g)
    return _trunk(x_prompt, x_sample, state_ssd, state_gla, c, c_ctx, p)
```

```python
import functools
import math

import numpy as np
import jax
import jax.numpy as jnp
from jax import lax
from jax.experimental import pallas as pl
from jax.experimental.pallas import tpu as pltpu

F32 = jnp.float32
BF16 = jnp.bfloat16

D_MODEL = 1024
GRID_W = 64
EPS = 1e-6
SSD_HEADS = 16
SSD_HEADDIM = 64
SSD_INNER = SSD_HEADS * SSD_HEADDIM
SSD_GROUPS = 2
SSD_STATE = 64
SSD_CONV = 5
SSD_CHUNK = 128
GLA_HEADS = 8
GLA_DK = 64
GLA_DV = 128
GLA_KDIM = GLA_HEADS * GLA_DK
GLA_VDIM = GLA_HEADS * GLA_DV
GLA_GATE_RANK = 16
GLA_GATE_TAU = 16.0
GLA_CHUNK = 64
N_EXPERTS = 8

LANES = 128
SUBLANES = 8
BF16_SUBLANES = 16
VMEM_LIMIT_BYTES = 56 * 1024 * 1024

COL_Z, COL_V, COL_R, COL_GA, COL_GB, COL_QK = 0, 1024, 2048, 3072, 4096, 5120
D_PROJ = 6144
CONV_X, CONV_BC = 0, 1024
D_CONV = 1280
SM_LR = 2 * SSD_HEADS

TM_PROJ, TN_PROJ = 1024, 3072
TM_MERGE = 512
TM_FFN, TH_FFN = 512, 1408
TM_MOE = 512
MOE_ROW_BLOCK = 256
MOE_ROW_ALIGN = 16
TR_EXPERT = 512
TM_NORM = 1024
TN_MODS = 1536
MOD_ROWS = 16
SCAN_STEP = 256


def _params(*sem):
    return pltpu.CompilerParams(dimension_semantics=sem, vmem_limit_bytes=VMEM_LIMIT_BYTES)


def _sigmoid(x):
    return 0.5 * jnp.tanh(0.5 * x) + 0.5


def _silu(x):
    return x * _sigmoid(x)


def _softplus(x):
    return jnp.maximum(x, 0.0) + jnp.log(1.0 + jnp.exp(-jnp.abs(x)))


def _split2(a):
    a1 = a.astype(BF16)
    a2 = (a - a1.astype(F32)).astype(BF16)
    return a1, a2


def _dot(a, b):
    return jnp.dot(a, b, preferred_element_type=F32)


def _dot_nt(a, b):
    return lax.dot_general(a, b, (((1,), (1,)), ((), ())), preferred_element_type=F32)


def _dot_tn(a, b):
    return lax.dot_general(a, b, (((0,), (0,)), ((), ())), preferred_element_type=F32)


def _dot_exact_lhs(t01, a):
    a1, a2 = _split2(a)
    return _dot(t01, a1) + _dot(t01, a2)


def _dot_hilo(a, b):
    a1, a2 = _split2(a)
    b1, b2 = _split2(b)
    return _dot(a1, b1) + _dot(a2, b1) + _dot(a1, b2)


def _block_tri(n, blk, upper):
    i = lax.broadcasted_iota(jnp.int32, (n, n), 0)
    j = lax.broadcasted_iota(jnp.int32, (n, n), 1)
    tri = (j >= i) if upper else (j <= i)
    if blk == n:
        return tri
    return jnp.logical_and(tri, (i // blk) == (j // blk))


def _rms(x, g):
    return x * lax.rsqrt(jnp.mean(x * x, axis=-1, keepdims=True) + EPS) * g


def _chunk_pos(c, q, regions):
    (n0, l0), (n1, l1) = regions
    per0, per1 = l0 // q, l1 // q
    nc0 = n0 * per0
    c1 = jnp.maximum(c - nc0, 0)
    in0 = c < nc0
    seq = jnp.where(in0, c // per0, n0 + c1 // per1)
    pos = jnp.where(in0, c % per0, c1 % per1)
    last = jnp.where(in0, per0 - 1, per1 - 1)
    return seq, pos == 0, pos == last


def _mod_row(i, tm, regions):
    (n0, l0), (n1, l1) = regions
    t1 = jnp.maximum(i * tm - n0 * l0, 0)
    return jnp.where(i * tm < n0 * l0, 0, 1 + t1 // l1)


def _assemble_kernel(xp_ref, xs_ref, pe_ref, o_ref, *, n_ctx_tiles):
    i = pl.program_id(0)

    @pl.when(i < n_ctx_tiles)
    def _():
        o_ref[...] = xp_ref[...]

    @pl.when(i >= n_ctx_tiles)
    def _():
        o_ref[...] = xs_ref[...] + pe_ref[...]


def _assemble_tokens(xp, xs, pe, regions):
    (n0, l0), (n1, l1) = regions
    tm = min(1024, l0 * n0, l1)
    t0, t1 = n0 * l0, n1 * l1
    n_ctx = t0 // tm
    pe_tiles = l1 // tm
    return pl.pallas_call(
        functools.partial(_assemble_kernel, n_ctx_tiles=n_ctx),
        out_shape=jax.ShapeDtypeStruct((t0 + t1, D_MODEL), F32),
        grid=((t0 + t1) // tm,),
        in_specs=[
            pl.BlockSpec((tm, D_MODEL), lambda i: (jnp.minimum(i, n_ctx - 1), 0)),
            pl.BlockSpec((tm, D_MODEL), lambda i: (jnp.maximum(i - n_ctx, 0), 0)),
            pl.BlockSpec((tm, D_MODEL), lambda i: (jnp.maximum(i - n_ctx, 0) % pe_tiles, 0)),
        ],
        out_specs=pl.BlockSpec((tm, D_MODEL), lambda i: (i, 0)),
        compiler_params=_params("arbitrary"),
        name="assemble_tokens",
    )(xp.reshape(t0, D_MODEL), xs.reshape(t1, D_MODEL), pe)


def _mods_kernel(cc_ref, w_ref, b_ref, o_ref):
    a = _silu(cc_ref[...])
    o_ref[...] = _dot_hilo(a, w_ref[...]) + b_ref[...]


def _modulation_table(cc, ada_w, ada_b):
    depth = ada_w.shape[0]
    n = ada_w.shape[2]
    return pl.pallas_call(
        _mods_kernel,
        out_shape=jax.ShapeDtypeStruct((depth, MOD_ROWS, n), F32),
        grid=(depth, n // TN_MODS),
        in_specs=[
            pl.BlockSpec((MOD_ROWS, D_MODEL), lambda l, j: (0, 0)),
            pl.BlockSpec((None, D_MODEL, TN_MODS), lambda l, j: (l, 0, j)),
            pl.BlockSpec((None, 1, TN_MODS), lambda l, j: (l, 0, j)),
        ],
        out_specs=pl.BlockSpec((None, MOD_ROWS, TN_MODS), lambda l, j: (l, 0, j)),
        compiler_params=_params("arbitrary", "arbitrary"),
        name="modulation_table",
    )(cc, ada_w, ada_b.reshape(depth, 1, n))


def _inproj_kernel(x_ref, mod_ref, g_ref, w_ref, wc_ref, ws_ref, o_ref, oc_ref, os_ref, u_ref):
    def modnorm(x):
        y = _rms(x, g_ref[...])
        return y * (1.0 + mod_ref[:, D_MODEL:2 * D_MODEL]) + mod_ref[:, 0:D_MODEL]

    @pl.when(pl.program_id(1) == 0)
    def _():
        u_ref[...] = modnorm(x_ref[...]).astype(BF16)
        os_ref[...] = _dot(u_ref[...], ws_ref[...])

    u = u_ref[...]
    oc_ref[...] = _dot(u, wc_ref[...]).astype(BF16)
    o_ref[...] = _dot(u, w_ref[...]).astype(BF16)


def _in_projection(x, mods, g, lw, regions):
    t = x.shape[0]
    tm = min(TM_PROJ, regions[0][0] * regions[0][1], regions[1][1])
    n_col = D_PROJ // TN_PROJ
    tc = D_CONV // n_col
    return pl.pallas_call(
        _inproj_kernel,
        out_shape=(jax.ShapeDtypeStruct((t, D_PROJ), BF16), jax.ShapeDtypeStruct((t, D_CONV), BF16),
                   jax.ShapeDtypeStruct((t, LANES), F32)),
        grid=(t // tm, n_col),
        in_specs=[
            pl.BlockSpec((tm, D_MODEL), lambda i, j: (i, 0)),
            pl.BlockSpec((None, 1, 6 * D_MODEL), lambda i, j: (_mod_row(i, tm, regions), 0, 0)),
            pl.BlockSpec((1, D_MODEL), lambda i, j: (0, 0)),
            pl.BlockSpec((D_MODEL, TN_PROJ), lambda i, j: (0, j)),
            pl.BlockSpec((D_MODEL, tc), lambda i, j: (0, j)),
            pl.BlockSpec((D_MODEL, LANES), lambda i, j: (0, 0)),
        ],
        out_specs=(pl.BlockSpec((tm, TN_PROJ), lambda i, j: (i, j)),
                   pl.BlockSpec((tm, tc), lambda i, j: (i, j)),
                   pl.BlockSpec((tm, LANES), lambda i, j: (i, 0))),
        scratch_shapes=[pltpu.VMEM((tm, D_MODEL), BF16)],
        compiler_params=_params("parallel", "arbitrary"),
        name="norm_in_projection",
    )(x, mods, g, lw["w_in"], lw["w_in_conv"], lw["w_in_small"])


def _ssd_decay_terms(sm, dtb, alog, q):
    lane = lax.broadcasted_iota(jnp.int32, (1, LANES), 1)
    dtv = _softplus(sm + dtb)
    a_neg = jnp.where(lane < 2 * SSD_HEADS, -jnp.exp(alog), 0.0)
    a = dtv * a_neg
    acs = _dot_exact_lhs(_block_tri(q, q, False).astype(BF16), a)
    rev = _dot_exact_lhs(_block_tri(q, q, True).astype(BF16), a)
    m = jnp.where(lane < SSD_HEADS, acs, rev)
    tot = acs[q - 1:q, :]
    return m, tot, dtv


def _col(x, idx, width=LANES):
    return jnp.broadcast_to(x[:, idx:idx + 1], (x.shape[0], width))


def _head_expand_matrix(base):
    k = lax.broadcasted_iota(jnp.int32, (LANES, SSD_INNER), 0)
    c = lax.broadcasted_iota(jnp.int32, (LANES, SSD_INNER), 1)
    return jnp.where(k == base + c // SSD_HEADDIM, 1.0, 0.0).astype(BF16)


def _pair_cols(x, base, p, lo_half):
    return jnp.where(lo_half, _col(x, base + 2 * p), _col(x, base + 2 * p + 1))


def _expand_state(s):
    lane = lax.broadcasted_iota(jnp.int32, s.shape, 1)
    half = SSD_INNER // SSD_GROUPS
    return jnp.concatenate([jnp.where(lane < half, s, 0.0), jnp.where(lane >= half, s, 0.0)], axis=0)


def _compact_state(s2):
    return s2[0:SSD_STATE, :] + s2[SSD_STATE:2 * SSD_STATE, :]


def _state_update_mask():
    row = lax.broadcasted_iota(jnp.int32, (SSD_GROUPS * SSD_STATE, SSD_INNER), 0)
    lane = lax.broadcasted_iota(jnp.int32, (SSD_GROUPS * SSD_STATE, SSD_INNER), 1)
    return (row < SSD_STATE) == (lane < SSD_INNER // SSD_GROUPS)


HALO = BF16_SUBLANES
CONV_PAD = SSD_CONV // 2


def _ssd_bwd_kernel(xbc_ref, prev_ref, next_ref, sm_ref, cw_ref, cb_ref, dtb_ref, alog_ref, init_ref,
                    conv_ref, tab_ref, enter_ref, fin_ref, s_ref, ext_ref, *, regions, n_steps):
    q = SSD_CHUNK
    st = SCAN_STEP
    c = n_steps - 1 - pl.program_id(0)
    _, first, last = _chunk_pos(c, st, regions)

    @pl.when(last)
    def _():
        s_ref[...] = _expand_state(init_ref[...])

    ext_ref[0:HALO, :] = jnp.where(first, 0.0, prev_ref[...].astype(F32))
    ext_ref[HALO:HALO + st, :] = xbc_ref[...].astype(F32)
    ext_ref[HALO + st:2 * HALO + st, :] = jnp.where(last, 0.0, next_ref[...].astype(F32))
    acc = cb_ref[...]
    for k in range(SSD_CONV):
        off = HALO - CONV_PAD + k
        acc = acc + cw_ref[k:k + 1, :] * ext_ref[off:off + st, :]
    conv_ref[...] = _silu(acc).astype(BF16)

    lo_half = lax.broadcasted_iota(jnp.int32, (1, LANES), 1) < SSD_HEADDIM
    upd_mask = _state_update_mask()

    s_cur = s_ref[...]
    for ci in reversed(range(st // q)):
        rows = slice(ci * q, (ci + 1) * q)
        xs = conv_ref[rows, 0:SSD_INNER].astype(F32)
        bsb = conv_ref[rows, SSD_INNER:SSD_INNER + LANES]
        m, tot, dtv = _ssd_decay_terms(sm_ref[rows, :], dtb_ref[...], alog_ref[...], q)
        tab_ref[rows, 0:LANES] = m
        tab_ref[rows, LANES:2 * LANES] = dtv
        wgt = jnp.exp(tot - m) * dtv
        etot = jnp.exp(tot)
        enter_ref[ci] = _compact_state(s_cur)
        xw, dec = [], []
        for p in range(SSD_HEADS // 2):
            sl = slice(p * LANES, (p + 1) * LANES)
            xw.append((xs[:, sl] * _pair_cols(wgt, SSD_HEADS, p, lo_half)).astype(BF16))
            dec.append(_pair_cols(etot, SSD_HEADS, p, lo_half))
        upd = _dot_tn(bsb, jnp.concatenate(xw, axis=1))
        s_cur = s_cur * jnp.concatenate(dec, axis=1) + jnp.where(upd_mask, upd, 0.0)
    s_ref[...] = s_cur

    @pl.when(first)
    def _():
        fin_ref[...] = _compact_state(s_cur)


def _ssd_out_kernel(z_ref, x_ref, bc_ref, tab_ref, dexp_ref, ng_ref,
                    init_ref, enter_ref, y_ref, fin_ref, s_ref, *, regions):
    q = SSD_CHUNK
    c = pl.program_id(0)
    _, first, last = _chunk_pos(c, SCAN_STEP, regions)

    @pl.when(first)
    def _():
        s_ref[...] = _expand_state(init_ref[...])

    lane = lax.broadcasted_iota(jnp.int32, (1, LANES), 1)
    lo_half = lane < SSD_HEADDIM
    tril = _block_tri(q, q, False)
    ii = lax.broadcasted_iota(jnp.int32, (q, q), 0)
    jj = lax.broadcasted_iota(jnp.int32, (q, q), 1)
    eye = ii == jj
    upd_mask = _state_update_mask()
    expand = _head_expand_matrix(0)
    dexp = dexp_ref[...]
    ng = ng_ref[...]

    s_cur = s_ref[...]
    for ci in range(SCAN_STEP // q):
        rows = slice(ci * q, (ci + 1) * q)
        xs = x_ref[rows, :].astype(F32)
        bsb = bc_ref[rows, 0:LANES]
        csb = bc_ref[rows, LANES:2 * LANES]
        m = tab_ref[rows, 0:LANES]
        dtv = tab_ref[rows, LANES:2 * LANES]
        wgt = jnp.exp(jnp.where(lane < SSD_HEADS, m[q - 1:q, :] - m, 0.0)) * dtv
        mt = (m - jnp.log(dtv)).T
        ldt = jnp.log(dtv + pltpu.roll(dtv, LANES - SSD_HEADS, 1)).T
        zero_b = jnp.zeros_like(csb)
        cb = [_dot_nt(jnp.where(lo_half, csb, zero_b), bsb).astype(BF16),
              _dot_nt(jnp.where(lo_half, zero_b, csb), bsb).astype(BF16)]
        cs_f = _dot(csb, s_cur.astype(BF16))
        cs_b = _dot(csb, _expand_state(enter_ref[ci]).astype(BF16))

        ys, dec = [], []
        for p in range(SSD_HEADS // 2):
            sl = slice(p * LANES, (p + 1) * LANES)
            g = (2 * p) // (SSD_HEADS // SSD_GROUPS)
            xs_p = xs[:, sl]
            xs_pb = xs_p.astype(BF16)
            yd, colf, colb = [], [], []
            for h in (2 * p, 2 * p + 1):
                hb = SSD_HEADS + h
                cf = _col(m, h, q)
                cbk = _col(m, hb, q)
                seg = jnp.where(tril, cf - mt[h:h + 1, :], cbk - mt[hb:hb + 1, :])
                seg = jnp.where(eye, ldt[h:h + 1, :], seg)
                yd.append(_dot(cb[g] * jnp.exp(seg).astype(BF16), xs_pb))
                colf.append(cf)
                colb.append(cbk)
            ef = jnp.exp(jnp.where(lo_half, colf[0], colf[1]))
            eb = jnp.exp(jnp.where(lo_half, colb[0], colb[1]))
            y_p = jnp.where(lo_half, yd[0], yd[1]) + ef * cs_f[:, sl] + eb * cs_b[:, sl] + dexp[:, sl] * xs_p
            ys.append(y_p)
            dec.append(ef[q - 1:q, :])

        y = jnp.concatenate(ys, axis=1) * _silu(z_ref[rows, :].astype(F32))
        y_ref[rows, :] = _rms(y, ng).astype(y_ref.dtype)

        xw = (xs * _dot(wgt.astype(BF16), expand)).astype(BF16)
        upd = _dot_tn(bsb, xw)
        s_cur = s_cur * jnp.concatenate(dec, axis=1) + jnp.where(upd_mask, upd, 0.0)
    s_ref[...] = s_cur

    @pl.when(last)
    def _():
        fin_ref[...] = _compact_state(s_cur)


def _const_spec(a):
    return pl.BlockSpec(a.shape, lambda s: (0,) * a.ndim)


def _sweep_call(parts, n_steps, name):
    n_in = [len(p["inputs"]) for p in parts]
    n_out = [len(p["out_shape"]) for p in parts]
    n_scr = [len(p["scratch"]) for p in parts]

    def kernel(*refs):
        ins, outs, scr = refs[:sum(n_in)], refs[sum(n_in):sum(n_in) + sum(n_out)], refs[sum(n_in) + sum(n_out):]
        for k, p in enumerate(parts):
            i0, o0, s0 = sum(n_in[:k]), sum(n_out[:k]), sum(n_scr[:k])
            p["kernel"](*ins[i0:i0 + n_in[k]], *outs[o0:o0 + n_out[k]], *scr[s0:s0 + n_scr[k]])

    res = pl.pallas_call(
        kernel,
        out_shape=tuple(o for p in parts for o in p["out_shape"]),
        grid=(n_steps,),
        in_specs=[s for p in parts for s in p["in_specs"]],
        out_specs=tuple(s for p in parts for s in p["out_specs"]),
        scratch_shapes=[s for p in parts for s in p["scratch"]],
        compiler_params=_params("arbitrary"),
        name=name,
    )(*[a for p in parts for a in p["inputs"]])
    return [res[sum(n_out[:k]):sum(n_out[:k]) + n_out[k]] for k in range(len(parts))]


def _ssd_bwd_part(xbc, small, init, lw, regions):
    st = SCAN_STEP
    cps = st // SSD_CHUNK
    t = xbc.shape[0]
    n_steps = t // st
    n_seq = regions[0][0] + regions[1][0]
    bidx = lambda s: n_steps - 1 - s
    seq_b = lambda s: _chunk_pos(bidx(s), st, regions)[0]
    hb = st // HALO
    n_hb = t // HALO
    consts = [lw["conv_w"], lw["conv_b"], lw["dt_bias"], lw["a_log"]]
    return dict(
        kernel=functools.partial(_ssd_bwd_kernel, regions=regions, n_steps=n_steps),
        inputs=[xbc, xbc, xbc, small, *consts, init],
        in_specs=[pl.BlockSpec((st, D_CONV), lambda s: (bidx(s), 0)),
                  pl.BlockSpec((HALO, D_CONV), lambda s: (jnp.maximum(bidx(s) * hb - 1, 0), 0)),
                  pl.BlockSpec((HALO, D_CONV), lambda s: (jnp.minimum((bidx(s) + 1) * hb, n_hb - 1), 0)),
                  pl.BlockSpec((st, LANES), lambda s: (bidx(s), 0))]
        + [_const_spec(a) for a in consts]
        + [pl.BlockSpec((None, None, SSD_STATE, SSD_INNER), lambda s: (seq_b(s), 1, 0, 0))],
        out_shape=[jax.ShapeDtypeStruct((t, D_CONV), BF16),
                   jax.ShapeDtypeStruct((t, 2 * LANES), F32),
                   jax.ShapeDtypeStruct((n_steps * cps, SSD_STATE, SSD_INNER), F32),
                   jax.ShapeDtypeStruct((n_seq, SSD_STATE, SSD_INNER), F32)],
        out_specs=[pl.BlockSpec((st, D_CONV), lambda s: (bidx(s), 0)),
                   pl.BlockSpec((st, 2 * LANES), lambda s: (bidx(s), 0)),
                   pl.BlockSpec((cps, SSD_STATE, SSD_INNER), lambda s: (bidx(s), 0, 0)),
                   pl.BlockSpec((None, SSD_STATE, SSD_INNER), lambda s: (seq_b(s), 0, 0))],
        scratch=[pltpu.VMEM((SSD_GROUPS * SSD_STATE, SSD_INNER), F32),
                 pltpu.VMEM((st + 2 * HALO, D_CONV), F32)])


def _ssd_fwd_part(proj, xbc, tables, init, enter_b, lw, regions):
    st = SCAN_STEP
    cps = st // SSD_CHUNK
    t = proj.shape[0]
    n_seq = regions[0][0] + regions[1][0]
    seq_f = lambda s: _chunk_pos(s, st, regions)[0]
    consts = [lw["d_exp"], lw["ssd_norm_g"]]
    return dict(
        kernel=functools.partial(_ssd_out_kernel, regions=regions),
        inputs=[proj, xbc, xbc, tables, *consts, init, enter_b],
        in_specs=[pl.BlockSpec((st, SSD_INNER), lambda s: (s, COL_Z // SSD_INNER)),
                  pl.BlockSpec((st, SSD_INNER), lambda s: (s, CONV_X // SSD_INNER)),
                  pl.BlockSpec((st, 2 * LANES), lambda s: (s, CONV_BC // (2 * LANES))),
                  pl.BlockSpec((st, 2 * LANES), lambda s: (s, 0))]
        + [_const_spec(a) for a in consts]
        + [pl.BlockSpec((None, None, SSD_STATE, SSD_INNER), lambda s: (seq_f(s), 0, 0, 0)),
           pl.BlockSpec((cps, SSD_STATE, SSD_INNER), lambda s: (s, 0, 0))],
        out_shape=[jax.ShapeDtypeStruct((t, SSD_INNER), BF16),
                   jax.ShapeDtypeStruct((n_seq, SSD_STATE, SSD_INNER), F32)],
        out_specs=[pl.BlockSpec((st, SSD_INNER), lambda s: (s, 0)),
                   pl.BlockSpec((None, SSD_STATE, SSD_INNER), lambda s: (seq_f(s), 0, 0))],
        scratch=[pltpu.VMEM((SSD_GROUPS * SSD_STATE, SSD_INNER), F32)])


def _gla_log_decay(sm, gw_hi_ref, gw_lo_ref, gb_ref):
    s1, s2 = _split2(sm)
    logit = _dot(s1, gw_hi_ref[...]) + _dot(s2, gw_hi_ref[...]) + _dot(s1, gw_lo_ref[...]) + gb_ref[...]
    return -_softplus(-logit) * (1.0 / GLA_GATE_TAU)


def _chunk_rows(x, n_chunks, rows_per_chunk):
    return jnp.concatenate([jnp.broadcast_to(x[c:c + 1, :], (rows_per_chunk, x.shape[1]))
                            for c in range(n_chunks)], axis=0)


def _gla_bwd_kernel(qk_ref, v_ref, sm_ref, gwh_ref, gwl_ref, gb_ref, init_ref,
                    rev_ref, enter_ref, fin_ref, s_ref, *, regions, n_steps):
    cq = GLA_CHUNK
    st = SCAN_STEP
    ncs = st // cq
    c = n_steps - 1 - pl.program_id(0)
    _, first, last = _chunk_pos(c, st, regions)

    @pl.when(last)
    def _():
        s_ref[...] = init_ref[...]

    bwd = slice(GLA_KDIM, 2 * GLA_KDIM)
    lg = _gla_log_decay(sm_ref[...], gwh_ref.at[:, bwd], gwl_ref.at[:, bwd], gb_ref.at[:, bwd])
    rev = _dot_exact_lhs(_block_tri(st, cq, True).astype(BF16), lg)
    rev_ref[...] = rev
    tot = jnp.concatenate([rev[ci * cq:ci * cq + 1, :] for ci in range(ncs)], axis=0)
    ko = (qk_ref[:, GLA_KDIM:2 * GLA_KDIM].astype(F32) * jnp.exp(_chunk_rows(tot, ncs, cq) - rev)).astype(BF16)
    etot = jnp.exp(tot)
    lo_half = lax.broadcasted_iota(jnp.int32, (1, LANES), 1) < GLA_DK

    s_cur = s_ref[...]
    for ci in reversed(range(ncs)):
        rows = slice(ci * cq, (ci + 1) * cq)
        enter_ref[ci] = s_cur
        new = []
        for p in range(GLA_HEADS // 2):
            sl = slice(p * LANES, (p + 1) * LANES)
            u0 = _dot_tn(v_ref[rows, (2 * p) * GLA_DV:(2 * p + 1) * GLA_DV], ko[rows, sl])
            u1 = _dot_tn(v_ref[rows, (2 * p + 1) * GLA_DV:(2 * p + 2) * GLA_DV], ko[rows, sl])
            new.append(s_cur[:, sl] * etot[ci:ci + 1, sl] + jnp.where(lo_half, u0, u1))
        s_cur = jnp.concatenate(new, axis=1)
    s_ref[...] = s_cur

    @pl.when(first)
    def _():
        fin_ref[...] = s_cur


def _gla_out_kernel(qk_ref, v_ref, r_ref, sm_ref, gwh_ref, gwl_ref, gb_ref, ng_ref, init_ref, enter_ref, rev_ref,
                    y_ref, fin_ref, s_ref, *, regions):
    cq = GLA_CHUNK
    st = SCAN_STEP
    ncs = st // cq
    c = pl.program_id(0)
    _, first, last = _chunk_pos(c, st, regions)

    @pl.when(first)
    def _():
        s_ref[...] = init_ref[...]

    fwd = slice(0, GLA_KDIM)
    lg = _gla_log_decay(sm_ref[...], gwh_ref.at[:, fwd], gwl_ref.at[:, fwd], gb_ref.at[:, fwd])
    gcs = _dot_exact_lhs(_block_tri(st, cq, False).astype(BF16), lg)
    rev = rev_ref[...]
    lo_half = lax.broadcasted_iota(jnp.int32, (1, LANES), 1) < GLA_DK
    zero_b = jnp.zeros((cq, LANES), BF16)
    ii = lax.broadcasted_iota(jnp.int32, (2 * cq, cq), 0) % cq
    jj = lax.broadcasted_iota(jnp.int32, (2 * cq, cq), 1)
    tril2 = jj <= ii
    triu2 = jj >= ii
    ng = ng_ref[...]

    def stack_heads(x):
        return jnp.concatenate([jnp.where(lo_half, x, zero_b), jnp.where(lo_half, zero_b, x)], axis=0)

    s_cur = s_ref[...]
    for ci in range(ncs):
        rows = slice(ci * cq, (ci + 1) * cq)
        g_f = gcs[rows]
        g_b = rev[rows]
        etot = jnp.exp(g_f[cq - 1:cq, :])
        qs = qk_ref[rows, 0:GLA_KDIM].astype(F32) * (GLA_DK ** -0.5)
        ks = qk_ref[rows, GLA_KDIM:2 * GLA_KDIM].astype(F32)
        qe_f = (qs * jnp.exp(g_f)).astype(BF16)
        ke = ks * jnp.exp(-g_f)
        ke_f = ke.astype(BF16)
        ko_f = (ke * etot).astype(BF16)
        qe_b = (qs * jnp.exp(g_b)).astype(BF16)
        ke_b = (ks * jnp.exp(-g_b)).astype(BF16)
        s_curb = s_cur.astype(BF16)
        s_entb = enter_ref[ci].astype(BF16)
        new = []
        for p in range(GLA_HEADS // 2):
            sl = slice(p * LANES, (p + 1) * LANES)
            qf2 = stack_heads(qe_f[:, sl])
            qb2 = stack_heads(qe_b[:, sl])
            att = (jnp.where(tril2, _dot_nt(qf2, ke_f[:, sl]), 0.0)
                   + jnp.where(triu2, _dot_nt(qb2, ke_b[:, sl]), 0.0)).astype(BF16)
            inter = _dot_nt(qf2, s_curb[:, sl]) + _dot_nt(qb2, s_entb[:, sl])
            upd = []
            for hh in (0, 1):
                h = 2 * p + hh
                hs = slice(h * GLA_DV, (h + 1) * GLA_DV)
                hr = slice(hh * cq, (hh + 1) * cq)
                vh = v_ref[rows, hs]
                o = _dot(att[hr], vh) + inter[hr]
                o = o * lax.rsqrt(jnp.mean(o * o, axis=-1, keepdims=True) + EPS) * ng[:, hs]
                y_ref[rows, hs] = (o * _silu(r_ref[rows, hs].astype(F32))).astype(y_ref.dtype)
                upd.append(_dot_tn(vh, ko_f[:, sl]))
            new.append(s_cur[:, sl] * etot[:, sl] + jnp.where(lo_half, upd[0], upd[1]))
        s_cur = jnp.concatenate(new, axis=1)
    s_ref[...] = s_cur

    @pl.when(last)
    def _():
        fin_ref[...] = s_cur


def _gla_bwd_part(proj, small, init, lw, regions):
    st = SCAN_STEP
    ncs = st // GLA_CHUNK
    t = proj.shape[0]
    n_steps = t // st
    n_seq = regions[0][0] + regions[1][0]
    consts = [lw["gate_w_hi"], lw["gate_w_lo"], lw["gate_b"]]
    bidx = lambda s: n_steps - 1 - s
    seq_b = lambda s: _chunk_pos(bidx(s), st, regions)[0]
    return dict(
        kernel=functools.partial(_gla_bwd_kernel, regions=regions, n_steps=n_steps),
        inputs=[proj, proj, small, *consts, init],
        in_specs=[pl.BlockSpec((st, 2 * GLA_KDIM), lambda s: (bidx(s), COL_QK // (2 * GLA_KDIM))),
                  pl.BlockSpec((st, GLA_VDIM), lambda s: (bidx(s), COL_V // GLA_VDIM)),
                  pl.BlockSpec((st, LANES), lambda s: (bidx(s), 0))]
        + [_const_spec(a) for a in consts]
        + [pl.BlockSpec((None, None, GLA_DV, GLA_KDIM), lambda s: (seq_b(s), 1, 0, 0))],
        out_shape=[jax.ShapeDtypeStruct((t, GLA_KDIM), F32),
                   jax.ShapeDtypeStruct((n_steps * ncs, GLA_DV, GLA_KDIM), F32),
                   jax.ShapeDtypeStruct((n_seq, GLA_DV, GLA_KDIM), F32)],
        out_specs=[pl.BlockSpec((st, GLA_KDIM), lambda s: (bidx(s), 0)),
                   pl.BlockSpec((ncs, GLA_DV, GLA_KDIM), lambda s: (bidx(s), 0, 0)),
                   pl.BlockSpec((None, GLA_DV, GLA_KDIM), lambda s: (seq_b(s), 0, 0))],
        scratch=[pltpu.VMEM((GLA_DV, GLA_KDIM), F32)])


def _gla_fwd_part(proj, small, init, enter_b, rev_b, lw, regions):
    st = SCAN_STEP
    ncs = st // GLA_CHUNK
    t = proj.shape[0]
    n_seq = regions[0][0] + regions[1][0]
    consts = [lw["gate_w_hi"], lw["gate_w_lo"], lw["gate_b"], lw["gla_norm_g"]]
    seq_f = lambda s: _chunk_pos(s, st, regions)[0]
    return dict(
        kernel=functools.partial(_gla_out_kernel, regions=regions),
        inputs=[proj, proj, proj, small, *consts, init, enter_b, rev_b],
        in_specs=[pl.BlockSpec((st, 2 * GLA_KDIM), lambda s: (s, COL_QK // (2 * GLA_KDIM))),
                  pl.BlockSpec((st, GLA_VDIM), lambda s: (s, COL_V // GLA_VDIM)),
                  pl.BlockSpec((st, GLA_VDIM), lambda s: (s, COL_R // GLA_VDIM)),
                  pl.BlockSpec((st, LANES), lambda s: (s, 0))]
        + [_const_spec(a) for a in consts]
        + [pl.BlockSpec((None, None, GLA_DV, GLA_KDIM), lambda s: (seq_f(s), 0, 0, 0)),
           pl.BlockSpec((ncs, GLA_DV, GLA_KDIM), lambda s: (s, 0, 0)),
           pl.BlockSpec((st, GLA_KDIM), lambda s: (s, 0))],
        out_shape=[jax.ShapeDtypeStruct((t, GLA_VDIM), BF16),
                   jax.ShapeDtypeStruct((n_seq, GLA_DV, GLA_KDIM), F32)],
        out_specs=[pl.BlockSpec((st, GLA_VDIM), lambda s: (s, 0)),
                   pl.BlockSpec((None, GLA_DV, GLA_KDIM), lambda s: (seq_f(s), 0, 0))],
        scratch=[pltpu.VMEM((GLA_DV, GLA_KDIM), F32)])


def _mixers(proj, xbc, small, init_ssd, init_gla, lw, regions):
    n_steps = proj.shape[0] // SCAN_STEP
    (xbc_c, tab_s, enter_s, fin_sb), (rev_g, enter_g, fin_gb) = _sweep_call(
        [_ssd_bwd_part(xbc, small, init_ssd, lw, regions), _gla_bwd_part(proj, small, init_gla, lw, regions)],
        n_steps, "scan_backward_states")
    (y_ssd, fin_sf), (y_gla, fin_gf) = _sweep_call(
        [_ssd_fwd_part(proj, xbc_c, tab_s, init_ssd, enter_s, lw, regions),
         _gla_fwd_part(proj, small, init_gla, enter_g, rev_g, lw, regions)],
        n_steps, "scan_forward_outputs")
    return y_ssd, fin_sf, fin_sb, y_gla, fin_gf, fin_gb


def _merge_kernel(x_ref, ys_ref, yg_ref, ga_ref, gb_ref, mod_ref, wbs_ref, wbg_ref, wo_ref, o_ref):
    merged = (_sigmoid(ga_ref[...].astype(F32)) * _dot(ys_ref[...], wbs_ref[...])
              + _sigmoid(gb_ref[...].astype(F32)) * _dot(yg_ref[...], wbg_ref[...]))
    mix = _dot(merged.astype(BF16), wo_ref[...])
    o_ref[...] = x_ref[...] + mod_ref[:, 2 * D_MODEL:3 * D_MODEL] * mix


def _merge(x, y_ssd, y_gla, proj, mods, lw, regions):
    t = x.shape[0]
    tm = min(TM_MERGE, regions[0][0] * regions[0][1], regions[1][1])
    tok = lambda col: pl.BlockSpec((tm, D_MODEL), lambda i: (i, col))
    wspec = pl.BlockSpec((D_MODEL, D_MODEL), lambda i: (0, 0))
    return pl.pallas_call(
        _merge_kernel,
        out_shape=jax.ShapeDtypeStruct((t, D_MODEL), F32),
        grid=(t // tm,),
        in_specs=[tok(0), tok(0), tok(0), tok(COL_GA // D_MODEL), tok(COL_GB // D_MODEL),
                  pl.BlockSpec((None, 1, 6 * D_MODEL), lambda i: (_mod_row(i, tm, regions), 0, 0)),
                  wspec, wspec, wspec],
        out_specs=tok(0),
        compiler_params=_params("parallel"),
        name="merge_out_projection",
    )(x, y_ssd, y_gla, proj, proj, mods, lw["w_bs"], lw["w_bg"], lw["w_o"])


def _ffn_prologue(x_ref, mod_ref, g_ref):
    y = _rms(x_ref[...], g_ref[...])
    return y * (1.0 + mod_ref[:, 4 * D_MODEL:5 * D_MODEL]) + mod_ref[:, 3 * D_MODEL:4 * D_MODEL]


def _swiglu_resident(v, w1_ref, w3_ref, w2_ref):
    acc = None
    for hh in range(w1_ref.shape[1] // TH_FFN):
        cols = slice(hh * TH_FFN, (hh + 1) * TH_FFN)
        h = _silu(_dot(v, w1_ref[:, cols])) * _dot(v, w3_ref[:, cols])
        part = _dot(h.astype(BF16), w2_ref[cols, :])
        acc = part if acc is None else acc + part
    return acc


def _ffn_kernel(x_ref, mod_ref, g_ref, w1_ref, w3_ref, w2_ref, o_ref):
    v = _ffn_prologue(x_ref, mod_ref, g_ref).astype(BF16)
    o_ref[...] = x_ref[...] + mod_ref[:, 5 * D_MODEL:6 * D_MODEL] * _swiglu_resident(v, w1_ref, w3_ref, w2_ref)


def _dense_ffn(x, mods, g, w1, w3, w2, regions):
    t = x.shape[0]
    tm = min(TM_FFN, regions[0][0] * regions[0][1], regions[1][1])
    hid = w1.shape[1]
    once = pl.Buffered(1)
    return pl.pallas_call(
        _ffn_kernel,
        out_shape=jax.ShapeDtypeStruct((t, D_MODEL), F32),
        grid=(t // tm,),
        in_specs=[pl.BlockSpec((tm, D_MODEL), lambda i: (i, 0)),
                  pl.BlockSpec((None, 1, 6 * D_MODEL), lambda i: (_mod_row(i, tm, regions), 0, 0)),
                  pl.BlockSpec((1, D_MODEL), lambda i: (0, 0)),
                  pl.BlockSpec((D_MODEL, hid), lambda i: (0, 0), pipeline_mode=once),
                  pl.BlockSpec((D_MODEL, hid), lambda i: (0, 0), pipeline_mode=once),
                  pl.BlockSpec((hid, D_MODEL), lambda i: (0, 0), pipeline_mode=once)],
        out_specs=pl.BlockSpec((tm, D_MODEL), lambda i: (i, 0)),
        compiler_params=_params("parallel"),
        name="dense_swiglu",
    )(x, mods, g, w1, w3, w2)


def _top2_gates(logits):
    lane = lax.broadcasted_iota(jnp.int32, logits.shape, 1)
    lg = jnp.where(lane < N_EXPERTS, logits, -jnp.inf)
    m1 = jnp.max(lg, axis=-1, keepdims=True)
    i1 = jnp.min(jnp.where(lg == m1, lane, LANES), axis=-1, keepdims=True)
    lg2 = jnp.where(lane == i1, -jnp.inf, lg)
    m2 = jnp.max(lg2, axis=-1, keepdims=True)
    i2 = jnp.min(jnp.where(lg2 == m2, lane, LANES), axis=-1, keepdims=True)
    e2 = jnp.exp(m2 - m1)
    den = 1.0 + e2
    return jnp.where(lane == i1, 1.0 / den, 0.0) + jnp.where(lane == i2, e2 / den, 0.0)


def _lane_scalar(vec, e):
    lane = lax.broadcasted_iota(jnp.int32, vec.shape, 1)
    return jnp.sum(jnp.where(lane == e, vec, 0.0)).astype(jnp.int32)


def _n_row_blocks(n):
    return (n + MOE_ROW_BLOCK - 1) // MOE_ROW_BLOCK


def _moe_block_copy(stage_ref, sem, slot, hbm_ref, row, to_hbm):
    if not isinstance(row, int):
        row = pl.multiple_of(row, MOE_ROW_ALIGN)
    rows = hbm_ref.at[pl.ds(row, MOE_ROW_BLOCK), :]
    if to_hbm:
        return pltpu.make_async_copy(stage_ref.at[slot], rows, sem.at[slot])
    return pltpu.make_async_copy(rows, stage_ref.at[slot], sem.at[slot])


def _moe_dispatch_kernel(x_ref, mod_ref, g_ref, rw_ref, rb_ref, donor_hbm,
                         xs_hbm, gate_ref, rank_ref, off_ref, tot_ref,
                         v_ref, gate_t_ref, rank_t_ref, stage_ref, sem, run_ref, pend_ref, *, cap):
    del donor_hbm
    i = pl.program_id(0)
    tm = x_ref.shape[0]
    rb = MOE_ROW_BLOCK

    @pl.when(i == 0)
    def _():
        run_ref[...] = jnp.zeros_like(run_ref)
        pend_ref[0] = 0

    v = _ffn_prologue(x_ref, mod_ref, g_ref)
    v_ref[...] = v.astype(BF16)
    gate = _top2_gates(_dot_hilo(v, rw_ref[...]) + rb_ref[...])
    sel_f = jnp.where(gate > 0.0, 1.0, 0.0)
    sel = sel_f.astype(BF16)
    gate_ref[...] = gate
    gate_t_ref[...] = gate.T
    for blk in range(tm // rb):
        ii = lax.broadcasted_iota(jnp.int32, (rb, tm), 0) + blk * rb
        jj = lax.broadcasted_iota(jnp.int32, (rb, tm), 1)
        rank = _dot(jnp.where(jj < ii, 1.0, 0.0).astype(BF16), sel)
        rank_ref[blk * rb:(blk + 1) * rb, :] = rank
        rank_t_ref[:, blk * rb:(blk + 1) * rb] = rank.T

    cnt = jnp.sum(sel_f, axis=0, keepdims=True)
    cnt_pad = jnp.floor((cnt + (MOE_ROW_ALIGN - 1)) * (1.0 / MOE_ROW_ALIGN)) * MOE_ROW_ALIGN
    run = run_ref[...]
    off_ref[...] = run

    def wait_prev(s, carry):
        _moe_block_copy(stage_ref, sem, s, xs_hbm, 0, True).wait()
        return carry
    lax.fori_loop(0, pend_ref[0], wait_prev, 0)

    slot = jnp.int32(0)
    for e in range(N_EXPERTS):
        n_e = _lane_scalar(cnt, e)
        base = e * cap + _lane_scalar(run, e)
        g_row = gate_t_ref[e:e + 1, :]
        r_row = rank_t_ref[e:e + 1, :].astype(jnp.int32)

        def pack(b, s, base=base, g_row=g_row, r_row=r_row):
            row = lax.broadcasted_iota(jnp.int32, (rb, tm), 0) + b * rb
            onehot = jnp.where(jnp.logical_and(row == r_row, g_row > 0.0), 1.0, 0.0).astype(BF16)
            stage_ref[s] = _dot(onehot, v_ref[...]).astype(BF16)
            _moe_block_copy(stage_ref, sem, s, xs_hbm, base + b * rb, True).start()
            return s + 1
        slot = lax.fori_loop(0, _n_row_blocks(n_e), pack, slot)

    pend_ref[0] = slot
    run_ref[...] = run + cnt_pad
    tot_ref[...] = run + cnt_pad

    @pl.when(i == pl.num_programs(0) - 1)
    def _():
        def wait_last(s, carry):
            _moe_block_copy(stage_ref, sem, s, xs_hbm, 0, True).wait()
            return carry
        lax.fori_loop(0, slot, wait_last, 0)


def _moe_expert_kernel(te_ref, tr_ref, na_ref, xs_ref, w1_ref, w3_ref, w2_ref, ys_ref):
    del te_ref, tr_ref

    @pl.when(pl.program_id(0) < na_ref[0])
    def _():
        ys_ref[...] = _swiglu_resident(xs_ref[...], w1_ref, w3_ref, w2_ref).astype(ys_ref.dtype)


def _moe_combine_kernel(off_ref, x_ref, mod_ref, gate_ref, rank_ref, ys_hbm, o_ref, stage_ref, sem, *, cap):
    i = pl.program_id(0)
    tm = x_ref.shape[0]
    rb = MOE_ROW_BLOCK
    gate = gate_ref[...]
    rank = rank_ref[...]
    cnt = jnp.sum(jnp.where(gate > 0.0, 1.0, 0.0), axis=0, keepdims=True)
    n_blocks = [_n_row_blocks(_lane_scalar(cnt, e)) for e in range(N_EXPERTS)]
    bases = [e * cap + off_ref[i * LANES + e] for e in range(N_EXPERTS)]

    slot = jnp.int32(0)
    for e in range(N_EXPERTS):
        def fetch(b, s, base=bases[e]):
            _moe_block_copy(stage_ref, sem, s, ys_hbm, base + b * rb, False).start()
            return s + 1
        slot = lax.fori_loop(0, n_blocks[e], fetch, slot)

    o_ref[...] = jnp.zeros_like(o_ref)
    slot = jnp.int32(0)
    for e in range(N_EXPERTS):
        g_col = gate[:, e:e + 1]
        r_col = rank[:, e:e + 1].astype(jnp.int32)

        def combine(b, s, g_col=g_col, r_col=r_col):
            _moe_block_copy(stage_ref, sem, s, ys_hbm, 0, False).wait()
            col = lax.broadcasted_iota(jnp.int32, (tm, rb), 1) + b * rb
            onehot = jnp.where(jnp.logical_and(col == r_col, g_col > 0.0), 1.0, 0.0).astype(BF16)
            o_ref[...] += g_col * _dot(onehot, stage_ref[s])
            return s + 1
        slot = lax.fori_loop(0, n_blocks[e], combine, slot)

    o_ref[...] = x_ref[...] + mod_ref[:, 5 * D_MODEL:6 * D_MODEL] * o_ref[...]


def _moe_tile_table(totals, cap, n_steps):
    extent = totals
    n_tiles = (extent + TR_EXPERT - 1) // TR_EXPERT
    ends = jnp.cumsum(n_tiles)
    starts = ends - n_tiles
    n_active = ends[-1]
    t = jnp.minimum(jnp.arange(n_steps, dtype=jnp.int32), n_active - 1)
    e = jnp.sum((t[:, None] >= ends[None, :]).astype(jnp.int32), axis=1)
    row_block = e * (cap // TR_EXPERT) + (t - starts[e])
    return e.astype(jnp.int32), row_block.astype(jnp.int32), n_active.reshape(1).astype(jnp.int32)


def _moe_geometry(t, regions, n_e):
    tm = min(TM_MOE, regions[0][0] * regions[0][1], regions[1][1])
    n_tiles = t // tm
    cap_rows = t + n_tiles * (MOE_ROW_ALIGN - 1) + MOE_ROW_BLOCK
    cap = -(-cap_rows // TR_EXPERT) * TR_EXPERT
    return tm, n_tiles, cap


def _moe_ffn(x, mods, g, rw, rb, w1, w3, w2, regions, donor):
    t = x.shape[0]
    n_e, _, hid = w1.shape
    tm, n_tiles, cap = _moe_geometry(t, regions, n_e)
    max_slots = 2 * tm // MOE_ROW_BLOCK + n_e
    n_steps = (2 * t + n_tiles * n_e * (MOE_ROW_ALIGN - 1) + n_e * MOE_ROW_BLOCK) // TR_EXPERT + n_e
    mod_spec = pl.BlockSpec((None, 1, 6 * D_MODEL), lambda i, *_: (_mod_row(i, tm, regions), 0, 0))
    stage = [pltpu.VMEM((max_slots, MOE_ROW_BLOCK, D_MODEL), BF16), pltpu.SemaphoreType.DMA((max_slots,))]

    xs, gate, rank, offs, totals = pl.pallas_call(
        functools.partial(_moe_dispatch_kernel, cap=cap),
        out_shape=(jax.ShapeDtypeStruct((n_e * cap, D_MODEL), BF16),
                   jax.ShapeDtypeStruct((t, LANES), F32), jax.ShapeDtypeStruct((t, LANES), F32),
                   jax.ShapeDtypeStruct((n_tiles, 1, LANES), F32), jax.ShapeDtypeStruct((1, LANES), F32)),
        grid=(n_tiles,),
        in_specs=[pl.BlockSpec((tm, D_MODEL), lambda i: (i, 0)), mod_spec,
                  pl.BlockSpec((1, D_MODEL), lambda i: (0, 0)),
                  pl.BlockSpec((D_MODEL, LANES), lambda i: (0, 0)),
                  pl.BlockSpec((1, LANES), lambda i: (0, 0)),
                  pl.BlockSpec(memory_space=pl.ANY)],
        input_output_aliases={5: 0},
        out_specs=(pl.BlockSpec(memory_space=pl.ANY),
                   pl.BlockSpec((tm, LANES), lambda i: (i, 0)), pl.BlockSpec((tm, LANES), lambda i: (i, 0)),
                   pl.BlockSpec((None, 1, LANES), lambda i: (i, 0, 0)),
                   pl.BlockSpec((1, LANES), lambda i: (0, 0))),
        scratch_shapes=[pltpu.VMEM((tm, D_MODEL), BF16),
                        pltpu.VMEM((LANES, tm), F32), pltpu.VMEM((LANES, tm), F32)] + stage
        + [pltpu.VMEM((1, LANES), F32), pltpu.SMEM((1,), jnp.int32)],
        compiler_params=_params("arbitrary"),
        name="moe_dispatch",
    )(x, mods, g, rw, rb, donor)

    tile_e, tile_rb, n_active = _moe_tile_table(totals[0, :n_e].astype(jnp.int32), cap, n_steps)
    ys = pl.pallas_call(
        _moe_expert_kernel,
        out_shape=jax.ShapeDtypeStruct((n_e * cap, D_MODEL), BF16),
        grid_spec=pltpu.PrefetchScalarGridSpec(
            num_scalar_prefetch=3,
            grid=(n_steps,),
            in_specs=[pl.BlockSpec((TR_EXPERT, D_MODEL), lambda s, te, tr, na: (tr[s], 0)),
                      pl.BlockSpec((None, D_MODEL, hid), lambda s, te, tr, na: (te[s], 0, 0)),
                      pl.BlockSpec((None, D_MODEL, hid), lambda s, te, tr, na: (te[s], 0, 0)),
                      pl.BlockSpec((None, hid, D_MODEL), lambda s, te, tr, na: (te[s], 0, 0))],
            out_specs=pl.BlockSpec((TR_EXPERT, D_MODEL), lambda s, te, tr, na: (tr[s], 0))),
        input_output_aliases={3: 0},
        compiler_params=_params("arbitrary"),
        name="moe_experts",
    )(tile_e, tile_rb, n_active, xs, w1, w3, w2)

    out = pl.pallas_call(
        functools.partial(_moe_combine_kernel, cap=cap),
        out_shape=jax.ShapeDtypeStruct((t, D_MODEL), F32),
        grid_spec=pltpu.PrefetchScalarGridSpec(
            num_scalar_prefetch=1,
            grid=(n_tiles,),
            in_specs=[pl.BlockSpec((tm, D_MODEL), lambda i, off: (i, 0)), mod_spec,
                      pl.BlockSpec((tm, LANES), lambda i, off: (i, 0)),
                      pl.BlockSpec((tm, LANES), lambda i, off: (i, 0)),
                      pl.BlockSpec(memory_space=pl.ANY)],
            out_specs=pl.BlockSpec((tm, D_MODEL), lambda i, off: (i, 0)),
            scratch_shapes=stage),
        compiler_params=_params("arbitrary"),
        name="moe_combine",
    )(offs.reshape(-1).astype(jnp.int32), x, mods, gate, rank, ys)
    return out, ys


def _final_norm_kernel(x_ref, g_ref, o_ref):
    o_ref[...] = _rms(x_ref[...], g_ref[...])


def _final_norm(x, g, tok_off, n_tok):
    tm = min(TM_NORM, n_tok)
    off = tok_off // tm
    return pl.pallas_call(
        _final_norm_kernel,
        out_shape=jax.ShapeDtypeStruct((n_tok, D_MODEL), F32),
        grid=(n_tok // tm,),
        in_specs=[pl.BlockSpec((tm, D_MODEL), lambda i: (i + off, 0)),
                  pl.BlockSpec((1, D_MODEL), lambda i: (0, 0))],
        out_specs=pl.BlockSpec((tm, D_MODEL), lambda i: (i, 0)),
        compiler_params=_params("parallel"),
        name="final_norm",
    )(x, g)


CAST_BLOCK_BYTES = 6 * 1024 * 1024


def _cast_kernel(x_ref, o_ref):
    o_ref[...] = x_ref[...].astype(o_ref.dtype)


def _to_bf16(w):
    w2 = w.reshape(-1, w.shape[-1])
    rows, cols = w2.shape
    fits = [r for r in range(BF16_SUBLANES, rows + 1, BF16_SUBLANES)
            if rows % r == 0 and r * cols * 4 <= CAST_BLOCK_BYTES]
    tr = max(fits)
    out = pl.pallas_call(
        _cast_kernel,
        out_shape=jax.ShapeDtypeStruct((rows, cols), BF16),
        grid=(rows // tr,),
        in_specs=[pl.BlockSpec((tr, cols), lambda i: (i, 0))],
        out_specs=pl.BlockSpec((tr, cols), lambda i: (i, 0)),
        compiler_params=_params("parallel"),
        name="weights_to_bf16",
    )(w2)
    return out.reshape(w.shape)

def _grid_pos_embed(l, d):
    rows = l // GRID_W
    row = jnp.repeat(jnp.arange(rows, dtype=F32), GRID_W)
    col = jnp.tile(jnp.arange(GRID_W, dtype=F32), rows)
    quarter = d // 4
    omega = jnp.exp(-math.log(10000.0) * jnp.arange(quarter, dtype=F32) / quarter)
    er = row[:, None] * omega
    ec = col[:, None] * omega
    return jnp.concatenate([jnp.sin(er), jnp.cos(er), jnp.sin(ec), jnp.cos(ec)], axis=-1)


def _pad_lanes(a, width):
    return jnp.pad(a, [(0, 0)] * (a.ndim - 1) + [(0, width - a.shape[-1])])


def _reorder_w_in(w):
    o = np.cumsum([0, SSD_INNER, SSD_INNER + 2 * SSD_GROUPS * SSD_STATE, 2 * SSD_HEADS, GLA_KDIM, GLA_KDIM,
                   GLA_VDIM, GLA_VDIM, 2 * GLA_GATE_RANK, D_MODEL, D_MODEL])
    z, xbc, dt, qq, kk, vv, rr, lr, ga, gb = [w[:, int(o[i]):int(o[i + 1])] for i in range(10)]
    main = jnp.concatenate([z, vv, rr, ga, gb, qq, kk], axis=1).astype(BF16)
    small = _pad_lanes(jnp.concatenate([dt, lr], axis=1), LANES).astype(BF16)
    return main, xbc.astype(BF16), small


def _layer_weights(i, p):
    conv_w = jnp.pad(p["ssd_conv_w"][i], ((0, SUBLANES - SSD_CONV), (0, 0)))
    conv_b = p["ssd_conv_b"][i][None, :]
    gate_w = jnp.zeros((LANES, 2 * GLA_KDIM), F32)
    gate_w = gate_w.at[SM_LR:SM_LR + GLA_GATE_RANK, :GLA_KDIM].set(p["gla_gate_w"][i, 0])
    gate_w = gate_w.at[SM_LR + GLA_GATE_RANK:SM_LR + 2 * GLA_GATE_RANK, GLA_KDIM:].set(p["gla_gate_w"][i, 1])
    gate_w_hi = gate_w.astype(BF16)
    gate_w_lo = (gate_w - gate_w_hi.astype(F32)).astype(BF16)
    w_main, w_conv, w_small = _reorder_w_in(p["w_in"][i])
    return {
        "w_in": w_main, "w_in_conv": w_conv, "w_in_small": w_small,
        "conv_w": conv_w, "conv_b": conv_b,
        "dt_bias": _pad_lanes(p["ssd_dt_bias"][i].reshape(1, -1), LANES),
        "a_log": _pad_lanes(p["ssd_a_log"][i].reshape(1, -1), LANES),
        "d_exp": jnp.repeat(p["ssd_d"][i], SSD_HEADDIM)[None, :],
        "ssd_norm_g": p["ssd_norm_g"][i][None, :],
        "gate_w_hi": gate_w_hi, "gate_w_lo": gate_w_lo, "gate_b": p["gla_gate_b"][i].reshape(1, -1),
        "gla_norm_g": p["gla_norm_g"][i][None, :],
        "w_bs": p["w_branch_ssd"][i].astype(BF16), "w_bg": p["w_branch_gla"][i].astype(BF16),
        "w_o": p["w_out"][i].astype(BF16),
    }


def _ssd_state_in(s):
    b = s.shape[0]
    return jnp.transpose(s, (0, 1, 4, 2, 3)).reshape(b, 2, SSD_STATE, SSD_INNER)


def _ssd_state_out(f, b):
    s = jnp.stack([f, b], axis=1).reshape(-1, 2, SSD_STATE, SSD_HEADS, SSD_HEADDIM)
    return jnp.transpose(s, (0, 1, 3, 4, 2))


def _gla_state_in(s):
    b = s.shape[0]
    return jnp.transpose(s, (0, 1, 4, 2, 3)).reshape(b, 2, GLA_DV, GLA_KDIM)


def _gla_state_out(f, b):
    s = jnp.stack([f, b], axis=1).reshape(-1, 2, GLA_DV, GLA_HEADS, GLA_DK)
    return jnp.transpose(s, (0, 1, 3, 4, 2))


def _trunk(x_prompt, x_sample, state_ssd, state_gla, c, c_ctx, p):
    n0, l0, _ = x_prompt.shape
    n1, l1, _ = x_sample.shape
    regions = ((n0, l0), (n1, l1))
    depth = p["w_in"].shape[0]

    cc = jnp.zeros((MOD_ROWS, D_MODEL), F32).at[0].set(c_ctx).at[1:1 + n1].set(c)
    mods = _modulation_table(cc, p["ada_w"], p["ada_b"])[:, :, None, :]
    x = _assemble_tokens(x_prompt, x_sample, _grid_pos_embed(l1, D_MODEL), regions)

    ssd_states, gla_states = [], []
    moe_rows = None
    for i in range(depth):
        lw = _layer_weights(i, p)
        init_ssd = jnp.concatenate([jnp.zeros((n0, 2, SSD_STATE, SSD_INNER), F32),
                                    _ssd_state_in(state_ssd[:, i])], axis=0)
        init_gla = jnp.concatenate([jnp.zeros((n0, 2, GLA_DV, GLA_KDIM), F32),
                                    _gla_state_in(state_gla[:, i])], axis=0)
        proj, xbc, small = _in_projection(x, mods[i], p["norm1_g"][i][None, :], lw, regions)
        y_ssd, sf, sb, y_gla, gf, gb = _mixers(proj, xbc, small, init_ssd, init_gla, lw, regions)
        ssd_states.append(_ssd_state_out(sf[:n0], sb[:n0]))
        gla_states.append(_gla_state_out(gf[:n0], gb[:n0]))
        x = _merge(x, y_ssd, y_gla, proj, mods[i], lw, regions)
        j = i // 2
        g2 = p["norm2_g"][i][None, :]
        if i % 2 == 0:
            x = _dense_ffn(x, mods[i], g2, p["ffn_w1"][j].astype(BF16), p["ffn_w3"][j].astype(BF16),
                           p["ffn_w2"][j].astype(BF16), regions)
        else:
            if moe_rows is None:
                n_e = p["moe_w1"].shape[1]
                moe_rows = jnp.zeros((n_e * _moe_geometry(x.shape[0], regions, n_e)[2], D_MODEL), BF16)
            x, moe_rows = _moe_ffn(x, mods[i], g2, _pad_lanes(p["router_w"][j], LANES),
                                   _pad_lanes(p["router_b"][j][None, :], LANES),
                                   _to_bf16(p["moe_w1"][j]), _to_bf16(p["moe_w3"][j]),
                                   _to_bf16(p["moe_w2"][j]), regions, moe_rows)

    gfin = p["final_norm_g"][None, :]
    y_prompt = _final_norm(x, gfin, 0, n0 * l0).reshape(n0, l0, D_MODEL)
    y_sample = _final_norm(x, gfin, n0 * l0, n1 * l1).reshape(n1, l1, D_MODEL)
    return y_prompt, y_sample, jnp.stack(ssd_states, axis=1), jnp.stack(gla_states, axis=1)


def kernel(x_prompt, x_sample, state_ssd, state_gla, c, c_ctx, ada_w, ada_b, norm1_g, norm2_g, w_in, ssd_conv_w, ssd_conv_b, ssd_dt_bias, ssd_a_log, ssd_d, ssd_norm_g, gla_gate_w, gla_gate_b, gla_norm_g, w_branch_ssd, w_branch_gla, w_out, ffn_w1, ffn_w3, ffn_w2, router_w, router_b, moe_w1, moe_w3, moe_w2, final_norm_g):
    p = dict(ada_w=ada_w, ada_b=ada_b, norm1_g=norm1_g, norm2_g=norm2_g, w_in=w_in, ssd_conv_w=ssd_conv_w,
             ssd_conv_b=ssd_conv_b, ssd_dt_bias=ssd_dt_bias, ssd_a_log=ssd_a_log, ssd_d=ssd_d,
             ssd_norm_g=ssd_norm_g, gla_gate_w=gla_gate_w, gla_gate_b=gla_gate_b, gla_norm_g=gla_norm_g,
             w_branch_ssd=w_branch_ssd, w_branch_gla=w_branch_gla, w_out=w_out, ffn_w1=ffn_w1, ffn_w3=ffn_w3,
             ffn_w2=ffn_w2, router_w=router_w, router_b=router_b, moe_w1=moe_w1, moe_w3=moe_w3, moe_w2=moe_w2,
             final_norm_g=final_norm_g)
    return _trunk(x_prompt, x_sample, state_ssd, state_gla, c, c_ctx, p)
```

```python
import functools
import math

import numpy as np
import jax
import jax.numpy as jnp
from jax import lax
from jax.experimental import pallas as pl
from jax.experimental.pallas import tpu as pltpu

F32 = jnp.float32
BF16 = jnp.bfloat16

D_MODEL = 1024
GRID_W = 64
EPS = 1e-6
SSD_HEADS = 16
SSD_HEADDIM = 64
SSD_INNER = SSD_HEADS * SSD_HEADDIM
SSD_GROUPS = 2
SSD_STATE = 64
SSD_CONV = 5
SSD_CHUNK = 128
GLA_HEADS = 8
GLA_DK = 64
GLA_DV = 128
GLA_KDIM = GLA_HEADS * GLA_DK
GLA_VDIM = GLA_HEADS * GLA_DV
GLA_GATE_RANK = 16
GLA_GATE_TAU = 16.0
GLA_CHUNK = 64
N_EXPERTS = 8

LANES = 128
SUBLANES = 8
BF16_SUBLANES = 16
VMEM_LIMIT_BYTES = 56 * 1024 * 1024

COL_Z, COL_V, COL_R, COL_GA, COL_GB, COL_QK = 0, 1024, 2048, 3072, 4096, 5120
D_PROJ = 6144
CONV_X, CONV_BC = 0, 1024
D_CONV = 1280
SM_LR = 2 * SSD_HEADS

TM_PROJ, TN_PROJ = 1024, 3072
TM_MERGE = 512
TM_FFN, TH_FFN = 512, 1408
TM_MOE = 512
MOE_ROW_BLOCK = 256
MOE_ROW_ALIGN = 16
TR_EXPERT = 512
TM_NORM = 1024
TN_MODS = 1536
MOD_ROWS = 16
SCAN_STEP = 256


def _params(*sem):
    return pltpu.CompilerParams(dimension_semantics=sem, vmem_limit_bytes=VMEM_LIMIT_BYTES)


def _sigmoid(x):
    return 0.5 * jnp.tanh(0.5 * x) + 0.5


def _silu(x):
    return x * _sigmoid(x)


def _softplus(x):
    return jnp.maximum(x, 0.0) + jnp.log(1.0 + jnp.exp(-jnp.abs(x)))


def _split2(a):
    a1 = a.astype(BF16)
    a2 = (a - a1.astype(F32)).astype(BF16)
    return a1, a2


def _dot(a, b):
    return jnp.dot(a, b, preferred_element_type=F32)


def _dot_nt(a, b):
    return lax.dot_general(a, b, (((1,), (1,)), ((), ())), preferred_element_type=F32)


def _dot_tn(a, b):
    return lax.dot_general(a, b, (((0,), (0,)), ((), ())), preferred_element_type=F32)


def _dot_exact_lhs(t01, a):
    a1, a2 = _split2(a)
    return _dot(t01, a1) + _dot(t01, a2)


def _dot_hilo(a, b):
    a1, a2 = _split2(a)
    b1, b2 = _split2(b)
    return _dot(a1, b1) + _dot(a2, b1) + _dot(a1, b2)


def _block_tri(n, blk, upper):
    i = lax.broadcasted_iota(jnp.int32, (n, n), 0)
    j = lax.broadcasted_iota(jnp.int32, (n, n), 1)
    tri = (j >= i) if upper else (j <= i)
    if blk == n:
        return tri
    return jnp.logical_and(tri, (i // blk) == (j // blk))


def _rms(x, g):
    return x * lax.rsqrt(jnp.mean(x * x, axis=-1, keepdims=True) + EPS) * g


def _chunk_pos(c, q, regions):
    (n0, l0), (n1, l1) = regions
    per0, per1 = l0 // q, l1 // q
    nc0 = n0 * per0
    c1 = jnp.maximum(c - nc0, 0)
    in0 = c < nc0
    seq = jnp.where(in0, c // per0, n0 + c1 // per1)
    pos = jnp.where(in0, c % per0, c1 % per1)
    last = jnp.where(in0, per0 - 1, per1 - 1)
    return seq, pos == 0, pos == last


def _mod_row(i, tm, regions):
    (n0, l0), (n1, l1) = regions
    t1 = jnp.maximum(i * tm - n0 * l0, 0)
    return jnp.where(i * tm < n0 * l0, 0, 1 + t1 // l1)


def _assemble_kernel(xp_ref, xs_ref, pe_ref, o_ref, *, n_ctx_tiles):
    i = pl.program_id(0)

    @pl.when(i < n_ctx_tiles)
    def _():
        o_ref[...] = xp_ref[...]

    @pl.when(i >= n_ctx_tiles)
    def _():
        o_ref[...] = xs_ref[...] + pe_ref[...]


def _assemble_tokens(xp, xs, pe, regions):
    (n0, l0), (n1, l1) = regions
    tm = min(1024, l0 * n0, l1)
    t0, t1 = n0 * l0, n1 * l1
    n_ctx = t0 // tm
    pe_tiles = l1 // tm
    return pl.pallas_call(
        functools.partial(_assemble_kernel, n_ctx_tiles=n_ctx),
        out_shape=jax.ShapeDtypeStruct((t0 + t1, D_MODEL), F32),
        grid=((t0 + t1) // tm,),
        in_specs=[
            pl.BlockSpec((tm, D_MODEL), lambda i: (jnp.minimum(i, n_ctx - 1), 0)),
            pl.BlockSpec((tm, D_MODEL), lambda i: (jnp.maximum(i - n_ctx, 0), 0)),
            pl.BlockSpec((tm, D_MODEL), lambda i: (jnp.maximum(i - n_ctx, 0) % pe_tiles, 0)),
        ],
        out_specs=pl.BlockSpec((tm, D_MODEL), lambda i: (i, 0)),
        compiler_params=_params("arbitrary"),
        name="assemble_tokens",
    )(xp.reshape(t0, D_MODEL), xs.reshape(t1, D_MODEL), pe)


def _mods_kernel(cc_ref, w_ref, b_ref, o_ref):
    a = _silu(cc_ref[...])
    o_ref[...] = _dot_hilo(a, w_ref[...]) + b_ref[...]


def _modulation_table(cc, ada_w, ada_b):
    depth = ada_w.shape[0]
    n = ada_w.shape[2]
    return pl.pallas_call(
        _mods_kernel,
        out_shape=jax.ShapeDtypeStruct((depth, MOD_ROWS, n), F32),
        grid=(depth, n // TN_MODS),
        in_specs=[
            pl.BlockSpec((MOD_ROWS, D_MODEL), lambda l, j: (0, 0)),
            pl.BlockSpec((None, D_MODEL, TN_MODS), lambda l, j: (l, 0, j)),
            pl.BlockSpec((None, 1, TN_MODS), lambda l, j: (l, 0, j)),
        ],
        out_specs=pl.BlockSpec((None, MOD_ROWS, TN_MODS), lambda l, j: (l, 0, j)),
        compiler_params=_params("arbitrary", "arbitrary"),
        name="modulation_table",
    )(cc, ada_w, ada_b.reshape(depth, 1, n))


def _inproj_kernel(x_ref, mod_ref, g_ref, w_ref, wc_ref, ws_ref, o_ref, oc_ref, os_ref, u_ref):
    def modnorm(x):
        y = _rms(x, g_ref[...])
        return y * (1.0 + mod_ref[:, D_MODEL:2 * D_MODEL]) + mod_ref[:, 0:D_MODEL]

    @pl.when(pl.program_id(1) == 0)
    def _():
        u_ref[...] = modnorm(x_ref[...]).astype(BF16)
        os_ref[...] = _dot(u_ref[...], ws_ref[...])

    u = u_ref[...]
    oc_ref[...] = _dot(u, wc_ref[...]).astype(BF16)
    o_ref[...] = _dot(u, w_ref[...]).astype(BF16)


def _in_projection(x, mods, g, lw, regions):
    t = x.shape[0]
    tm = min(TM_PROJ, regions[0][0] * regions[0][1], regions[1][1])
    n_col = D_PROJ // TN_PROJ
    tc = D_CONV // n_col
    return pl.pallas_call(
        _inproj_kernel,
        out_shape=(jax.ShapeDtypeStruct((t, D_PROJ), BF16), jax.ShapeDtypeStruct((t, D_CONV), BF16),
                   jax.ShapeDtypeStruct((t, LANES), F32)),
        grid=(t // tm, n_col),
        in_specs=[
            pl.BlockSpec((tm, D_MODEL), lambda i, j: (i, 0)),
            pl.BlockSpec((None, 1, 6 * D_MODEL), lambda i, j: (_mod_row(i, tm, regions), 0, 0)),
            pl.BlockSpec((1, D_MODEL), lambda i, j: (0, 0)),
            pl.BlockSpec((D_MODEL, TN_PROJ), lambda i, j: (0, j)),
            pl.BlockSpec((D_MODEL, tc), lambda i, j: (0, j)),
            pl.BlockSpec((D_MODEL, LANES), lambda i, j: (0, 0)),
        ],
        out_specs=(pl.BlockSpec((tm, TN_PROJ), lambda i, j: (i, j)),
                   pl.BlockSpec((tm, tc), lambda i, j: (i, j)),
                   pl.BlockSpec((tm, LANES), lambda i, j: (i, 0))),
        scratch_shapes=[pltpu.VMEM((tm, D_MODEL), BF16)],
        compiler_params=_params("parallel", "arbitrary"),
        name="norm_in_projection",
    )(x, mods, g, lw["w_in"], lw["w_in_conv"], lw["w_in_small"])


def _ssd_decay_terms(sm, dtb, alog, q):
    lane = lax.broadcasted_iota(jnp.int32, (1, LANES), 1)
    dtv = _softplus(sm + dtb)
    a_neg = jnp.where(lane < 2 * SSD_HEADS, -jnp.exp(alog), 0.0)
    a = dtv * a_neg
    acs = _dot_exact_lhs(_block_tri(q, q, False).astype(BF16), a)
    rev = _dot_exact_lhs(_block_tri(q, q, True).astype(BF16), a)
    m = jnp.where(lane < SSD_HEADS, acs, rev)
    tot = acs[q - 1:q, :]
    return m, tot, dtv


def _col(x, idx, width=LANES):
    return jnp.broadcast_to(x[:, idx:idx + 1], (x.shape[0], width))


def _head_expand_matrix(base):
    k = lax.broadcasted_iota(jnp.int32, (LANES, SSD_INNER), 0)
    c = lax.broadcasted_iota(jnp.int32, (LANES, SSD_INNER), 1)
    return jnp.where(k == base + c // SSD_HEADDIM, 1.0, 0.0).astype(BF16)


def _pair_cols(x, base, p, lo_half):
    return jnp.where(lo_half, _col(x, base + 2 * p), _col(x, base + 2 * p + 1))


def _expand_state(s):
    lane = lax.broadcasted_iota(jnp.int32, s.shape, 1)
    half = SSD_INNER // SSD_GROUPS
    return jnp.concatenate([jnp.where(lane < half, s, 0.0), jnp.where(lane >= half, s, 0.0)], axis=0)


def _compact_state(s2):
    return s2[0:SSD_STATE, :] + s2[SSD_STATE:2 * SSD_STATE, :]


def _state_update_mask():
    row = lax.broadcasted_iota(jnp.int32, (SSD_GROUPS * SSD_STATE, SSD_INNER), 0)
    lane = lax.broadcasted_iota(jnp.int32, (SSD_GROUPS * SSD_STATE, SSD_INNER), 1)
    return (row < SSD_STATE) == (lane < SSD_INNER // SSD_GROUPS)


HALO = BF16_SUBLANES
CONV_PAD = SSD_CONV // 2


def _ssd_bwd_kernel(xbc_ref, prev_ref, next_ref, sm_ref, cw_ref, cb_ref, dtb_ref, alog_ref, init_ref,
                    conv_ref, tab_ref, enter_ref, fin_ref, s_ref, ext_ref, *, regions, n_steps):
    q = SSD_CHUNK
    st = SCAN_STEP
    c = n_steps - 1 - pl.program_id(0)
    _, first, last = _chunk_pos(c, st, regions)

    @pl.when(last)
    def _():
        s_ref[...] = _expand_state(init_ref[...])

    ext_ref[0:HALO, :] = jnp.where(first, 0.0, prev_ref[...].astype(F32))
    ext_ref[HALO:HALO + st, :] = xbc_ref[...].astype(F32)
    ext_ref[HALO + st:2 * HALO + st, :] = jnp.where(last, 0.0, next_ref[...].astype(F32))
    acc = cb_ref[...]
    for k in range(SSD_CONV):
        off = HALO - CONV_PAD + k
        acc = acc + cw_ref[k:k + 1, :] * ext_ref[off:off + st, :]
    conv_ref[...] = _silu(acc).astype(BF16)

    lo_half = lax.broadcasted_iota(jnp.int32, (1, LANES), 1) < SSD_HEADDIM
    upd_mask = _state_update_mask()

    s_cur = s_ref[...]
    for ci in reversed(range(st // q)):
        rows = slice(ci * q, (ci + 1) * q)
        xs = conv_ref[rows, 0:SSD_INNER].astype(F32)
        bsb = conv_ref[rows, SSD_INNER:SSD_INNER + LANES]
        m, tot, dtv = _ssd_decay_terms(sm_ref[rows, :], dtb_ref[...], alog_ref[...], q)
        tab_ref[rows, 0:LANES] = m
        tab_ref[rows, LANES:2 * LANES] = dtv
        wgt = jnp.exp(tot - m) * dtv
        etot = jnp.exp(tot)
        enter_ref[ci] = _compact_state(s_cur)
        xw, dec = [], []
        for p in range(SSD_HEADS // 2):
            sl = slice(p * LANES, (p + 1) * LANES)
            xw.append((xs[:, sl] * _pair_cols(wgt, SSD_HEADS, p, lo_half)).astype(BF16))
            dec.append(_pair_cols(etot, SSD_HEADS, p, lo_half))
        upd = _dot_tn(bsb, jnp.concatenate(xw, axis=1))
        s_cur = s_cur * jnp.concatenate(dec, axis=1) + jnp.where(upd_mask, upd, 0.0)
    s_ref[...] = s_cur

    @pl.when(first)
    def _():
        fin_ref[...] = _compact_state(s_cur)


def _ssd_out_kernel(z_ref, x_ref, bc_ref, tab_ref, dexp_ref, ng_ref,
                    init_ref, enter_ref, y_ref, fin_ref, s_ref, *, regions):
    q = SSD_CHUNK
    c = pl.program_id(0)
    _, first, last = _chunk_pos(c, SCAN_STEP, regions)

    @pl.when(first)
    def _():
        s_ref[...] = _expand_state(init_ref[...])

    lane = lax.broadcasted_iota(jnp.int32, (1, LANES), 1)
    lo_half = lane < SSD_HEADDIM
    tril = _block_tri(q, q, False)
    ii = lax.broadcasted_iota(jnp.int32, (q, q), 0)
    jj = lax.broadcasted_iota(jnp.int32, (q, q), 1)
    eye = ii == jj
    upd_mask = _state_update_mask()
    expand = _head_expand_matrix(0)
    dexp = dexp_ref[...]
    ng = ng_ref[...]

    s_cur = s_ref[...]
    for ci in range(SCAN_STEP // q):
        rows = slice(ci * q, (ci + 1) * q)
        xs = x_ref[rows, :].astype(F32)
        bsb = bc_ref[rows, 0:LANES]
        csb = bc_ref[rows, LANES:2 * LANES]
        m = tab_ref[rows, 0:LANES]
        dtv = tab_ref[rows, LANES:2 * LANES]
        wgt = jnp.exp(jnp.where(lane < SSD_HEADS, m[q - 1:q, :] - m, 0.0)) * dtv
        mt = (m - jnp.log(dtv)).T
        ldt = jnp.log(dtv + pltpu.roll(dtv, LANES - SSD_HEADS, 1)).T
        zero_b = jnp.zeros_like(csb)
        cb = [_dot_nt(jnp.where(lo_half, csb, zero_b), bsb).astype(BF16),
              _dot_nt(jnp.where(lo_half, zero_b, csb), bsb).astype(BF16)]
        cs_f = _dot(csb, s_cur.astype(BF16))
        cs_b = _dot(csb, _expand_state(enter_ref[ci]).astype(BF16))

        ys, dec = [], []
        for p in range(SSD_HEADS // 2):
            sl = slice(p * LANES, (p + 1) * LANES)
            g = (2 * p) // (SSD_HEADS // SSD_GROUPS)
            xs_p = xs[:, sl]
            xs_pb = xs_p.astype(BF16)
            yd, colf, colb = [], [], []
            for h in (2 * p, 2 * p + 1):
                hb = SSD_HEADS + h
                cf = _col(m, h, q)
                cbk = _col(m, hb, q)
                seg = jnp.where(tril, cf - mt[h:h + 1, :], cbk - mt[hb:hb + 1, :])
                seg = jnp.where(eye, ldt[h:h + 1, :], seg)
                yd.append(_dot(cb[g] * jnp.exp(seg).astype(BF16), xs_pb))
                colf.append(cf)
                colb.append(cbk)
            ef = jnp.exp(jnp.where(lo_half, colf[0], colf[1]))
            eb = jnp.exp(jnp.where(lo_half, colb[0], colb[1]))
            y_p = jnp.where(lo_half, yd[0], yd[1]) + ef * cs_f[:, sl] + eb * cs_b[:, sl] + dexp[:, sl] * xs_p
            ys.append(y_p)
            dec.append(ef[q - 1:q, :])

        y = jnp.concatenate(ys, axis=1) * _silu(z_ref[rows, :].astype(F32))
        y_ref[rows, :] = _rms(y, ng).astype(y_ref.dtype)

        xw = (xs * _dot(wgt.astype(BF16), expand)).astype(BF16)
        upd = _dot_tn(bsb, xw)
        s_cur = s_cur * jnp.concatenate(dec, axis=1) + jnp.where(upd_mask, upd, 0.0)
    s_ref[...] = s_cur

    @pl.when(last)
    def _():
        fin_ref[...] = _compact_state(s_cur)


def _const_spec(a):
    return pl.BlockSpec(a.shape, lambda s: (0,) * a.ndim)


def _sweep_call(parts, n_steps, name):
    n_in = [len(p["inputs"]) for p in parts]
    n_out = [len(p["out_shape"]) for p in parts]
    n_scr = [len(p["scratch"]) for p in parts]

    def kernel(*refs):
        ins, outs, scr = refs[:sum(n_in)], refs[sum(n_in):sum(n_in) + sum(n_out)], refs[sum(n_in) + sum(n_out):]
        for k, p in enumerate(parts):
            i0, o0, s0 = sum(n_in[:k]), sum(n_out[:k]), sum(n_scr[:k])
            p["kernel"](*ins[i0:i0 + n_in[k]], *outs[o0:o0 + n_out[k]], *scr[s0:s0 + n_scr[k]])

    res = pl.pallas_call(
        kernel,
        out_shape=tuple(o for p in parts for o in p["out_shape"]),
        grid=(n_steps,),
        in_specs=[s for p in parts for s in p["in_specs"]],
        out_specs=tuple(s for p in parts for s in p["out_specs"]),
        scratch_shapes=[s for p in parts for s in p["scratch"]],
        compiler_params=_params("arbitrary"),
        name=name,
    )(*[a for p in parts for a in p["inputs"]])
    return [res[sum(n_out[:k]):sum(n_out[:k]) + n_out[k]] for k in range(len(parts))]


def _ssd_bwd_part(xbc, small, init, lw, regions):
    st = SCAN_STEP
    cps = st // SSD_CHUNK
    t = xbc.shape[0]
    n_steps = t // st
    n_seq = regions[0][0] + regions[1][0]
    bidx = lambda s: n_steps - 1 - s
    seq_b = lambda s: _chunk_pos(bidx(s), st, regions)[0]
    hb = st // HALO
    n_hb = t // HALO
    consts = [lw["conv_w"], lw["conv_b"], lw["dt_bias"], lw["a_log"]]
    return dict(
        kernel=functools.partial(_ssd_bwd_kernel, regions=regions, n_steps=n_steps),
        inputs=[xbc, xbc, xbc, small, *consts, init],
        in_specs=[pl.BlockSpec((st, D_CONV), lambda s: (bidx(s), 0)),
                  pl.BlockSpec((HALO, D_CONV), lambda s: (jnp.maximum(bidx(s) * hb - 1, 0), 0)),
                  pl.BlockSpec((HALO, D_CONV), lambda s: (jnp.minimum((bidx(s) + 1) * hb, n_hb - 1), 0)),
                  pl.BlockSpec((st, LANES), lambda s: (bidx(s), 0))]
        + [_const_spec(a) for a in consts]
        + [pl.BlockSpec((None, None, SSD_STATE, SSD_INNER), lambda s: (seq_b(s), 1, 0, 0))],
        out_shape=[jax.ShapeDtypeStruct((t, D_CONV), BF16),
                   jax.ShapeDtypeStruct((t, 2 * LANES), F32),
                   jax.ShapeDtypeStruct((n_steps * cps, SSD_STATE, SSD_INNER), F32),
                   jax.ShapeDtypeStruct((n_seq, SSD_STATE, SSD_INNER), F32)],
        out_specs=[pl.BlockSpec((st, D_CONV), lambda s: (bidx(s), 0)),
                   pl.BlockSpec((st, 2 * LANES), lambda s: (bidx(s), 0)),
                   pl.BlockSpec((cps, SSD_STATE, SSD_INNER), lambda s: (bidx(s), 0, 0)),
                   pl.BlockSpec((None, SSD_STATE, SSD_INNER), lambda s: (seq_b(s), 0, 0))],
        scratch=[pltpu.VMEM((SSD_GROUPS * SSD_STATE, SSD_INNER), F32),
                 pltpu.VMEM((st + 2 * HALO, D_CONV), F32)])


def _ssd_fwd_part(proj, xbc, tables, init, enter_b, lw, regions):
    st = SCAN_STEP
    cps = st // SSD_CHUNK
    t = proj.shape[0]
    n_seq = regions[0][0] + regions[1][0]
    seq_f = lambda s: _chunk_pos(s, st, regions)[0]
    consts = [lw["d_exp"], lw["ssd_norm_g"]]
    return dict(
        kernel=functools.partial(_ssd_out_kernel, regions=regions),
        inputs=[proj, xbc, xbc, tables, *consts, init, enter_b],
        in_specs=[pl.BlockSpec((st, SSD_INNER), lambda s: (s, COL_Z // SSD_INNER)),
                  pl.BlockSpec((st, SSD_INNER), lambda s: (s, CONV_X // SSD_INNER)),
                  pl.BlockSpec((st, 2 * LANES), lambda s: (s, CONV_BC // (2 * LANES))),
                  pl.BlockSpec((st, 2 * LANES), lambda s: (s, 0))]
        + [_const_spec(a) for a in consts]
        + [pl.BlockSpec((None, None, SSD_STATE, SSD_INNER), lambda s: (seq_f(s), 0, 0, 0)),
           pl.BlockSpec((cps, SSD_STATE, SSD_INNER), lambda s: (s, 0, 0))],
        out_shape=[jax.ShapeDtypeStruct((t, SSD_INNER), BF16),
                   jax.ShapeDtypeStruct((n_seq, SSD_STATE, SSD_INNER), F32)],
        out_specs=[pl.BlockSpec((st, SSD_INNER), lambda s: (s, 0)),
                   pl.BlockSpec((None, SSD_STATE, SSD_INNER), lambda s: (seq_f(s), 0, 0))],
        scratch=[pltpu.VMEM((SSD_GROUPS * SSD_STATE, SSD_INNER), F32)])


def _gla_log_decay(sm, gw_hi_ref, gw_lo_ref, gb_ref):
    s1, s2 = _split2(sm)
    logit = _dot(s1, gw_hi_ref[...]) + _dot(s2, gw_hi_ref[...]) + _dot(s1, gw_lo_ref[...]) + gb_ref[...]
    return -_softplus(-logit) * (1.0 / GLA_GATE_TAU)


def _chunk_rows(x, n_chunks, rows_per_chunk):
    return jnp.concatenate([jnp.broadcast_to(x[c:c + 1, :], (rows_per_chunk, x.shape[1]))
                            for c in range(n_chunks)], axis=0)


def _gla_bwd_kernel(qk_ref, v_ref, sm_ref, gwh_ref, gwl_ref, gb_ref, init_ref,
                    rev_ref, enter_ref, fin_ref, s_ref, *, regions, n_steps):
    cq = GLA_CHUNK
    st = SCAN_STEP
    ncs = st // cq
    c = n_steps - 1 - pl.program_id(0)
    _, first, last = _chunk_pos(c, st, regions)

    @pl.when(last)
    def _():
        s_ref[...] = init_ref[...]

    bwd = slice(GLA_KDIM, 2 * GLA_KDIM)
    lg = _gla_log_decay(sm_ref[...], gwh_ref.at[:, bwd], gwl_ref.at[:, bwd], gb_ref.at[:, bwd])
    rev = _dot_exact_lhs(_block_tri(st, cq, True).astype(BF16), lg)
    rev_ref[...] = rev
    tot = jnp.concatenate([rev[ci * cq:ci * cq + 1, :] for ci in range(ncs)], axis=0)
    ko = (qk_ref[:, GLA_KDIM:2 * GLA_KDIM].astype(F32) * jnp.exp(_chunk_rows(tot, ncs, cq) - rev)).astype(BF16)
    etot = jnp.exp(tot)
    lo_half = lax.broadcasted_iota(jnp.int32, (1, LANES), 1) < GLA_DK

    s_cur = s_ref[...]
    for ci in reversed(range(ncs)):
        rows = slice(ci * cq, (ci + 1) * cq)
        enter_ref[ci] = s_cur
        new = []
        for p in range(GLA_HEADS // 2):
            sl = slice(p * LANES, (p + 1) * LANES)
            u0 = _dot_tn(v_ref[rows, (2 * p) * GLA_DV:(2 * p + 1) * GLA_DV], ko[rows, sl])
            u1 = _dot_tn(v_ref[rows, (2 * p + 1) * GLA_DV:(2 * p + 2) * GLA_DV], ko[rows, sl])
            new.append(s_cur[:, sl] * etot[ci:ci + 1, sl] + jnp.where(lo_half, u0, u1))
        s_cur = jnp.concatenate(new, axis=1)
    s_ref[...] = s_cur

    @pl.when(first)
    def _():
        fin_ref[...] = s_cur


def _gla_out_kernel(qk_ref, v_ref, r_ref, sm_ref, gwh_ref, gwl_ref, gb_ref, ng_ref, init_ref, enter_ref, rev_ref,
                    y_ref, fin_ref, s_ref, *, regions):
    cq = GLA_CHUNK
    st = SCAN_STEP
    ncs = st // cq
    c = pl.program_id(0)
    _, first, last = _chunk_pos(c, st, regions)

    @pl.when(first)
    def _():
        s_ref[...] = init_ref[...]

    fwd = slice(0, GLA_KDIM)
    lg = _gla_log_decay(sm_ref[...], gwh_ref.at[:, fwd], gwl_ref.at[:, fwd], gb_ref.at[:, fwd])
    gcs = _dot_exact_lhs(_block_tri(st, cq, False).astype(BF16), lg)
    rev = rev_ref[...]
    lo_half = lax.broadcasted_iota(jnp.int32, (1, LANES), 1) < GLA_DK
    zero_b = jnp.zeros((cq, LANES), BF16)
    ii = lax.broadcasted_iota(jnp.int32, (2 * cq, cq), 0) % cq
    jj = lax.broadcasted_iota(jnp.int32, (2 * cq, cq), 1)
    tril2 = jj <= ii
    triu2 = jj >= ii
    ng = ng_ref[...]

    def stack_heads(x):
        return jnp.concatenate([jnp.where(lo_half, x, zero_b), jnp.where(lo_half, zero_b, x)], axis=0)

    s_cur = s_ref[...]
    for ci in range(ncs):
        rows = slice(ci * cq, (ci + 1) * cq)
        g_f = gcs[rows]
        g_b = rev[rows]
        etot = jnp.exp(g_f[cq - 1:cq, :])
        qs = qk_ref[rows, 0:GLA_KDIM].astype(F32) * (GLA_DK ** -0.5)
        ks = qk_ref[rows, GLA_KDIM:2 * GLA_KDIM].astype(F32)
        qe_f = (qs * jnp.exp(g_f)).astype(BF16)
        ke = ks * jnp.exp(-g_f)
        ke_f = ke.astype(BF16)
        ko_f = (ke * etot).astype(BF16)
        qe_b = (qs * jnp.exp(g_b)).astype(BF16)
        ke_b = (ks * jnp.exp(-g_b)).astype(BF16)
        s_curb = s_cur.astype(BF16)
        s_entb = enter_ref[ci].astype(BF16)
        new = []
        for p in range(GLA_HEADS // 2):
            sl = slice(p * LANES, (p + 1) * LANES)
            qf2 = stack_heads(qe_f[:, sl])
            qb2 = stack_heads(qe_b[:, sl])
            att = (jnp.where(tril2, _dot_nt(qf2, ke_f[:, sl]), 0.0)
                   + jnp.where(triu2, _dot_nt(qb2, ke_b[:, sl]), 0.0)).astype(BF16)
            inter = _dot_nt(qf2, s_curb[:, sl]) + _dot_nt(qb2, s_entb[:, sl])
            upd = []
            for hh in (0, 1):
                h = 2 * p + hh
                hs = slice(h * GLA_DV, (h + 1) * GLA_DV)
                hr = slice(hh * cq, (hh + 1) * cq)
                vh = v_ref[rows, hs]
                o = _dot(att[hr], vh) + inter[hr]
                o = o * lax.rsqrt(jnp.mean(o * o, axis=-1, keepdims=True) + EPS) * ng[:, hs]
                y_ref[rows, hs] = (o * _silu(r_ref[rows, hs].astype(F32))).astype(y_ref.dtype)
                upd.append(_dot_tn(vh, ko_f[:, sl]))
            new.append(s_cur[:, sl] * etot[:, sl] + jnp.where(lo_half, upd[0], upd[1]))
        s_cur = jnp.concatenate(new, axis=1)
    s_ref[...] = s_cur

    @pl.when(last)
    def _():
        fin_ref[...] = s_cur


def _gla_bwd_part(proj, small, init, lw, regions):
    st = SCAN_STEP
    ncs = st // GLA_CHUNK
    t = proj.shape[0]
    n_steps = t // st
    n_seq = regions[0][0] + regions[1][0]
    consts = [lw["gate_w_hi"], lw["gate_w_lo"], lw["gate_b"]]
    bidx = lambda s: n_steps - 1 - s
    seq_b = lambda s: _chunk_pos(bidx(s), st, regions)[0]
    return dict(
        kernel=functools.partial(_gla_bwd_kernel, regions=regions, n_steps=n_steps),
        inputs=[proj, proj, small, *consts, init],
        in_specs=[pl.BlockSpec((st, 2 * GLA_KDIM), lambda s: (bidx(s), COL_QK // (2 * GLA_KDIM))),
                  pl.BlockSpec((st, GLA_VDIM), lambda s: (bidx(s), COL_V // GLA_VDIM)),
                  pl.BlockSpec((st, LANES), lambda s: (bidx(s), 0))]
        + [_const_spec(a) for a in consts]
        + [pl.BlockSpec((None, None, GLA_DV, GLA_KDIM), lambda s: (seq_b(s), 1, 0, 0))],
        out_shape=[jax.ShapeDtypeStruct((t, GLA_KDIM), F32),
                   jax.ShapeDtypeStruct((n_steps * ncs, GLA_DV, GLA_KDIM), F32),
                   jax.ShapeDtypeStruct((n_seq, GLA_DV, GLA_KDIM), F32)],
        out_specs=[pl.BlockSpec((st, GLA_KDIM), lambda s: (bidx(s), 0)),
                   pl.BlockSpec((ncs, GLA_DV, GLA_KDIM), lambda s: (bidx(s), 0, 0)),
                   pl.BlockSpec((None, GLA_DV, GLA_KDIM), lambda s: (seq_b(s), 0, 0))],
        scratch=[pltpu.VMEM((GLA_DV, GLA_KDIM), F32)])


def _gla_fwd_part(proj, small, init, enter_b, rev_b, lw, regions):
    st = SCAN_STEP
    ncs = st // GLA_CHUNK
    t = proj.shape[0]
    n_seq = regions[0][0] + regions[1][0]
    consts = [lw["gate_w_hi"], lw["gate_w_lo"], lw["gate_b"], lw["gla_norm_g"]]
    seq_f = lambda s: _chunk_pos(s, st, regions)[0]
    return dict(
        kernel=functools.partial(_gla_out_kernel, regions=regions),
        inputs=[proj, proj, proj, small, *consts, init, enter_b, rev_b],
        in_specs=[pl.BlockSpec((st, 2 * GLA_KDIM), lambda s: (s, COL_QK // (2 * GLA_KDIM))),
                  pl.BlockSpec((st, GLA_VDIM), lambda s: (s, COL_V // GLA_VDIM)),
                  pl.BlockSpec((st, GLA_VDIM), lambda s: (s, COL_R // GLA_VDIM)),
                  pl.BlockSpec((st, LANES), lambda s: (s, 0))]
        + [_const_spec(a) for a in consts]
        + [pl.BlockSpec((None, None, GLA_DV, GLA_KDIM), lambda s: (seq_f(s), 0, 0, 0)),
           pl.BlockSpec((ncs, GLA_DV, GLA_KDIM), lambda s: (s, 0, 0)),
           pl.BlockSpec((st, GLA_KDIM), lambda s: (s, 0))],
        out_shape=[jax.ShapeDtypeStruct((t, GLA_VDIM), BF16),
                   jax.ShapeDtypeStruct((n_seq, GLA_DV, GLA_KDIM), F32)],
        out_specs=[pl.BlockSpec((st, GLA_VDIM), lambda s: (s, 0)),
                   pl.BlockSpec((None, GLA_DV, GLA_KDIM), lambda s: (seq_f(s), 0, 0))],
        scratch=[pltpu.VMEM((GLA_DV, GLA_KDIM), F32)])


def _mixers(proj, xbc, small, init_ssd, init_gla, lw, regions):
    n_steps = proj.shape[0] // SCAN_STEP
    (xbc_c, tab_s, enter_s, fin_sb), (rev_g, enter_g, fin_gb) = _sweep_call(
        [_ssd_bwd_part(xbc, small, init_ssd, lw, regions), _gla_bwd_part(proj, small, init_gla, lw, regions)],
        n_steps, "scan_backward_states")
    (y_ssd, fin_sf), (y_gla, fin_gf) = _sweep_call(
        [_ssd_fwd_part(proj, xbc_c, tab_s, init_ssd, enter_s, lw, regions),
         _gla_fwd_part(proj, small, init_gla, enter_g, rev_g, lw, regions)],
        n_steps, "scan_forward_outputs")
    return y_ssd, fin_sf, fin_sb, y_gla, fin_gf, fin_gb


def _merge_kernel(x_ref, ys_ref, yg_ref, ga_ref, gb_ref, mod_ref, wbs_ref, wbg_ref, wo_ref, o_ref):
    merged = (_sigmoid(ga_ref[...].astype(F32)) * _dot(ys_ref[...], wbs_ref[...])
              + _sigmoid(gb_ref[...].astype(F32)) * _dot(yg_ref[...], wbg_ref[...]))
    mix = _dot(merged.astype(BF16), wo_ref[...])
    o_ref[...] = x_ref[...] + mod_ref[:, 2 * D_MODEL:3 * D_MODEL] * mix


def _merge(x, y_ssd, y_gla, proj, mods, lw, regions):
    t = x.shape[0]
    tm = min(TM_MERGE, regions[0][0] * regions[0][1], regions[1][1])
    tok = lambda col: pl.BlockSpec((tm, D_MODEL), lambda i: (i, col))
    wspec = pl.BlockSpec((D_MODEL, D_MODEL), lambda i: (0, 0))
    return pl.pallas_call(
        _merge_kernel,
        out_shape=jax.ShapeDtypeStruct((t, D_MODEL), F32),
        grid=(t // tm,),
        in_specs=[tok(0), tok(0), tok(0), tok(COL_GA // D_MODEL), tok(COL_GB // D_MODEL),
                  pl.BlockSpec((None, 1, 6 * D_MODEL), lambda i: (_mod_row(i, tm, regions), 0, 0)),
                  wspec, wspec, wspec],
        out_specs=tok(0),
        compiler_params=_params("parallel"),
        name="merge_out_projection",
    )(x, y_ssd, y_gla, proj, proj, mods, lw["w_bs"], lw["w_bg"], lw["w_o"])


def _ffn_prologue(x_ref, mod_ref, g_ref):
    y = _rms(x_ref[...], g_ref[...])
    return y * (1.0 + mod_ref[:, 4 * D_MODEL:5 * D_MODEL]) + mod_ref[:, 3 * D_MODEL:4 * D_MODEL]


def _swiglu_resident(v, w1_ref, w3_ref, w2_ref):
    acc = None
    for hh in range(w1_ref.shape[1] // TH_FFN):
        cols = slice(hh * TH_FFN, (hh + 1) * TH_FFN)
        h = _silu(_dot(v, w1_ref[:, cols])) * _dot(v, w3_ref[:, cols])
        part = _dot(h.astype(BF16), w2_ref[cols, :])
        acc = part if acc is None else acc + part
    return acc


def _ffn_kernel(x_ref, mod_ref, g_ref, w1_ref, w3_ref, w2_ref, o_ref):
    v = _ffn_prologue(x_ref, mod_ref, g_ref).astype(BF16)
    o_ref[...] = x_ref[...] + mod_ref[:, 5 * D_MODEL:6 * D_MODEL] * _swiglu_resident(v, w1_ref, w3_ref, w2_ref)


def _dense_ffn(x, mods, g, w1, w3, w2, regions):
    t = x.shape[0]
    tm = min(TM_FFN, regions[0][0] * regions[0][1], regions[1][1])
    hid = w1.shape[1]
    once = pl.Buffered(1)
    return pl.pallas_call(
        _ffn_kernel,
        out_shape=jax.ShapeDtypeStruct((t, D_MODEL), F32),
        grid=(t // tm,),
        in_specs=[pl.BlockSpec((tm, D_MODEL), lambda i: (i, 0)),
                  pl.BlockSpec((None, 1, 6 * D_MODEL), lambda i: (_mod_row(i, tm, regions), 0, 0)),
                  pl.BlockSpec((1, D_MODEL), lambda i: (0, 0)),
                  pl.BlockSpec((D_MODEL, hid), lambda i: (0, 0), pipeline_mode=once),
                  pl.BlockSpec((D_MODEL, hid), lambda i: (0, 0), pipeline_mode=once),
                  pl.BlockSpec((hid, D_MODEL), lambda i: (0, 0), pipeline_mode=once)],
        out_specs=pl.BlockSpec((tm, D_MODEL), lambda i: (i, 0)),
        compiler_params=_params("parallel"),
        name="dense_swiglu",
    )(x, mods, g, w1, w3, w2)


def _top2_gates(logits):
    lane = lax.broadcasted_iota(jnp.int32, logits.shape, 1)
    lg = jnp.where(lane < N_EXPERTS, logits, -jnp.inf)
    m1 = jnp.max(lg, axis=-1, keepdims=True)
    i1 = jnp.min(jnp.where(lg == m1, lane, LANES), axis=-1, keepdims=True)
    lg2 = jnp.where(lane == i1, -jnp.inf, lg)
    m2 = jnp.max(lg2, axis=-1, keepdims=True)
    i2 = jnp.min(jnp.where(lg2 == m2, lane, LANES), axis=-1, keepdims=True)
    e2 = jnp.exp(m2 - m1)
    den = 1.0 + e2
    return jnp.where(lane == i1, 1.0 / den, 0.0) + jnp.where(lane == i2, e2 / den, 0.0)


def _lane_scalar(vec, e):
    lane = lax.broadcasted_iota(jnp.int32, vec.shape, 1)
    return jnp.sum(jnp.where(lane == e, vec, 0.0)).astype(jnp.int32)


def _n_row_blocks(n):
    return (n + MOE_ROW_BLOCK - 1) // MOE_ROW_BLOCK


def _moe_block_copy(stage_ref, sem, slot, hbm_ref, row, to_hbm):
    if not isinstance(row, int):
        row = pl.multiple_of(row, MOE_ROW_ALIGN)
    rows = hbm_ref.at[pl.ds(row, MOE_ROW_BLOCK), :]
    if to_hbm:
        return pltpu.make_async_copy(stage_ref.at[slot], rows, sem.at[slot])
    return pltpu.make_async_copy(rows, stage_ref.at[slot], sem.at[slot])


def _moe_dispatch_kernel(x_ref, mod_ref, g_ref, rw_ref, rb_ref, donor_hbm,
                         xs_hbm, gate_ref, rank_ref, off_ref, tot_ref,
                         v_ref, gate_t_ref, rank_t_ref, stage_ref, sem, run_ref, pend_ref, *, cap):
    del donor_hbm
    i = pl.program_id(0)
    tm = x_ref.shape[0]
    rb = MOE_ROW_BLOCK

    @pl.when(i == 0)
    def _():
        run_ref[...] = jnp.zeros_like(run_ref)
        pend_ref[0] = 0

    v = _ffn_prologue(x_ref, mod_ref, g_ref)
    v_ref[...] = v.astype(BF16)
    gate = _top2_gates(_dot_hilo(v, rw_ref[...]) + rb_ref[...])
    sel_f = jnp.where(gate > 0.0, 1.0, 0.0)
    sel = sel_f.astype(BF16)
    gate_ref[...] = gate
    gate_t_ref[...] = gate.T
    for blk in range(tm // rb):
        ii = lax.broadcasted_iota(jnp.int32, (rb, tm), 0) + blk * rb
        jj = lax.broadcasted_iota(jnp.int32, (rb, tm), 1)
        rank = _dot(jnp.where(jj < ii, 1.0, 0.0).astype(BF16), sel)
        rank_ref[blk * rb:(blk + 1) * rb, :] = rank
        rank_t_ref[:, blk * rb:(blk + 1) * rb] = rank.T

    cnt = jnp.sum(sel_f, axis=0, keepdims=True)
    cnt_pad = jnp.floor((cnt + (MOE_ROW_ALIGN - 1)) * (1.0 / MOE_ROW_ALIGN)) * MOE_ROW_ALIGN
    run = run_ref[...]
    off_ref[...] = run

    def wait_prev(s, carry):
        _moe_block_copy(stage_ref, sem, s, xs_hbm, 0, True).wait()
        return carry
    lax.fori_loop(0, pend_ref[0], wait_prev, 0)

    slot = jnp.int32(0)
    for e in range(N_EXPERTS):
        n_e = _lane_scalar(cnt, e)
        base = e * cap + _lane_scalar(run, e)
        g_row = gate_t_ref[e:e + 1, :]
        r_row = rank_t_ref[e:e + 1, :].astype(jnp.int32)

        def pack(b, s, base=base, g_row=g_row, r_row=r_row):
            row = lax.broadcasted_iota(jnp.int32, (rb, tm), 0) + b * rb
            onehot = jnp.where(jnp.logical_and(row == r_row, g_row > 0.0), 1.0, 0.0).astype(BF16)
            stage_ref[s] = _dot(onehot, v_ref[...]).astype(BF16)
            _moe_block_copy(stage_ref, sem, s, xs_hbm, base + b * rb, True).start()
            return s + 1
        slot = lax.fori_loop(0, _n_row_blocks(n_e), pack, slot)

    pend_ref[0] = slot
    run_ref[...] = run + cnt_pad
    tot_ref[...] = run + cnt_pad

    @pl.when(i == pl.num_programs(0) - 1)
    def _():
        def wait_last(s, carry):
            _moe_block_copy(stage_ref, sem, s, xs_hbm, 0, True).wait()
            return carry
        lax.fori_loop(0, slot, wait_last, 0)


def _moe_expert_kernel(te_ref, tr_ref, na_ref, xs_ref, w1_ref, w3_ref, w2_ref, ys_ref):
    del te_ref, tr_ref

    @pl.when(pl.program_id(0) < na_ref[0])
    def _():
        ys_ref[...] = _swiglu_resident(xs_ref[...], w1_ref, w3_ref, w2_ref).astype(ys_ref.dtype)


def _moe_combine_kernel(off_ref, x_ref, mod_ref, gate_ref, rank_ref, ys_hbm, o_ref, stage_ref, sem, *, cap):
    i = pl.program_id(0)
    tm = x_ref.shape[0]
    rb = MOE_ROW_BLOCK
    gate = gate_ref[...]
    rank = rank_ref[...]
    cnt = jnp.sum(jnp.where(gate > 0.0, 1.0, 0.0), axis=0, keepdims=True)
    n_blocks = [_n_row_blocks(_lane_scalar(cnt, e)) for e in range(N_EXPERTS)]
    bases = [e * cap + off_ref[i * LANES + e] for e in range(N_EXPERTS)]

    slot = jnp.int32(0)
    for e in range(N_EXPERTS):
        def fetch(b, s, base=bases[e]):
            _moe_block_copy(stage_ref, sem, s, ys_hbm, base + b * rb, False).start()
            return s + 1
        slot = lax.fori_loop(0, n_blocks[e], fetch, slot)

    o_ref[...] = jnp.zeros_like(o_ref)
    slot = jnp.int32(0)
    for e in range(N_EXPERTS):
        g_col = gate[:, e:e + 1]
        r_col = rank[:, e:e + 1].astype(jnp.int32)

        def combine(b, s, g_col=g_col, r_col=r_col):
            _moe_block_copy(stage_ref, sem, s, ys_hbm, 0, False).wait()
            col = lax.broadcasted_iota(jnp.int32, (tm, rb), 1) + b * rb
            onehot = jnp.where(jnp.logical_and(col == r_col, g_col > 0.0), 1.0, 0.0).astype(BF16)
            o_ref[...] += g_col * _dot(onehot, stage_ref[s])
            return s + 1
        slot = lax.fori_loop(0, n_blocks[e], combine, slot)

    o_ref[...] = x_ref[...] + mod_ref[:, 5 * D_MODEL:6 * D_MODEL] * o_ref[...]


def _moe_tile_table(totals, cap, n_steps):
    extent = totals
    n_tiles = (extent + TR_EXPERT - 1) // TR_EXPERT
    ends = jnp.cumsum(n_tiles)
    starts = ends - n_tiles
    n_active = ends[-1]
    t = jnp.minimum(jnp.arange(n_steps, dtype=jnp.int32), n_active - 1)
    e = jnp.sum((t[:, None] >= ends[None, :]).astype(jnp.int32), axis=1)
    row_block = e * (cap // TR_EXPERT) + (t - starts[e])
    return e.astype(jnp.int32), row_block.astype(jnp.int32), n_active.reshape(1).astype(jnp.int32)


def _moe_geometry(t, regions, n_e):
    tm = min(TM_MOE, regions[0][0] * regions[0][1], regions[1][1])
    n_tiles = t // tm
    cap_rows = t + n_tiles * (MOE_ROW_ALIGN - 1) + MOE_ROW_BLOCK
    cap = -(-cap_rows // TR_EXPERT) * TR_EXPERT
    return tm, n_tiles, cap


def _moe_ffn(x, mods, g, rw, rb, w1, w3, w2, regions, donor):
    t = x.shape[0]
    n_e, _, hid = w1.shape
    tm, n_tiles, cap = _moe_geometry(t, regions, n_e)
    max_slots = 2 * tm // MOE_ROW_BLOCK + n_e
    n_steps = (2 * t + n_tiles * n_e * (MOE_ROW_ALIGN - 1) + n_e * MOE_ROW_BLOCK) // TR_EXPERT + n_e
    mod_spec = pl.BlockSpec((None, 1, 6 * D_MODEL), lambda i, *_: (_mod_row(i, tm, regions), 0, 0))
    stage = [pltpu.VMEM((max_slots, MOE_ROW_BLOCK, D_MODEL), BF16), pltpu.SemaphoreType.DMA((max_slots,))]

    xs, gate, rank, offs, totals = pl.pallas_call(
        functools.partial(_moe_dispatch_kernel, cap=cap),
        out_shape=(jax.ShapeDtypeStruct((n_e * cap, D_MODEL), BF16),
                   jax.ShapeDtypeStruct((t, LANES), F32), jax.ShapeDtypeStruct((t, LANES), F32),
                   jax.ShapeDtypeStruct((n_tiles, 1, LANES), F32), jax.ShapeDtypeStruct((1, LANES), F32)),
        grid=(n_tiles,),
        in_specs=[pl.BlockSpec((tm, D_MODEL), lambda i: (i, 0)), mod_spec,
                  pl.BlockSpec((1, D_MODEL), lambda i: (0, 0)),
                  pl.BlockSpec((D_MODEL, LANES), lambda i: (0, 0)),
                  pl.BlockSpec((1, LANES), lambda i: (0, 0)),
                  pl.BlockSpec(memory_space=pl.ANY)],
        input_output_aliases={5: 0},
        out_specs=(pl.BlockSpec(memory_space=pl.ANY),
                   pl.BlockSpec((tm, LANES), lambda i: (i, 0)), pl.BlockSpec((tm, LANES), lambda i: (i, 0)),
                   pl.BlockSpec((None, 1, LANES), lambda i: (i, 0, 0)),
                   pl.BlockSpec((1, LANES), lambda i: (0, 0))),
        scratch_shapes=[pltpu.VMEM((tm, D_MODEL), BF16),
                        pltpu.VMEM((LANES, tm), F32), pltpu.VMEM((LANES, tm), F32)] + stage
        + [pltpu.VMEM((1, LANES), F32), pltpu.SMEM((1,), jnp.int32)],
        compiler_params=_params("arbitrary"),
        name="moe_dispatch",
    )(x, mods, g, rw, rb, donor)

    tile_e, tile_rb, n_active = _moe_tile_table(totals[0, :n_e].astype(jnp.int32), cap, n_steps)
    ys = pl.pallas_call(
        _moe_expert_kernel,
        out_shape=jax.ShapeDtypeStruct((n_e * cap, D_MODEL), BF16),
        grid_spec=pltpu.PrefetchScalarGridSpec(
            num_scalar_prefetch=3,
            grid=(n_steps,),
            in_specs=[pl.BlockSpec((TR_EXPERT, D_MODEL), lambda s, te, tr, na: (tr[s], 0)),
                      pl.BlockSpec((None, D_MODEL, hid), lambda s, te, tr, na: (te[s], 0, 0)),
                      pl.BlockSpec((None, D_MODEL, hid), lambda s, te, tr, na: (te[s], 0, 0)),
                      pl.BlockSpec((None, hid, D_MODEL), lambda s, te, tr, na: (te[s], 0, 0))],
            out_specs=pl.BlockSpec((TR_EXPERT, D_MODEL), lambda s, te, tr, na: (tr[s], 0))),
        input_output_aliases={3: 0},
        compiler_params=_params("arbitrary"),
        name="moe_experts",
    )(tile_e, tile_rb, n_active, xs, w1, w3, w2)

    out = pl.pallas_call(
        functools.partial(_moe_combine_kernel, cap=cap),
        out_shape=jax.ShapeDtypeStruct((t, D_MODEL), F32),
        grid_spec=pltpu.PrefetchScalarGridSpec(
            num_scalar_prefetch=1,
            grid=(n_tiles,),
            in_specs=[pl.BlockSpec((tm, D_MODEL), lambda i, off: (i, 0)), mod_spec,
                      pl.BlockSpec((tm, LANES), lambda i, off: (i, 0)),
                      pl.BlockSpec((tm, LANES), lambda i, off: (i, 0)),
                      pl.BlockSpec(memory_space=pl.ANY)],
            out_specs=pl.BlockSpec((tm, D_MODEL), lambda i, off: (i, 0)),
            scratch_shapes=stage),
        compiler_params=_params("arbitrary"),
        name="moe_combine",
    )(offs.reshape(-1).astype(jnp.int32), x, mods, gate, rank, ys)
    return out, ys


def _final_norm_kernel(x_ref, g_ref, o_ref):
    o_ref[...] = _rms(x_ref[...], g_ref[...])


def _final_norm(x, g, tok_off, n_tok):
    tm = min(TM_NORM, n_tok)
    off = tok_off // tm
    return pl.pallas_call(
        _final_norm_kernel,
        out_shape=jax.ShapeDtypeStruct((n_tok, D_MODEL), F32),
        grid=(n_tok // tm,),
        in_specs=[pl.BlockSpec((tm, D_MODEL), lambda i: (i + off, 0)),
                  pl.BlockSpec((1, D_MODEL), lambda i: (0, 0))],
        out_specs=pl.BlockSpec((tm, D_MODEL), lambda i: (i, 0)),
        compiler_params=_params("parallel"),
        name="final_norm",
    )(x, g)


CAST_BLOCK_BYTES = 6 * 1024 * 1024


def _cast_kernel(x_ref, o_ref):
    o_ref[...] = x_ref[...].astype(o_ref.dtype)


def _to_bf16(w, j):
    w3 = w.reshape(w.shape[0], -1, w.shape[-1])
    _, rows, cols = w3.shape
    fits = [r for r in range(BF16_SUBLANES, rows + 1, BF16_SUBLANES)
            if rows % r == 0 and r * cols * 4 <= CAST_BLOCK_BYTES]
    tr = max(fits)
    out = pl.pallas_call(
        _cast_kernel,
        out_shape=jax.ShapeDtypeStruct((rows, cols), BF16),
        grid=(rows // tr,),
        in_specs=[pl.BlockSpec((None, tr, cols), lambda i: (j, i, 0))],
        out_specs=pl.BlockSpec((tr, cols), lambda i: (i, 0)),
        compiler_params=_params("parallel"),
        name="weights_to_bf16",
    )(w3)
    return out.reshape(w.shape[1:])

def _grid_pos_embed(l, d):
    rows = l // GRID_W
    row = jnp.repeat(jnp.arange(rows, dtype=F32), GRID_W)
    col = jnp.tile(jnp.arange(GRID_W, dtype=F32), rows)
    quarter = d // 4
    omega = jnp.exp(-math.log(10000.0) * jnp.arange(quarter, dtype=F32) / quarter)
    er = row[:, None] * omega
    ec = col[:, None] * omega
    return jnp.concatenate([jnp.sin(er), jnp.cos(er), jnp.sin(ec), jnp.cos(ec)], axis=-1)


def _pad_lanes(a, width):
    return jnp.pad(a, [(0, 0)] * (a.ndim - 1) + [(0, width - a.shape[-1])])


def _reorder_w_in(w):
    o = np.cumsum([0, SSD_INNER, SSD_INNER + 2 * SSD_GROUPS * SSD_STATE, 2 * SSD_HEADS, GLA_KDIM, GLA_KDIM,
                   GLA_VDIM, GLA_VDIM, 2 * GLA_GATE_RANK, D_MODEL, D_MODEL])
    z, xbc, dt, qq, kk, vv, rr, lr, ga, gb = [w[:, int(o[i]):int(o[i + 1])] for i in range(10)]
    main = jnp.concatenate([z, vv, rr, ga, gb, qq, kk], axis=1).astype(BF16)
    small = _pad_lanes(jnp.concatenate([dt, lr], axis=1), LANES).astype(BF16)
    return main, xbc.astype(BF16), small


def _layer_weights(i, p):
    conv_w = jnp.pad(p["ssd_conv_w"][i], ((0, SUBLANES - SSD_CONV), (0, 0)))
    conv_b = p["ssd_conv_b"][i][None, :]
    gate_w = jnp.zeros((LANES, 2 * GLA_KDIM), F32)
    gate_w = gate_w.at[SM_LR:SM_LR + GLA_GATE_RANK, :GLA_KDIM].set(p["gla_gate_w"][i, 0])
    gate_w = gate_w.at[SM_LR + GLA_GATE_RANK:SM_LR + 2 * GLA_GATE_RANK, GLA_KDIM:].set(p["gla_gate_w"][i, 1])
    gate_w_hi = gate_w.astype(BF16)
    gate_w_lo = (gate_w - gate_w_hi.astype(F32)).astype(BF16)
    w_main, w_conv, w_small = _reorder_w_in(p["w_in"][i])
    return {
        "w_in": w_main, "w_in_conv": w_conv, "w_in_small": w_small,
        "conv_w": conv_w, "conv_b": conv_b,
        "dt_bias": _pad_lanes(p["ssd_dt_bias"][i].reshape(1, -1), LANES),
        "a_log": _pad_lanes(p["ssd_a_log"][i].reshape(1, -1), LANES),
        "d_exp": jnp.repeat(p["ssd_d"][i], SSD_HEADDIM)[None, :],
        "ssd_norm_g": p["ssd_norm_g"][i][None, :],
        "gate_w_hi": gate_w_hi, "gate_w_lo": gate_w_lo, "gate_b": p["gla_gate_b"][i].reshape(1, -1),
        "gla_norm_g": p["gla_norm_g"][i][None, :],
        "w_bs": p["w_branch_ssd"][i].astype(BF16), "w_bg": p["w_branch_gla"][i].astype(BF16),
        "w_o": p["w_out"][i].astype(BF16),
    }


def _ssd_state_in(s):
    b = s.shape[0]
    return jnp.transpose(s, (0, 1, 4, 2, 3)).reshape(b, 2, SSD_STATE, SSD_INNER)


def _ssd_state_out(f, b):
    s = jnp.stack([f, b], axis=1).reshape(-1, 2, SSD_STATE, SSD_HEADS, SSD_HEADDIM)
    return jnp.transpose(s, (0, 1, 3, 4, 2))


def _gla_state_in(s):
    b = s.shape[0]
    return jnp.transpose(s, (0, 1, 4, 2, 3)).reshape(b, 2, GLA_DV, GLA_KDIM)


def _gla_state_out(f, b):
    s = jnp.stack([f, b], axis=1).reshape(-1, 2, GLA_DV, GLA_HEADS, GLA_DK)
    return jnp.transpose(s, (0, 1, 3, 4, 2))


def _trunk(x_prompt, x_sample, state_ssd, state_gla, c, c_ctx, p):
    n0, l0, _ = x_prompt.shape
    n1, l1, _ = x_sample.shape
    regions = ((n0, l0), (n1, l1))
    depth = p["w_in"].shape[0]

    cc = jnp.zeros((MOD_ROWS, D_MODEL), F32).at[0].set(c_ctx).at[1:1 + n1].set(c)
    mods = _modulation_table(cc, p["ada_w"], p["ada_b"])[:, :, None, :]
    x = _assemble_tokens(x_prompt, x_sample, _grid_pos_embed(l1, D_MODEL), regions)

    ssd_states, gla_states = [], []
    moe_rows = None
    for i in range(depth):
        lw = _layer_weights(i, p)
        init_ssd = jnp.concatenate([jnp.zeros((n0, 2, SSD_STATE, SSD_INNER), F32),
                                    _ssd_state_in(state_ssd[:, i])], axis=0)
        init_gla = jnp.concatenate([jnp.zeros((n0, 2, GLA_DV, GLA_KDIM), F32),
                                    _gla_state_in(state_gla[:, i])], axis=0)
        proj, xbc, small = _in_projection(x, mods[i], p["norm1_g"][i][None, :], lw, regions)
        y_ssd, sf, sb, y_gla, gf, gb = _mixers(proj, xbc, small, init_ssd, init_gla, lw, regions)
        ssd_states.append(_ssd_state_out(sf[:n0], sb[:n0]))
        gla_states.append(_gla_state_out(gf[:n0], gb[:n0]))
        x = _merge(x, y_ssd, y_gla, proj, mods[i], lw, regions)
        j = i // 2
        g2 = p["norm2_g"][i][None, :]
        if i % 2 == 0:
            x = _dense_ffn(x, mods[i], g2, p["ffn_w1"][j].astype(BF16), p["ffn_w3"][j].astype(BF16),
                           p["ffn_w2"][j].astype(BF16), regions)
        else:
            if moe_rows is None:
                n_e = p["moe_w1"].shape[1]
                moe_rows = jnp.zeros((n_e * _moe_geometry(x.shape[0], regions, n_e)[2], D_MODEL), BF16)
            x, moe_rows = _moe_ffn(x, mods[i], g2, _pad_lanes(p["router_w"][j], LANES),
                                   _pad_lanes(p["router_b"][j][None, :], LANES),
                                   _to_bf16(p["moe_w1"], j), _to_bf16(p["moe_w3"], j),
                                   _to_bf16(p["moe_w2"], j), regions, moe_rows)

    gfin = p["final_norm_g"][None, :]
    y_prompt = _final_norm(x, gfin, 0, n0 * l0).reshape(n0, l0, D_MODEL)
    y_sample = _final_norm(x, gfin, n0 * l0, n1 * l1).reshape(n1, l1, D_MODEL)
    return y_prompt, y_sample, jnp.stack(ssd_states, axis=1), jnp.stack(gla_states, axis=1)


def kernel(x_prompt, x_sample, state_ssd, state_gla, c, c_ctx, ada_w, ada_b, norm1_g, norm2_g, w_in, ssd_conv_w, ssd_conv_b, ssd_dt_bias, ssd_a_log, ssd_d, ssd_norm_g, gla_gate_w, gla_gate_b, gla_norm_g, w_branch_ssd, w_branch_gla, w_out, ffn_w1, ffn_w3, ffn_w2, router_w, router_b, moe_w1, moe_w3, moe_w2, final_norm_g):
    p = dict(ada_w=ada_w, ada_b=ada_b, norm1_g=norm1_g, norm2_g=norm2_g, w_in=w_in, ssd_conv_w=ssd_conv_w,
             ssd_conv_b=ssd_conv_b, ssd_dt_bias=ssd_dt_bias, ssd_a_log=ssd_a_log, ssd_d=ssd_d,
             ssd_norm_g=ssd_norm_g, gla_gate_w=gla_gate_w, gla_gate_b=gla_gate_b, gla_norm_g=gla_norm_g,
             w_branch_ssd=w_branch_ssd, w_branch_gla=w_branch_gla, w_out=w_out, ffn_w1=ffn_w1, ffn_w3=ffn_w3,
             ffn_w2=ffn_w2, router_w=router_w, router_b=router_b, moe_w1=moe_w1, moe_w3=moe_w3, moe_w2=moe_w2,
             final_norm_g=final_norm_g)
    return _trunk(x_prompt, x_sample, state_ssd, state_gla, c, c_ctx, p)
```

```python
import functools
import math

import numpy as np
import jax
import jax.numpy as jnp
from jax import lax
from jax.experimental import pallas as pl
from jax.experimental.pallas import tpu as pltpu

F32 = jnp.float32
BF16 = jnp.bfloat16

D_MODEL = 1024
GRID_W = 64
EPS = 1e-6
SSD_HEADS = 16
SSD_HEADDIM = 64
SSD_INNER = SSD_HEADS * SSD_HEADDIM
SSD_GROUPS = 2
SSD_STATE = 64
SSD_CONV = 5
SSD_CHUNK = 128
GLA_HEADS = 8
GLA_DK = 64
GLA_DV = 128
GLA_KDIM = GLA_HEADS * GLA_DK
GLA_VDIM = GLA_HEADS * GLA_DV
GLA_GATE_RANK = 16
GLA_GATE_TAU = 16.0
GLA_CHUNK = 64
N_EXPERTS = 8

LANES = 128
SUBLANES = 8
BF16_SUBLANES = 16
VMEM_LIMIT_BYTES = 56 * 1024 * 1024

COL_Z, COL_V, COL_R, COL_GA, COL_GB, COL_QK = 0, 1024, 2048, 3072, 4096, 5120
D_PROJ = 6144
CONV_X, CONV_BC = 0, 1024
D_CONV = 1280
SM_LR = 2 * SSD_HEADS

TM_PROJ, TN_PROJ = 512, 1536
TM_MERGE = 512
TM_FFN, TH_FFN = 512, 1408
TM_MOE = 512
MOE_ROW_BLOCK = 256
MOE_ROW_ALIGN = 16
TR_EXPERT = 512
TM_NORM = 1024
TN_MODS = 1536
MOD_ROWS = 16
SCAN_STEP = 256


def _params(*sem):
    return pltpu.CompilerParams(dimension_semantics=sem, vmem_limit_bytes=VMEM_LIMIT_BYTES)


def _sigmoid(x):
    return 0.5 * jnp.tanh(0.5 * x) + 0.5


def _silu(x):
    return x * _sigmoid(x)


def _softplus(x):
    return jnp.maximum(x, 0.0) + jnp.log(1.0 + jnp.exp(-jnp.abs(x)))


def _split2(a):
    a1 = a.astype(BF16)
    a2 = (a - a1.astype(F32)).astype(BF16)
    return a1, a2


def _dot(a, b):
    return jnp.dot(a, b, preferred_element_type=F32)


def _dot_nt(a, b):
    return lax.dot_general(a, b, (((1,), (1,)), ((), ())), preferred_element_type=F32)


def _dot_tn(a, b):
    return lax.dot_general(a, b, (((0,), (0,)), ((), ())), preferred_element_type=F32)


def _dot_exact_lhs(t01, a):
    a1, a2 = _split2(a)
    return _dot(t01, a1) + _dot(t01, a2)


def _dot_hilo(a, b):
    a1, a2 = _split2(a)
    b1, b2 = _split2(b)
    return _dot(a1, b1) + _dot(a2, b1) + _dot(a1, b2)


def _block_tri(n, blk, upper):
    i = lax.broadcasted_iota(jnp.int32, (n, n), 0)
    j = lax.broadcasted_iota(jnp.int32, (n, n), 1)
    tri = (j >= i) if upper else (j <= i)
    if blk == n:
        return tri
    return jnp.logical_and(tri, (i // blk) == (j // blk))


def _rms(x, g):
    return x * lax.rsqrt(jnp.mean(x * x, axis=-1, keepdims=True) + EPS) * g


def _chunk_pos(c, q, regions):
    (n0, l0), (n1, l1) = regions
    per0, per1 = l0 // q, l1 // q
    nc0 = n0 * per0
    c1 = jnp.maximum(c - nc0, 0)
    in0 = c < nc0
    seq = jnp.where(in0, c // per0, n0 + c1 // per1)
    pos = jnp.where(in0, c % per0, c1 % per1)
    last = jnp.where(in0, per0 - 1, per1 - 1)
    return seq, pos == 0, pos == last


def _mod_row(i, tm, regions):
    (n0, l0), (n1, l1) = regions
    t1 = jnp.maximum(i * tm - n0 * l0, 0)
    return jnp.where(i * tm < n0 * l0, 0, 1 + t1 // l1)


def _assemble_kernel(xp_ref, xs_ref, pe_ref, o_ref, *, n_ctx_tiles):
    i = pl.program_id(0)

    @pl.when(i < n_ctx_tiles)
    def _():
        o_ref[...] = xp_ref[...]

    @pl.when(i >= n_ctx_tiles)
    def _():
        o_ref[...] = xs_ref[...] + pe_ref[...]


def _assemble_tokens(xp, xs, pe, regions):
    (n0, l0), (n1, l1) = regions
    tm = min(1024, l0 * n0, l1)
    t0, t1 = n0 * l0, n1 * l1
    n_ctx = t0 // tm
    pe_tiles = l1 // tm
    return pl.pallas_call(
        functools.partial(_assemble_kernel, n_ctx_tiles=n_ctx),
        out_shape=jax.ShapeDtypeStruct((t0 + t1, D_MODEL), F32),
        grid=((t0 + t1) // tm,),
        in_specs=[
            pl.BlockSpec((tm, D_MODEL), lambda i: (jnp.minimum(i, n_ctx - 1), 0)),
            pl.BlockSpec((tm, D_MODEL), lambda i: (jnp.maximum(i - n_ctx, 0), 0)),
            pl.BlockSpec((tm, D_MODEL), lambda i: (jnp.maximum(i - n_ctx, 0) % pe_tiles, 0)),
        ],
        out_specs=pl.BlockSpec((tm, D_MODEL), lambda i: (i, 0)),
        compiler_params=_params("arbitrary"),
        name="assemble_tokens",
    )(xp.reshape(t0, D_MODEL), xs.reshape(t1, D_MODEL), pe)


def _mods_kernel(cc_ref, w_ref, b_ref, o_ref):
    a = _silu(cc_ref[...])
    o_ref[...] = _dot_hilo(a, w_ref[...]) + b_ref[...]


def _modulation_table(cc, ada_w, ada_b):
    depth = ada_w.shape[0]
    n = ada_w.shape[2]
    return pl.pallas_call(
        _mods_kernel,
        out_shape=jax.ShapeDtypeStruct((depth, MOD_ROWS, n), F32),
        grid=(depth, n // TN_MODS),
        in_specs=[
            pl.BlockSpec((MOD_ROWS, D_MODEL), lambda l, j: (0, 0)),
            pl.BlockSpec((None, D_MODEL, TN_MODS), lambda l, j: (l, 0, j)),
            pl.BlockSpec((None, 1, TN_MODS), lambda l, j: (l, 0, j)),
        ],
        out_specs=pl.BlockSpec((None, MOD_ROWS, TN_MODS), lambda l, j: (l, 0, j)),
        compiler_params=_params("arbitrary", "arbitrary"),
        name="modulation_table",
    )(cc, ada_w, ada_b.reshape(depth, 1, n))


def _inproj_kernel(x_ref, mod_ref, g_ref, w_ref, wc_ref, ws_ref, o_ref, oc_ref, os_ref):
    y = _rms(x_ref[...], g_ref[...])
    u = (y * (1.0 + mod_ref[:, D_MODEL:2 * D_MODEL]) + mod_ref[:, 0:D_MODEL]).astype(BF16)
    os_ref[...] = _dot(u, ws_ref[...])
    oc_ref[...] = _dot(u, wc_ref[...]).astype(BF16)
    for c0 in range(0, D_PROJ, TN_PROJ):
        o_ref[:, c0:c0 + TN_PROJ] = _dot(u, w_ref[:, c0:c0 + TN_PROJ]).astype(BF16)


def _in_projection(x, mods, g, lw, regions):
    t = x.shape[0]
    tm = min(TM_PROJ, regions[0][0] * regions[0][1], regions[1][1])
    once = pl.Buffered(1)
    return pl.pallas_call(
        _inproj_kernel,
        out_shape=(jax.ShapeDtypeStruct((t, D_PROJ), BF16), jax.ShapeDtypeStruct((t, D_CONV), BF16),
                   jax.ShapeDtypeStruct((t, LANES), F32)),
        grid=(t // tm,),
        in_specs=[
            pl.BlockSpec((tm, D_MODEL), lambda i: (i, 0)),
            pl.BlockSpec((None, 1, 6 * D_MODEL), lambda i: (_mod_row(i, tm, regions), 0, 0)),
            pl.BlockSpec((1, D_MODEL), lambda i: (0, 0)),
            pl.BlockSpec((D_MODEL, D_PROJ), lambda i: (0, 0), pipeline_mode=once),
            pl.BlockSpec((D_MODEL, D_CONV), lambda i: (0, 0), pipeline_mode=once),
            pl.BlockSpec((D_MODEL, LANES), lambda i: (0, 0), pipeline_mode=once),
        ],
        out_specs=(pl.BlockSpec((tm, D_PROJ), lambda i: (i, 0)),
                   pl.BlockSpec((tm, D_CONV), lambda i: (i, 0)),
                   pl.BlockSpec((tm, LANES), lambda i: (i, 0))),
        compiler_params=_params("parallel"),
        name="norm_in_projection",
    )(x, mods, g, lw["w_in"], lw["w_in_conv"], lw["w_in_small"])


def _ssd_decay_terms(sm, dtb, alog, q):
    lane = lax.broadcasted_iota(jnp.int32, (1, LANES), 1)
    dtv = _softplus(sm + dtb)
    a_neg = jnp.where(lane < 2 * SSD_HEADS, -jnp.exp(alog), 0.0)
    a = dtv * a_neg
    acs = _dot_exact_lhs(_block_tri(q, q, False).astype(BF16), a)
    rev = _dot_exact_lhs(_block_tri(q, q, True).astype(BF16), a)
    m = jnp.where(lane < SSD_HEADS, acs, rev)
    tot = acs[q - 1:q, :]
    return m, tot, dtv


def _col(x, idx, width=LANES):
    return jnp.broadcast_to(x[:, idx:idx + 1], (x.shape[0], width))


def _head_expand_matrix(base):
    k = lax.broadcasted_iota(jnp.int32, (LANES, SSD_INNER), 0)
    c = lax.broadcasted_iota(jnp.int32, (LANES, SSD_INNER), 1)
    return jnp.where(k == base + c // SSD_HEADDIM, 1.0, 0.0).astype(BF16)


def _pair_cols(x, base, p, lo_half):
    return jnp.where(lo_half, _col(x, base + 2 * p), _col(x, base + 2 * p + 1))


def _expand_state(s):
    lane = lax.broadcasted_iota(jnp.int32, s.shape, 1)
    half = SSD_INNER // SSD_GROUPS
    return jnp.concatenate([jnp.where(lane < half, s, 0.0), jnp.where(lane >= half, s, 0.0)], axis=0)


def _compact_state(s2):
    return s2[0:SSD_STATE, :] + s2[SSD_STATE:2 * SSD_STATE, :]


def _state_update_mask():
    row = lax.broadcasted_iota(jnp.int32, (SSD_GROUPS * SSD_STATE, SSD_INNER), 0)
    lane = lax.broadcasted_iota(jnp.int32, (SSD_GROUPS * SSD_STATE, SSD_INNER), 1)
    return (row < SSD_STATE) == (lane < SSD_INNER // SSD_GROUPS)


HALO = BF16_SUBLANES
CONV_PAD = SSD_CONV // 2


def _ssd_bwd_kernel(xbc_ref, prev_ref, next_ref, sm_ref, cw_ref, cb_ref, dtb_ref, alog_ref, init_ref,
                    conv_ref, tab_ref, enter_ref, fin_ref, s_ref, ext_ref, *, regions, n_steps):
    q = SSD_CHUNK
    st = SCAN_STEP
    c = n_steps - 1 - pl.program_id(0)
    _, first, last = _chunk_pos(c, st, regions)

    @pl.when(last)
    def _():
        s_ref[...] = _expand_state(init_ref[...])

    ext_ref[0:HALO, :] = jnp.where(first, 0.0, prev_ref[...].astype(F32))
    ext_ref[HALO:HALO + st, :] = xbc_ref[...].astype(F32)
    ext_ref[HALO + st:2 * HALO + st, :] = jnp.where(last, 0.0, next_ref[...].astype(F32))
    acc = cb_ref[...]
    for k in range(SSD_CONV):
        off = HALO - CONV_PAD + k
        acc = acc + cw_ref[k:k + 1, :] * ext_ref[off:off + st, :]
    conv_ref[...] = _silu(acc).astype(BF16)

    lo_half = lax.broadcasted_iota(jnp.int32, (1, LANES), 1) < SSD_HEADDIM
    upd_mask = _state_update_mask()

    s_cur = s_ref[...]
    for ci in reversed(range(st // q)):
        rows = slice(ci * q, (ci + 1) * q)
        xs = conv_ref[rows, 0:SSD_INNER].astype(F32)
        bsb = conv_ref[rows, SSD_INNER:SSD_INNER + LANES]
        m, tot, dtv = _ssd_decay_terms(sm_ref[rows, :], dtb_ref[...], alog_ref[...], q)
        tab_ref[rows, 0:LANES] = m
        tab_ref[rows, LANES:2 * LANES] = dtv
        wgt = jnp.exp(tot - m) * dtv
        etot = jnp.exp(tot)
        enter_ref[ci] = _compact_state(s_cur)
        xw, dec = [], []
        for p in range(SSD_HEADS // 2):
            sl = slice(p * LANES, (p + 1) * LANES)
            xw.append((xs[:, sl] * _pair_cols(wgt, SSD_HEADS, p, lo_half)).astype(BF16))
            dec.append(_pair_cols(etot, SSD_HEADS, p, lo_half))
        upd = _dot_tn(bsb, jnp.concatenate(xw, axis=1))
        s_cur = s_cur * jnp.concatenate(dec, axis=1) + jnp.where(upd_mask, upd, 0.0)
    s_ref[...] = s_cur

    @pl.when(first)
    def _():
        fin_ref[...] = _compact_state(s_cur)


def _ssd_out_kernel(z_ref, x_ref, bc_ref, tab_ref, dexp_ref, ng_ref,
                    init_ref, enter_ref, y_ref, fin_ref, s_ref, *, regions):
    q = SSD_CHUNK
    c = pl.program_id(0)
    _, first, last = _chunk_pos(c, SCAN_STEP, regions)

    @pl.when(first)
    def _():
        s_ref[...] = _expand_state(init_ref[...])

    lane = lax.broadcasted_iota(jnp.int32, (1, LANES), 1)
    lo_half = lane < SSD_HEADDIM
    tril = _block_tri(q, q, False)
    ii = lax.broadcasted_iota(jnp.int32, (q, q), 0)
    jj = lax.broadcasted_iota(jnp.int32, (q, q), 1)
    eye = ii == jj
    upd_mask = _state_update_mask()
    expand = _head_expand_matrix(0)
    dexp = dexp_ref[...]
    ng = ng_ref[...]

    s_cur = s_ref[...]
    for ci in range(SCAN_STEP // q):
        rows = slice(ci * q, (ci + 1) * q)
        xs = x_ref[rows, :].astype(F32)
        bsb = bc_ref[rows, 0:LANES]
        csb = bc_ref[rows, LANES:2 * LANES]
        m = tab_ref[rows, 0:LANES]
        dtv = tab_ref[rows, LANES:2 * LANES]
        wgt = jnp.exp(jnp.where(lane < SSD_HEADS, m[q - 1:q, :] - m, 0.0)) * dtv
        mt = (m - jnp.log(dtv)).T
        ldt = jnp.log(dtv + pltpu.roll(dtv, LANES - SSD_HEADS, 1)).T
        zero_b = jnp.zeros_like(csb)
        cb = [_dot_nt(jnp.where(lo_half, csb, zero_b), bsb).astype(BF16),
              _dot_nt(jnp.where(lo_half, zero_b, csb), bsb).astype(BF16)]
        cs_f = _dot(csb, s_cur.astype(BF16))
        cs_b = _dot(csb, _expand_state(enter_ref[ci]).astype(BF16))

        ys, dec = [], []
        for p in range(SSD_HEADS // 2):
            sl = slice(p * LANES, (p + 1) * LANES)
            g = (2 * p) // (SSD_HEADS // SSD_GROUPS)
            xs_p = xs[:, sl]
            xs_pb = xs_p.astype(BF16)
            yd, colf, colb = [], [], []
            for h in (2 * p, 2 * p + 1):
                hb = SSD_HEADS + h
                cf = _col(m, h, q)
                cbk = _col(m, hb, q)
                seg = jnp.where(tril, cf - mt[h:h + 1, :], cbk - mt[hb:hb + 1, :])
                seg = jnp.where(eye, ldt[h:h + 1, :], seg)
                yd.append(_dot(cb[g] * jnp.exp(seg).astype(BF16), xs_pb))
                colf.append(cf)
                colb.append(cbk)
            ef = jnp.exp(jnp.where(lo_half, colf[0], colf[1]))
            eb = jnp.exp(jnp.where(lo_half, colb[0], colb[1]))
            y_p = jnp.where(lo_half, yd[0], yd[1]) + ef * cs_f[:, sl] + eb * cs_b[:, sl] + dexp[:, sl] * xs_p
            ys.append(y_p)
            dec.append(ef[q - 1:q, :])

        y = jnp.concatenate(ys, axis=1) * _silu(z_ref[rows, :].astype(F32))
        y_ref[rows, :] = _rms(y, ng).astype(y_ref.dtype)

        xw = (xs * _dot(wgt.astype(BF16), expand)).astype(BF16)
        upd = _dot_tn(bsb, xw)
        s_cur = s_cur * jnp.concatenate(dec, axis=1) + jnp.where(upd_mask, upd, 0.0)
    s_ref[...] = s_cur

    @pl.when(last)
    def _():
        fin_ref[...] = _compact_state(s_cur)


def _const_spec(a):
    return pl.BlockSpec(a.shape, lambda s: (0,) * a.ndim)


def _sweep_call(parts, n_steps, name):
    n_in = [len(p["inputs"]) for p in parts]
    n_out = [len(p["out_shape"]) for p in parts]
    n_scr = [len(p["scratch"]) for p in parts]

    def kernel(*refs):
        ins, outs, scr = refs[:sum(n_in)], refs[sum(n_in):sum(n_in) + sum(n_out)], refs[sum(n_in) + sum(n_out):]
        for k, p in enumerate(parts):
            i0, o0, s0 = sum(n_in[:k]), sum(n_out[:k]), sum(n_scr[:k])
            p["kernel"](*ins[i0:i0 + n_in[k]], *outs[o0:o0 + n_out[k]], *scr[s0:s0 + n_scr[k]])

    res = pl.pallas_call(
        kernel,
        out_shape=tuple(o for p in parts for o in p["out_shape"]),
        grid=(n_steps,),
        in_specs=[s for p in parts for s in p["in_specs"]],
        out_specs=tuple(s for p in parts for s in p["out_specs"]),
        scratch_shapes=[s for p in parts for s in p["scratch"]],
        compiler_params=_params("arbitrary"),
        name=name,
    )(*[a for p in parts for a in p["inputs"]])
    return [res[sum(n_out[:k]):sum(n_out[:k]) + n_out[k]] for k in range(len(parts))]


def _ssd_bwd_part(xbc, small, init, lw, regions):
    st = SCAN_STEP
    cps = st // SSD_CHUNK
    t = xbc.shape[0]
    n_steps = t // st
    n_seq = regions[0][0] + regions[1][0]
    bidx = lambda s: n_steps - 1 - s
    seq_b = lambda s: _chunk_pos(bidx(s), st, regions)[0]
    hb = st // HALO
    n_hb = t // HALO
    consts = [lw["conv_w"], lw["conv_b"], lw["dt_bias"], lw["a_log"]]
    return dict(
        kernel=functools.partial(_ssd_bwd_kernel, regions=regions, n_steps=n_steps),
        inputs=[xbc, xbc, xbc, small, *consts, init],
        in_specs=[pl.BlockSpec((st, D_CONV), lambda s: (bidx(s), 0)),
                  pl.BlockSpec((HALO, D_CONV), lambda s: (jnp.maximum(bidx(s) * hb - 1, 0), 0)),
                  pl.BlockSpec((HALO, D_CONV), lambda s: (jnp.minimum((bidx(s) + 1) * hb, n_hb - 1), 0)),
                  pl.BlockSpec((st, LANES), lambda s: (bidx(s), 0))]
        + [_const_spec(a) for a in consts]
        + [pl.BlockSpec((None, None, SSD_STATE, SSD_INNER), lambda s: (seq_b(s), 1, 0, 0))],
        out_shape=[jax.ShapeDtypeStruct((t, D_CONV), BF16),
                   jax.ShapeDtypeStruct((t, 2 * LANES), F32),
                   jax.ShapeDtypeStruct((n_steps * cps, SSD_STATE, SSD_INNER), F32),
                   jax.ShapeDtypeStruct((n_seq, SSD_STATE, SSD_INNER), F32)],
        out_specs=[pl.BlockSpec((st, D_CONV), lambda s: (bidx(s), 0)),
                   pl.BlockSpec((st, 2 * LANES), lambda s: (bidx(s), 0)),
                   pl.BlockSpec((cps, SSD_STATE, SSD_INNER), lambda s: (bidx(s), 0, 0)),
                   pl.BlockSpec((None, SSD_STATE, SSD_INNER), lambda s: (seq_b(s), 0, 0))],
        scratch=[pltpu.VMEM((SSD_GROUPS * SSD_STATE, SSD_INNER), F32),
                 pltpu.VMEM((st + 2 * HALO, D_CONV), F32)])


def _ssd_fwd_part(proj, xbc, tables, init, enter_b, lw, regions):
    st = SCAN_STEP
    cps = st // SSD_CHUNK
    t = proj.shape[0]
    n_seq = regions[0][0] + regions[1][0]
    seq_f = lambda s: _chunk_pos(s, st, regions)[0]
    consts = [lw["d_exp"], lw["ssd_norm_g"]]
    return dict(
        kernel=functools.partial(_ssd_out_kernel, regions=regions),
        inputs=[proj, xbc, xbc, tables, *consts, init, enter_b],
        in_specs=[pl.BlockSpec((st, SSD_INNER), lambda s: (s, COL_Z // SSD_INNER)),
                  pl.BlockSpec((st, SSD_INNER), lambda s: (s, CONV_X // SSD_INNER)),
                  pl.BlockSpec((st, 2 * LANES), lambda s: (s, CONV_BC // (2 * LANES))),
                  pl.BlockSpec((st, 2 * LANES), lambda s: (s, 0))]
        + [_const_spec(a) for a in consts]
        + [pl.BlockSpec((None, None, SSD_STATE, SSD_INNER), lambda s: (seq_f(s), 0, 0, 0)),
           pl.BlockSpec((cps, SSD_STATE, SSD_INNER), lambda s: (s, 0, 0))],
        out_shape=[jax.ShapeDtypeStruct((t, SSD_INNER), BF16),
                   jax.ShapeDtypeStruct((n_seq, SSD_STATE, SSD_INNER), F32)],
        out_specs=[pl.BlockSpec((st, SSD_INNER), lambda s: (s, 0)),
                   pl.BlockSpec((None, SSD_STATE, SSD_INNER), lambda s: (seq_f(s), 0, 0))],
        scratch=[pltpu.VMEM((SSD_GROUPS * SSD_STATE, SSD_INNER), F32)])


def _gla_log_decay(sm, gw_hi_ref, gw_lo_ref, gb_ref):
    s1, s2 = _split2(sm)
    logit = _dot(s1, gw_hi_ref[...]) + _dot(s2, gw_hi_ref[...]) + _dot(s1, gw_lo_ref[...]) + gb_ref[...]
    return -_softplus(-logit) * (1.0 / GLA_GATE_TAU)


def _chunk_rows(x, n_chunks, rows_per_chunk):
    return jnp.concatenate([jnp.broadcast_to(x[c:c + 1, :], (rows_per_chunk, x.shape[1]))
                            for c in range(n_chunks)], axis=0)


def _gla_bwd_kernel(qk_ref, v_ref, sm_ref, gwh_ref, gwl_ref, gb_ref, init_ref,
                    rev_ref, enter_ref, fin_ref, s_ref, *, regions, n_steps):
    cq = GLA_CHUNK
    st = SCAN_STEP
    ncs = st // cq
    c = n_steps - 1 - pl.program_id(0)
    _, first, last = _chunk_pos(c, st, regions)

    @pl.when(last)
    def _():
        s_ref[...] = init_ref[...]

    bwd = slice(GLA_KDIM, 2 * GLA_KDIM)
    lg = _gla_log_decay(sm_ref[...], gwh_ref.at[:, bwd], gwl_ref.at[:, bwd], gb_ref.at[:, bwd])
    rev = _dot_exact_lhs(_block_tri(st, cq, True).astype(BF16), lg)
    rev_ref[...] = rev
    tot = jnp.concatenate([rev[ci * cq:ci * cq + 1, :] for ci in range(ncs)], axis=0)
    ko = (qk_ref[:, GLA_KDIM:2 * GLA_KDIM].astype(F32) * jnp.exp(_chunk_rows(tot, ncs, cq) - rev)).astype(BF16)
    etot = jnp.exp(tot)
    lo_half = lax.broadcasted_iota(jnp.int32, (1, LANES), 1) < GLA_DK

    s_cur = s_ref[...]
    for ci in reversed(range(ncs)):
        rows = slice(ci * cq, (ci + 1) * cq)
        enter_ref[ci] = s_cur
        new = []
        for p in range(GLA_HEADS // 2):
            sl = slice(p * LANES, (p + 1) * LANES)
            u0 = _dot_tn(v_ref[rows, (2 * p) * GLA_DV:(2 * p + 1) * GLA_DV], ko[rows, sl])
            u1 = _dot_tn(v_ref[rows, (2 * p + 1) * GLA_DV:(2 * p + 2) * GLA_DV], ko[rows, sl])
            new.append(s_cur[:, sl] * etot[ci:ci + 1, sl] + jnp.where(lo_half, u0, u1))
        s_cur = jnp.concatenate(new, axis=1)
    s_ref[...] = s_cur

    @pl.when(first)
    def _():
        fin_ref[...] = s_cur


def _gla_out_kernel(qk_ref, v_ref, r_ref, sm_ref, gwh_ref, gwl_ref, gb_ref, ng_ref, init_ref, enter_ref, rev_ref,
                    y_ref, fin_ref, s_ref, *, regions):
    cq = GLA_CHUNK
    st = SCAN_STEP
    ncs = st // cq
    c = pl.program_id(0)
    _, first, last = _chunk_pos(c, st, regions)

    @pl.when(first)
    def _():
        s_ref[...] = init_ref[...]

    fwd = slice(0, GLA_KDIM)
    lg = _gla_log_decay(sm_ref[...], gwh_ref.at[:, fwd], gwl_ref.at[:, fwd], gb_ref.at[:, fwd])
    gcs = _dot_exact_lhs(_block_tri(st, cq, False).astype(BF16), lg)
    rev = rev_ref[...]
    lo_half = lax.broadcasted_iota(jnp.int32, (1, LANES), 1) < GLA_DK
    zero_b = jnp.zeros((cq, LANES), BF16)
    ii = lax.broadcasted_iota(jnp.int32, (2 * cq, cq), 0) % cq
    jj = lax.broadcasted_iota(jnp.int32, (2 * cq, cq), 1)
    tril2 = jj <= ii
    triu2 = jj >= ii
    ng = ng_ref[...]

    def stack_heads(x):
        return jnp.concatenate([jnp.where(lo_half, x, zero_b), jnp.where(lo_half, zero_b, x)], axis=0)

    s_cur = s_ref[...]
    for ci in range(ncs):
        rows = slice(ci * cq, (ci + 1) * cq)
        g_f = gcs[rows]
        g_b = rev[rows]
        etot = jnp.exp(g_f[cq - 1:cq, :])
        qs = qk_ref[rows, 0:GLA_KDIM].astype(F32) * (GLA_DK ** -0.5)
        ks = qk_ref[rows, GLA_KDIM:2 * GLA_KDIM].astype(F32)
        qe_f = (qs * jnp.exp(g_f)).astype(BF16)
        ke = ks * jnp.exp(-g_f)
        ke_f = ke.astype(BF16)
        ko_f = (ke * etot).astype(BF16)
        qe_b = (qs * jnp.exp(g_b)).astype(BF16)
        ke_b = (ks * jnp.exp(-g_b)).astype(BF16)
        s_curb = s_cur.astype(BF16)
        s_entb = enter_ref[ci].astype(BF16)
        new = []
        for p in range(GLA_HEADS // 2):
            sl = slice(p * LANES, (p + 1) * LANES)
            qf2 = stack_heads(qe_f[:, sl])
            qb2 = stack_heads(qe_b[:, sl])
            att = (jnp.where(tril2, _dot_nt(qf2, ke_f[:, sl]), 0.0)
                   + jnp.where(triu2, _dot_nt(qb2, ke_b[:, sl]), 0.0)).astype(BF16)
            inter = _dot_nt(qf2, s_curb[:, sl]) + _dot_nt(qb2, s_entb[:, sl])
            upd = []
            for hh in (0, 1):
                h = 2 * p + hh
                hs = slice(h * GLA_DV, (h + 1) * GLA_DV)
                hr = slice(hh * cq, (hh + 1) * cq)
                vh = v_ref[rows, hs]
                o = _dot(att[hr], vh) + inter[hr]
                o = o * lax.rsqrt(jnp.mean(o * o, axis=-1, keepdims=True) + EPS) * ng[:, hs]
                y_ref[rows, hs] = (o * _silu(r_ref[rows, hs].astype(F32))).astype(y_ref.dtype)
                upd.append(_dot_tn(vh, ko_f[:, sl]))
            new.append(s_cur[:, sl] * etot[:, sl] + jnp.where(lo_half, upd[0], upd[1]))
        s_cur = jnp.concatenate(new, axis=1)
    s_ref[...] = s_cur

    @pl.when(last)
    def _():
        fin_ref[...] = s_cur


def _gla_bwd_part(proj, small, init, lw, regions):
    st = SCAN_STEP
    ncs = st // GLA_CHUNK
    t = proj.shape[0]
    n_steps = t // st
    n_seq = regions[0][0] + regions[1][0]
    consts = [lw["gate_w_hi"], lw["gate_w_lo"], lw["gate_b"]]
    bidx = lambda s: n_steps - 1 - s
    seq_b = lambda s: _chunk_pos(bidx(s), st, regions)[0]
    return dict(
        kernel=functools.partial(_gla_bwd_kernel, regions=regions, n_steps=n_steps),
        inputs=[proj, proj, small, *consts, init],
        in_specs=[pl.BlockSpec((st, 2 * GLA_KDIM), lambda s: (bidx(s), COL_QK // (2 * GLA_KDIM))),
                  pl.BlockSpec((st, GLA_VDIM), lambda s: (bidx(s), COL_V // GLA_VDIM)),
                  pl.BlockSpec((st, LANES), lambda s: (bidx(s), 0))]
        + [_const_spec(a) for a in consts]
        + [pl.BlockSpec((None, None, GLA_DV, GLA_KDIM), lambda s: (seq_b(s), 1, 0, 0))],
        out_shape=[jax.ShapeDtypeStruct((t, GLA_KDIM), F32),
                   jax.ShapeDtypeStruct((n_steps * ncs, GLA_DV, GLA_KDIM), F32),
                   jax.ShapeDtypeStruct((n_seq, GLA_DV, GLA_KDIM), F32)],
        out_specs=[pl.BlockSpec((st, GLA_KDIM), lambda s: (bidx(s), 0)),
                   pl.BlockSpec((ncs, GLA_DV, GLA_KDIM), lambda s: (bidx(s), 0, 0)),
                   pl.BlockSpec((None, GLA_DV, GLA_KDIM), lambda s: (seq_b(s), 0, 0))],
        scratch=[pltpu.VMEM((GLA_DV, GLA_KDIM), F32)])


def _gla_fwd_part(proj, small, init, enter_b, rev_b, lw, regions):
    st = SCAN_STEP
    ncs = st // GLA_CHUNK
    t = proj.shape[0]
    n_seq = regions[0][0] + regions[1][0]
    consts = [lw["gate_w_hi"], lw["gate_w_lo"], lw["gate_b"], lw["gla_norm_g"]]
    seq_f = lambda s: _chunk_pos(s, st, regions)[0]
    return dict(
        kernel=functools.partial(_gla_out_kernel, regions=regions),
        inputs=[proj, proj, proj, small, *consts, init, enter_b, rev_b],
        in_specs=[pl.BlockSpec((st, 2 * GLA_KDIM), lambda s: (s, COL_QK // (2 * GLA_KDIM))),
                  pl.BlockSpec((st, GLA_VDIM), lambda s: (s, COL_V // GLA_VDIM)),
                  pl.BlockSpec((st, GLA_VDIM), lambda s: (s, COL_R // GLA_VDIM)),
                  pl.BlockSpec((st, LANES), lambda s: (s, 0))]
        + [_const_spec(a) for a in consts]
        + [pl.BlockSpec((None, None, GLA_DV, GLA_KDIM), lambda s: (seq_f(s), 0, 0, 0)),
           pl.BlockSpec((ncs, GLA_DV, GLA_KDIM), lambda s: (s, 0, 0)),
           pl.BlockSpec((st, GLA_KDIM), lambda s: (s, 0))],
        out_shape=[jax.ShapeDtypeStruct((t, GLA_VDIM), BF16),
                   jax.ShapeDtypeStruct((n_seq, GLA_DV, GLA_KDIM), F32)],
        out_specs=[pl.BlockSpec((st, GLA_VDIM), lambda s: (s, 0)),
                   pl.BlockSpec((None, GLA_DV, GLA_KDIM), lambda s: (seq_f(s), 0, 0))],
        scratch=[pltpu.VMEM((GLA_DV, GLA_KDIM), F32)])


def _mixers(proj, xbc, small, init_ssd, init_gla, lw, regions):
    n_steps = proj.shape[0] // SCAN_STEP
    (xbc_c, tab_s, enter_s, fin_sb), (rev_g, enter_g, fin_gb) = _sweep_call(
        [_ssd_bwd_part(xbc, small, init_ssd, lw, regions), _gla_bwd_part(proj, small, init_gla, lw, regions)],
        n_steps, "scan_backward_states")
    (y_ssd, fin_sf), (y_gla, fin_gf) = _sweep_call(
        [_ssd_fwd_part(proj, xbc_c, tab_s, init_ssd, enter_s, lw, regions),
         _gla_fwd_part(proj, small, init_gla, enter_g, rev_g, lw, regions)],
        n_steps, "scan_forward_outputs")
    return y_ssd, fin_sf, fin_sb, y_gla, fin_gf, fin_gb


def _merge_kernel(x_ref, ys_ref, yg_ref, ga_ref, gb_ref, mod_ref, wbs_ref, wbg_ref, wo_ref, o_ref):
    merged = (_sigmoid(ga_ref[...].astype(F32)) * _dot(ys_ref[...], wbs_ref[...])
              + _sigmoid(gb_ref[...].astype(F32)) * _dot(yg_ref[...], wbg_ref[...]))
    mix = _dot(merged.astype(BF16), wo_ref[...])
    o_ref[...] = x_ref[...] + mod_ref[:, 2 * D_MODEL:3 * D_MODEL] * mix


def _merge(x, y_ssd, y_gla, proj, mods, lw, regions):
    t = x.shape[0]
    tm = min(TM_MERGE, regions[0][0] * regions[0][1], regions[1][1])
    tok = lambda col: pl.BlockSpec((tm, D_MODEL), lambda i: (i, col))
    wspec = pl.BlockSpec((D_MODEL, D_MODEL), lambda i: (0, 0))
    return pl.pallas_call(
        _merge_kernel,
        out_shape=jax.ShapeDtypeStruct((t, D_MODEL), F32),
        grid=(t // tm,),
        in_specs=[tok(0), tok(0), tok(0), tok(COL_GA // D_MODEL), tok(COL_GB // D_MODEL),
                  pl.BlockSpec((None, 1, 6 * D_MODEL), lambda i: (_mod_row(i, tm, regions), 0, 0)),
                  wspec, wspec, wspec],
        out_specs=tok(0),
        compiler_params=_params("parallel"),
        name="merge_out_projection",
    )(x, y_ssd, y_gla, proj, proj, mods, lw["w_bs"], lw["w_bg"], lw["w_o"])


def _ffn_prologue(x_ref, mod_ref, g_ref):
    y = _rms(x_ref[...], g_ref[...])
    return y * (1.0 + mod_ref[:, 4 * D_MODEL:5 * D_MODEL]) + mod_ref[:, 3 * D_MODEL:4 * D_MODEL]


def _swiglu_resident(v, w1_ref, w3_ref, w2_ref):
    acc = None
    for hh in range(w1_ref.shape[1] // TH_FFN):
        cols = slice(hh * TH_FFN, (hh + 1) * TH_FFN)
        h = _silu(_dot(v, w1_ref[:, cols])) * _dot(v, w3_ref[:, cols])
        part = _dot(h.astype(BF16), w2_ref[cols, :])
        acc = part if acc is None else acc + part
    return acc


def _ffn_kernel(x_ref, mod_ref, g_ref, w1_ref, w3_ref, w2_ref, o_ref):
    v = _ffn_prologue(x_ref, mod_ref, g_ref).astype(BF16)
    o_ref[...] = x_ref[...] + mod_ref[:, 5 * D_MODEL:6 * D_MODEL] * _swiglu_resident(v, w1_ref, w3_ref, w2_ref)


def _dense_ffn(x, mods, g, w1, w3, w2, regions):
    t = x.shape[0]
    tm = min(TM_FFN, regions[0][0] * regions[0][1], regions[1][1])
    hid = w1.shape[1]
    once = pl.Buffered(1)
    return pl.pallas_call(
        _ffn_kernel,
        out_shape=jax.ShapeDtypeStruct((t, D_MODEL), F32),
        grid=(t // tm,),
        in_specs=[pl.BlockSpec((tm, D_MODEL), lambda i: (i, 0)),
                  pl.BlockSpec((None, 1, 6 * D_MODEL), lambda i: (_mod_row(i, tm, regions), 0, 0)),
                  pl.BlockSpec((1, D_MODEL), lambda i: (0, 0)),
                  pl.BlockSpec((D_MODEL, hid), lambda i: (0, 0), pipeline_mode=once),
                  pl.BlockSpec((D_MODEL, hid), lambda i: (0, 0), pipeline_mode=once),
                  pl.BlockSpec((hid, D_MODEL), lambda i: (0, 0), pipeline_mode=once)],
        out_specs=pl.BlockSpec((tm, D_MODEL), lambda i: (i, 0)),
        compiler_params=_params("parallel"),
        name="dense_swiglu",
    )(x, mods, g, w1, w3, w2)


def _top2_gates(logits):
    lane = lax.broadcasted_iota(jnp.int32, logits.shape, 1)
    lg = jnp.where(lane < N_EXPERTS, logits, -jnp.inf)
    m1 = jnp.max(lg, axis=-1, keepdims=True)
    i1 = jnp.min(jnp.where(lg == m1, lane, LANES), axis=-1, keepdims=True)
    lg2 = jnp.where(lane == i1, -jnp.inf, lg)
    m2 = jnp.max(lg2, axis=-1, keepdims=True)
    i2 = jnp.min(jnp.where(lg2 == m2, lane, LANES), axis=-1, keepdims=True)
    e2 = jnp.exp(m2 - m1)
    den = 1.0 + e2
    return jnp.where(lane == i1, 1.0 / den, 0.0) + jnp.where(lane == i2, e2 / den, 0.0)


def _lane_scalar(vec, e):
    lane = lax.broadcasted_iota(jnp.int32, vec.shape, 1)
    return jnp.sum(jnp.where(lane == e, vec, 0.0)).astype(jnp.int32)


def _n_row_blocks(n):
    return (n + MOE_ROW_BLOCK - 1) // MOE_ROW_BLOCK


def _moe_block_copy(stage_ref, sem, slot, hbm_ref, row, to_hbm):
    if not isinstance(row, int):
        row = pl.multiple_of(row, MOE_ROW_ALIGN)
    rows = hbm_ref.at[pl.ds(row, MOE_ROW_BLOCK), :]
    if to_hbm:
        return pltpu.make_async_copy(stage_ref.at[slot], rows, sem.at[slot])
    return pltpu.make_async_copy(rows, stage_ref.at[slot], sem.at[slot])


def _moe_dispatch_kernel(x_ref, mod_ref, g_ref, rw_ref, rb_ref, donor_hbm,
                         xs_hbm, gate_ref, rank_ref, off_ref, tot_ref,
                         v_ref, gate_t_ref, rank_t_ref, stage_ref, sem, run_ref, pend_ref, *, cap):
    del donor_hbm
    i = pl.program_id(0)
    tm = x_ref.shape[0]
    rb = MOE_ROW_BLOCK

    @pl.when(i == 0)
    def _():
        run_ref[...] = jnp.zeros_like(run_ref)
        pend_ref[0] = 0

    v = _ffn_prologue(x_ref, mod_ref, g_ref)
    v_ref[...] = v.astype(BF16)
    gate = _top2_gates(_dot_hilo(v, rw_ref[...]) + rb_ref[...])
    sel_f = jnp.where(gate > 0.0, 1.0, 0.0)
    sel = sel_f.astype(BF16)
    gate_ref[...] = gate
    gate_t_ref[...] = gate.T
    for blk in range(tm // rb):
        ii = lax.broadcasted_iota(jnp.int32, (rb, tm), 0) + blk * rb
        jj = lax.broadcasted_iota(jnp.int32, (rb, tm), 1)
        rank = _dot(jnp.where(jj < ii, 1.0, 0.0).astype(BF16), sel)
        rank_ref[blk * rb:(blk + 1) * rb, :] = rank
        rank_t_ref[:, blk * rb:(blk + 1) * rb] = rank.T

    cnt = jnp.sum(sel_f, axis=0, keepdims=True)
    cnt_pad = jnp.floor((cnt + (MOE_ROW_ALIGN - 1)) * (1.0 / MOE_ROW_ALIGN)) * MOE_ROW_ALIGN
    run = run_ref[...]
    off_ref[...] = run

    def wait_prev(s, carry):
        _moe_block_copy(stage_ref, sem, s, xs_hbm, 0, True).wait()
        return carry
    lax.fori_loop(0, pend_ref[0], wait_prev, 0)

    slot = jnp.int32(0)
    for e in range(N_EXPERTS):
        n_e = _lane_scalar(cnt, e)
        base = e * cap + _lane_scalar(run, e)
        g_row = gate_t_ref[e:e + 1, :]
        r_row = rank_t_ref[e:e + 1, :].astype(jnp.int32)

        def pack(b, s, base=base, g_row=g_row, r_row=r_row):
            row = lax.broadcasted_iota(jnp.int32, (rb, tm), 0) + b * rb
            onehot = jnp.where(jnp.logical_and(row == r_row, g_row > 0.0), 1.0, 0.0).astype(BF16)
            stage_ref[s] = _dot(onehot, v_ref[...]).astype(BF16)
            _moe_block_copy(stage_ref, sem, s, xs_hbm, base + b * rb, True).start()
            return s + 1
        slot = lax.fori_loop(0, _n_row_blocks(n_e), pack, slot)

    pend_ref[0] = slot
    run_ref[...] = run + cnt_pad
    tot_ref[...] = run + cnt_pad

    @pl.when(i == pl.num_programs(0) - 1)
    def _():
        def wait_last(s, carry):
            _moe_block_copy(stage_ref, sem, s, xs_hbm, 0, True).wait()
            return carry
        lax.fori_loop(0, slot, wait_last, 0)


def _moe_expert_kernel(te_ref, tr_ref, na_ref, xs_ref, w1_ref, w3_ref, w2_ref, ys_ref):
    del te_ref, tr_ref

    @pl.when(pl.program_id(0) < na_ref[0])
    def _():
        ys_ref[...] = _swiglu_resident(xs_ref[...], w1_ref, w3_ref, w2_ref).astype(ys_ref.dtype)


def _moe_combine_kernel(off_ref, x_ref, mod_ref, gate_ref, rank_ref, ys_hbm, o_ref, stage_ref, sem, *, cap):
    i = pl.program_id(0)
    tm = x_ref.shape[0]
    rb = MOE_ROW_BLOCK
    gate = gate_ref[...]
    rank = rank_ref[...]
    cnt = jnp.sum(jnp.where(gate > 0.0, 1.0, 0.0), axis=0, keepdims=True)
    n_blocks = [_n_row_blocks(_lane_scalar(cnt, e)) for e in range(N_EXPERTS)]
    bases = [e * cap + off_ref[i * LANES + e] for e in range(N_EXPERTS)]

    slot = jnp.int32(0)
    for e in range(N_EXPERTS):
        def fetch(b, s, base=bases[e]):
            _moe_block_copy(stage_ref, sem, s, ys_hbm, base + b * rb, False).start()
            return s + 1
        slot = lax.fori_loop(0, n_blocks[e], fetch, slot)

    o_ref[...] = jnp.zeros_like(o_ref)
    slot = jnp.int32(0)
    for e in range(N_EXPERTS):
        g_col = gate[:, e:e + 1]
        r_col = rank[:, e:e + 1].astype(jnp.int32)

        def combine(b, s, g_col=g_col, r_col=r_col):
            _moe_block_copy(stage_ref, sem, s, ys_hbm, 0, False).wait()
            col = lax.broadcasted_iota(jnp.int32, (tm, rb), 1) + b * rb
            onehot = jnp.where(jnp.logical_and(col == r_col, g_col > 0.0), 1.0, 0.0).astype(BF16)
            o_ref[...] += g_col * _dot(onehot, stage_ref[s])
            return s + 1
        slot = lax.fori_loop(0, n_blocks[e], combine, slot)

    o_ref[...] = x_ref[...] + mod_ref[:, 5 * D_MODEL:6 * D_MODEL] * o_ref[...]


def _moe_tile_table(totals, cap, n_steps):
    extent = totals
    n_tiles = (extent + TR_EXPERT - 1) // TR_EXPERT
    ends = jnp.cumsum(n_tiles)
    starts = ends - n_tiles
    n_active = ends[-1]
    t = jnp.minimum(jnp.arange(n_steps, dtype=jnp.int32), n_active - 1)
    e = jnp.sum((t[:, None] >= ends[None, :]).astype(jnp.int32), axis=1)
    row_block = e * (cap // TR_EXPERT) + (t - starts[e])
    return e.astype(jnp.int32), row_block.astype(jnp.int32), n_active.reshape(1).astype(jnp.int32)


def _moe_geometry(t, regions, n_e):
    tm = min(TM_MOE, regions[0][0] * regions[0][1], regions[1][1])
    n_tiles = t // tm
    cap_rows = t + n_tiles * (MOE_ROW_ALIGN - 1) + MOE_ROW_BLOCK
    cap = -(-cap_rows // TR_EXPERT) * TR_EXPERT
    return tm, n_tiles, cap


def _moe_ffn(x, mods, g, rw, rb, w1, w3, w2, regions, donor):
    t = x.shape[0]
    n_e, _, hid = w1.shape
    tm, n_tiles, cap = _moe_geometry(t, regions, n_e)
    max_slots = 2 * tm // MOE_ROW_BLOCK + n_e
    n_steps = (2 * t + n_tiles * n_e * (MOE_ROW_ALIGN - 1) + n_e * MOE_ROW_BLOCK) // TR_EXPERT + n_e
    mod_spec = pl.BlockSpec((None, 1, 6 * D_MODEL), lambda i, *_: (_mod_row(i, tm, regions), 0, 0))
    stage = [pltpu.VMEM((max_slots, MOE_ROW_BLOCK, D_MODEL), BF16), pltpu.SemaphoreType.DMA((max_slots,))]

    xs, gate, rank, offs, totals = pl.pallas_call(
        functools.partial(_moe_dispatch_kernel, cap=cap),
        out_shape=(jax.ShapeDtypeStruct((n_e * cap, D_MODEL), BF16),
                   jax.ShapeDtypeStruct((t, LANES), F32), jax.ShapeDtypeStruct((t, LANES), F32),
                   jax.ShapeDtypeStruct((n_tiles, 1, LANES), F32), jax.ShapeDtypeStruct((1, LANES), F32)),
        grid=(n_tiles,),
        in_specs=[pl.BlockSpec((tm, D_MODEL), lambda i: (i, 0)), mod_spec,
                  pl.BlockSpec((1, D_MODEL), lambda i: (0, 0)),
                  pl.BlockSpec((D_MODEL, LANES), lambda i: (0, 0)),
                  pl.BlockSpec((1, LANES), lambda i: (0, 0)),
                  pl.BlockSpec(memory_space=pl.ANY)],
        input_output_aliases={5: 0},
        out_specs=(pl.BlockSpec(memory_space=pl.ANY),
                   pl.BlockSpec((tm, LANES), lambda i: (i, 0)), pl.BlockSpec((tm, LANES), lambda i: (i, 0)),
                   pl.BlockSpec((None, 1, LANES), lambda i: (i, 0, 0)),
                   pl.BlockSpec((1, LANES), lambda i: (0, 0))),
        scratch_shapes=[pltpu.VMEM((tm, D_MODEL), BF16),
                        pltpu.VMEM((LANES, tm), F32), pltpu.VMEM((LANES, tm), F32)] + stage
        + [pltpu.VMEM((1, LANES), F32), pltpu.SMEM((1,), jnp.int32)],
        compiler_params=_params("arbitrary"),
        name="moe_dispatch",
    )(x, mods, g, rw, rb, donor)

    tile_e, tile_rb, n_active = _moe_tile_table(totals[0, :n_e].astype(jnp.int32), cap, n_steps)
    ys = pl.pallas_call(
        _moe_expert_kernel,
        out_shape=jax.ShapeDtypeStruct((n_e * cap, D_MODEL), BF16),
        grid_spec=pltpu.PrefetchScalarGridSpec(
            num_scalar_prefetch=3,
            grid=(n_steps,),
            in_specs=[pl.BlockSpec((TR_EXPERT, D_MODEL), lambda s, te, tr, na: (tr[s], 0)),
                      pl.BlockSpec((None, D_MODEL, hid), lambda s, te, tr, na: (te[s], 0, 0)),
                      pl.BlockSpec((None, D_MODEL, hid), lambda s, te, tr, na: (te[s], 0, 0)),
                      pl.BlockSpec((None, hid, D_MODEL), lambda s, te, tr, na: (te[s], 0, 0))],
            out_specs=pl.BlockSpec((TR_EXPERT, D_MODEL), lambda s, te, tr, na: (tr[s], 0))),
        input_output_aliases={3: 0},
        compiler_params=_params("arbitrary"),
        name="moe_experts",
    )(tile_e, tile_rb, n_active, xs, w1, w3, w2)

    out = pl.pallas_call(
        functools.partial(_moe_combine_kernel, cap=cap),
        out_shape=jax.ShapeDtypeStruct((t, D_MODEL), F32),
        grid_spec=pltpu.PrefetchScalarGridSpec(
            num_scalar_prefetch=1,
            grid=(n_tiles,),
            in_specs=[pl.BlockSpec((tm, D_MODEL), lambda i, off: (i, 0)), mod_spec,
                      pl.BlockSpec((tm, LANES), lambda i, off: (i, 0)),
                      pl.BlockSpec((tm, LANES), lambda i, off: (i, 0)),
                      pl.BlockSpec(memory_space=pl.ANY)],
            out_specs=pl.BlockSpec((tm, D_MODEL), lambda i, off: (i, 0)),
            scratch_shapes=stage),
        compiler_params=_params("arbitrary"),
        name="moe_combine",
    )(offs.reshape(-1).astype(jnp.int32), x, mods, gate, rank, ys)
    return out, ys


def _final_norm_kernel(x_ref, g_ref, o_ref):
    o_ref[...] = _rms(x_ref[...], g_ref[...])


def _final_norm(x, g, tok_off, n_tok):
    tm = min(TM_NORM, n_tok)
    off = tok_off // tm
    return pl.pallas_call(
        _final_norm_kernel,
        out_shape=jax.ShapeDtypeStruct((n_tok, D_MODEL), F32),
        grid=(n_tok // tm,),
        in_specs=[pl.BlockSpec((tm, D_MODEL), lambda i: (i + off, 0)),
                  pl.BlockSpec((1, D_MODEL), lambda i: (0, 0))],
        out_specs=pl.BlockSpec((tm, D_MODEL), lambda i: (i, 0)),
        compiler_params=_params("parallel"),
        name="final_norm",
    )(x, g)


CAST_BLOCK_BYTES = 6 * 1024 * 1024


def _cast_kernel(x_ref, o_ref):
    o_ref[...] = x_ref[...].astype(o_ref.dtype)


def _to_bf16(w, j):
    w3 = w.reshape(w.shape[0], -1, w.shape[-1])
    _, rows, cols = w3.shape
    fits = [r for r in range(BF16_SUBLANES, rows + 1, BF16_SUBLANES)
            if rows % r == 0 and r * cols * 4 <= CAST_BLOCK_BYTES]
    tr = max(fits)
    out = pl.pallas_call(
        _cast_kernel,
        out_shape=jax.ShapeDtypeStruct((rows, cols), BF16),
        grid=(rows // tr,),
        in_specs=[pl.BlockSpec((None, tr, cols), lambda i: (j, i, 0))],
        out_specs=pl.BlockSpec((tr, cols), lambda i: (i, 0)),
        compiler_params=_params("parallel"),
        name="weights_to_bf16",
    )(w3)
    return out.reshape(w.shape[1:])

def _grid_pos_embed(l, d):
    rows = l // GRID_W
    row = jnp.repeat(jnp.arange(rows, dtype=F32), GRID_W)
    col = jnp.tile(jnp.arange(GRID_W, dtype=F32), rows)
    quarter = d // 4
    omega = jnp.exp(-math.log(10000.0) * jnp.arange(quarter, dtype=F32) / quarter)
    er = row[:, None] * omega
    ec = col[:, None] * omega
    return jnp.concatenate([jnp.sin(er), jnp.cos(er), jnp.sin(ec), jnp.cos(ec)], axis=-1)


def _pad_lanes(a, width):
    return jnp.pad(a, [(0, 0)] * (a.ndim - 1) + [(0, width - a.shape[-1])])


def _reorder_w_in(w):
    o = np.cumsum([0, SSD_INNER, SSD_INNER + 2 * SSD_GROUPS * SSD_STATE, 2 * SSD_HEADS, GLA_KDIM, GLA_KDIM,
                   GLA_VDIM, GLA_VDIM, 2 * GLA_GATE_RANK, D_MODEL, D_MODEL])
    z, xbc, dt, qq, kk, vv, rr, lr, ga, gb = [w[:, int(o[i]):int(o[i + 1])] for i in range(10)]
    main = jnp.concatenate([z, vv, rr, ga, gb, qq, kk], axis=1).astype(BF16)
    small = _pad_lanes(jnp.concatenate([dt, lr], axis=1), LANES).astype(BF16)
    return main, xbc.astype(BF16), small


def _layer_weights(i, p):
    conv_w = jnp.pad(p["ssd_conv_w"][i], ((0, SUBLANES - SSD_CONV), (0, 0)))
    conv_b = p["ssd_conv_b"][i][None, :]
    gate_w = jnp.zeros((LANES, 2 * GLA_KDIM), F32)
    gate_w = gate_w.at[SM_LR:SM_LR + GLA_GATE_RANK, :GLA_KDIM].set(p["gla_gate_w"][i, 0])
    gate_w = gate_w.at[SM_LR + GLA_GATE_RANK:SM_LR + 2 * GLA_GATE_RANK, GLA_KDIM:].set(p["gla_gate_w"][i, 1])
    gate_w_hi = gate_w.astype(BF16)
    gate_w_lo = (gate_w - gate_w_hi.astype(F32)).astype(BF16)
    w_main, w_conv, w_small = _reorder_w_in(p["w_in"][i])
    return {
        "w_in": w_main, "w_in_conv": w_conv, "w_in_small": w_small,
        "conv_w": conv_w, "conv_b": conv_b,
        "dt_bias": _pad_lanes(p["ssd_dt_bias"][i].reshape(1, -1), LANES),
        "a_log": _pad_lanes(p["ssd_a_log"][i].reshape(1, -1), LANES),
        "d_exp": jnp.repeat(p["ssd_d"][i], SSD_HEADDIM)[None, :],
        "ssd_norm_g": p["ssd_norm_g"][i][None, :],
        "gate_w_hi": gate_w_hi, "gate_w_lo": gate_w_lo, "gate_b": p["gla_gate_b"][i].reshape(1, -1),
        "gla_norm_g": p["gla_norm_g"][i][None, :],
        "w_bs": p["w_branch_ssd"][i].astype(BF16), "w_bg": p["w_branch_gla"][i].astype(BF16),
        "w_o": p["w_out"][i].astype(BF16),
    }


def _ssd_state_in(s):
    b = s.shape[0]
    return jnp.transpose(s, (0, 1, 4, 2, 3)).reshape(b, 2, SSD_STATE, SSD_INNER)


def _ssd_state_out(f, b):
    s = jnp.stack([f, b], axis=1).reshape(-1, 2, SSD_STATE, SSD_HEADS, SSD_HEADDIM)
    return jnp.transpose(s, (0, 1, 3, 4, 2))


def _gla_state_in(s):
    b = s.shape[0]
    return jnp.transpose(s, (0, 1, 4, 2, 3)).reshape(b, 2, GLA_DV, GLA_KDIM)


def _gla_state_out(f, b):
    s = jnp.stack([f, b], axis=1).reshape(-1, 2, GLA_DV, GLA_HEADS, GLA_DK)
    return jnp.transpose(s, (0, 1, 3, 4, 2))


def _trunk(x_prompt, x_sample, state_ssd, state_gla, c, c_ctx, p):
    n0, l0, _ = x_prompt.shape
    n1, l1, _ = x_sample.shape
    regions = ((n0, l0), (n1, l1))
    depth = p["w_in"].shape[0]

    cc = jnp.zeros((MOD_ROWS, D_MODEL), F32).at[0].set(c_ctx).at[1:1 + n1].set(c)
    mods = _modulation_table(cc, p["ada_w"], p["ada_b"])[:, :, None, :]
    x = _assemble_tokens(x_prompt, x_sample, _grid_pos_embed(l1, D_MODEL), regions)

    ssd_states, gla_states = [], []
    moe_rows = None
    for i in range(depth):
        lw = _layer_weights(i, p)
        init_ssd = jnp.concatenate([jnp.zeros((n0, 2, SSD_STATE, SSD_INNER), F32),
                                    _ssd_state_in(state_ssd[:, i])], axis=0)
        init_gla = jnp.concatenate([jnp.zeros((n0, 2, GLA_DV, GLA_KDIM), F32),
                                    _gla_state_in(state_gla[:, i])], axis=0)
        proj, xbc, small = _in_projection(x, mods[i], p["norm1_g"][i][None, :], lw, regions)
        y_ssd, sf, sb, y_gla, gf, gb = _mixers(proj, xbc, small, init_ssd, init_gla, lw, regions)
        ssd_states.append(_ssd_state_out(sf[:n0], sb[:n0]))
        gla_states.append(_gla_state_out(gf[:n0], gb[:n0]))
        x = _merge(x, y_ssd, y_gla, proj, mods[i], lw, regions)
        j = i // 2
        g2 = p["norm2_g"][i][None, :]
        if i % 2 == 0:
            x = _dense_ffn(x, mods[i], g2, p["ffn_w1"][j].astype(BF16), p["ffn_w3"][j].astype(BF16),
                           p["ffn_w2"][j].astype(BF16), regions)
        else:
            if moe_rows is None:
                n_e = p["moe_w1"].shape[1]
                moe_rows = jnp.zeros((n_e * _moe_geometry(x.shape[0], regions, n_e)[2], D_MODEL), BF16)
            x, moe_rows = _moe_ffn(x, mods[i], g2, _pad_lanes(p["router_w"][j], LANES),
                                   _pad_lanes(p["router_b"][j][None, :], LANES),
                                   _to_bf16(p["moe_w1"], j), _to_bf16(p["moe_w3"], j),
                                   _to_bf16(p["moe_w2"], j), regions, moe_rows)

    gfin = p["final_norm_g"][None, :]
    y_prompt = _final_norm(x, gfin, 0, n0 * l0).reshape(n0, l0, D_MODEL)
    y_sample = _final_norm(x, gfin, n0 * l0, n1 * l1).reshape(n1, l1, D_MODEL)
    return y_prompt, y_sample, jnp.stack(ssd_states, axis=1), jnp.stack(gla_states, axis=1)


def kernel(x_prompt, x_sample, state_ssd, state_gla, c, c_ctx, ada_w, ada_b, norm1_g, norm2_g, w_in, ssd_conv_w, ssd_conv_b, ssd_dt_bias, ssd_a_log, ssd_d, ssd_norm_g, gla_gate_w, gla_gate_b, gla_norm_g, w_branch_ssd, w_branch_gla, w_out, ffn_w1, ffn_w3, ffn_w2, router_w, router_b, moe_w1, moe_w3, moe_w2, final_norm_g):
    p = dict(ada_w=ada_w, ada_b=ada_b, norm1_g=norm1_g, norm2_g=norm2_g, w_in=w_in, ssd_conv_w=ssd_conv_w,
             ssd_conv_b=ssd_conv_b, ssd_dt_bias=ssd_dt_bias, ssd_a_log=ssd_a_log, ssd_d=ssd_d,
             ssd_norm_g=ssd_norm_g, gla_gate_w=gla_gate_w, gla_gate_b=gla_gate_b, gla_norm_g=gla_norm_g,
             w_branch_ssd=w_branch_ssd, w_branch_gla=w_branch_gla, w_out=w_out, ffn_w1=ffn_w1, ffn_w3=ffn_w3,
             ffn_w2=ffn_w2, router_w=router_w, router_b=router_b, moe_w1=moe_w1, moe_w3=moe_w3, moe_w2=moe_w2,
             final_norm_g=final_norm_g)
    return _trunk(x_prompt, x_sample, state_ssd, state_gla, c, c_ctx, p)
```

```python
import functools
import math

import numpy as np
import jax
import jax.numpy as jnp
from jax import lax
from jax.experimental import pallas as pl
from jax.experimental.pallas import tpu as pltpu

F32 = jnp.float32
BF16 = jnp.bfloat16

D_MODEL = 1024
GRID_W = 64
EPS = 1e-6
SSD_HEADS = 16
SSD_HEADDIM = 64
SSD_INNER = SSD_HEADS * SSD_HEADDIM
SSD_GROUPS = 2
SSD_STATE = 64
SSD_CONV = 5
SSD_CHUNK = 128
GLA_HEADS = 8
GLA_DK = 64
GLA_DV = 128
GLA_KDIM = GLA_HEADS * GLA_DK
GLA_VDIM = GLA_HEADS * GLA_DV
GLA_GATE_RANK = 16
GLA_GATE_TAU = 16.0
GLA_CHUNK = 64
N_EXPERTS = 8

LANES = 128
SUBLANES = 8
BF16_SUBLANES = 16
VMEM_LIMIT_BYTES = 56 * 1024 * 1024

COL_Z, COL_V, COL_R, COL_GA, COL_GB, COL_QK = 0, 1024, 2048, 3072, 4096, 5120
D_PROJ = 6144
CONV_X, CONV_BC = 0, 1024
D_CONV = 1280
SM_LR = 2 * SSD_HEADS

TM_PROJ, TN_PROJ = 512, 1536
TM_MERGE = 512
TM_FFN, TH_FFN = 512, 2816
TM_MOE = 512
MOE_ROW_BLOCK = 256
MOE_ROW_ALIGN = BF16_SUBLANES
TR_EXPERT = 512
TM_NORM = 1024
TN_MODS = 1536
MOD_ROWS = 16
SCAN_STEP = 256


def _params(*sem):
    return pltpu.CompilerParams(dimension_semantics=sem, vmem_limit_bytes=VMEM_LIMIT_BYTES)


def _sigmoid(x):
    return 0.5 * jnp.tanh(0.5 * x) + 0.5


def _silu(x):
    return x * _sigmoid(x)


def _softplus(x):
    return jnp.maximum(x, 0.0) + jnp.log(1.0 + jnp.exp(-jnp.abs(x)))


def _split2(a):
    a1 = a.astype(BF16)
    a2 = (a - a1.astype(F32)).astype(BF16)
    return a1, a2


def _dot(a, b):
    return jnp.dot(a, b, preferred_element_type=F32)


def _dot_nt(a, b):
    return lax.dot_general(a, b, (((1,), (1,)), ((), ())), preferred_element_type=F32)


def _dot_tn(a, b):
    return lax.dot_general(a, b, (((0,), (0,)), ((), ())), preferred_element_type=F32)


def _dot_exact_lhs(t01, a):
    a1, a2 = _split2(a)
    return _dot(t01, a1) + _dot(t01, a2)


def _dot_hilo(a, b):
    a1, a2 = _split2(a)
    b1, b2 = _split2(b)
    return _dot(a1, b1) + _dot(a2, b1) + _dot(a1, b2)


def _block_tri(n, blk, upper):
    i = lax.broadcasted_iota(jnp.int32, (n, n), 0)
    j = lax.broadcasted_iota(jnp.int32, (n, n), 1)
    tri = (j >= i) if upper else (j <= i)
    if blk == n:
        return tri
    return jnp.logical_and(tri, (i // blk) == (j // blk))


def _rms(x, g):
    return x * lax.rsqrt(jnp.mean(x * x, axis=-1, keepdims=True) + EPS) * g


def _chunk_pos(c, q, regions):
    (n0, l0), (n1, l1) = regions
    per0, per1 = l0 // q, l1 // q
    nc0 = n0 * per0
    c1 = jnp.maximum(c - nc0, 0)
    in0 = c < nc0
    seq = jnp.where(in0, c // per0, n0 + c1 // per1)
    pos = jnp.where(in0, c % per0, c1 % per1)
    last = jnp.where(in0, per0 - 1, per1 - 1)
    return seq, pos == 0, pos == last


def _mod_row(i, tm, regions):
    (n0, l0), (n1, l1) = regions
    t1 = jnp.maximum(i * tm - n0 * l0, 0)
    return jnp.where(i * tm < n0 * l0, 0, 1 + t1 // l1)


def _assemble_kernel(xp_ref, xs_ref, pe_ref, o_ref, *, n_ctx_tiles):
    i = pl.program_id(0)

    @pl.when(i < n_ctx_tiles)
    def _():
        o_ref[...] = xp_ref[...]

    @pl.when(i >= n_ctx_tiles)
    def _():
        o_ref[...] = xs_ref[...] + pe_ref[...]


def _assemble_tokens(xp, xs, pe, regions):
    (n0, l0), (n1, l1) = regions
    tm = min(1024, l0 * n0, l1)
    t0, t1 = n0 * l0, n1 * l1
    n_ctx = t0 // tm
    pe_tiles = l1 // tm
    return pl.pallas_call(
        functools.partial(_assemble_kernel, n_ctx_tiles=n_ctx),
        out_shape=jax.ShapeDtypeStruct((t0 + t1, D_MODEL), F32),
        grid=((t0 + t1) // tm,),
        in_specs=[
            pl.BlockSpec((tm, D_MODEL), lambda i: (jnp.minimum(i, n_ctx - 1), 0)),
            pl.BlockSpec((tm, D_MODEL), lambda i: (jnp.maximum(i - n_ctx, 0), 0)),
            pl.BlockSpec((tm, D_MODEL), lambda i: (jnp.maximum(i - n_ctx, 0) % pe_tiles, 0)),
        ],
        out_specs=pl.BlockSpec((tm, D_MODEL), lambda i: (i, 0)),
        compiler_params=_params("arbitrary"),
        name="assemble_tokens",
    )(xp.reshape(t0, D_MODEL), xs.reshape(t1, D_MODEL), pe)


def _mods_kernel(cc_ref, w_ref, b_ref, o_ref):
    a = _silu(cc_ref[...])
    o_ref[...] = _dot_hilo(a, w_ref[...]) + b_ref[...]


def _modulation_table(cc, ada_w, ada_b):
    depth = ada_w.shape[0]
    n = ada_w.shape[2]
    return pl.pallas_call(
        _mods_kernel,
        out_shape=jax.ShapeDtypeStruct((depth, MOD_ROWS, n), F32),
        grid=(depth, n // TN_MODS),
        in_specs=[
            pl.BlockSpec((MOD_ROWS, D_MODEL), lambda l, j: (0, 0)),
            pl.BlockSpec((None, D_MODEL, TN_MODS), lambda l, j: (l, 0, j)),
            pl.BlockSpec((None, 1, TN_MODS), lambda l, j: (l, 0, j)),
        ],
        out_specs=pl.BlockSpec((None, MOD_ROWS, TN_MODS), lambda l, j: (l, 0, j)),
        compiler_params=_params("arbitrary", "arbitrary"),
        name="modulation_table",
    )(cc, ada_w, ada_b.reshape(depth, 1, n))


def _inproj_kernel(x_ref, mod_ref, g_ref, w_ref, wc_ref, ws_ref, o_ref, oc_ref, os_ref):
    y = _rms(x_ref[...], g_ref[...])
    u = (y * (1.0 + mod_ref[:, D_MODEL:2 * D_MODEL]) + mod_ref[:, 0:D_MODEL]).astype(BF16)
    os_ref[...] = _dot(u, ws_ref[...])
    oc_ref[...] = _dot(u, wc_ref[...]).astype(BF16)
    for c0 in range(0, D_PROJ, TN_PROJ):
        o_ref[:, c0:c0 + TN_PROJ] = _dot(u, w_ref[:, c0:c0 + TN_PROJ]).astype(BF16)


def _in_projection(x, mods, g, lw, regions):
    t = x.shape[0]
    tm = min(TM_PROJ, regions[0][0] * regions[0][1], regions[1][1])
    once = pl.Buffered(1)
    return pl.pallas_call(
        _inproj_kernel,
        out_shape=(jax.ShapeDtypeStruct((t, D_PROJ), BF16), jax.ShapeDtypeStruct((t, D_CONV), BF16),
                   jax.ShapeDtypeStruct((t, LANES), F32)),
        grid=(t // tm,),
        in_specs=[
            pl.BlockSpec((tm, D_MODEL), lambda i: (i, 0)),
            pl.BlockSpec((None, 1, 6 * D_MODEL), lambda i: (_mod_row(i, tm, regions), 0, 0)),
            pl.BlockSpec((1, D_MODEL), lambda i: (0, 0)),
            pl.BlockSpec((D_MODEL, D_PROJ), lambda i: (0, 0), pipeline_mode=once),
            pl.BlockSpec((D_MODEL, D_CONV), lambda i: (0, 0), pipeline_mode=once),
            pl.BlockSpec((D_MODEL, LANES), lambda i: (0, 0), pipeline_mode=once),
        ],
        out_specs=(pl.BlockSpec((tm, D_PROJ), lambda i: (i, 0)),
                   pl.BlockSpec((tm, D_CONV), lambda i: (i, 0)),
                   pl.BlockSpec((tm, LANES), lambda i: (i, 0))),
        compiler_params=_params("parallel"),
        name="norm_in_projection",
    )(x, mods, g, lw["w_in"], lw["w_in_conv"], lw["w_in_small"])


def _ssd_decay_terms(sm, dtb, alog, q):
    lane = lax.broadcasted_iota(jnp.int32, (1, LANES), 1)
    dtv = _softplus(sm + dtb)
    a_neg = jnp.where(lane < 2 * SSD_HEADS, -jnp.exp(alog), 0.0)
    a = dtv * a_neg
    acs = _dot_exact_lhs(_block_tri(q, q, False).astype(BF16), a)
    rev = _dot_exact_lhs(_block_tri(q, q, True).astype(BF16), a)
    m = jnp.where(lane < SSD_HEADS, acs, rev)
    tot = acs[q - 1:q, :]
    return m, tot, dtv


def _col(x, idx, width=LANES):
    return jnp.broadcast_to(x[:, idx:idx + 1], (x.shape[0], width))


def _head_expand_matrix(base):
    k = lax.broadcasted_iota(jnp.int32, (LANES, SSD_INNER), 0)
    c = lax.broadcasted_iota(jnp.int32, (LANES, SSD_INNER), 1)
    return jnp.where(k == base + c // SSD_HEADDIM, 1.0, 0.0).astype(BF16)


def _pair_cols(x, base, p, lo_half):
    return jnp.where(lo_half, _col(x, base + 2 * p), _col(x, base + 2 * p + 1))


def _expand_state(s):
    lane = lax.broadcasted_iota(jnp.int32, s.shape, 1)
    half = SSD_INNER // SSD_GROUPS
    return jnp.concatenate([jnp.where(lane < half, s, 0.0), jnp.where(lane >= half, s, 0.0)], axis=0)


def _compact_state(s2):
    return s2[0:SSD_STATE, :] + s2[SSD_STATE:2 * SSD_STATE, :]


def _state_update_mask():
    row = lax.broadcasted_iota(jnp.int32, (SSD_GROUPS * SSD_STATE, SSD_INNER), 0)
    lane = lax.broadcasted_iota(jnp.int32, (SSD_GROUPS * SSD_STATE, SSD_INNER), 1)
    return (row < SSD_STATE) == (lane < SSD_INNER // SSD_GROUPS)


HALO = BF16_SUBLANES
CONV_PAD = SSD_CONV // 2


def _ssd_bwd_kernel(xbc_ref, prev_ref, next_ref, sm_ref, cw_ref, cb_ref, dtb_ref, alog_ref, init_ref,
                    conv_ref, tab_ref, enter_ref, fin_ref, s_ref, ext_ref, *, regions, n_steps):
    q = SSD_CHUNK
    st = SCAN_STEP
    c = n_steps - 1 - pl.program_id(0)
    _, first, last = _chunk_pos(c, st, regions)

    @pl.when(last)
    def _():
        s_ref[...] = _expand_state(init_ref[...])

    ext_ref[0:HALO, :] = jnp.where(first, 0.0, prev_ref[...].astype(F32))
    ext_ref[HALO:HALO + st, :] = xbc_ref[...].astype(F32)
    ext_ref[HALO + st:2 * HALO + st, :] = jnp.where(last, 0.0, next_ref[...].astype(F32))
    acc = cb_ref[...]
    for k in range(SSD_CONV):
        off = HALO - CONV_PAD + k
        acc = acc + cw_ref[k:k + 1, :] * ext_ref[off:off + st, :]
    conv_ref[...] = _silu(acc).astype(BF16)

    lo_half = lax.broadcasted_iota(jnp.int32, (1, LANES), 1) < SSD_HEADDIM
    upd_mask = _state_update_mask()

    s_cur = s_ref[...]
    for ci in reversed(range(st // q)):
        rows = slice(ci * q, (ci + 1) * q)
        xs = conv_ref[rows, 0:SSD_INNER].astype(F32)
        bsb = conv_ref[rows, SSD_INNER:SSD_INNER + LANES]
        m, tot, dtv = _ssd_decay_terms(sm_ref[rows, :], dtb_ref[...], alog_ref[...], q)
        tab_ref[rows, 0:LANES] = m
        tab_ref[rows, LANES:2 * LANES] = dtv
        wgt = jnp.exp(tot - m) * dtv
        etot = jnp.exp(tot)
        enter_ref[ci] = _compact_state(s_cur)
        xw, dec = [], []
        for p in range(SSD_HEADS // 2):
            sl = slice(p * LANES, (p + 1) * LANES)
            xw.append((xs[:, sl] * _pair_cols(wgt, SSD_HEADS, p, lo_half)).astype(BF16))
            dec.append(_pair_cols(etot, SSD_HEADS, p, lo_half))
        upd = _dot_tn(bsb, jnp.concatenate(xw, axis=1))
        s_cur = s_cur * jnp.concatenate(dec, axis=1) + jnp.where(upd_mask, upd, 0.0)
    s_ref[...] = s_cur

    @pl.when(first)
    def _():
        fin_ref[...] = _compact_state(s_cur)


def _ssd_out_kernel(z_ref, x_ref, bc_ref, tab_ref, dexp_ref, ng_ref,
                    init_ref, enter_ref, y_ref, fin_ref, s_ref, *, regions):
    q = SSD_CHUNK
    c = pl.program_id(0)
    _, first, last = _chunk_pos(c, SCAN_STEP, regions)

    @pl.when(first)
    def _():
        s_ref[...] = _expand_state(init_ref[...])

    lane = lax.broadcasted_iota(jnp.int32, (1, LANES), 1)
    lo_half = lane < SSD_HEADDIM
    tril = _block_tri(q, q, False)
    ii = lax.broadcasted_iota(jnp.int32, (q, q), 0)
    jj = lax.broadcasted_iota(jnp.int32, (q, q), 1)
    eye = ii == jj
    upd_mask = _state_update_mask()
    expand = _head_expand_matrix(0)
    dexp = dexp_ref[...]
    ng = ng_ref[...]

    s_cur = s_ref[...]
    for ci in range(SCAN_STEP // q):
        rows = slice(ci * q, (ci + 1) * q)
        xs = x_ref[rows, :].astype(F32)
        bsb = bc_ref[rows, 0:LANES]
        csb = bc_ref[rows, LANES:2 * LANES]
        m = tab_ref[rows, 0:LANES]
        dtv = tab_ref[rows, LANES:2 * LANES]
        wgt = jnp.exp(jnp.where(lane < SSD_HEADS, m[q - 1:q, :] - m, 0.0)) * dtv
        mt = (m - jnp.log(dtv)).T
        ldt = jnp.log(dtv + pltpu.roll(dtv, LANES - SSD_HEADS, 1)).T
        zero_b = jnp.zeros_like(csb)
        cb = [_dot_nt(jnp.where(lo_half, csb, zero_b), bsb).astype(BF16),
              _dot_nt(jnp.where(lo_half, zero_b, csb), bsb).astype(BF16)]
        cs_f = _dot(csb, s_cur.astype(BF16))
        cs_b = _dot(csb, _expand_state(enter_ref[ci]).astype(BF16))

        ys, dec = [], []
        for p in range(SSD_HEADS // 2):
            sl = slice(p * LANES, (p + 1) * LANES)
            g = (2 * p) // (SSD_HEADS // SSD_GROUPS)
            xs_p = xs[:, sl]
            xs_pb = xs_p.astype(BF16)
            yd, colf, colb = [], [], []
            for h in (2 * p, 2 * p + 1):
                hb = SSD_HEADS + h
                cf = _col(m, h, q)
                cbk = _col(m, hb, q)
                seg = jnp.where(tril, cf - mt[h:h + 1, :], cbk - mt[hb:hb + 1, :])
                seg = jnp.where(eye, ldt[h:h + 1, :], seg)
                yd.append(_dot(cb[g] * jnp.exp(seg).astype(BF16), xs_pb))
                colf.append(cf)
                colb.append(cbk)
            ef = jnp.exp(jnp.where(lo_half, colf[0], colf[1]))
            eb = jnp.exp(jnp.where(lo_half, colb[0], colb[1]))
            y_p = jnp.where(lo_half, yd[0], yd[1]) + ef * cs_f[:, sl] + eb * cs_b[:, sl] + dexp[:, sl] * xs_p
            ys.append(y_p)
            dec.append(ef[q - 1:q, :])

        y = jnp.concatenate(ys, axis=1) * _silu(z_ref[rows, :].astype(F32))
        y_ref[rows, :] = _rms(y, ng).astype(y_ref.dtype)

        xw = (xs * _dot(wgt.astype(BF16), expand)).astype(BF16)
        upd = _dot_tn(bsb, xw)
        s_cur = s_cur * jnp.concatenate(dec, axis=1) + jnp.where(upd_mask, upd, 0.0)
    s_ref[...] = s_cur

    @pl.when(last)
    def _():
        fin_ref[...] = _compact_state(s_cur)


def _const_spec(a):
    return pl.BlockSpec(a.shape, lambda s: (0,) * a.ndim)


def _sweep_call(parts, n_steps, name):
    n_in = [len(p["inputs"]) for p in parts]
    n_out = [len(p["out_shape"]) for p in parts]
    n_scr = [len(p["scratch"]) for p in parts]

    def kernel(*refs):
        ins, outs, scr = refs[:sum(n_in)], refs[sum(n_in):sum(n_in) + sum(n_out)], refs[sum(n_in) + sum(n_out):]
        for k, p in enumerate(parts):
            i0, o0, s0 = sum(n_in[:k]), sum(n_out[:k]), sum(n_scr[:k])
            p["kernel"](*ins[i0:i0 + n_in[k]], *outs[o0:o0 + n_out[k]], *scr[s0:s0 + n_scr[k]])

    res = pl.pallas_call(
        kernel,
        out_shape=tuple(o for p in parts for o in p["out_shape"]),
        grid=(n_steps,),
        in_specs=[s for p in parts for s in p["in_specs"]],
        out_specs=tuple(s for p in parts for s in p["out_specs"]),
        scratch_shapes=[s for p in parts for s in p["scratch"]],
        compiler_params=_params("arbitrary"),
        name=name,
    )(*[a for p in parts for a in p["inputs"]])
    return [res[sum(n_out[:k]):sum(n_out[:k]) + n_out[k]] for k in range(len(parts))]


def _ssd_bwd_part(xbc, small, init, lw, regions):
    st = SCAN_STEP
    cps = st // SSD_CHUNK
    t = xbc.shape[0]
    n_steps = t // st
    n_seq = regions[0][0] + regions[1][0]
    bidx = lambda s: n_steps - 1 - s
    seq_b = lambda s: _chunk_pos(bidx(s), st, regions)[0]
    hb = st // HALO
    n_hb = t // HALO
    consts = [lw["conv_w"], lw["conv_b"], lw["dt_bias"], lw["a_log"]]
    return dict(
        kernel=functools.partial(_ssd_bwd_kernel, regions=regions, n_steps=n_steps),
        inputs=[xbc, xbc, xbc, small, *consts, init],
        in_specs=[pl.BlockSpec((st, D_CONV), lambda s: (bidx(s), 0)),
                  pl.BlockSpec((HALO, D_CONV), lambda s: (jnp.maximum(bidx(s) * hb - 1, 0), 0)),
                  pl.BlockSpec((HALO, D_CONV), lambda s: (jnp.minimum((bidx(s) + 1) * hb, n_hb - 1), 0)),
                  pl.BlockSpec((st, LANES), lambda s: (bidx(s), 0))]
        + [_const_spec(a) for a in consts]
        + [pl.BlockSpec((None, None, SSD_STATE, SSD_INNER), lambda s: (seq_b(s), 1, 0, 0))],
        out_shape=[jax.ShapeDtypeStruct((t, D_CONV), BF16),
                   jax.ShapeDtypeStruct((t, 2 * LANES), F32),
                   jax.ShapeDtypeStruct((n_steps * cps, SSD_STATE, SSD_INNER), F32),
                   jax.ShapeDtypeStruct((n_seq, SSD_STATE, SSD_INNER), F32)],
        out_specs=[pl.BlockSpec((st, D_CONV), lambda s: (bidx(s), 0)),
                   pl.BlockSpec((st, 2 * LANES), lambda s: (bidx(s), 0)),
                   pl.BlockSpec((cps, SSD_STATE, SSD_INNER), lambda s: (bidx(s), 0, 0)),
                   pl.BlockSpec((None, SSD_STATE, SSD_INNER), lambda s: (seq_b(s), 0, 0))],
        scratch=[pltpu.VMEM((SSD_GROUPS * SSD_STATE, SSD_INNER), F32),
                 pltpu.VMEM((st + 2 * HALO, D_CONV), F32)])


def _ssd_fwd_part(proj, xbc, tables, init, enter_b, lw, regions):
    st = SCAN_STEP
    cps = st // SSD_CHUNK
    t = proj.shape[0]
    n_seq = regions[0][0] + regions[1][0]
    seq_f = lambda s: _chunk_pos(s, st, regions)[0]
    consts = [lw["d_exp"], lw["ssd_norm_g"]]
    return dict(
        kernel=functools.partial(_ssd_out_kernel, regions=regions),
        inputs=[proj, xbc, xbc, tables, *consts, init, enter_b],
        in_specs=[pl.BlockSpec((st, SSD_INNER), lambda s: (s, COL_Z // SSD_INNER)),
                  pl.BlockSpec((st, SSD_INNER), lambda s: (s, CONV_X // SSD_INNER)),
                  pl.BlockSpec((st, 2 * LANES), lambda s: (s, CONV_BC // (2 * LANES))),
                  pl.BlockSpec((st, 2 * LANES), lambda s: (s, 0))]
        + [_const_spec(a) for a in consts]
        + [pl.BlockSpec((None, None, SSD_STATE, SSD_INNER), lambda s: (seq_f(s), 0, 0, 0)),
           pl.BlockSpec((cps, SSD_STATE, SSD_INNER), lambda s: (s, 0, 0))],
        out_shape=[jax.ShapeDtypeStruct((t, SSD_INNER), BF16),
                   jax.ShapeDtypeStruct((n_seq, SSD_STATE, SSD_INNER), F32)],
        out_specs=[pl.BlockSpec((st, SSD_INNER), lambda s: (s, 0)),
                   pl.BlockSpec((None, SSD_STATE, SSD_INNER), lambda s: (seq_f(s), 0, 0))],
        scratch=[pltpu.VMEM((SSD_GROUPS * SSD_STATE, SSD_INNER), F32)])


def _gla_log_decay(sm, gw_hi_ref, gw_lo_ref, gb_ref):
    s1, s2 = _split2(sm)
    logit = _dot(s1, gw_hi_ref[...]) + _dot(s2, gw_hi_ref[...]) + _dot(s1, gw_lo_ref[...]) + gb_ref[...]
    return -_softplus(-logit) * (1.0 / GLA_GATE_TAU)


def _chunk_rows(x, n_chunks, rows_per_chunk):
    return jnp.concatenate([jnp.broadcast_to(x[c:c + 1, :], (rows_per_chunk, x.shape[1]))
                            for c in range(n_chunks)], axis=0)


def _gla_bwd_kernel(qk_ref, v_ref, sm_ref, gwh_ref, gwl_ref, gb_ref, init_ref,
                    rev_ref, enter_ref, fin_ref, s_ref, *, regions, n_steps):
    cq = GLA_CHUNK
    st = SCAN_STEP
    ncs = st // cq
    c = n_steps - 1 - pl.program_id(0)
    _, first, last = _chunk_pos(c, st, regions)

    @pl.when(last)
    def _():
        s_ref[...] = init_ref[...]

    bwd = slice(GLA_KDIM, 2 * GLA_KDIM)
    lg = _gla_log_decay(sm_ref[...], gwh_ref.at[:, bwd], gwl_ref.at[:, bwd], gb_ref.at[:, bwd])
    rev = _dot_exact_lhs(_block_tri(st, cq, True).astype(BF16), lg)
    rev_ref[...] = rev
    tot = jnp.concatenate([rev[ci * cq:ci * cq + 1, :] for ci in range(ncs)], axis=0)
    ko = (qk_ref[:, GLA_KDIM:2 * GLA_KDIM].astype(F32) * jnp.exp(_chunk_rows(tot, ncs, cq) - rev)).astype(BF16)
    etot = jnp.exp(tot)
    lo_half = lax.broadcasted_iota(jnp.int32, (1, LANES), 1) < GLA_DK

    s_cur = s_ref[...]
    for ci in reversed(range(ncs)):
        rows = slice(ci * cq, (ci + 1) * cq)
        enter_ref[ci] = s_cur
        new = []
        for p in range(GLA_HEADS // 2):
            sl = slice(p * LANES, (p + 1) * LANES)
            u0 = _dot_tn(v_ref[rows, (2 * p) * GLA_DV:(2 * p + 1) * GLA_DV], ko[rows, sl])
            u1 = _dot_tn(v_ref[rows, (2 * p + 1) * GLA_DV:(2 * p + 2) * GLA_DV], ko[rows, sl])
            new.append(s_cur[:, sl] * etot[ci:ci + 1, sl] + jnp.where(lo_half, u0, u1))
        s_cur = jnp.concatenate(new, axis=1)
    s_ref[...] = s_cur

    @pl.when(first)
    def _():
        fin_ref[...] = s_cur


def _gla_out_kernel(qk_ref, v_ref, r_ref, sm_ref, gwh_ref, gwl_ref, gb_ref, ng_ref, init_ref, enter_ref, rev_ref,
                    y_ref, fin_ref, s_ref, *, regions):
    cq = GLA_CHUNK
    st = SCAN_STEP
    ncs = st // cq
    c = pl.program_id(0)
    _, first, last = _chunk_pos(c, st, regions)

    @pl.when(first)
    def _():
        s_ref[...] = init_ref[...]

    fwd = slice(0, GLA_KDIM)
    lg = _gla_log_decay(sm_ref[...], gwh_ref.at[:, fwd], gwl_ref.at[:, fwd], gb_ref.at[:, fwd])
    gcs = _dot_exact_lhs(_block_tri(st, cq, False).astype(BF16), lg)
    rev = rev_ref[...]
    lo_half = lax.broadcasted_iota(jnp.int32, (1, LANES), 1) < GLA_DK
    zero_b = jnp.zeros((cq, LANES), BF16)
    ii = lax.broadcasted_iota(jnp.int32, (2 * cq, cq), 0) % cq
    jj = lax.broadcasted_iota(jnp.int32, (2 * cq, cq), 1)
    tril2 = jj <= ii
    triu2 = jj >= ii
    ng = ng_ref[...]

    def stack_heads(x):
        return jnp.concatenate([jnp.where(lo_half, x, zero_b), jnp.where(lo_half, zero_b, x)], axis=0)

    s_cur = s_ref[...]
    for ci in range(ncs):
        rows = slice(ci * cq, (ci + 1) * cq)
        g_f = gcs[rows]
        g_b = rev[rows]
        etot = jnp.exp(g_f[cq - 1:cq, :])
        qs = qk_ref[rows, 0:GLA_KDIM].astype(F32) * (GLA_DK ** -0.5)
        ks = qk_ref[rows, GLA_KDIM:2 * GLA_KDIM].astype(F32)
        qe_f = (qs * jnp.exp(g_f)).astype(BF16)
        ke = ks * jnp.exp(-g_f)
        ke_f = ke.astype(BF16)
        ko_f = (ke * etot).astype(BF16)
        qe_b = (qs * jnp.exp(g_b)).astype(BF16)
        ke_b = (ks * jnp.exp(-g_b)).astype(BF16)
        s_curb = s_cur.astype(BF16)
        s_entb = enter_ref[ci].astype(BF16)
        new = []
        for p in range(GLA_HEADS // 2):
            sl = slice(p * LANES, (p + 1) * LANES)
            qf2 = stack_heads(qe_f[:, sl])
            qb2 = stack_heads(qe_b[:, sl])
            att = (jnp.where(tril2, _dot_nt(qf2, ke_f[:, sl]), 0.0)
                   + jnp.where(triu2, _dot_nt(qb2, ke_b[:, sl]), 0.0)).astype(BF16)
            inter = _dot_nt(qf2, s_curb[:, sl]) + _dot_nt(qb2, s_entb[:, sl])
            upd = []
            for hh in (0, 1):
                h = 2 * p + hh
                hs = slice(h * GLA_DV, (h + 1) * GLA_DV)
                hr = slice(hh * cq, (hh + 1) * cq)
                vh = v_ref[rows, hs]
                o = _dot(att[hr], vh) + inter[hr]
                o = o * lax.rsqrt(jnp.mean(o * o, axis=-1, keepdims=True) + EPS) * ng[:, hs]
                y_ref[rows, hs] = (o * _silu(r_ref[rows, hs].astype(F32))).astype(y_ref.dtype)
                upd.append(_dot_tn(vh, ko_f[:, sl]))
            new.append(s_cur[:, sl] * etot[:, sl] + jnp.where(lo_half, upd[0], upd[1]))
        s_cur = jnp.concatenate(new, axis=1)
    s_ref[...] = s_cur

    @pl.when(last)
    def _():
        fin_ref[...] = s_cur


def _gla_bwd_part(proj, small, init, lw, regions):
    st = SCAN_STEP
    ncs = st // GLA_CHUNK
    t = proj.shape[0]
    n_steps = t // st
    n_seq = regions[0][0] + regions[1][0]
    consts = [lw["gate_w_hi"], lw["gate_w_lo"], lw["gate_b"]]
    bidx = lambda s: n_steps - 1 - s
    seq_b = lambda s: _chunk_pos(bidx(s), st, regions)[0]
    return dict(
        kernel=functools.partial(_gla_bwd_kernel, regions=regions, n_steps=n_steps),
        inputs=[proj, proj, small, *consts, init],
        in_specs=[pl.BlockSpec((st, 2 * GLA_KDIM), lambda s: (bidx(s), COL_QK // (2 * GLA_KDIM))),
                  pl.BlockSpec((st, GLA_VDIM), lambda s: (bidx(s), COL_V // GLA_VDIM)),
                  pl.BlockSpec((st, LANES), lambda s: (bidx(s), 0))]
        + [_const_spec(a) for a in consts]
        + [pl.BlockSpec((None, None, GLA_DV, GLA_KDIM), lambda s: (seq_b(s), 1, 0, 0))],
        out_shape=[jax.ShapeDtypeStruct((t, GLA_KDIM), F32),
                   jax.ShapeDtypeStruct((n_steps * ncs, GLA_DV, GLA_KDIM), F32),
                   jax.ShapeDtypeStruct((n_seq, GLA_DV, GLA_KDIM), F32)],
        out_specs=[pl.BlockSpec((st, GLA_KDIM), lambda s: (bidx(s), 0)),
                   pl.BlockSpec((ncs, GLA_DV, GLA_KDIM), lambda s: (bidx(s), 0, 0)),
                   pl.BlockSpec((None, GLA_DV, GLA_KDIM), lambda s: (seq_b(s), 0, 0))],
        scratch=[pltpu.VMEM((GLA_DV, GLA_KDIM), F32)])


def _gla_fwd_part(proj, small, init, enter_b, rev_b, lw, regions):
    st = SCAN_STEP
    ncs = st // GLA_CHUNK
    t = proj.shape[0]
    n_seq = regions[0][0] + regions[1][0]
    consts = [lw["gate_w_hi"], lw["gate_w_lo"], lw["gate_b"], lw["gla_norm_g"]]
    seq_f = lambda s: _chunk_pos(s, st, regions)[0]
    return dict(
        kernel=functools.partial(_gla_out_kernel, regions=regions),
        inputs=[proj, proj, proj, small, *consts, init, enter_b, rev_b],
        in_specs=[pl.BlockSpec((st, 2 * GLA_KDIM), lambda s: (s, COL_QK // (2 * GLA_KDIM))),
                  pl.BlockSpec((st, GLA_VDIM), lambda s: (s, COL_V // GLA_VDIM)),
                  pl.BlockSpec((st, GLA_VDIM), lambda s: (s, COL_R // GLA_VDIM)),
                  pl.BlockSpec((st, LANES), lambda s: (s, 0))]
        + [_const_spec(a) for a in consts]
        + [pl.BlockSpec((None, None, GLA_DV, GLA_KDIM), lambda s: (seq_f(s), 0, 0, 0)),
           pl.BlockSpec((ncs, GLA_DV, GLA_KDIM), lambda s: (s, 0, 0)),
           pl.BlockSpec((st, GLA_KDIM), lambda s: (s, 0))],
        out_shape=[jax.ShapeDtypeStruct((t, GLA_VDIM), BF16),
                   jax.ShapeDtypeStruct((n_seq, GLA_DV, GLA_KDIM), F32)],
        out_specs=[pl.BlockSpec((st, GLA_VDIM), lambda s: (s, 0)),
                   pl.BlockSpec((None, GLA_DV, GLA_KDIM), lambda s: (seq_f(s), 0, 0))],
        scratch=[pltpu.VMEM((GLA_DV, GLA_KDIM), F32)])


def _mixers(proj, xbc, small, init_ssd, init_gla, lw, regions):
    n_steps = proj.shape[0] // SCAN_STEP
    (xbc_c, tab_s, enter_s, fin_sb), (rev_g, enter_g, fin_gb) = _sweep_call(
        [_ssd_bwd_part(xbc, small, init_ssd, lw, regions), _gla_bwd_part(proj, small, init_gla, lw, regions)],
        n_steps, "scan_backward_states")
    (y_ssd, fin_sf), (y_gla, fin_gf) = _sweep_call(
        [_ssd_fwd_part(proj, xbc_c, tab_s, init_ssd, enter_s, lw, regions),
         _gla_fwd_part(proj, small, init_gla, enter_g, rev_g, lw, regions)],
        n_steps, "scan_forward_outputs")
    return y_ssd, fin_sf, fin_sb, y_gla, fin_gf, fin_gb


def _merge_kernel(x_ref, ys_ref, yg_ref, ga_ref, gb_ref, mod_ref, wbs_ref, wbg_ref, wo_ref, o_ref):
    merged = (_sigmoid(ga_ref[...].astype(F32)) * _dot(ys_ref[...], wbs_ref[...])
              + _sigmoid(gb_ref[...].astype(F32)) * _dot(yg_ref[...], wbg_ref[...]))
    mix = _dot(merged.astype(BF16), wo_ref[...])
    o_ref[...] = x_ref[...] + mod_ref[:, 2 * D_MODEL:3 * D_MODEL] * mix


def _merge(x, y_ssd, y_gla, proj, mods, lw, regions):
    t = x.shape[0]
    tm = min(TM_MERGE, regions[0][0] * regions[0][1], regions[1][1])
    tok = lambda col: pl.BlockSpec((tm, D_MODEL), lambda i: (i, col))
    wspec = pl.BlockSpec((D_MODEL, D_MODEL), lambda i: (0, 0))
    return pl.pallas_call(
        _merge_kernel,
        out_shape=jax.ShapeDtypeStruct((t, D_MODEL), F32),
        grid=(t // tm,),
        in_specs=[tok(0), tok(0), tok(0), tok(COL_GA // D_MODEL), tok(COL_GB // D_MODEL),
                  pl.BlockSpec((None, 1, 6 * D_MODEL), lambda i: (_mod_row(i, tm, regions), 0, 0)),
                  wspec, wspec, wspec],
        out_specs=tok(0),
        compiler_params=_params("parallel"),
        name="merge_out_projection",
    )(x, y_ssd, y_gla, proj, proj, mods, lw["w_bs"], lw["w_bg"], lw["w_o"])


def _ffn_prologue(x_ref, mod_ref, g_ref):
    y = _rms(x_ref[...], g_ref[...])
    return y * (1.0 + mod_ref[:, 4 * D_MODEL:5 * D_MODEL]) + mod_ref[:, 3 * D_MODEL:4 * D_MODEL]


def _swiglu_resident(v, w1_ref, w3_ref, w2_ref):
    acc = None
    for hh in range(w1_ref.shape[1] // TH_FFN):
        cols = slice(hh * TH_FFN, (hh + 1) * TH_FFN)
        h = _silu(_dot(v, w1_ref[:, cols])) * _dot(v, w3_ref[:, cols])
        part = _dot(h.astype(BF16), w2_ref[cols, :])
        acc = part if acc is None else acc + part
    return acc


def _ffn_kernel(x_ref, mod_ref, g_ref, w1_ref, w3_ref, w2_ref, o_ref):
    v = _ffn_prologue(x_ref, mod_ref, g_ref).astype(BF16)
    o_ref[...] = x_ref[...] + mod_ref[:, 5 * D_MODEL:6 * D_MODEL] * _swiglu_resident(v, w1_ref, w3_ref, w2_ref)


def _dense_ffn(x, mods, g, w1, w3, w2, regions):
    t = x.shape[0]
    tm = min(TM_FFN, regions[0][0] * regions[0][1], regions[1][1])
    hid = w1.shape[1]
    once = pl.Buffered(1)
    return pl.pallas_call(
        _ffn_kernel,
        out_shape=jax.ShapeDtypeStruct((t, D_MODEL), F32),
        grid=(t // tm,),
        in_specs=[pl.BlockSpec((tm, D_MODEL), lambda i: (i, 0)),
                  pl.BlockSpec((None, 1, 6 * D_MODEL), lambda i: (_mod_row(i, tm, regions), 0, 0)),
                  pl.BlockSpec((1, D_MODEL), lambda i: (0, 0)),
                  pl.BlockSpec((D_MODEL, hid), lambda i: (0, 0), pipeline_mode=once),
                  pl.BlockSpec((D_MODEL, hid), lambda i: (0, 0), pipeline_mode=once),
                  pl.BlockSpec((hid, D_MODEL), lambda i: (0, 0), pipeline_mode=once)],
        out_specs=pl.BlockSpec((tm, D_MODEL), lambda i: (i, 0)),
        compiler_params=_params("parallel"),
        name="dense_swiglu",
    )(x, mods, g, w1, w3, w2)


def _top2_gates(logits):
    lane = lax.broadcasted_iota(jnp.int32, logits.shape, 1)
    lg = jnp.where(lane < N_EXPERTS, logits, -jnp.inf)
    m1 = jnp.max(lg, axis=-1, keepdims=True)
    i1 = jnp.min(jnp.where(lg == m1, lane, LANES), axis=-1, keepdims=True)
    lg2 = jnp.where(lane == i1, -jnp.inf, lg)
    m2 = jnp.max(lg2, axis=-1, keepdims=True)
    i2 = jnp.min(jnp.where(lg2 == m2, lane, LANES), axis=-1, keepdims=True)
    e2 = jnp.exp(m2 - m1)
    den = 1.0 + e2
    return jnp.where(lane == i1, 1.0 / den, 0.0) + jnp.where(lane == i2, e2 / den, 0.0)


def _lane_scalar(vec, e):
    lane = lax.broadcasted_iota(jnp.int32, vec.shape, 1)
    return jnp.sum(jnp.where(lane == e, vec, 0.0)).astype(jnp.int32)


def _n_row_blocks(n):
    return (n + MOE_ROW_BLOCK - 1) // MOE_ROW_BLOCK


def _moe_block_copy(stage_ref, sem, slot, hbm_ref, row, to_hbm):
    if not isinstance(row, int):
        row = pl.multiple_of(row, MOE_ROW_ALIGN)
    rows = hbm_ref.at[pl.ds(row, MOE_ROW_BLOCK), :]
    if to_hbm:
        return pltpu.make_async_copy(stage_ref.at[slot], rows, sem.at[slot])
    return pltpu.make_async_copy(rows, stage_ref.at[slot], sem.at[slot])


def _moe_dispatch_kernel(x_ref, mod_ref, g_ref, rw_ref, rb_ref, donor_hbm,
                         xs_hbm, gate_ref, rank_ref, off_ref, tot_ref,
                         v_ref, gate_t_ref, rank_t_ref, stage_ref, sem, run_ref, pend_ref, *, cap):
    del donor_hbm
    i = pl.program_id(0)
    tm = x_ref.shape[0]
    rb = MOE_ROW_BLOCK

    @pl.when(i == 0)
    def _():
        run_ref[...] = jnp.zeros_like(run_ref)
        pend_ref[0] = 0

    v = _ffn_prologue(x_ref, mod_ref, g_ref)
    v_ref[...] = v.astype(BF16)
    gate = _top2_gates(_dot_hilo(v, rw_ref[...]) + rb_ref[...])
    sel_f = jnp.where(gate > 0.0, 1.0, 0.0)
    sel = sel_f.astype(BF16)
    gate_ref[...] = gate
    gate_t_ref[...] = gate.T
    for blk in range(tm // rb):
        ii = lax.broadcasted_iota(jnp.int32, (rb, tm), 0) + blk * rb
        jj = lax.broadcasted_iota(jnp.int32, (rb, tm), 1)
        rank = _dot(jnp.where(jj < ii, 1.0, 0.0).astype(BF16), sel)
        rank_ref[blk * rb:(blk + 1) * rb, :] = rank
        rank_t_ref[:, blk * rb:(blk + 1) * rb] = rank.T

    cnt = jnp.sum(sel_f, axis=0, keepdims=True)
    cnt_pad = jnp.floor((cnt + (MOE_ROW_ALIGN - 1)) * (1.0 / MOE_ROW_ALIGN)) * MOE_ROW_ALIGN
    run = run_ref[...]
    off_ref[...] = run

    def wait_prev(s, carry):
        _moe_block_copy(stage_ref, sem, s, xs_hbm, 0, True).wait()
        return carry
    lax.fori_loop(0, pend_ref[0], wait_prev, 0)

    slot = jnp.int32(0)
    for e in range(N_EXPERTS):
        n_e = _lane_scalar(cnt, e)
        base = e * cap + _lane_scalar(run, e)
        g_row = gate_t_ref[e:e + 1, :]
        r_row = rank_t_ref[e:e + 1, :].astype(jnp.int32)

        def pack(b, s, base=base, g_row=g_row, r_row=r_row):
            row = lax.broadcasted_iota(jnp.int32, (rb, tm), 0) + b * rb
            onehot = jnp.where(jnp.logical_and(row == r_row, g_row > 0.0), 1.0, 0.0).astype(BF16)
            stage_ref[s] = _dot(onehot, v_ref[...]).astype(BF16)
            _moe_block_copy(stage_ref, sem, s, xs_hbm, base + b * rb, True).start()
            return s + 1
        slot = lax.fori_loop(0, _n_row_blocks(n_e), pack, slot)

    pend_ref[0] = slot
    run_ref[...] = run + cnt_pad
    tot_ref[...] = run + cnt_pad

    @pl.when(i == pl.num_programs(0) - 1)
    def _():
        def wait_last(s, carry):
            _moe_block_copy(stage_ref, sem, s, xs_hbm, 0, True).wait()
            return carry
        lax.fori_loop(0, slot, wait_last, 0)


def _moe_expert_kernel(te_ref, tr_ref, na_ref, xs_ref, w1_ref, w3_ref, w2_ref, ys_ref):
    del te_ref, tr_ref

    @pl.when(pl.program_id(0) < na_ref[0])
    def _():
        ys_ref[...] = _swiglu_resident(xs_ref[...], w1_ref, w3_ref, w2_ref).astype(ys_ref.dtype)


def _moe_combine_kernel(off_ref, x_ref, mod_ref, gate_ref, rank_ref, ys_hbm, o_ref, stage_ref, sem, *, cap):
    i = pl.program_id(0)
    tm = x_ref.shape[0]
    rb = MOE_ROW_BLOCK
    gate = gate_ref[...]
    rank = rank_ref[...]
    cnt = jnp.sum(jnp.where(gate > 0.0, 1.0, 0.0), axis=0, keepdims=True)
    n_blocks = [_n_row_blocks(_lane_scalar(cnt, e)) for e in range(N_EXPERTS)]
    bases = [e * cap + off_ref[i * LANES + e] for e in range(N_EXPERTS)]

    slot = jnp.int32(0)
    for e in range(N_EXPERTS):
        def fetch(b, s, base=bases[e]):
            _moe_block_copy(stage_ref, sem, s, ys_hbm, base + b * rb, False).start()
            return s + 1
        slot = lax.fori_loop(0, n_blocks[e], fetch, slot)

    o_ref[...] = jnp.zeros_like(o_ref)
    slot = jnp.int32(0)
    for e in range(N_EXPERTS):
        g_col = gate[:, e:e + 1]
        r_col = rank[:, e:e + 1].astype(jnp.int32)

        def combine(b, s, g_col=g_col, r_col=r_col):
            _moe_block_copy(stage_ref, sem, s, ys_hbm, 0, False).wait()
            col = lax.broadcasted_iota(jnp.int32, (tm, rb), 1) + b * rb
            onehot = jnp.where(jnp.logical_and(col == r_col, g_col > 0.0), 1.0, 0.0).astype(BF16)
            o_ref[...] += g_col * _dot(onehot, stage_ref[s])
            return s + 1
        slot = lax.fori_loop(0, n_blocks[e], combine, slot)

    o_ref[...] = x_ref[...] + mod_ref[:, 5 * D_MODEL:6 * D_MODEL] * o_ref[...]


def _moe_tile_table(totals, cap, n_steps):
    extent = totals
    n_tiles = (extent + TR_EXPERT - 1) // TR_EXPERT
    ends = jnp.cumsum(n_tiles)
    starts = ends - n_tiles
    n_active = ends[-1]
    t = jnp.minimum(jnp.arange(n_steps, dtype=jnp.int32), n_active - 1)
    e = jnp.sum((t[:, None] >= ends[None, :]).astype(jnp.int32), axis=1)
    row_block = e * (cap // TR_EXPERT) + (t - starts[e])
    return e.astype(jnp.int32), row_block.astype(jnp.int32), n_active.reshape(1).astype(jnp.int32)


def _moe_geometry(t, regions, n_e):
    tm = min(TM_MOE, regions[0][0] * regions[0][1], regions[1][1])
    n_tiles = t // tm
    cap_rows = t + n_tiles * (MOE_ROW_ALIGN - 1) + MOE_ROW_BLOCK
    cap = -(-cap_rows // TR_EXPERT) * TR_EXPERT
    return tm, n_tiles, cap


def _moe_ffn(x, mods, g, rw, rb, w1, w3, w2, regions, donor):
    t = x.shape[0]
    n_e, _, hid = w1.shape
    tm, n_tiles, cap = _moe_geometry(t, regions, n_e)
    max_slots = 2 * tm // MOE_ROW_BLOCK + n_e
    n_steps = (2 * t + n_tiles * n_e * (MOE_ROW_ALIGN - 1) + n_e * MOE_ROW_BLOCK) // TR_EXPERT + n_e
    mod_spec = pl.BlockSpec((None, 1, 6 * D_MODEL), lambda i, *_: (_mod_row(i, tm, regions), 0, 0))
    stage = [pltpu.VMEM((max_slots, MOE_ROW_BLOCK, D_MODEL), BF16), pltpu.SemaphoreType.DMA((max_slots,))]

    xs, gate, rank, offs, totals = pl.pallas_call(
        functools.partial(_moe_dispatch_kernel, cap=cap),
        out_shape=(jax.ShapeDtypeStruct((n_e * cap, D_MODEL), BF16),
                   jax.ShapeDtypeStruct((t, LANES), F32), jax.ShapeDtypeStruct((t, LANES), F32),
                   jax.ShapeDtypeStruct((n_tiles, 1, LANES), F32), jax.ShapeDtypeStruct((1, LANES), F32)),
        grid=(n_tiles,),
        in_specs=[pl.BlockSpec((tm, D_MODEL), lambda i: (i, 0)), mod_spec,
                  pl.BlockSpec((1, D_MODEL), lambda i: (0, 0)),
                  pl.BlockSpec((D_MODEL, LANES), lambda i: (0, 0)),
                  pl.BlockSpec((1, LANES), lambda i: (0, 0)),
                  pl.BlockSpec(memory_space=pl.ANY)],
        input_output_aliases={5: 0},
        out_specs=(pl.BlockSpec(memory_space=pl.ANY),
                   pl.BlockSpec((tm, LANES), lambda i: (i, 0)), pl.BlockSpec((tm, LANES), lambda i: (i, 0)),
                   pl.BlockSpec((None, 1, LANES), lambda i: (i, 0, 0)),
                   pl.BlockSpec((1, LANES), lambda i: (0, 0))),
        scratch_shapes=[pltpu.VMEM((tm, D_MODEL), BF16),
                        pltpu.VMEM((LANES, tm), F32), pltpu.VMEM((LANES, tm), F32)] + stage
        + [pltpu.VMEM((1, LANES), F32), pltpu.SMEM((1,), jnp.int32)],
        compiler_params=_params("arbitrary"),
        name="moe_dispatch",
    )(x, mods, g, rw, rb, donor)

    tile_e, tile_rb, n_active = _moe_tile_table(totals[0, :n_e].astype(jnp.int32), cap, n_steps)
    ys = pl.pallas_call(
        _moe_expert_kernel,
        out_shape=jax.ShapeDtypeStruct((n_e * cap, D_MODEL), BF16),
        grid_spec=pltpu.PrefetchScalarGridSpec(
            num_scalar_prefetch=3,
            grid=(n_steps,),
            in_specs=[pl.BlockSpec((TR_EXPERT, D_MODEL), lambda s, te, tr, na: (tr[s], 0)),
                      pl.BlockSpec((None, D_MODEL, hid), lambda s, te, tr, na: (te[s], 0, 0)),
                      pl.BlockSpec((None, D_MODEL, hid), lambda s, te, tr, na: (te[s], 0, 0)),
                      pl.BlockSpec((None, hid, D_MODEL), lambda s, te, tr, na: (te[s], 0, 0))],
            out_specs=pl.BlockSpec((TR_EXPERT, D_MODEL), lambda s, te, tr, na: (tr[s], 0))),
        input_output_aliases={3: 0},
        compiler_params=_params("arbitrary"),
        name="moe_experts",
    )(tile_e, tile_rb, n_active, xs, w1, w3, w2)

    out = pl.pallas_call(
        functools.partial(_moe_combine_kernel, cap=cap),
        out_shape=jax.ShapeDtypeStruct((t, D_MODEL), F32),
        grid_spec=pltpu.PrefetchScalarGridSpec(
            num_scalar_prefetch=1,
            grid=(n_tiles,),
            in_specs=[pl.BlockSpec((tm, D_MODEL), lambda i, off: (i, 0)), mod_spec,
                      pl.BlockSpec((tm, LANES), lambda i, off: (i, 0)),
                      pl.BlockSpec((tm, LANES), lambda i, off: (i, 0)),
                      pl.BlockSpec(memory_space=pl.ANY)],
            out_specs=pl.BlockSpec((tm, D_MODEL), lambda i, off: (i, 0)),
            scratch_shapes=stage),
        compiler_params=_params("arbitrary"),
        name="moe_combine",
    )(offs.reshape(-1).astype(jnp.int32), x, mods, gate, rank, ys)
    return out, ys


def _final_norm_kernel(x_ref, g_ref, o_ref):
    o_ref[...] = _rms(x_ref[...], g_ref[...])


def _final_norm(x, g, tok_off, n_tok):
    tm = min(TM_NORM, n_tok)
    off = tok_off // tm
    return pl.pallas_call(
        _final_norm_kernel,
        out_shape=jax.ShapeDtypeStruct((n_tok, D_MODEL), F32),
        grid=(n_tok // tm,),
        in_specs=[pl.BlockSpec((tm, D_MODEL), lambda i: (i + off, 0)),
                  pl.BlockSpec((1, D_MODEL), lambda i: (0, 0))],
        out_specs=pl.BlockSpec((tm, D_MODEL), lambda i: (i, 0)),
        compiler_params=_params("parallel"),
        name="final_norm",
    )(x, g)


CAST_BLOCK_BYTES = 6 * 1024 * 1024


def _cast_kernel(x_ref, o_ref):
    o_ref[...] = x_ref[...].astype(o_ref.dtype)


def _to_bf16(w, j):
    w3 = w.reshape(w.shape[0], -1, w.shape[-1])
    _, rows, cols = w3.shape
    fits = [r for r in range(BF16_SUBLANES, rows + 1, BF16_SUBLANES)
            if rows % r == 0 and r * cols * 4 <= CAST_BLOCK_BYTES]
    tr = max(fits)
    out = pl.pallas_call(
        _cast_kernel,
        out_shape=jax.ShapeDtypeStruct((rows, cols), BF16),
        grid=(rows // tr,),
        in_specs=[pl.BlockSpec((None, tr, cols), lambda i: (j, i, 0))],
        out_specs=pl.BlockSpec((tr, cols), lambda i: (i, 0)),
        compiler_params=_params("parallel"),
        name="weights_to_bf16",
    )(w3)
    return out.reshape(w.shape[1:])

def _grid_pos_embed(l, d):
    rows = l // GRID_W
    row = jnp.repeat(jnp.arange(rows, dtype=F32), GRID_W)
    col = jnp.tile(jnp.arange(GRID_W, dtype=F32), rows)
    quarter = d // 4
    omega = jnp.exp(-math.log(10000.0) * jnp.arange(quarter, dtype=F32) / quarter)
    er = row[:, None] * omega
    ec = col[:, None] * omega
    return jnp.concatenate([jnp.sin(er), jnp.cos(er), jnp.sin(ec), jnp.cos(ec)], axis=-1)


def _pad_lanes(a, width):
    return jnp.pad(a, [(0, 0)] * (a.ndim - 1) + [(0, width - a.shape[-1])])


def _reorder_w_in(w):
    o = np.cumsum([0, SSD_INNER, SSD_INNER + 2 * SSD_GROUPS * SSD_STATE, 2 * SSD_HEADS, GLA_KDIM, GLA_KDIM,
                   GLA_VDIM, GLA_VDIM, 2 * GLA_GATE_RANK, D_MODEL, D_MODEL])
    z, xbc, dt, qq, kk, vv, rr, lr, ga, gb = [w[:, int(o[i]):int(o[i + 1])] for i in range(10)]
    main = jnp.concatenate([z, vv, rr, ga, gb, qq, kk], axis=1).astype(BF16)
    small = _pad_lanes(jnp.concatenate([dt, lr], axis=1), LANES).astype(BF16)
    return main, xbc.astype(BF16), small


def _layer_weights(i, p):
    conv_w = jnp.pad(p["ssd_conv_w"][i], ((0, SUBLANES - SSD_CONV), (0, 0)))
    conv_b = p["ssd_conv_b"][i][None, :]
    gate_w = jnp.zeros((LANES, 2 * GLA_KDIM), F32)
    gate_w = gate_w.at[SM_LR:SM_LR + GLA_GATE_RANK, :GLA_KDIM].set(p["gla_gate_w"][i, 0])
    gate_w = gate_w.at[SM_LR + GLA_GATE_RANK:SM_LR + 2 * GLA_GATE_RANK, GLA_KDIM:].set(p["gla_gate_w"][i, 1])
    gate_w_hi = gate_w.astype(BF16)
    gate_w_lo = (gate_w - gate_w_hi.astype(F32)).astype(BF16)
    w_main, w_conv, w_small = _reorder_w_in(p["w_in"][i])
    return {
        "w_in": w_main, "w_in_conv": w_conv, "w_in_small": w_small,
        "conv_w": conv_w, "conv_b": conv_b,
        "dt_bias": _pad_lanes(p["ssd_dt_bias"][i].reshape(1, -1), LANES),
        "a_log": _pad_lanes(p["ssd_a_log"][i].reshape(1, -1), LANES),
        "d_exp": jnp.repeat(p["ssd_d"][i], SSD_HEADDIM)[None, :],
        "ssd_norm_g": p["ssd_norm_g"][i][None, :],
        "gate_w_hi": gate_w_hi, "gate_w_lo": gate_w_lo, "gate_b": p["gla_gate_b"][i].reshape(1, -1),
        "gla_norm_g": p["gla_norm_g"][i][None, :],
        "w_bs": p["w_branch_ssd"][i].astype(BF16), "w_bg": p["w_branch_gla"][i].astype(BF16),
        "w_o": p["w_out"][i].astype(BF16),
    }


def _ssd_state_in(s):
    b = s.shape[0]
    return jnp.transpose(s, (0, 1, 4, 2, 3)).reshape(b, 2, SSD_STATE, SSD_INNER)


def _ssd_state_out(f, b):
    s = jnp.stack([f, b], axis=1).reshape(-1, 2, SSD_STATE, SSD_HEADS, SSD_HEADDIM)
    return jnp.transpose(s, (0, 1, 3, 4, 2))


def _gla_state_in(s):
    b = s.shape[0]
    return jnp.transpose(s, (0, 1, 4, 2, 3)).reshape(b, 2, GLA_DV, GLA_KDIM)


def _gla_state_out(f, b):
    s = jnp.stack([f, b], axis=1).reshape(-1, 2, GLA_DV, GLA_HEADS, GLA_DK)
    return jnp.transpose(s, (0, 1, 3, 4, 2))


def _trunk(x_prompt, x_sample, state_ssd, state_gla, c, c_ctx, p):
    n0, l0, _ = x_prompt.shape
    n1, l1, _ = x_sample.shape
    regions = ((n0, l0), (n1, l1))
    depth = p["w_in"].shape[0]

    cc = jnp.zeros((MOD_ROWS, D_MODEL), F32).at[0].set(c_ctx).at[1:1 + n1].set(c)
    mods = _modulation_table(cc, p["ada_w"], p["ada_b"])[:, :, None, :]
    x = _assemble_tokens(x_prompt, x_sample, _grid_pos_embed(l1, D_MODEL), regions)

    ssd_states, gla_states = [], []
    moe_rows = None
    for i in range(depth):
        lw = _layer_weights(i, p)
        init_ssd = jnp.concatenate([jnp.zeros((n0, 2, SSD_STATE, SSD_INNER), F32),
                                    _ssd_state_in(state_ssd[:, i])], axis=0)
        init_gla = jnp.concatenate([jnp.zeros((n0, 2, GLA_DV, GLA_KDIM), F32),
                                    _gla_state_in(state_gla[:, i])], axis=0)
        proj, xbc, small = _in_projection(x, mods[i], p["norm1_g"][i][None, :], lw, regions)
        y_ssd, sf, sb, y_gla, gf, gb = _mixers(proj, xbc, small, init_ssd, init_gla, lw, regions)
        ssd_states.append(_ssd_state_out(sf[:n0], sb[:n0]))
        gla_states.append(_gla_state_out(gf[:n0], gb[:n0]))
        x = _merge(x, y_ssd, y_gla, proj, mods[i], lw, regions)
        j = i // 2
        g2 = p["norm2_g"][i][None, :]
        if i % 2 == 0:
            x = _dense_ffn(x, mods[i], g2, p["ffn_w1"][j].astype(BF16), p["ffn_w3"][j].astype(BF16),
                           p["ffn_w2"][j].astype(BF16), regions)
        else:
            if moe_rows is None:
                n_e = p["moe_w1"].shape[1]
                moe_rows = jnp.zeros((n_e * _moe_geometry(x.shape[0], regions, n_e)[2], D_MODEL), BF16)
            x, moe_rows = _moe_ffn(x, mods[i], g2, _pad_lanes(p["router_w"][j], LANES),
                                   _pad_lanes(p["router_b"][j][None, :], LANES),
                                   _to_bf16(p["moe_w1"], j), _to_bf16(p["moe_w3"], j),
                                   _to_bf16(p["moe_w2"], j), regions, moe_rows)

    gfin = p["final_norm_g"][None, :]
    y_prompt = _final_norm(x, gfin, 0, n0 * l0).reshape(n0, l0, D_MODEL)
    y_sample = _final_norm(x, gfin, n0 * l0, n1 * l1).reshape(n1, l1, D_MODEL)
    return y_prompt, y_sample, jnp.stack(ssd_states, axis=1), jnp.stack(gla_states, axis=1)


def kernel(x_prompt, x_sample, state_ssd, state_gla, c, c_ctx, ada_w, ada_b, norm1_g, norm2_g, w_in, ssd_conv_w, ssd_conv_b, ssd_dt_bias, ssd_a_log, ssd_d, ssd_norm_g, gla_gate_w, gla_gate_b, gla_norm_g, w_branch_ssd, w_branch_gla, w_out, ffn_w1, ffn_w3, ffn_w2, router_w, router_b, moe_w1, moe_w3, moe_w2, final_norm_g):
    p = dict(ada_w=ada_w, ada_b=ada_b, norm1_g=norm1_g, norm2_g=norm2_g, w_in=w_in, ssd_conv_w=ssd_conv_w,
             ssd_conv_b=ssd_conv_b, ssd_dt_bias=ssd_dt_bias, ssd_a_log=ssd_a_log, ssd_d=ssd_d,
             ssd_norm_g=ssd_norm_g, gla_gate_w=gla_gate_w, gla_gate_b=gla_gate_b, gla_norm_g=gla_norm_g,
             w_branch_ssd=w_branch_ssd, w_branch_gla=w_branch_gla, w_out=w_out, ffn_w1=ffn_w1, ffn_w3=ffn_w3,
             ffn_w2=ffn_w2, router_w=router_w, router_b=router_b, moe_w1=moe_w1, moe_w3=moe_w3, moe_w2=moe_w2,
             final_norm_g=final_norm_g)
    return _trunk(x_prompt, x_sample, state_ssd, state_gla, c, c_ctx, p)
```

```python
import functools
import math

import numpy as np
import jax
import jax.numpy as jnp
from jax import lax
from jax.experimental import pallas as pl
from jax.experimental.pallas import tpu as pltpu

F32 = jnp.float32
BF16 = jnp.bfloat16

D_MODEL = 1024
GRID_W = 64
EPS = 1e-6
SSD_HEADS = 16
SSD_HEADDIM = 64
SSD_INNER = SSD_HEADS * SSD_HEADDIM
SSD_GROUPS = 2
SSD_STATE = 64
SSD_CONV = 5
SSD_CHUNK = 128
GLA_HEADS = 8
GLA_DK = 64
GLA_DV = 128
GLA_KDIM = GLA_HEADS * GLA_DK
GLA_VDIM = GLA_HEADS * GLA_DV
GLA_GATE_RANK = 16
GLA_GATE_TAU = 16.0
GLA_CHUNK = 64
N_EXPERTS = 8

LANES = 128
SUBLANES = 8
BF16_SUBLANES = 16
VMEM_LIMIT_BYTES = 56 * 1024 * 1024

COL_Z, COL_V, COL_R, COL_GA, COL_GB, COL_QK = 0, 1024, 2048, 3072, 4096, 5120
D_PROJ = 6144
CONV_X, CONV_BC = 0, 1024
D_CONV = 1280
SM_LR = 2 * SSD_HEADS

TM_PROJ, TN_PROJ = 512, 1536
TM_MERGE = 512
TM_FFN, TH_FFN = 512, 2816
TM_MOE = 512
MOE_ROW_BLOCK = 256
MOE_ROW_ALIGN = BF16_SUBLANES
TR_EXPERT = 512
TM_NORM = 1024
TN_MODS = 1536
MOD_ROWS = 16
SCAN_STEP = 256


def _params(*sem):
    return pltpu.CompilerParams(dimension_semantics=sem, vmem_limit_bytes=VMEM_LIMIT_BYTES)


def _sigmoid(x):
    return 0.5 * jnp.tanh(0.5 * x) + 0.5


def _silu(x):
    return x * _sigmoid(x)


def _softplus(x):
    return jnp.maximum(x, 0.0) + jnp.log(1.0 + jnp.exp(-jnp.abs(x)))


def _split2(a):
    a1 = a.astype(BF16)
    a2 = (a - a1.astype(F32)).astype(BF16)
    return a1, a2


def _dot(a, b):
    return jnp.dot(a, b, preferred_element_type=F32)


def _dot_nt(a, b):
    return lax.dot_general(a, b, (((1,), (1,)), ((), ())), preferred_element_type=F32)


def _dot_tn(a, b):
    return lax.dot_general(a, b, (((0,), (0,)), ((), ())), preferred_element_type=F32)


def _dot_exact_lhs(t01, a):
    a1, a2 = _split2(a)
    return _dot(t01, a1) + _dot(t01, a2)


def _dot_hilo(a, b):
    a1, a2 = _split2(a)
    b1, b2 = _split2(b)
    return _dot(a1, b1) + _dot(a2, b1) + _dot(a1, b2)


def _block_tri(n, blk, upper):
    i = lax.broadcasted_iota(jnp.int32, (n, n), 0)
    j = lax.broadcasted_iota(jnp.int32, (n, n), 1)
    tri = (j >= i) if upper else (j <= i)
    if blk == n:
        return tri
    return jnp.logical_and(tri, (i // blk) == (j // blk))


def _rms(x, g):
    return x * lax.rsqrt(jnp.mean(x * x, axis=-1, keepdims=True) + EPS) * g


def _chunk_pos(c, q, regions):
    (n0, l0), (n1, l1) = regions
    per0, per1 = l0 // q, l1 // q
    nc0 = n0 * per0
    c1 = jnp.maximum(c - nc0, 0)
    in0 = c < nc0
    seq = jnp.where(in0, c // per0, n0 + c1 // per1)
    pos = jnp.where(in0, c % per0, c1 % per1)
    last = jnp.where(in0, per0 - 1, per1 - 1)
    return seq, pos == 0, pos == last


def _mod_row(i, tm, regions):
    (n0, l0), (n1, l1) = regions
    t1 = jnp.maximum(i * tm - n0 * l0, 0)
    return jnp.where(i * tm < n0 * l0, 0, 1 + t1 // l1)


def _assemble_kernel(xp_ref, xs_ref, pe_ref, o_ref, *, n_ctx_tiles):
    i = pl.program_id(0)

    @pl.when(i < n_ctx_tiles)
    def _():
        o_ref[...] = xp_ref[...]

    @pl.when(i >= n_ctx_tiles)
    def _():
        o_ref[...] = xs_ref[...] + pe_ref[...]


def _assemble_tokens(xp, xs, pe, regions):
    (n0, l0), (n1, l1) = regions
    tm = min(1024, l0 * n0, l1)
    t0, t1 = n0 * l0, n1 * l1
    n_ctx = t0 // tm
    pe_tiles = l1 // tm
    return pl.pallas_call(
        functools.partial(_assemble_kernel, n_ctx_tiles=n_ctx),
        out_shape=jax.ShapeDtypeStruct((t0 + t1, D_MODEL), F32),
        grid=((t0 + t1) // tm,),
        in_specs=[
            pl.BlockSpec((tm, D_MODEL), lambda i: (jnp.minimum(i, n_ctx - 1), 0)),
            pl.BlockSpec((tm, D_MODEL), lambda i: (jnp.maximum(i - n_ctx, 0), 0)),
            pl.BlockSpec((tm, D_MODEL), lambda i: (jnp.maximum(i - n_ctx, 0) % pe_tiles, 0)),
        ],
        out_specs=pl.BlockSpec((tm, D_MODEL), lambda i: (i, 0)),
        compiler_params=_params("arbitrary"),
        name="assemble_tokens",
    )(xp.reshape(t0, D_MODEL), xs.reshape(t1, D_MODEL), pe)


def _mods_kernel(cc_ref, w_ref, b_ref, o_ref):
    a = _silu(cc_ref[...])
    o_ref[...] = _dot_hilo(a, w_ref[...]) + b_ref[...]


def _modulation_table(cc, ada_w, ada_b):
    depth = ada_w.shape[0]
    n = ada_w.shape[2]
    return pl.pallas_call(
        _mods_kernel,
        out_shape=jax.ShapeDtypeStruct((depth, MOD_ROWS, n), F32),
        grid=(depth, n // TN_MODS),
        in_specs=[
            pl.BlockSpec((MOD_ROWS, D_MODEL), lambda l, j: (0, 0)),
            pl.BlockSpec((None, D_MODEL, TN_MODS), lambda l, j: (l, 0, j)),
            pl.BlockSpec((None, 1, TN_MODS), lambda l, j: (l, 0, j)),
        ],
        out_specs=pl.BlockSpec((None, MOD_ROWS, TN_MODS), lambda l, j: (l, 0, j)),
        compiler_params=_params("arbitrary", "arbitrary"),
        name="modulation_table",
    )(cc, ada_w, ada_b.reshape(depth, 1, n))


def _inproj_kernel(x_ref, mod_ref, g_ref, w_ref, wc_ref, ws_ref, o_ref, oc_ref, os_ref):
    y = _rms(x_ref[...], g_ref[...])
    u = (y * (1.0 + mod_ref[:, D_MODEL:2 * D_MODEL]) + mod_ref[:, 0:D_MODEL]).astype(BF16)
    os_ref[...] = _dot(u, ws_ref[...])
    oc_ref[...] = _dot(u, wc_ref[...]).astype(BF16)
    for c0 in range(0, D_PROJ, TN_PROJ):
        o_ref[:, c0:c0 + TN_PROJ] = _dot(u, w_ref[:, c0:c0 + TN_PROJ]).astype(BF16)


def _in_projection(x, mods, g, lw, regions):
    t = x.shape[0]
    tm = min(TM_PROJ, regions[0][0] * regions[0][1], regions[1][1])
    once = pl.Buffered(1)
    return pl.pallas_call(
        _inproj_kernel,
        out_shape=(jax.ShapeDtypeStruct((t, D_PROJ), BF16), jax.ShapeDtypeStruct((t, D_CONV), BF16),
                   jax.ShapeDtypeStruct((t, LANES), F32)),
        grid=(t // tm,),
        in_specs=[
            pl.BlockSpec((tm, D_MODEL), lambda i: (i, 0)),
            pl.BlockSpec((None, 1, 6 * D_MODEL), lambda i: (_mod_row(i, tm, regions), 0, 0)),
            pl.BlockSpec((1, D_MODEL), lambda i: (0, 0)),
            pl.BlockSpec((D_MODEL, D_PROJ), lambda i: (0, 0), pipeline_mode=once),
            pl.BlockSpec((D_MODEL, D_CONV), lambda i: (0, 0), pipeline_mode=once),
            pl.BlockSpec((D_MODEL, LANES), lambda i: (0, 0), pipeline_mode=once),
        ],
        out_specs=(pl.BlockSpec((tm, D_PROJ), lambda i: (i, 0)),
                   pl.BlockSpec((tm, D_CONV), lambda i: (i, 0)),
                   pl.BlockSpec((tm, LANES), lambda i: (i, 0))),
        compiler_params=_params("parallel"),
        name="norm_in_projection",
    )(x, mods, g, lw["w_in"], lw["w_in_conv"], lw["w_in_small"])


def _ssd_decay_terms(sm, dtb, alog, q):
    lane = lax.broadcasted_iota(jnp.int32, (1, LANES), 1)
    dtv = _softplus(sm + dtb)
    a_neg = jnp.where(lane < 2 * SSD_HEADS, -jnp.exp(alog), 0.0)
    a = dtv * a_neg
    acs = _dot_exact_lhs(_block_tri(q, q, False).astype(BF16), a)
    rev = _dot_exact_lhs(_block_tri(q, q, True).astype(BF16), a)
    m = jnp.where(lane < SSD_HEADS, acs, rev)
    tot = acs[q - 1:q, :]
    return m, tot, dtv


def _col(x, idx, width=LANES):
    return jnp.broadcast_to(x[:, idx:idx + 1], (x.shape[0], width))


def _head_expand_matrix(base):
    k = lax.broadcasted_iota(jnp.int32, (LANES, SSD_INNER), 0)
    c = lax.broadcasted_iota(jnp.int32, (LANES, SSD_INNER), 1)
    return jnp.where(k == base + c // SSD_HEADDIM, 1.0, 0.0).astype(BF16)


def _pair_cols(x, base, p, lo_half):
    return jnp.where(lo_half, _col(x, base + 2 * p), _col(x, base + 2 * p + 1))


def _expand_state(s):
    lane = lax.broadcasted_iota(jnp.int32, s.shape, 1)
    half = SSD_INNER // SSD_GROUPS
    return jnp.concatenate([jnp.where(lane < half, s, 0.0), jnp.where(lane >= half, s, 0.0)], axis=0)


def _compact_state(s2):
    return s2[0:SSD_STATE, :] + s2[SSD_STATE:2 * SSD_STATE, :]


def _state_update_mask():
    row = lax.broadcasted_iota(jnp.int32, (SSD_GROUPS * SSD_STATE, SSD_INNER), 0)
    lane = lax.broadcasted_iota(jnp.int32, (SSD_GROUPS * SSD_STATE, SSD_INNER), 1)
    return (row < SSD_STATE) == (lane < SSD_INNER // SSD_GROUPS)


HALO = BF16_SUBLANES
CONV_PAD = SSD_CONV // 2


def _ssd_bwd_kernel(xbc_ref, prev_ref, next_ref, sm_ref, cw_ref, cb_ref, dtb_ref, alog_ref, init_ref,
                    conv_ref, tab_ref, enter_ref, fin_ref, s_ref, ext_ref, *, regions, n_steps):
    q = SSD_CHUNK
    st = SCAN_STEP
    c = n_steps - 1 - pl.program_id(0)
    _, first, last = _chunk_pos(c, st, regions)

    @pl.when(last)
    def _():
        s_ref[...] = _expand_state(init_ref[...])

    ext_ref[0:HALO, :] = jnp.where(first, 0.0, prev_ref[...].astype(F32))
    ext_ref[HALO:HALO + st, :] = xbc_ref[...].astype(F32)
    ext_ref[HALO + st:2 * HALO + st, :] = jnp.where(last, 0.0, next_ref[...].astype(F32))
    acc = cb_ref[...]
    for k in range(SSD_CONV):
        off = HALO - CONV_PAD + k
        acc = acc + cw_ref[k:k + 1, :] * ext_ref[off:off + st, :]
    conv_ref[...] = _silu(acc).astype(BF16)

    lo_half = lax.broadcasted_iota(jnp.int32, (1, LANES), 1) < SSD_HEADDIM
    upd_mask = _state_update_mask()

    s_cur = s_ref[...]
    for ci in reversed(range(st // q)):
        rows = slice(ci * q, (ci + 1) * q)
        xs = conv_ref[rows, 0:SSD_INNER].astype(F32)
        bsb = conv_ref[rows, SSD_INNER:SSD_INNER + LANES]
        m, tot, dtv = _ssd_decay_terms(sm_ref[rows, :], dtb_ref[...], alog_ref[...], q)
        tab_ref[rows, 0:LANES] = m
        tab_ref[rows, LANES:2 * LANES] = dtv
        wgt = jnp.exp(tot - m) * dtv
        etot = jnp.exp(tot)
        enter_ref[ci] = _compact_state(s_cur)
        xw, dec = [], []
        for p in range(SSD_HEADS // 2):
            sl = slice(p * LANES, (p + 1) * LANES)
            xw.append((xs[:, sl] * _pair_cols(wgt, SSD_HEADS, p, lo_half)).astype(BF16))
            dec.append(_pair_cols(etot, SSD_HEADS, p, lo_half))
        upd = _dot_tn(bsb, jnp.concatenate(xw, axis=1))
        s_cur = s_cur * jnp.concatenate(dec, axis=1) + jnp.where(upd_mask, upd, 0.0)
    s_ref[...] = s_cur

    @pl.when(first)
    def _():
        fin_ref[...] = _compact_state(s_cur)


def _ssd_out_kernel(z_ref, x_ref, bc_ref, tab_ref, dexp_ref, ng_ref,
                    init_ref, enter_ref, y_ref, fin_ref, s_ref, yacc_ref, *, regions):
    q = SSD_CHUNK
    c = pl.program_id(0)
    _, first, last = _chunk_pos(c, SCAN_STEP, regions)

    @pl.when(first)
    def _():
        s_ref[...] = _expand_state(init_ref[...])

    lane = lax.broadcasted_iota(jnp.int32, (1, LANES), 1)
    lo_half = lane < SSD_HEADDIM
    tril = _block_tri(q, q, False)
    ii = lax.broadcasted_iota(jnp.int32, (q, q), 0)
    jj = lax.broadcasted_iota(jnp.int32, (q, q), 1)
    eye = ii == jj
    upd_mask = _state_update_mask()
    expand = _head_expand_matrix(0)
    dexp = dexp_ref[...]
    ng = ng_ref[...]

    s_cur = s_ref[...]
    for ci in range(SCAN_STEP // q):
        rows = slice(ci * q, (ci + 1) * q)
        bsb = bc_ref[rows, 0:LANES]
        csb = bc_ref[rows, LANES:2 * LANES]
        m = tab_ref[rows, 0:LANES]
        dtv = tab_ref[rows, LANES:2 * LANES]
        wgt = jnp.exp(jnp.where(lane < SSD_HEADS, m[q - 1:q, :] - m, 0.0)) * dtv
        mt = (m - jnp.log(dtv)).T
        ldt = jnp.log(dtv + pltpu.roll(dtv, LANES - SSD_HEADS, 1)).T
        zero_b = jnp.zeros_like(csb)
        cb = [_dot_nt(jnp.where(lo_half, csb, zero_b), bsb).astype(BF16),
              _dot_nt(jnp.where(lo_half, zero_b, csb), bsb).astype(BF16)]
        cs_f = _dot(csb, s_cur.astype(BF16))
        cs_b = _dot(csb, _expand_state(enter_ref[ci]).astype(BF16))
        wexp = _dot(wgt.astype(BF16), expand)

        xw, dec = [], []
        ssq = jnp.zeros((q, LANES), F32)
        for p in range(SSD_HEADS // 2):
            sl = slice(p * LANES, (p + 1) * LANES)
            g = (2 * p) // (SSD_HEADS // SSD_GROUPS)
            xs_pb = x_ref[rows, sl]
            xs_p = xs_pb.astype(F32)
            yd, colf, colb = [], [], []
            for h in (2 * p, 2 * p + 1):
                hb = SSD_HEADS + h
                cf = _col(m, h, q)
                cbk = _col(m, hb, q)
                seg = jnp.where(tril, cf - mt[h:h + 1, :], cbk - mt[hb:hb + 1, :])
                seg = jnp.where(eye, ldt[h:h + 1, :], seg)
                yd.append(_dot(cb[g] * jnp.exp(seg).astype(BF16), xs_pb))
                colf.append(cf)
                colb.append(cbk)
            ef = jnp.exp(jnp.where(lo_half, colf[0], colf[1]))
            eb = jnp.exp(jnp.where(lo_half, colb[0], colb[1]))
            y_p = (jnp.where(lo_half, yd[0], yd[1]) + ef * cs_f[:, sl] + eb * cs_b[:, sl]
                   + dexp[:, sl] * xs_p) * _silu(z_ref[rows, sl].astype(F32))
            yacc_ref[:, sl] = y_p
            ssq = ssq + y_p * y_p
            dec.append(ef[q - 1:q, :])
            xw.append((xs_p * wexp[:, sl]).astype(BF16))

        scale = lax.rsqrt(jnp.sum(ssq, axis=-1, keepdims=True) * (1.0 / SSD_INNER) + EPS)
        y_ref[rows, :] = (yacc_ref[...] * scale * ng).astype(y_ref.dtype)

        upd = _dot_tn(bsb, jnp.concatenate(xw, axis=1))
        s_cur = s_cur * jnp.concatenate(dec, axis=1) + jnp.where(upd_mask, upd, 0.0)
    s_ref[...] = s_cur

    @pl.when(last)
    def _():
        fin_ref[...] = _compact_state(s_cur)


def _const_spec(a):
    return pl.BlockSpec(a.shape, lambda s: (0,) * a.ndim)


def _sweep_call(parts, n_steps, name):
    n_in = [len(p["inputs"]) for p in parts]
    n_out = [len(p["out_shape"]) for p in parts]
    n_scr = [len(p["scratch"]) for p in parts]

    def kernel(*refs):
        ins, outs, scr = refs[:sum(n_in)], refs[sum(n_in):sum(n_in) + sum(n_out)], refs[sum(n_in) + sum(n_out):]
        for k, p in enumerate(parts):
            i0, o0, s0 = sum(n_in[:k]), sum(n_out[:k]), sum(n_scr[:k])
            p["kernel"](*ins[i0:i0 + n_in[k]], *outs[o0:o0 + n_out[k]], *scr[s0:s0 + n_scr[k]])

    res = pl.pallas_call(
        kernel,
        out_shape=tuple(o for p in parts for o in p["out_shape"]),
        grid=(n_steps,),
        in_specs=[s for p in parts for s in p["in_specs"]],
        out_specs=tuple(s for p in parts for s in p["out_specs"]),
        scratch_shapes=[s for p in parts for s in p["scratch"]],
        compiler_params=_params("arbitrary"),
        name=name,
    )(*[a for p in parts for a in p["inputs"]])
    return [res[sum(n_out[:k]):sum(n_out[:k]) + n_out[k]] for k in range(len(parts))]


def _ssd_bwd_part(xbc, small, init, lw, regions):
    st = SCAN_STEP
    cps = st // SSD_CHUNK
    t = xbc.shape[0]
    n_steps = t // st
    n_seq = regions[0][0] + regions[1][0]
    bidx = lambda s: n_steps - 1 - s
    seq_b = lambda s: _chunk_pos(bidx(s), st, regions)[0]
    hb = st // HALO
    n_hb = t // HALO
    consts = [lw["conv_w"], lw["conv_b"], lw["dt_bias"], lw["a_log"]]
    return dict(
        kernel=functools.partial(_ssd_bwd_kernel, regions=regions, n_steps=n_steps),
        inputs=[xbc, xbc, xbc, small, *consts, init],
        in_specs=[pl.BlockSpec((st, D_CONV), lambda s: (bidx(s), 0)),
                  pl.BlockSpec((HALO, D_CONV), lambda s: (jnp.maximum(bidx(s) * hb - 1, 0), 0)),
                  pl.BlockSpec((HALO, D_CONV), lambda s: (jnp.minimum((bidx(s) + 1) * hb, n_hb - 1), 0)),
                  pl.BlockSpec((st, LANES), lambda s: (bidx(s), 0))]
        + [_const_spec(a) for a in consts]
        + [pl.BlockSpec((None, None, SSD_STATE, SSD_INNER), lambda s: (seq_b(s), 1, 0, 0))],
        out_shape=[jax.ShapeDtypeStruct((t, D_CONV), BF16),
                   jax.ShapeDtypeStruct((t, 2 * LANES), F32),
                   jax.ShapeDtypeStruct((n_steps * cps, SSD_STATE, SSD_INNER), F32),
                   jax.ShapeDtypeStruct((n_seq, SSD_STATE, SSD_INNER), F32)],
        out_specs=[pl.BlockSpec((st, D_CONV), lambda s: (bidx(s), 0)),
                   pl.BlockSpec((st, 2 * LANES), lambda s: (bidx(s), 0)),
                   pl.BlockSpec((cps, SSD_STATE, SSD_INNER), lambda s: (bidx(s), 0, 0)),
                   pl.BlockSpec((None, SSD_STATE, SSD_INNER), lambda s: (seq_b(s), 0, 0))],
        scratch=[pltpu.VMEM((SSD_GROUPS * SSD_STATE, SSD_INNER), F32),
                 pltpu.VMEM((st + 2 * HALO, D_CONV), F32)])


def _ssd_fwd_part(proj, xbc, tables, init, enter_b, lw, regions):
    st = SCAN_STEP
    cps = st // SSD_CHUNK
    t = proj.shape[0]
    n_seq = regions[0][0] + regions[1][0]
    seq_f = lambda s: _chunk_pos(s, st, regions)[0]
    consts = [lw["d_exp"], lw["ssd_norm_g"]]
    return dict(
        kernel=functools.partial(_ssd_out_kernel, regions=regions),
        inputs=[proj, xbc, xbc, tables, *consts, init, enter_b],
        in_specs=[pl.BlockSpec((st, SSD_INNER), lambda s: (s, COL_Z // SSD_INNER)),
                  pl.BlockSpec((st, SSD_INNER), lambda s: (s, CONV_X // SSD_INNER)),
                  pl.BlockSpec((st, 2 * LANES), lambda s: (s, CONV_BC // (2 * LANES))),
                  pl.BlockSpec((st, 2 * LANES), lambda s: (s, 0))]
        + [_const_spec(a) for a in consts]
        + [pl.BlockSpec((None, None, SSD_STATE, SSD_INNER), lambda s: (seq_f(s), 0, 0, 0)),
           pl.BlockSpec((cps, SSD_STATE, SSD_INNER), lambda s: (s, 0, 0))],
        out_shape=[jax.ShapeDtypeStruct((t, SSD_INNER), BF16),
                   jax.ShapeDtypeStruct((n_seq, SSD_STATE, SSD_INNER), F32)],
        out_specs=[pl.BlockSpec((st, SSD_INNER), lambda s: (s, 0)),
                   pl.BlockSpec((None, SSD_STATE, SSD_INNER), lambda s: (seq_f(s), 0, 0))],
        scratch=[pltpu.VMEM((SSD_GROUPS * SSD_STATE, SSD_INNER), F32),
                 pltpu.VMEM((SSD_CHUNK, SSD_INNER), F32)])


def _gla_log_decay(sm, gw_hi_ref, gw_lo_ref, gb_ref):
    s1, s2 = _split2(sm)
    logit = _dot(s1, gw_hi_ref[...]) + _dot(s2, gw_hi_ref[...]) + _dot(s1, gw_lo_ref[...]) + gb_ref[...]
    return -_softplus(-logit) * (1.0 / GLA_GATE_TAU)


def _chunk_rows(x, n_chunks, rows_per_chunk):
    return jnp.concatenate([jnp.broadcast_to(x[c:c + 1, :], (rows_per_chunk, x.shape[1]))
                            for c in range(n_chunks)], axis=0)


def _gla_bwd_kernel(qk_ref, v_ref, sm_ref, gwh_ref, gwl_ref, gb_ref, init_ref,
                    rev_ref, enter_ref, fin_ref, s_ref, *, regions, n_steps):
    cq = GLA_CHUNK
    st = SCAN_STEP
    ncs = st // cq
    c = n_steps - 1 - pl.program_id(0)
    _, first, last = _chunk_pos(c, st, regions)

    @pl.when(last)
    def _():
        s_ref[...] = init_ref[...]

    bwd = slice(GLA_KDIM, 2 * GLA_KDIM)
    lg = _gla_log_decay(sm_ref[...], gwh_ref.at[:, bwd], gwl_ref.at[:, bwd], gb_ref.at[:, bwd])
    rev = _dot_exact_lhs(_block_tri(st, cq, True).astype(BF16), lg)
    rev_ref[...] = rev
    tot = jnp.concatenate([rev[ci * cq:ci * cq + 1, :] for ci in range(ncs)], axis=0)
    ko = (qk_ref[:, GLA_KDIM:2 * GLA_KDIM].astype(F32) * jnp.exp(_chunk_rows(tot, ncs, cq) - rev)).astype(BF16)
    etot = jnp.exp(tot)
    lo_half = lax.broadcasted_iota(jnp.int32, (1, LANES), 1) < GLA_DK

    s_cur = s_ref[...]
    for ci in reversed(range(ncs)):
        rows = slice(ci * cq, (ci + 1) * cq)
        enter_ref[ci] = s_cur
        new = []
        for p in range(GLA_HEADS // 2):
            sl = slice(p * LANES, (p + 1) * LANES)
            u0 = _dot_tn(v_ref[rows, (2 * p) * GLA_DV:(2 * p + 1) * GLA_DV], ko[rows, sl])
            u1 = _dot_tn(v_ref[rows, (2 * p + 1) * GLA_DV:(2 * p + 2) * GLA_DV], ko[rows, sl])
            new.append(s_cur[:, sl] * etot[ci:ci + 1, sl] + jnp.where(lo_half, u0, u1))
        s_cur = jnp.concatenate(new, axis=1)
    s_ref[...] = s_cur

    @pl.when(first)
    def _():
        fin_ref[...] = s_cur


def _gla_out_kernel(qk_ref, v_ref, r_ref, sm_ref, gwh_ref, gwl_ref, gb_ref, ng_ref, init_ref, enter_ref, rev_ref,
                    y_ref, fin_ref, s_ref, *, regions):
    cq = GLA_CHUNK
    st = SCAN_STEP
    ncs = st // cq
    c = pl.program_id(0)
    _, first, last = _chunk_pos(c, st, regions)

    @pl.when(first)
    def _():
        s_ref[...] = init_ref[...]

    fwd = slice(0, GLA_KDIM)
    lg = _gla_log_decay(sm_ref[...], gwh_ref.at[:, fwd], gwl_ref.at[:, fwd], gb_ref.at[:, fwd])
    gcs = _dot_exact_lhs(_block_tri(st, cq, False).astype(BF16), lg)
    rev = rev_ref[...]
    lo_half = lax.broadcasted_iota(jnp.int32, (1, LANES), 1) < GLA_DK
    zero_b = jnp.zeros((cq, LANES), BF16)
    ii = lax.broadcasted_iota(jnp.int32, (2 * cq, cq), 0) % cq
    jj = lax.broadcasted_iota(jnp.int32, (2 * cq, cq), 1)
    tril2 = jj <= ii
    triu2 = jj >= ii
    ng = ng_ref[...]

    def stack_heads(x):
        return jnp.concatenate([jnp.where(lo_half, x, zero_b), jnp.where(lo_half, zero_b, x)], axis=0)

    s_cur = s_ref[...]
    for ci in range(ncs):
        rows = slice(ci * cq, (ci + 1) * cq)
        g_f = gcs[rows]
        g_b = rev[rows]
        etot = jnp.exp(g_f[cq - 1:cq, :])
        qs = qk_ref[rows, 0:GLA_KDIM].astype(F32) * (GLA_DK ** -0.5)
        ks = qk_ref[rows, GLA_KDIM:2 * GLA_KDIM].astype(F32)
        qe_f = (qs * jnp.exp(g_f)).astype(BF16)
        ke = ks * jnp.exp(-g_f)
        ke_f = ke.astype(BF16)
        ko_f = (ke * etot).astype(BF16)
        qe_b = (qs * jnp.exp(g_b)).astype(BF16)
        ke_b = (ks * jnp.exp(-g_b)).astype(BF16)
        s_curb = s_cur.astype(BF16)
        s_entb = enter_ref[ci].astype(BF16)
        new = []
        for p in range(GLA_HEADS // 2):
            sl = slice(p * LANES, (p + 1) * LANES)
            qf2 = stack_heads(qe_f[:, sl])
            qb2 = stack_heads(qe_b[:, sl])
            att = (jnp.where(tril2, _dot_nt(qf2, ke_f[:, sl]), 0.0)
                   + jnp.where(triu2, _dot_nt(qb2, ke_b[:, sl]), 0.0)).astype(BF16)
            inter = _dot_nt(qf2, s_curb[:, sl]) + _dot_nt(qb2, s_entb[:, sl])
            upd = []
            for hh in (0, 1):
                h = 2 * p + hh
                hs = slice(h * GLA_DV, (h + 1) * GLA_DV)
                hr = slice(hh * cq, (hh + 1) * cq)
                vh = v_ref[rows, hs]
                o = _dot(att[hr], vh) + inter[hr]
                o = o * lax.rsqrt(jnp.mean(o * o, axis=-1, keepdims=True) + EPS) * ng[:, hs]
                y_ref[rows, hs] = (o * _silu(r_ref[rows, hs].astype(F32))).astype(y_ref.dtype)
                upd.append(_dot_tn(vh, ko_f[:, sl]))
            new.append(s_cur[:, sl] * etot[:, sl] + jnp.where(lo_half, upd[0], upd[1]))
        s_cur = jnp.concatenate(new, axis=1)
    s_ref[...] = s_cur

    @pl.when(last)
    def _():
        fin_ref[...] = s_cur


def _gla_bwd_part(proj, small, init, lw, regions):
    st = SCAN_STEP
    ncs = st // GLA_CHUNK
    t = proj.shape[0]
    n_steps = t // st
    n_seq = regions[0][0] + regions[1][0]
    consts = [lw["gate_w_hi"], lw["gate_w_lo"], lw["gate_b"]]
    bidx = lambda s: n_steps - 1 - s
    seq_b = lambda s: _chunk_pos(bidx(s), st, regions)[0]
    return dict(
        kernel=functools.partial(_gla_bwd_kernel, regions=regions, n_steps=n_steps),
        inputs=[proj, proj, small, *consts, init],
        in_specs=[pl.BlockSpec((st, 2 * GLA_KDIM), lambda s: (bidx(s), COL_QK // (2 * GLA_KDIM))),
                  pl.BlockSpec((st, GLA_VDIM), lambda s: (bidx(s), COL_V // GLA_VDIM)),
                  pl.BlockSpec((st, LANES), lambda s: (bidx(s), 0))]
        + [_const_spec(a) for a in consts]
        + [pl.BlockSpec((None, None, GLA_DV, GLA_KDIM), lambda s: (seq_b(s), 1, 0, 0))],
        out_shape=[jax.ShapeDtypeStruct((t, GLA_KDIM), F32),
                   jax.ShapeDtypeStruct((n_steps * ncs, GLA_DV, GLA_KDIM), F32),
                   jax.ShapeDtypeStruct((n_seq, GLA_DV, GLA_KDIM), F32)],
        out_specs=[pl.BlockSpec((st, GLA_KDIM), lambda s: (bidx(s), 0)),
                   pl.BlockSpec((ncs, GLA_DV, GLA_KDIM), lambda s: (bidx(s), 0, 0)),
                   pl.BlockSpec((None, GLA_DV, GLA_KDIM), lambda s: (seq_b(s), 0, 0))],
        scratch=[pltpu.VMEM((GLA_DV, GLA_KDIM), F32)])


def _gla_fwd_part(proj, small, init, enter_b, rev_b, lw, regions):
    st = SCAN_STEP
    ncs = st // GLA_CHUNK
    t = proj.shape[0]
    n_seq = regions[0][0] + regions[1][0]
    consts = [lw["gate_w_hi"], lw["gate_w_lo"], lw["gate_b"], lw["gla_norm_g"]]
    seq_f = lambda s: _chunk_pos(s, st, regions)[0]
    return dict(
        kernel=functools.partial(_gla_out_kernel, regions=regions),
        inputs=[proj, proj, proj, small, *consts, init, enter_b, rev_b],
        in_specs=[pl.BlockSpec((st, 2 * GLA_KDIM), lambda s: (s, COL_QK // (2 * GLA_KDIM))),
                  pl.BlockSpec((st, GLA_VDIM), lambda s: (s, COL_V // GLA_VDIM)),
                  pl.BlockSpec((st, GLA_VDIM), lambda s: (s, COL_R // GLA_VDIM)),
                  pl.BlockSpec((st, LANES), lambda s: (s, 0))]
        + [_const_spec(a) for a in consts]
        + [pl.BlockSpec((None, None, GLA_DV, GLA_KDIM), lambda s: (seq_f(s), 0, 0, 0)),
           pl.BlockSpec((ncs, GLA_DV, GLA_KDIM), lambda s: (s, 0, 0)),
           pl.BlockSpec((st, GLA_KDIM), lambda s: (s, 0))],
        out_shape=[jax.ShapeDtypeStruct((t, GLA_VDIM), BF16),
                   jax.ShapeDtypeStruct((n_seq, GLA_DV, GLA_KDIM), F32)],
        out_specs=[pl.BlockSpec((st, GLA_VDIM), lambda s: (s, 0)),
                   pl.BlockSpec((None, GLA_DV, GLA_KDIM), lambda s: (seq_f(s), 0, 0))],
        scratch=[pltpu.VMEM((GLA_DV, GLA_KDIM), F32)])


def _mixers(proj, xbc, small, init_ssd, init_gla, lw, regions):
    n_steps = proj.shape[0] // SCAN_STEP
    (xbc_c, tab_s, enter_s, fin_sb), (rev_g, enter_g, fin_gb) = _sweep_call(
        [_ssd_bwd_part(xbc, small, init_ssd, lw, regions), _gla_bwd_part(proj, small, init_gla, lw, regions)],
        n_steps, "scan_backward_states")
    (y_ssd, fin_sf), (y_gla, fin_gf) = _sweep_call(
        [_ssd_fwd_part(proj, xbc_c, tab_s, init_ssd, enter_s, lw, regions),
         _gla_fwd_part(proj, small, init_gla, enter_g, rev_g, lw, regions)],
        n_steps, "scan_forward_outputs")
    return y_ssd, fin_sf, fin_sb, y_gla, fin_gf, fin_gb


def _merge_kernel(x_ref, ys_ref, yg_ref, ga_ref, gb_ref, mod_ref, wbs_ref, wbg_ref, wo_ref, o_ref):
    merged = (_sigmoid(ga_ref[...].astype(F32)) * _dot(ys_ref[...], wbs_ref[...])
              + _sigmoid(gb_ref[...].astype(F32)) * _dot(yg_ref[...], wbg_ref[...]))
    mix = _dot(merged.astype(BF16), wo_ref[...])
    o_ref[...] = x_ref[...] + mod_ref[:, 2 * D_MODEL:3 * D_MODEL] * mix


def _merge(x, y_ssd, y_gla, proj, mods, lw, regions):
    t = x.shape[0]
    tm = min(TM_MERGE, regions[0][0] * regions[0][1], regions[1][1])
    tok = lambda col: pl.BlockSpec((tm, D_MODEL), lambda i: (i, col))
    wspec = pl.BlockSpec((D_MODEL, D_MODEL), lambda i: (0, 0))
    return pl.pallas_call(
        _merge_kernel,
        out_shape=jax.ShapeDtypeStruct((t, D_MODEL), F32),
        grid=(t // tm,),
        in_specs=[tok(0), tok(0), tok(0), tok(COL_GA // D_MODEL), tok(COL_GB // D_MODEL),
                  pl.BlockSpec((None, 1, 6 * D_MODEL), lambda i: (_mod_row(i, tm, regions), 0, 0)),
                  wspec, wspec, wspec],
        out_specs=tok(0),
        compiler_params=_params("parallel"),
        name="merge_out_projection",
    )(x, y_ssd, y_gla, proj, proj, mods, lw["w_bs"], lw["w_bg"], lw["w_o"])


def _ffn_prologue(x_ref, mod_ref, g_ref):
    y = _rms(x_ref[...], g_ref[...])
    return y * (1.0 + mod_ref[:, 4 * D_MODEL:5 * D_MODEL]) + mod_ref[:, 3 * D_MODEL:4 * D_MODEL]


def _swiglu_resident(v, w1_ref, w3_ref, w2_ref):
    acc = None
    for hh in range(w1_ref.shape[1] // TH_FFN):
        cols = slice(hh * TH_FFN, (hh + 1) * TH_FFN)
        h = _silu(_dot(v, w1_ref[:, cols])) * _dot(v, w3_ref[:, cols])
        part = _dot(h.astype(BF16), w2_ref[cols, :])
        acc = part if acc is None else acc + part
    return acc


def _ffn_kernel(x_ref, mod_ref, g_ref, w1_ref, w3_ref, w2_ref, o_ref):
    v = _ffn_prologue(x_ref, mod_ref, g_ref).astype(BF16)
    o_ref[...] = x_ref[...] + mod_ref[:, 5 * D_MODEL:6 * D_MODEL] * _swiglu_resident(v, w1_ref, w3_ref, w2_ref)


def _dense_ffn(x, mods, g, w1, w3, w2, regions):
    t = x.shape[0]
    tm = min(TM_FFN, regions[0][0] * regions[0][1], regions[1][1])
    hid = w1.shape[1]
    once = pl.Buffered(1)
    return pl.pallas_call(
        _ffn_kernel,
        out_shape=jax.ShapeDtypeStruct((t, D_MODEL), F32),
        grid=(t // tm,),
        in_specs=[pl.BlockSpec((tm, D_MODEL), lambda i: (i, 0)),
                  pl.BlockSpec((None, 1, 6 * D_MODEL), lambda i: (_mod_row(i, tm, regions), 0, 0)),
                  pl.BlockSpec((1, D_MODEL), lambda i: (0, 0)),
                  pl.BlockSpec((D_MODEL, hid), lambda i: (0, 0), pipeline_mode=once),
                  pl.BlockSpec((D_MODEL, hid), lambda i: (0, 0), pipeline_mode=once),
                  pl.BlockSpec((hid, D_MODEL), lambda i: (0, 0), pipeline_mode=once)],
        out_specs=pl.BlockSpec((tm, D_MODEL), lambda i: (i, 0)),
        compiler_params=_params("parallel"),
        name="dense_swiglu",
    )(x, mods, g, w1, w3, w2)


def _top2_gates(logits):
    lane = lax.broadcasted_iota(jnp.int32, logits.shape, 1)
    lg = jnp.where(lane < N_EXPERTS, logits, -jnp.inf)
    m1 = jnp.max(lg, axis=-1, keepdims=True)
    i1 = jnp.min(jnp.where(lg == m1, lane, LANES), axis=-1, keepdims=True)
    lg2 = jnp.where(lane == i1, -jnp.inf, lg)
    m2 = jnp.max(lg2, axis=-1, keepdims=True)
    i2 = jnp.min(jnp.where(lg2 == m2, lane, LANES), axis=-1, keepdims=True)
    e2 = jnp.exp(m2 - m1)
    den = 1.0 + e2
    return jnp.where(lane == i1, 1.0 / den, 0.0) + jnp.where(lane == i2, e2 / den, 0.0)


def _lane_scalar(vec, e):
    lane = lax.broadcasted_iota(jnp.int32, vec.shape, 1)
    return jnp.sum(jnp.where(lane == e, vec, 0.0)).astype(jnp.int32)


def _n_row_blocks(n):
    return (n + MOE_ROW_BLOCK - 1) // MOE_ROW_BLOCK


def _moe_block_copy(stage_ref, sem, slot, hbm_ref, row, to_hbm):
    if not isinstance(row, int):
        row = pl.multiple_of(row, MOE_ROW_ALIGN)
    rows = hbm_ref.at[pl.ds(row, MOE_ROW_BLOCK), :]
    if to_hbm:
        return pltpu.make_async_copy(stage_ref.at[slot], rows, sem.at[slot])
    return pltpu.make_async_copy(rows, stage_ref.at[slot], sem.at[slot])


def _moe_dispatch_kernel(x_ref, mod_ref, g_ref, rw_ref, rb_ref, donor_hbm,
                         xs_hbm, gate_ref, rank_ref, off_ref, tot_ref,
                         v_ref, gate_t_ref, rank_t_ref, stage_ref, sem, run_ref, pend_ref, *, cap):
    del donor_hbm
    i = pl.program_id(0)
    tm = x_ref.shape[0]
    rb = MOE_ROW_BLOCK

    @pl.when(i == 0)
    def _():
        run_ref[...] = jnp.zeros_like(run_ref)
        pend_ref[0] = 0

    v = _ffn_prologue(x_ref, mod_ref, g_ref)
    v_ref[...] = v.astype(BF16)
    gate = _top2_gates(_dot_hilo(v, rw_ref[...]) + rb_ref[...])
    sel_f = jnp.where(gate > 0.0, 1.0, 0.0)
    sel = sel_f.astype(BF16)
    gate_ref[...] = gate
    gate_t_ref[...] = gate.T
    for blk in range(tm // rb):
        ii = lax.broadcasted_iota(jnp.int32, (rb, tm), 0) + blk * rb
        jj = lax.broadcasted_iota(jnp.int32, (rb, tm), 1)
        rank = _dot(jnp.where(jj < ii, 1.0, 0.0).astype(BF16), sel)
        rank_ref[blk * rb:(blk + 1) * rb, :] = rank
        rank_t_ref[:, blk * rb:(blk + 1) * rb] = rank.T

    cnt = jnp.sum(sel_f, axis=0, keepdims=True)
    cnt_pad = jnp.floor((cnt + (MOE_ROW_ALIGN - 1)) * (1.0 / MOE_ROW_ALIGN)) * MOE_ROW_ALIGN
    run = run_ref[...]
    off_ref[...] = run

    def wait_prev(s, carry):
        _moe_block_copy(stage_ref, sem, s, xs_hbm, 0, True).wait()
        return carry
    lax.fori_loop(0, pend_ref[0], wait_prev, 0)

    slot = jnp.int32(0)
    for e in range(N_EXPERTS):
        n_e = _lane_scalar(cnt, e)
        base = e * cap + _lane_scalar(run, e)
        g_row = gate_t_ref[e:e + 1, :]
        r_row = rank_t_ref[e:e + 1, :].astype(jnp.int32)

        def pack(b, s, base=base, g_row=g_row, r_row=r_row):
            row = lax.broadcasted_iota(jnp.int32, (rb, tm), 0) + b * rb
            onehot = jnp.where(jnp.logical_and(row == r_row, g_row > 0.0), 1.0, 0.0).astype(BF16)
            stage_ref[s] = _dot(onehot, v_ref[...]).astype(BF16)
            _moe_block_copy(stage_ref, sem, s, xs_hbm, base + b * rb, True).start()
            return s + 1
        slot = lax.fori_loop(0, _n_row_blocks(n_e), pack, slot)

    pend_ref[0] = slot
    run_ref[...] = run + cnt_pad
    tot_ref[...] = run + cnt_pad

    @pl.when(i == pl.num_programs(0) - 1)
    def _():
        def wait_last(s, carry):
            _moe_block_copy(stage_ref, sem, s, xs_hbm, 0, True).wait()
            return carry
        lax.fori_loop(0, slot, wait_last, 0)


def _moe_expert_kernel(te_ref, tr_ref, na_ref, xs_ref, w1_ref, w3_ref, w2_ref, ys_ref):
    del te_ref, tr_ref

    @pl.when(pl.program_id(0) < na_ref[0])
    def _():
        ys_ref[...] = _swiglu_resident(xs_ref[...], w1_ref, w3_ref, w2_ref).astype(ys_ref.dtype)


def _moe_combine_kernel(off_ref, x_ref, mod_ref, gate_ref, rank_ref, ys_hbm, o_ref, stage_ref, sem, *, cap):
    i = pl.program_id(0)
    tm = x_ref.shape[0]
    rb = MOE_ROW_BLOCK
    gate = gate_ref[...]
    rank = rank_ref[...]
    cnt = jnp.sum(jnp.where(gate > 0.0, 1.0, 0.0), axis=0, keepdims=True)
    n_blocks = [_n_row_blocks(_lane_scalar(cnt, e)) for e in range(N_EXPERTS)]
    bases = [e * cap + off_ref[i * LANES + e] for e in range(N_EXPERTS)]

    slot = jnp.int32(0)
    for e in range(N_EXPERTS):
        def fetch(b, s, base=bases[e]):
            _moe_block_copy(stage_ref, sem, s, ys_hbm, base + b * rb, False).start()
            return s + 1
        slot = lax.fori_loop(0, n_blocks[e], fetch, slot)

    o_ref[...] = jnp.zeros_like(o_ref)
    slot = jnp.int32(0)
    for e in range(N_EXPERTS):
        g_col = gate[:, e:e + 1]
        r_col = rank[:, e:e + 1].astype(jnp.int32)

        def combine(b, s, g_col=g_col, r_col=r_col):
            _moe_block_copy(stage_ref, sem, s, ys_hbm, 0, False).wait()
            col = lax.broadcasted_iota(jnp.int32, (tm, rb), 1) + b * rb
            onehot = jnp.where(jnp.logical_and(col == r_col, g_col > 0.0), 1.0, 0.0).astype(BF16)
            o_ref[...] += g_col * _dot(onehot, stage_ref[s])
            return s + 1
        slot = lax.fori_loop(0, n_blocks[e], combine, slot)

    o_ref[...] = x_ref[...] + mod_ref[:, 5 * D_MODEL:6 * D_MODEL] * o_ref[...]


def _moe_tile_table(totals, cap, n_steps):
    extent = totals
    n_tiles = (extent + TR_EXPERT - 1) // TR_EXPERT
    ends = jnp.cumsum(n_tiles)
    starts = ends - n_tiles
    n_active = ends[-1]
    t = jnp.minimum(jnp.arange(n_steps, dtype=jnp.int32), n_active - 1)
    e = jnp.sum((t[:, None] >= ends[None, :]).astype(jnp.int32), axis=1)
    row_block = e * (cap // TR_EXPERT) + (t - starts[e])
    return e.astype(jnp.int32), row_block.astype(jnp.int32), n_active.reshape(1).astype(jnp.int32)


def _moe_geometry(t, regions, n_e):
    tm = min(TM_MOE, regions[0][0] * regions[0][1], regions[1][1])
    n_tiles = t // tm
    cap_rows = t + n_tiles * (MOE_ROW_ALIGN - 1) + MOE_ROW_BLOCK
    cap = -(-cap_rows // TR_EXPERT) * TR_EXPERT
    return tm, n_tiles, cap


def _moe_ffn(x, mods, g, rw, rb, w1, w3, w2, regions, donor):
    t = x.shape[0]
    n_e, _, hid = w1.shape
    tm, n_tiles, cap = _moe_geometry(t, regions, n_e)
    max_slots = 2 * tm // MOE_ROW_BLOCK + n_e
    n_steps = (2 * t + n_tiles * n_e * (MOE_ROW_ALIGN - 1) + n_e * MOE_ROW_BLOCK) // TR_EXPERT + n_e
    mod_spec = pl.BlockSpec((None, 1, 6 * D_MODEL), lambda i, *_: (_mod_row(i, tm, regions), 0, 0))
    stage = [pltpu.VMEM((max_slots, MOE_ROW_BLOCK, D_MODEL), BF16), pltpu.SemaphoreType.DMA((max_slots,))]

    xs, gate, rank, offs, totals = pl.pallas_call(
        functools.partial(_moe_dispatch_kernel, cap=cap),
        out_shape=(jax.ShapeDtypeStruct((n_e * cap, D_MODEL), BF16),
                   jax.ShapeDtypeStruct((t, LANES), F32), jax.ShapeDtypeStruct((t, LANES), F32),
                   jax.ShapeDtypeStruct((n_tiles, 1, LANES), F32), jax.ShapeDtypeStruct((1, LANES), F32)),
        grid=(n_tiles,),
        in_specs=[pl.BlockSpec((tm, D_MODEL), lambda i: (i, 0)), mod_spec,
                  pl.BlockSpec((1, D_MODEL), lambda i: (0, 0)),
                  pl.BlockSpec((D_MODEL, LANES), lambda i: (0, 0)),
                  pl.BlockSpec((1, LANES), lambda i: (0, 0)),
                  pl.BlockSpec(memory_space=pl.ANY)],
        input_output_aliases={5: 0},
        out_specs=(pl.BlockSpec(memory_space=pl.ANY),
                   pl.BlockSpec((tm, LANES), lambda i: (i, 0)), pl.BlockSpec((tm, LANES), lambda i: (i, 0)),
                   pl.BlockSpec((None, 1, LANES), lambda i: (i, 0, 0)),
                   pl.BlockSpec((1, LANES), lambda i: (0, 0))),
        scratch_shapes=[pltpu.VMEM((tm, D_MODEL), BF16),
                        pltpu.VMEM((LANES, tm), F32), pltpu.VMEM((LANES, tm), F32)] + stage
        + [pltpu.VMEM((1, LANES), F32), pltpu.SMEM((1,), jnp.int32)],
        compiler_params=_params("arbitrary"),
        name="moe_dispatch",
    )(x, mods, g, rw, rb, donor)

    tile_e, tile_rb, n_active = _moe_tile_table(totals[0, :n_e].astype(jnp.int32), cap, n_steps)
    ys = pl.pallas_call(
        _moe_expert_kernel,
        out_shape=jax.ShapeDtypeStruct((n_e * cap, D_MODEL), BF16),
        grid_spec=pltpu.PrefetchScalarGridSpec(
            num_scalar_prefetch=3,
            grid=(n_steps,),
            in_specs=[pl.BlockSpec((TR_EXPERT, D_MODEL), lambda s, te, tr, na: (tr[s], 0)),
                      pl.BlockSpec((None, D_MODEL, hid), lambda s, te, tr, na: (te[s], 0, 0)),
                      pl.BlockSpec((None, D_MODEL, hid), lambda s, te, tr, na: (te[s], 0, 0)),
                      pl.BlockSpec((None, hid, D_MODEL), lambda s, te, tr, na: (te[s], 0, 0))],
            out_specs=pl.BlockSpec((TR_EXPERT, D_MODEL), lambda s, te, tr, na: (tr[s], 0))),
        input_output_aliases={3: 0},
        compiler_params=_params("arbitrary"),
        name="moe_experts",
    )(tile_e, tile_rb, n_active, xs, w1, w3, w2)

    out = pl.pallas_call(
        functools.partial(_moe_combine_kernel, cap=cap),
        out_shape=jax.ShapeDtypeStruct((t, D_MODEL), F32),
        grid_spec=pltpu.PrefetchScalarGridSpec(
            num_scalar_prefetch=1,
            grid=(n_tiles,),
            in_specs=[pl.BlockSpec((tm, D_MODEL), lambda i, off: (i, 0)), mod_spec,
                      pl.BlockSpec((tm, LANES), lambda i, off: (i, 0)),
                      pl.BlockSpec((tm, LANES), lambda i, off: (i, 0)),
                      pl.BlockSpec(memory_space=pl.ANY)],
            out_specs=pl.BlockSpec((tm, D_MODEL), lambda i, off: (i, 0)),
            scratch_shapes=stage),
        compiler_params=_params("arbitrary"),
        name="moe_combine",
    )(offs.reshape(-1).astype(jnp.int32), x, mods, gate, rank, ys)
    return out, ys


def _final_norm_kernel(x_ref, g_ref, o_ref):
    o_ref[...] = _rms(x_ref[...], g_ref[...])


def _final_norm(x, g, tok_off, n_tok):
    tm = min(TM_NORM, n_tok)
    off = tok_off // tm
    return pl.pallas_call(
        _final_norm_kernel,
        out_shape=jax.ShapeDtypeStruct((n_tok, D_MODEL), F32),
        grid=(n_tok // tm,),
        in_specs=[pl.BlockSpec((tm, D_MODEL), lambda i: (i + off, 0)),
                  pl.BlockSpec((1, D_MODEL), lambda i: (0, 0))],
        out_specs=pl.BlockSpec((tm, D_MODEL), lambda i: (i, 0)),
        compiler_params=_params("parallel"),
        name="final_norm",
    )(x, g)


CAST_BLOCK_BYTES = 6 * 1024 * 1024


def _cast_kernel(x_ref, o_ref):
    o_ref[...] = x_ref[...].astype(o_ref.dtype)


def _to_bf16(w, j):
    w3 = w.reshape(w.shape[0], -1, w.shape[-1])
    _, rows, cols = w3.shape
    fits = [r for r in range(BF16_SUBLANES, rows + 1, BF16_SUBLANES)
            if rows % r == 0 and r * cols * 4 <= CAST_BLOCK_BYTES]
    tr = max(fits)
    out = pl.pallas_call(
        _cast_kernel,
        out_shape=jax.ShapeDtypeStruct((rows, cols), BF16),
        grid=(rows // tr,),
        in_specs=[pl.BlockSpec((None, tr, cols), lambda i: (j, i, 0))],
        out_specs=pl.BlockSpec((tr, cols), lambda i: (i, 0)),
        compiler_params=_params("parallel"),
        name="weights_to_bf16",
    )(w3)
    return out.reshape(w.shape[1:])

def _grid_pos_embed(l, d):
    rows = l // GRID_W
    row = jnp.repeat(jnp.arange(rows, dtype=F32), GRID_W)
    col = jnp.tile(jnp.arange(GRID_W, dtype=F32), rows)
    quarter = d // 4
    omega = jnp.exp(-math.log(10000.0) * jnp.arange(quarter, dtype=F32) / quarter)
    er = row[:, None] * omega
    ec = col[:, None] * omega
    return jnp.concatenate([jnp.sin(er), jnp.cos(er), jnp.sin(ec), jnp.cos(ec)], axis=-1)


def _pad_lanes(a, width):
    return jnp.pad(a, [(0, 0)] * (a.ndim - 1) + [(0, width - a.shape[-1])])


def _reorder_w_in(w):
    o = np.cumsum([0, SSD_INNER, SSD_INNER + 2 * SSD_GROUPS * SSD_STATE, 2 * SSD_HEADS, GLA_KDIM, GLA_KDIM,
                   GLA_VDIM, GLA_VDIM, 2 * GLA_GATE_RANK, D_MODEL, D_MODEL])
    z, xbc, dt, qq, kk, vv, rr, lr, ga, gb = [w[:, int(o[i]):int(o[i + 1])] for i in range(10)]
    main = jnp.concatenate([z, vv, rr, ga, gb, qq, kk], axis=1).astype(BF16)
    small = _pad_lanes(jnp.concatenate([dt, lr], axis=1), LANES).astype(BF16)
    return main, xbc.astype(BF16), small


def _layer_weights(i, p):
    conv_w = jnp.pad(p["ssd_conv_w"][i], ((0, SUBLANES - SSD_CONV), (0, 0)))
    conv_b = p["ssd_conv_b"][i][None, :]
    gate_w = jnp.zeros((LANES, 2 * GLA_KDIM), F32)
    gate_w = gate_w.at[SM_LR:SM_LR + GLA_GATE_RANK, :GLA_KDIM].set(p["gla_gate_w"][i, 0])
    gate_w = gate_w.at[SM_LR + GLA_GATE_RANK:SM_LR + 2 * GLA_GATE_RANK, GLA_KDIM:].set(p["gla_gate_w"][i, 1])
    gate_w_hi = gate_w.astype(BF16)
    gate_w_lo = (gate_w - gate_w_hi.astype(F32)).astype(BF16)
    w_main, w_conv, w_small = _reorder_w_in(p["w_in"][i])
    return {
        "w_in": w_main, "w_in_conv": w_conv, "w_in_small": w_small,
        "conv_w": conv_w, "conv_b": conv_b,
        "dt_bias": _pad_lanes(p["ssd_dt_bias"][i].reshape(1, -1), LANES),
        "a_log": _pad_lanes(p["ssd_a_log"][i].reshape(1, -1), LANES),
        "d_exp": jnp.repeat(p["ssd_d"][i], SSD_HEADDIM)[None, :],
        "ssd_norm_g": p["ssd_norm_g"][i][None, :],
        "gate_w_hi": gate_w_hi, "gate_w_lo": gate_w_lo, "gate_b": p["gla_gate_b"][i].reshape(1, -1),
        "gla_norm_g": p["gla_norm_g"][i][None, :],
        "w_bs": p["w_branch_ssd"][i].astype(BF16), "w_bg": p["w_branch_gla"][i].astype(BF16),
        "w_o": p["w_out"][i].astype(BF16),
    }


def _ssd_state_in(s):
    b = s.shape[0]
    return jnp.transpose(s, (0, 1, 4, 2, 3)).reshape(b, 2, SSD_STATE, SSD_INNER)


def _ssd_state_out(f, b):
    s = jnp.stack([f, b], axis=1).reshape(-1, 2, SSD_STATE, SSD_HEADS, SSD_HEADDIM)
    return jnp.transpose(s, (0, 1, 3, 4, 2))


def _gla_state_in(s):
    b = s.shape[0]
    return jnp.transpose(s, (0, 1, 4, 2, 3)).reshape(b, 2, GLA_DV, GLA_KDIM)


def _gla_state_out(f, b):
    s = jnp.stack([f, b], axis=1).reshape(-1, 2, GLA_DV, GLA_HEADS, GLA_DK)
    return jnp.transpose(s, (0, 1, 3, 4, 2))


def _trunk(x_prompt, x_sample, state_ssd, state_gla, c, c_ctx, p):
    n0, l0, _ = x_prompt.shape
    n1, l1, _ = x_sample.shape
    regions = ((n0, l0), (n1, l1))
    depth = p["w_in"].shape[0]

    cc = jnp.zeros((MOD_ROWS, D_MODEL), F32).at[0].set(c_ctx).at[1:1 + n1].set(c)
    mods = _modulation_table(cc, p["ada_w"], p["ada_b"])[:, :, None, :]
    x = _assemble_tokens(x_prompt, x_sample, _grid_pos_embed(l1, D_MODEL), regions)

    ssd_states, gla_states = [], []
    moe_rows = None
    for i in range(depth):
        lw = _layer_weights(i, p)
        init_ssd = jnp.concatenate([jnp.zeros((n0, 2, SSD_STATE, SSD_INNER), F32),
                                    _ssd_state_in(state_ssd[:, i])], axis=0)
        init_gla = jnp.concatenate([jnp.zeros((n0, 2, GLA_DV, GLA_KDIM), F32),
                                    _gla_state_in(state_gla[:, i])], axis=0)
        proj, xbc, small = _in_projection(x, mods[i], p["norm1_g"][i][None, :], lw, regions)
        y_ssd, sf, sb, y_gla, gf, gb = _mixers(proj, xbc, small, init_ssd, init_gla, lw, regions)
        ssd_states.append(_ssd_state_out(sf[:n0], sb[:n0]))
        gla_states.append(_gla_state_out(gf[:n0], gb[:n0]))
        x = _merge(x, y_ssd, y_gla, proj, mods[i], lw, regions)
        j = i // 2
        g2 = p["norm2_g"][i][None, :]
        if i % 2 == 0:
            x = _dense_ffn(x, mods[i], g2, p["ffn_w1"][j].astype(BF16), p["ffn_w3"][j].astype(BF16),
                           p["ffn_w2"][j].astype(BF16), regions)
        else:
            if moe_rows is None:
                n_e = p["moe_w1"].shape[1]
                moe_rows = jnp.zeros((n_e * _moe_geometry(x.shape[0], regions, n_e)[2], D_MODEL), BF16)
            x, moe_rows = _moe_ffn(x, mods[i], g2, _pad_lanes(p["router_w"][j], LANES),
                                   _pad_lanes(p["router_b"][j][None, :], LANES),
                                   _to_bf16(p["moe_w1"], j), _to_bf16(p["moe_w3"], j),
                                   _to_bf16(p["moe_w2"], j), regions, moe_rows)

    gfin = p["final_norm_g"][None, :]
    y_prompt = _final_norm(x, gfin, 0, n0 * l0).reshape(n0, l0, D_MODEL)
    y_sample = _final_norm(x, gfin, n0 * l0, n1 * l1).reshape(n1, l1, D_MODEL)
    return y_prompt, y_sample, jnp.stack(ssd_states, axis=1), jnp.stack(gla_states, axis=1)


def kernel(x_prompt, x_sample, state_ssd, state_gla, c, c_ctx, ada_w, ada_b, norm1_g, norm2_g, w_in, ssd_conv_w, ssd_conv_b, ssd_dt_bias, ssd_a_log, ssd_d, ssd_norm_g, gla_gate_w, gla_gate_b, gla_norm_g, w_branch_ssd, w_branch_gla, w_out, ffn_w1, ffn_w3, ffn_w2, router_w, router_b, moe_w1, moe_w3, moe_w2, final_norm_g):
    p = dict(ada_w=ada_w, ada_b=ada_b, norm1_g=norm1_g, norm2_g=norm2_g, w_in=w_in, ssd_conv_w=ssd_conv_w,
             ssd_conv_b=ssd_conv_b, ssd_dt_bias=ssd_dt_bias, ssd_a_log=ssd_a_log, ssd_d=ssd_d,
             ssd_norm_g=ssd_norm_g, gla_gate_w=gla_gate_w, gla_gate_b=gla_gate_b, gla_norm_g=gla_norm_g,
             w_branch_ssd=w_branch_ssd, w_branch_gla=w_branch_gla, w_out=w_out, ffn_w1=ffn_w1, ffn_w3=ffn_w3,
             ffn_w2=ffn_w2, router_w=router_w, router_b=router_b, moe_w1=moe_w1, moe_w3=moe_w3, moe_w2=moe_w2,
             final_norm_g=final_norm_g)
    return _trunk(x_prompt, x_sample, state_ssd, state_gla, c, c_ctx, p)
```
